```python
import jax, jax.numpy as jnp
from jax import lax
import numpy as np

D_MODEL = 1024
BATCH = 16
SEQ = 256
DEPTH = 2
DEC_BATCH = 2
DEC_SEQ = 2048
PAST_LEN = 512

GRID_W = 64
HEAD_DIM = 64
SCALE = HEAD_DIM ** -0.5
Q_BLOCK = 128
A_HEADS = 8
A_KV_HEADS = 2
A_GROUP = A_HEADS // A_KV_HEADS
A_WINDOW = 128
A_BLOCK = 128
B_GROUPS = 8
B_GROUP_DIM = 64
B_WIDTH = B_GROUPS * B_GROUP_DIM
C_HEADS = 8
C_WIN_ROWS = 8
C_WIN_COLS = 16
C_QCOL_BLOCK = 16
C_KCOL_BLOCK = 32
A_Q = A_HEADS * HEAD_DIM
A_KV = A_KV_HEADS * HEAD_DIM
C_W = C_HEADS * HEAD_DIM
N_BRANCH = 3
IN_SIZES = (A_Q, A_KV, A_KV, B_WIDTH, C_W, C_W, C_W, N_BRANCH * D_MODEL)
IN_WIDTH = A_Q + 2 * A_KV + B_WIDTH + 3 * C_W + N_BRANCH * D_MODEL
D_FF = 2816
N_EXPERTS = 8
TOP_K = 2
D_FF_EXPERT = 3584
N_DENSE = (DEPTH + 1) // 2
N_MOE = DEPTH // 2
ROPE_BASE = 10000.0
RMS_EPS = 1e-6
NEG_INF = -1e30

kernel_name = "hybrid_diffusion_prefix_trunk_step"


def rms_norm(x, g):
    xf = x.astype(jnp.float32)
    y = xf * lax.rsqrt(jnp.mean(xf * xf, axis=-1, keepdims=True) + RMS_EPS)
    return (y * g.astype(jnp.float32)).astype(x.dtype)


def modulation(cond, w, b):
    m = jax.nn.silu(cond) @ w + b
    return [t[:, None, :] for t in jnp.split(m, 6, axis=-1)]


def adaln(x, g, shift, scale):
    return rms_norm(x, g) * (1 + scale) + shift


def axial_rope_tables(L):
    t = jnp.arange(L)
    row = (t // GRID_W).astype(jnp.float32)
    col = (t % GRID_W).astype(jnp.float32)
    n_freq = HEAD_DIM // 4
    inv = ROPE_BASE ** (-jnp.arange(n_freq, dtype=jnp.float32) / n_freq)
    ang = jnp.concatenate([row[:, None] * inv, col[:, None] * inv], axis=-1)
    return jnp.cos(ang), jnp.sin(ang)


def apply_rope(x, cos, sin):
    xf = x.astype(jnp.float32)
    x1, x2 = jnp.split(xf, 2, axis=-1)
    cs, sn = cos[None, :, None, :], sin[None, :, None, :]
    return jnp.concatenate([x1 * cs - x2 * sn, x1 * sn + x2 * cs], axis=-1).astype(x.dtype)


def split_projection(h, w_in_l, qk_a, qk_c):
    Bn, L = h.shape[:2]
    idx = np.cumsum(IN_SIZES)[:-1].tolist()
    qa, ka, va, fb, qc, kc, vc, gates = jnp.split(h @ w_in_l, idx, axis=-1)
    qa = rms_norm(qa.reshape(Bn, L, A_HEADS, HEAD_DIM), qk_a[0])
    ka = rms_norm(ka.reshape(Bn, L, A_KV_HEADS, HEAD_DIM), qk_a[1])
    va = va.reshape(Bn, L, A_KV_HEADS, HEAD_DIM)
    qc = rms_norm(qc.reshape(Bn, L, C_HEADS, HEAD_DIM), qk_c[0])
    kc = rms_norm(kc.reshape(Bn, L, C_HEADS, HEAD_DIM), qk_c[1])
    vc = vc.reshape(Bn, L, C_HEADS, HEAD_DIM)
    return qa, ka, va, fb, qc, kc, vc, gates


def sink_softmax(s, sink):
    m = jnp.maximum(jnp.max(s, axis=-1, keepdims=True), sink)
    e = jnp.exp(s - m)
    return e / (jnp.sum(e, axis=-1, keepdims=True) + jnp.exp(sink - m))


def ctx_attn_a(q, k, v, sink):
    Bn, L = q.shape[:2]
    qb = jnp.moveaxis(q.reshape(Bn, L // Q_BLOCK, Q_BLOCK, A_KV_HEADS, A_GROUP, HEAD_DIM), 1, 0)
    sink_b = sink.astype(jnp.float32).reshape(1, A_KV_HEADS, A_GROUP, 1, 1)

    def one(qblk):
        s = jnp.einsum('bqhgd,bkhd->bhgqk', qblk, k, preferred_element_type=jnp.float32) * SCALE
        p = sink_softmax(s, sink_b).astype(v.dtype)
        return jnp.einsum('bhgqk,bkhd->bqhgd', p, v)

    o = lax.map(one, qb)
    return jnp.moveaxis(o, 0, 1).reshape(Bn, L, A_Q)


def window_attn_latent(q, k, v, kc, vc, sink):
    Bn, L = q.shape[:2]
    nb = L // A_BLOCK
    pad = jnp.zeros((Bn, A_BLOCK, A_KV_HEADS, HEAD_DIM), k.dtype)
    kp = jnp.concatenate([pad, k, pad], axis=1)
    vp = jnp.concatenate([pad.astype(v.dtype), v, pad.astype(v.dtype)], axis=1)
    qb = jnp.moveaxis(q.reshape(Bn, nb, A_BLOCK, A_KV_HEADS, A_GROUP, HEAD_DIM), 1, 0)
    qi = jnp.arange(A_BLOCK)[:, None]
    kj = jnp.arange(3 * A_BLOCK)[None, :] - A_BLOCK
    band = jnp.abs(kj - qi) <= A_WINDOW
    sink_b = sink.astype(jnp.float32).reshape(1, A_KV_HEADS, A_GROUP, 1, 1)
    nk = 3 * A_BLOCK

    def one(args):
        qblk, b = args
        kb = lax.dynamic_slice_in_dim(kp, b * A_BLOCK, nk, axis=1)
        vb = lax.dynamic_slice_in_dim(vp, b * A_BLOCK, nk, axis=1)
        kpos = b * A_BLOCK + kj
        valid = band & (kpos >= 0) & (kpos < L)
        s_loc = jnp.einsum('bqhgd,bkhd->bhgqk', qblk, kb, preferred_element_type=jnp.float32) * SCALE
        s_loc = jnp.where(valid, s_loc, NEG_INF)
        s_ctx = jnp.einsum('bqhgd,bkhd->bhgqk', qblk, kc, preferred_element_type=jnp.float32) * SCALE
        p = sink_softmax(jnp.concatenate([s_loc, s_ctx], axis=-1), sink_b).astype(v.dtype)
        return (jnp.einsum('bhgqk,bkhd->bqhgd', p[..., :nk], vb)
                + jnp.einsum('bhgqk,bkhd->bqhgd', p[..., nk:], vc))

    o = lax.map(one, (qb, jnp.arange(nb)))
    return jnp.moveaxis(o, 0, 1).reshape(Bn, L, A_Q)


def ctx_attn_c(q, k, v):
    Bn, L = q.shape[:2]
    qb = jnp.moveaxis(q.reshape(Bn, L // Q_BLOCK, Q_BLOCK, C_HEADS, HEAD_DIM), 1, 0)

    def one(qblk):
        s = jnp.einsum('bqhd,bkhd->bhqk', qblk, k, preferred_element_type=jnp.float32) * SCALE
        p = jax.nn.softmax(s, axis=-1).astype(v.dtype)
        return jnp.einsum('bhqk,bkhd->bqhd', p, v)

    o = lax.map(one, qb)
    return jnp.moveaxis(o, 0, 1).reshape(Bn, L, C_W)


def neighborhood_attn_latent(q, k, v, kc, vc, rpb):
    Bn, L = q.shape[:2]
    rows = L // GRID_W
    kr = min(C_WIN_ROWS, rows)
    ncb = GRID_W // C_QCOL_BLOCK
    nk = kr * C_KCOL_BLOCK
    qg = jnp.moveaxis(q.reshape(Bn, rows, ncb, C_QCOL_BLOCK, C_HEADS, HEAD_DIM), 1, 0)
    kg = k.reshape(Bn, rows, GRID_W, C_HEADS, HEAD_DIM)
    vg = v.reshape(Bn, rows, GRID_W, C_HEADS, HEAD_DIM)
    qcol = jnp.arange(GRID_W).reshape(ncb, C_QCOL_BLOCK)
    qcs = jnp.clip(qcol - C_WIN_COLS // 2, 0, GRID_W - C_WIN_COLS)
    kstart = jnp.clip(jnp.arange(ncb) * C_QCOL_BLOCK - C_WIN_COLS // 2, 0, GRID_W - C_KCOL_BLOCK)
    kcol = kstart[:, None] + jnp.arange(C_KCOL_BLOCK)[None, :]
    col_ok = ((kcol[:, None, :] >= qcs[:, :, None])
              & (kcol[:, None, :] < qcs[:, :, None] + C_WIN_COLS))
    col_ok = jnp.broadcast_to(col_ok[:, :, None, :], (ncb, C_QCOL_BLOCK, kr, C_KCOL_BLOCK))
    col_ok = col_ok.reshape(ncb, 1, C_QCOL_BLOCK, nk)
    dc = jnp.clip(kcol[:, None, :] - qcol[:, :, None], -(C_WIN_COLS - 1), C_WIN_COLS - 1) + C_WIN_COLS - 1

    def one(args):
        qrow, r = args
        rs = jnp.clip(r - kr // 2, 0, rows - kr)
        kslab = lax.dynamic_slice_in_dim(kg, rs, kr, axis=1)[:, :, kcol]
        vslab = lax.dynamic_slice_in_dim(vg, rs, kr, axis=1)[:, :, kcol]
        kn = jnp.moveaxis(kslab, 2, 1).reshape(Bn, ncb, nk, C_HEADS, HEAD_DIM)
        vn = jnp.moveaxis(vslab, 2, 1).reshape(Bn, ncb, nk, C_HEADS, HEAD_DIM)
        dr = rs + jnp.arange(kr) - r + C_WIN_ROWS - 1
        bias = rpb[:, dr[None, None, :, None], dc[:, :, None, :]]
        bias = jnp.moveaxis(bias, 0, 1).reshape(ncb, C_HEADS, C_QCOL_BLOCK, nk).astype(jnp.float32)
        s_loc = jnp.einsum('bcqhd,bckhd->bchqk', qrow, kn, preferred_element_type=jnp.float32) * SCALE + bias
        s_loc = jnp.where(col_ok, s_loc, NEG_INF)
        s_ctx = jnp.einsum('bcqhd,bkhd->bchqk', qrow, kc, preferred_element_type=jnp.float32) * SCALE
        p = jax.nn.softmax(jnp.concatenate([s_loc, s_ctx], axis=-1), axis=-1).astype(v.dtype)
        o = (jnp.einsum('bchqk,bckhd->bcqhd', p[..., :nk], vn)
             + jnp.einsum('bchqk,bkhd->bcqhd', p[..., nk:], vc))
        return o.reshape(Bn, GRID_W, C_W)

    o = lax.map(one, (qg, jnp.arange(rows)))
    return jnp.moveaxis(o, 0, 1).reshape(Bn, L, C_W)


def fourier_mix(u):
    Bn, L, _ = u.shape
    ug = u.reshape(Bn, L, B_GROUPS, B_GROUP_DIM).astype(jnp.float32)
    f = jnp.fft.fft2(ug, axes=(1, 3), norm='ortho').real
    return f.reshape(Bn, L, B_WIDTH).astype(u.dtype)


def merge_branches(oa, ob, oc, gates, wa, wb, wc, wo):
    ga, gb, gc = jnp.split(jax.nn.sigmoid(gates), N_BRANCH, axis=-1)
    return (ga * (oa @ wa) + gb * (ob @ wb) + gc * (oc @ wc)) @ wo


def swiglu(t, wg, wu, wd):
    return (jax.nn.silu(t @ wg) * (t @ wu)) @ wd


def moe_swiglu(h, w_router, wg, wu, wd):
    Bn, L, D = h.shape
    t = h.reshape(Bn * L, D)
    logits = (t @ w_router).astype(jnp.float32)
    top_v, top_i = lax.top_k(logits, TOP_K)
    top_w = jax.nn.softmax(top_v, axis=-1)
    comb = jnp.sum(jax.nn.one_hot(top_i, N_EXPERTS, dtype=jnp.float32) * top_w[..., None], axis=1)
    out = jnp.zeros_like(t)
    for e in range(N_EXPERTS):
        out = out + comb[:, e:e + 1].astype(t.dtype) * swiglu(t, wg[e], wu[e], wd[e])
    return out.reshape(Bn, L, D)


def channel_mix(h, l, w_ff_gate, w_ff_up, w_ff_down, w_router, w_exp_gate, w_exp_up, w_exp_down):
    i = l // 2
    if l % 2 == 0:
        return swiglu(h, w_ff_gate[i], w_ff_up[i], w_ff_down[i])
    return moe_swiglu(h, w_router[i], w_exp_gate[i], w_exp_up[i], w_exp_down[i])


def setup_inputs(seed: int = 0) -> dict:
    key = jax.random.key(seed)
    ks = jax.random.split(key, 32)
    f32 = jnp.float32

    def nrm(k, shape, scale):
        return jax.random.normal(k, shape, f32) * scale

    D = D_MODEL
    return {
        "x_prompt": nrm(ks[0], (BATCH, SEQ, D), 1.0),
        "x_sample": nrm(ks[1], (DEC_BATCH, DEC_SEQ, D), 1.0),
        "cache_a_k": nrm(ks[2], (DEC_BATCH, DEPTH, PAST_LEN, A_KV_HEADS, HEAD_DIM), 1.0),
        "cache_a_v": nrm(ks[3], (DEC_BATCH, DEPTH, PAST_LEN, A_KV_HEADS, HEAD_DIM), 1.0),
        "cache_c_k": nrm(ks[4], (DEC_BATCH, DEPTH, PAST_LEN, C_HEADS, HEAD_DIM), 1.0),
        "cache_c_v": nrm(ks[5], (DEC_BATCH, DEPTH, PAST_LEN, C_HEADS, HEAD_DIM), 1.0),
        "c": nrm(ks[6], (DEC_BATCH, D), 1.0),
        "c_ctx": nrm(ks[7], (D,), 1.0),
        "w_mod": nrm(ks[8], (DEPTH, D, 6 * D), 0.5 * D ** -0.5),
        "b_mod": nrm(ks[9], (DEPTH, 6 * D), 0.02),
        "norm1_g": 1.0 + nrm(ks[10], (DEPTH, D), 0.05),
        "norm2_g": 1.0 + nrm(ks[11], (DEPTH, D), 0.05),
        "w_in": nrm(ks[12], (DEPTH, D, IN_WIDTH), D ** -0.5),
        "qk_norm_a": 1.0 + nrm(ks[13], (DEPTH, 2, HEAD_DIM), 0.05),
        "qk_norm_c": 1.0 + nrm(ks[14], (DEPTH, 2, HEAD_DIM), 0.05),
        "sink_a": nrm(ks[15], (DEPTH, A_HEADS), 0.5),
        "rpb_c": nrm(ks[16], (DEPTH, C_HEADS, 2 * C_WIN_ROWS - 1, 2 * C_WIN_COLS - 1), 0.5),
        "w_branch_a": nrm(ks[17], (DEPTH, A_Q, D), A_Q ** -0.5),
        "w_branch_b": nrm(ks[18], (DEPTH, B_WIDTH, D), B_WIDTH ** -0.5),
        "w_branch_c": nrm(ks[19], (DEPTH, C_W, D), C_W ** -0.5),
        "w_out": nrm(ks[20], (DEPTH, D, D), D ** -0.5),
        "w_ff_gate": nrm(ks[21], (N_DENSE, D, D_FF), D ** -0.5),
        "w_ff_up": nrm(ks[22], (N_DENSE, D, D_FF), D ** -0.5),
        "w_ff_down": nrm(ks[23], (N_DENSE, D_FF, D), D_FF ** -0.5),
        "w_router": nrm(ks[24], (N_MOE, D, N_EXPERTS), D ** -0.5),
        "w_exp_gate": nrm(ks[25], (N_MOE, N_EXPERTS, D, D_FF_EXPERT), D ** -0.5),
        "w_exp_up": nrm(ks[26], (N_MOE, N_EXPERTS, D, D_FF_EXPERT), D ** -0.5),
        "w_exp_down": nrm(ks[27], (N_MOE, N_EXPERTS, D_FF_EXPERT, D), D_FF_EXPERT ** -0.5),
    }


def reference(x_prompt, x_sample, cache_a_k, cache_a_v, cache_c_k, cache_c_v, c, c_ctx,
              w_mod, b_mod, norm1_g, norm2_g, w_in, qk_norm_a, qk_norm_c, sink_a, rpb_c,
              w_branch_a, w_branch_b, w_branch_c, w_out, w_ff_gate, w_ff_up, w_ff_down,
              w_router, w_exp_gate, w_exp_up, w_exp_down):
    xp, xs = x_prompt, x_sample
    Bp, Lp = xp.shape[:2]
    Bs, Ls = xs.shape[:2]
    cos, sin = axial_rope_tables(Ls)
    new_ak, new_av, new_ck, new_cv = [], [], [], []
    for l in range(DEPTH):
        sh1, sc1, g1, sh2, sc2, g2 = modulation(c_ctx[None, :], w_mod[l], b_mod[l])
        h = adaln(xp, norm1_g[l], sh1, sc1)
        qa, ka, va, fb, qc, kc, vc, gates = split_projection(h, w_in[l], qk_norm_a[l], qk_norm_c[l])
        oa = ctx_attn_a(qa.reshape(Bp, Lp, A_KV_HEADS, A_GROUP, HEAD_DIM), ka, va, sink_a[l])
        ob = fourier_mix(fb)
        oc = ctx_attn_c(qc, kc, vc)
        xp = xp + g1 * merge_branches(oa, ob, oc, gates, w_branch_a[l], w_branch_b[l], w_branch_c[l], w_out[l])
        h2 = adaln(xp, norm2_g[l], sh2, sc2)
        xp = xp + g2 * channel_mix(h2, l, w_ff_gate, w_ff_up, w_ff_down, w_router, w_exp_gate, w_exp_up, w_exp_down)
        new_ak.append(ka)
        new_av.append(va)
        new_ck.append(kc)
        new_cv.append(vc)

        sh1, sc1, g1, sh2, sc2, g2 = modulation(c, w_mod[l], b_mod[l])
        h = adaln(xs, norm1_g[l], sh1, sc1)
        qa, ka, va, fb, qc, kc, vc, gates = split_projection(h, w_in[l], qk_norm_a[l], qk_norm_c[l])
        qa = apply_rope(qa, cos, sin).reshape(Bs, Ls, A_KV_HEADS, A_GROUP, HEAD_DIM)
        ka = apply_rope(ka, cos, sin)
        oa = window_attn_latent(qa, ka, va, cache_a_k[:, l], cache_a_v[:, l], sink_a[l])
        ob = fourier_mix(fb)
        oc = neighborhood_attn_latent(qc, kc, vc, cache_c_k[:, l], cache_c_v[:, l], rpb_c[l])
        xs = xs + g1 * merge_branches(oa, ob, oc, gates, w_branch_a[l], w_branch_b[l], w_branch_c[l], w_out[l])
        h2 = adaln(xs, norm2_g[l], sh2, sc2)
        xs = xs + g2 * channel_mix(h2, l, w_ff_gate, w_ff_up, w_ff_down, w_router, w_exp_gate, w_exp_up, w_exp_down)

    new_a_k = jnp.stack(new_ak, axis=1)
    new_a_v = jnp.stack(new_av, axis=1)
    new_c_k = jnp.stack(new_ck, axis=1)
    new_c_v = jnp.stack(new_cv, axis=1)
    return (xp, xs, new_a_k, new_a_v, new_c_k, new_c_v)
```

```python
import functools

import numpy as np
import jax
import jax.numpy as jnp
from jax import lax
from jax.experimental import pallas as pl
from jax.experimental.pallas import tpu as pltpu

F32 = jnp.float32
BF16 = jnp.bfloat16

D_MODEL = 1024
BATCH = 16
SEQ = 256
DEPTH = 2
DEC_BATCH = 2
DEC_SEQ = 2048
PAST_LEN = 512
GRID_W = 64
HEAD_DIM = 64
SCALE = HEAD_DIM ** -0.5
A_HEADS = 8
A_KV_HEADS = 2
A_GROUP = A_HEADS // A_KV_HEADS
A_WINDOW = 128
A_BLOCK = 128
B_GROUPS = 8
B_GROUP_DIM = 64
B_WIDTH = B_GROUPS * B_GROUP_DIM
C_HEADS = 8
C_WIN_ROWS = 8
C_WIN_COLS = 16
A_Q = A_HEADS * HEAD_DIM
A_KV = A_KV_HEADS * HEAD_DIM
C_W = C_HEADS * HEAD_DIM
QKV_WIDTH = A_Q + 2 * A_KV + B_WIDTH + 3 * C_W
GATE_WIDTH = 3 * D_MODEL
D_FF = 2816
N_EXPERTS = 8
D_FF_EXPERT = 3584
ROPE_BASE = 10000.0
RMS_EPS = 1e-6
NEG_INF = -1e30

N_PROMPT = BATCH * SEQ
N_SAMPLE = DEC_BATCH * DEC_SEQ
N_TOK = N_PROMPT + N_SAMPLE
N_COND = 1 + DEC_BATCH
LANES = 128
C_QROWS = 4
C_QBLOCK = C_QROWS * GRID_W
VMEM_LIMIT = 56 * 1024 * 1024


def _cparams(sem):
    return pltpu.CompilerParams(dimension_semantics=sem, vmem_limit_bytes=VMEM_LIMIT)


def _sigmoid(x):
    return 1.0 / (1.0 + jnp.exp(-x))


def _cond_row(tile, tm):
    return jnp.maximum(tile * tm // DEC_SEQ - 1, 0)


def _mod_kernel(ct_ref, w_ref, b_ref, o_ref):
    tn = w_ref.shape[1]
    for r in range(N_COND):
        cb = ct_ref[r]
        s = cb * _sigmoid(cb)
        for cc in range(tn // LANES):
            sl = slice(cc * LANES, (cc + 1) * LANES)
            o_ref[r, :, sl] = jnp.sum(w_ref[:, sl] * s, axis=0, keepdims=True) + b_ref[:, sl]


def _modulation(cond_t, w_mod, b_mod):
    tn = 512
    n = 6 * D_MODEL
    return pl.pallas_call(
        _mod_kernel,
        grid=(DEPTH, n // tn),
        in_specs=[
            pl.BlockSpec((N_COND, D_MODEL, LANES), lambda l, j: (0, 0, 0)),
            pl.BlockSpec((None, D_MODEL, tn), lambda l, j: (l, 0, j)),
            pl.BlockSpec((None, 1, tn), lambda l, j: (l, 0, j)),
        ],
        out_specs=pl.BlockSpec((None, N_COND, 1, tn), lambda l, j: (l, 0, 0, j)),
        out_shape=jax.ShapeDtypeStruct((DEPTH, N_COND, 1, n), F32),
        compiler_params=_cparams(("parallel", "parallel")),
        name="modulation",
    )(cond_t, w_mod, b_mod.reshape(DEPTH, 1, n))


def _adaln_math(x, g, sh, sc):
    ms = jnp.mean(x * x, axis=-1, keepdims=True)
    y = x * lax.rsqrt(ms + RMS_EPS) * g
    return y * (1.0 + sc) + sh


def _adaln_kernel(x_ref, g_ref, sh_ref, sc_ref, o_ref):
    o_ref[...] = _adaln_math(x_ref[...], g_ref[...], sh_ref[...], sc_ref[...]).astype(BF16)


def _adaln_router_kernel(x_ref, g_ref, sh_ref, sc_ref, wr_ref, o_ref, lg_ref):
    h = _adaln_math(x_ref[...], g_ref[...], sh_ref[...], sc_ref[...])
    o_ref[...] = h.astype(BF16)
    lg_ref[...] = jnp.dot(h, wr_ref[...], precision=lax.Precision.HIGHEST, preferred_element_type=F32)


def _adaln(x, g, mod, l, shift_idx, w_router=None):
    tm = 512
    mspec = lambda which: pl.BlockSpec((None, None, 1, D_MODEL),
                                       lambda i: (l, _cond_row(i, tm), 0, which))
    in_specs = [pl.BlockSpec((tm, D_MODEL), lambda i: (i, 0)),
                pl.BlockSpec((1, D_MODEL), lambda i: (0, 0)),
                mspec(shift_idx), mspec(shift_idx + 1)]
    h_spec = pl.BlockSpec((tm, D_MODEL), lambda i: (i, 0))
    h_shape = jax.ShapeDtypeStruct((N_TOK, D_MODEL), BF16)
    if w_router is None:
        return pl.pallas_call(
            _adaln_kernel, grid=(N_TOK // tm,), in_specs=in_specs, out_specs=h_spec, out_shape=h_shape,
            compiler_params=_cparams(("parallel",)), name="adaln",
        )(x, g.reshape(1, D_MODEL), mod, mod)
    wr = jnp.pad(w_router, ((0, 0), (0, LANES - N_EXPERTS)))
    return pl.pallas_call(
        _adaln_router_kernel, grid=(N_TOK // tm,),
        in_specs=in_specs + [pl.BlockSpec((D_MODEL, LANES), lambda i: (0, 0))],
        out_specs=[h_spec, pl.BlockSpec((tm, LANES), lambda i: (i, 0))],
        out_shape=[h_shape, jax.ShapeDtypeStruct((N_TOK, LANES), F32)],
        compiler_params=_cparams(("parallel",)), name="adaln_router",
    )(x, g.reshape(1, D_MODEL), mod, mod, wr)


def _proj_kernel(a_ref, w_ref, o_ref, *, gate):
    acc = jnp.dot(a_ref[...], w_ref[...].astype(BF16), preferred_element_type=F32)
    if gate:
        acc = _sigmoid(acc)
    o_ref[...] = acc.astype(o_ref.dtype)


def _proj(h, w, n_out, tn, out_dtype, gate):
    tm = 1024
    return pl.pallas_call(
        functools.partial(_proj_kernel, gate=gate),
        grid=(N_TOK // tm, n_out // tn),
        in_specs=[pl.BlockSpec((tm, D_MODEL), lambda i, j: (i, 0)),
                  pl.BlockSpec((D_MODEL, tn), lambda i, j: (0, j))],
        out_specs=pl.BlockSpec((tm, tn), lambda i, j: (i, j)),
        out_shape=jax.ShapeDtypeStruct((N_TOK, n_out), out_dtype),
        compiler_params=_cparams(("parallel", "parallel")),
        name="proj_gate" if gate else "proj_qkv",
    )(h, w)


def _head_norm(x, gain, bd):
    sq = x * x
    hi = sq.astype(BF16)
    lo = (sq - hi.astype(F32)).astype(BF16)
    ms = jnp.dot(hi, bd, preferred_element_type=F32) + jnp.dot(lo, bd, preferred_element_type=F32)
    return x * lax.rsqrt(ms + RMS_EPS) * gain


def _rope(x, cos, sin_signed, first_half):
    swapped = jnp.where(first_half, pltpu.roll(x, LANES - HEAD_DIM // 2, 1), pltpu.roll(x, HEAD_DIM // 2, 1))
    return x * cos + swapped * sin_signed


def _qk_post_kernel(p_ref, cos_ref, sin_ref, ga_ref, gc_ref,
                    qa_ref, ka_ref, va_ref, fb_ref, qc_ref, kc_ref, vc_ref):
    r = lax.broadcasted_iota(jnp.int32, (LANES, LANES), 0) // HEAD_DIM
    c = lax.broadcasted_iota(jnp.int32, (LANES, LANES), 1) // HEAD_DIM
    bd = jnp.where(r == c, 1.0 / HEAD_DIM, 0.0).astype(BF16)
    lane = lax.broadcasted_iota(jnp.int32, (1, LANES), 1)
    first_half = (lane % HEAD_DIM) < HEAD_DIM // 2
    cos = cos_ref[...]
    sin = sin_ref[...]
    gqa, gka = ga_ref[0:1, :], ga_ref[1:2, :]
    gqc, gkc = gc_ref[0:1, :], gc_ref[1:2, :]
    for s in range(A_Q // LANES):
        sl = slice(s * LANES, (s + 1) * LANES)
        qa_ref[:, sl] = _rope(_head_norm(p_ref[:, sl], gqa, bd), cos, sin, first_half).astype(BF16)
    off = A_Q
    ka_ref[...] = _rope(_head_norm(p_ref[:, off:off + A_KV], gka, bd), cos, sin, first_half)
    off += A_KV
    va_ref[...] = p_ref[:, off:off + A_KV]
    off += A_KV
    fb_ref[...] = p_ref[:, off:off + B_WIDTH].astype(BF16)
    off += B_WIDTH
    for s in range(C_W // LANES):
        sl = slice(s * LANES, (s + 1) * LANES)
        qc_ref[:, sl] = _head_norm(p_ref[:, off + s * LANES:off + (s + 1) * LANES], gqc, bd).astype(BF16)
    off += C_W
    for s in range(C_W // LANES):
        sl = slice(s * LANES, (s + 1) * LANES)
        kc_ref[:, sl] = _head_norm(p_ref[:, off + s * LANES:off + (s + 1) * LANES], gkc, bd)
    off += C_W
    vc_ref[...] = p_ref[:, off:off + C_W]


def _qk_post(p, cos_t, sin_t, qk_a, qk_c):
    tm = 512
    row = lambda w: pl.BlockSpec((tm, w), lambda i: (i, 0))
    widths = (A_Q, A_KV, A_KV, B_WIDTH, C_W, C_W, C_W)
    dtypes = (BF16, F32, F32, BF16, BF16, F32, F32)
    return pl.pallas_call(
        _qk_post_kernel, grid=(N_TOK // tm,),
        in_specs=[row(QKV_WIDTH), row(LANES), row(LANES),
                  pl.BlockSpec((2, LANES), lambda i: (0, 0)), pl.BlockSpec((2, LANES), lambda i: (0, 0))],
        out_specs=[row(w) for w in widths],
        out_shape=[jax.ShapeDtypeStruct((N_TOK, w), d) for w, d in zip(widths, dtypes)],
        compiler_params=_cparams(("parallel",)), name="qk_post",
    )(p, cos_t, sin_t, jnp.tile(qk_a, (1, 2)), jnp.tile(qk_c, (1, 2)))


def _nt_dot(a, b):
    return lax.dot_general(a, b, (((1,), (1,)), ((), ())), preferred_element_type=F32)


def _softmax_pv(parts, sink):
    m = parts[0][0].max(axis=-1, keepdims=True)
    for s, _ in parts[1:]:
        m = jnp.maximum(m, s.max(axis=-1, keepdims=True))
    if sink is not None:
        m = jnp.maximum(m, sink)
    den = jnp.exp(sink - m) if sink is not None else 0.0
    o = None
    for s, v in parts:
        e = jnp.exp(s - m)
        den = den + e.sum(axis=-1, keepdims=True)
        pv = jnp.dot(e.astype(BF16), v, preferred_element_type=F32)
        o = pv if o is None else o + pv
    return o / den


def _ctx_attn_kernel(sink_ref, qa_ref, ka_ref, va_ref, qc_ref, kc_ref, vc_ref, oa_ref, oc_ref, *, l):
    for h in range(A_HEADS):
        g = h // A_GROUP
        q = qa_ref[:, h * HEAD_DIM:(h + 1) * HEAD_DIM] * SCALE
        k = ka_ref[:, g * HEAD_DIM:(g + 1) * HEAD_DIM].astype(BF16)
        v = va_ref[:, g * HEAD_DIM:(g + 1) * HEAD_DIM].astype(BF16)
        o = _softmax_pv([(_nt_dot(q, k), v)], sink_ref[l, h])
        oa_ref[:, h * HEAD_DIM:(h + 1) * HEAD_DIM] = o.astype(BF16)
    for h in range(C_HEADS):
        sl = slice(h * HEAD_DIM, (h + 1) * HEAD_DIM)
        o = _softmax_pv([(_nt_dot(qc_ref[:, sl] * SCALE, kc_ref[:, sl].astype(BF16)), vc_ref[:, sl].astype(BF16))], None)
        oc_ref[:, sl] = o.astype(BF16)


def _ctx_attn(sink_a, qa, ka, va, qc, kc, vc, l):
    blk = lambda w: pl.BlockSpec((SEQ, w), lambda b: (b, 0))
    return pl.pallas_call(
        functools.partial(_ctx_attn_kernel, l=l), grid=(BATCH,),
        in_specs=[pl.BlockSpec(memory_space=pltpu.SMEM),
                  blk(A_Q), blk(A_KV), blk(A_KV), blk(C_W), blk(C_W), blk(C_W)],
        out_specs=[blk(A_Q), blk(C_W)],
        out_shape=[jax.ShapeDtypeStruct((N_PROMPT, A_Q), BF16), jax.ShapeDtypeStruct((N_PROMPT, C_W), BF16)],
        compiler_params=_cparams(("parallel",)), name="ctx_attn",
    )(sink_a, qa, ka, va, qc, kc, vc)


def _win_attn_kernel(sink_ref, q_ref, kp_ref, kc_ref, kn_ref, vp_ref, vc_ref, vn_ref, ck_ref, cv_ref, o_ref, *, l):
    t = pl.program_id(1)
    rows = A_GROUP * A_BLOCK
    qi = lax.broadcasted_iota(jnp.int32, (rows, 3 * A_BLOCK), 0) % A_BLOCK
    kj = lax.broadcasted_iota(jnp.int32, (rows, 3 * A_BLOCK), 1) - A_BLOCK
    kpos = t * A_BLOCK + kj
    valid = (jnp.abs(kj - qi) <= A_WINDOW) & (kpos >= 0) & (kpos < DEC_SEQ)
    for g in range(A_KV_HEADS):
        sl = slice(g * HEAD_DIM, (g + 1) * HEAD_DIM)
        q = jnp.concatenate([q_ref[:, (g * A_GROUP + i) * HEAD_DIM:(g * A_GROUP + i + 1) * HEAD_DIM]
                             for i in range(A_GROUP)], axis=0) * SCALE
        sink = jnp.concatenate([jnp.full((A_BLOCK, 1), sink_ref[l, g * A_GROUP + i], F32)
                                for i in range(A_GROUP)], axis=0)
        k_loc = jnp.concatenate([kp_ref[:, sl], kc_ref[:, sl], kn_ref[:, sl]], axis=0).astype(BF16)
        v_loc = jnp.concatenate([vp_ref[:, sl], vc_ref[:, sl], vn_ref[:, sl]], axis=0).astype(BF16)
        s_loc = jnp.where(valid, _nt_dot(q, k_loc), NEG_INF)
        s_ctx = _nt_dot(q, ck_ref[:, sl].astype(BF16))
        o = _softmax_pv([(s_loc, v_loc), (s_ctx, cv_ref[:, sl].astype(BF16))], sink)
        for i in range(A_GROUP):
            h = g * A_GROUP + i
            o_ref[:, h * HEAD_DIM:(h + 1) * HEAD_DIM] = o[i * A_BLOCK:(i + 1) * A_BLOCK].astype(BF16)


def _win_attn(sink_a, qa, ka, va, cache_k, cache_v, l):
    nb = DEC_SEQ // A_BLOCK
    base = N_PROMPT // A_BLOCK

    def nbr(d):
        return lambda b, t: (base + b * nb + jnp.clip(t + d, 0, nb - 1), 0)

    kv = lambda d: pl.BlockSpec((A_BLOCK, A_KV), nbr(d))
    cache = pl.BlockSpec((None, None, PAST_LEN, A_KV), lambda b, t: (b, l, 0, 0))
    return pl.pallas_call(
        functools.partial(_win_attn_kernel, l=l), grid=(DEC_BATCH, nb),
        in_specs=[pl.BlockSpec(memory_space=pltpu.SMEM),
                  pl.BlockSpec((A_BLOCK, A_Q), nbr(0)),
                  kv(-1), kv(0), kv(1), kv(-1), kv(0), kv(1), cache, cache],
        out_specs=pl.BlockSpec((A_BLOCK, A_Q), lambda b, t: (b * nb + t, 0)),
        out_shape=jax.ShapeDtypeStruct((N_SAMPLE, A_Q), BF16),
        compiler_params=_cparams(("parallel", "parallel")), name="win_attn",
    )(sink_a, qa, ka, ka, ka, va, va, va, cache_k, cache_v)


def _nbr_attn_kernel(q_ref, kp_ref, kc_ref, kn_ref, vp_ref, vc_ref, vn_ref, ck_ref, cv_ref, bias_ref, o_ref):
    for h in range(C_HEADS):
        sl = slice(h * HEAD_DIM, (h + 1) * HEAD_DIM)
        q = q_ref[:, sl] * SCALE
        k_loc = jnp.concatenate([kp_ref[:, sl], kc_ref[:, sl], kn_ref[:, sl]], axis=0).astype(BF16)
        v_loc = jnp.concatenate([vp_ref[:, sl], vc_ref[:, sl], vn_ref[:, sl]], axis=0).astype(BF16)
        s_loc = _nt_dot(q, k_loc) + bias_ref[h]
        s_ctx = _nt_dot(q, ck_ref[:, sl].astype(BF16))
        o = _softmax_pv([(s_loc, v_loc), (s_ctx, cv_ref[:, sl].astype(BF16))], None)
        o_ref[:, sl] = o.astype(BF16)


def _nbr_attn(qc, kc, vc, cache_k, cache_v, bias, l):
    nb = DEC_SEQ // C_QBLOCK
    base = N_PROMPT // C_QBLOCK

    def nbr(d):
        return lambda j, b: (base + b * nb + jnp.clip(j + d, 0, nb - 1), 0)

    kv = lambda d: pl.BlockSpec((C_QBLOCK, C_W), nbr(d))
    cache = pl.BlockSpec((None, None, PAST_LEN, C_W), lambda j, b: (b, l, 0, 0))
    edge = lambda j, b: (l, jnp.where(j == 0, 0, jnp.where(j == nb - 1, 2, 1)), 0, 0, 0)
    return pl.pallas_call(
        _nbr_attn_kernel, grid=(nb, DEC_BATCH),
        in_specs=[kv(0), kv(-1), kv(0), kv(1), kv(-1), kv(0), kv(1), cache, cache,
                  pl.BlockSpec((None, None, C_HEADS, C_QBLOCK, 3 * C_QBLOCK), edge)],
        out_specs=pl.BlockSpec((C_QBLOCK, C_W), lambda j, b: (b * nb + j, 0)),
        out_shape=jax.ShapeDtypeStruct((N_SAMPLE, C_W), BF16),
        compiler_params=_cparams(("parallel", "parallel")), name="nbr_attn",
    )(qc, kc, kc, kc, vc, vc, vc, cache_k, cache_v, bias)


def _nbr_bias_tables(rpb):
    rows = DEC_SEQ // GRID_W
    qcol = np.arange(GRID_W)
    qcs = np.clip(qcol - C_WIN_COLS // 2, 0, GRID_W - C_WIN_COLS)
    kcol = np.arange(GRID_W)
    col_ok = (kcol[None, :] >= qcs[:, None]) & (kcol[None, :] < qcs[:, None] + C_WIN_COLS)
    dc = np.clip(kcol[None, :] - qcol[:, None], -(C_WIN_COLS - 1), C_WIN_COLS - 1) + C_WIN_COLS - 1
    onehot_dc = (dc.reshape(-1)[None, :] == np.arange(2 * C_WIN_COLS - 1)[:, None]).astype(np.float32)
    block_of_class = (0, 3, rows // C_QROWS - 1)
    dr = np.zeros((3, C_QROWS, 3, C_QROWS), np.int32)
    row_ok = np.zeros((3, C_QROWS, 3, C_QROWS), bool)
    for cls, j in enumerate(block_of_class):
        for qr in range(C_QROWS):
            r = C_QROWS * j + qr
            rs = min(max(r - C_WIN_ROWS // 2, 0), rows - C_WIN_ROWS)
            for kb in range(3):
                for kr in range(C_QROWS):
                    kabs = C_QROWS * (j - 1 + kb) + kr
                    ok = rs <= kabs < rs + C_WIN_ROWS
                    row_ok[cls, qr, kb, kr] = ok
                    dr[cls, qr, kb, kr] = min(max(kabs - r + C_WIN_ROWS - 1, 0), 2 * C_WIN_ROWS - 2)
    t = jnp.einsum('lhab,bx->lhax', rpb, jnp.asarray(onehot_dc), precision=lax.Precision.HIGHEST)
    t = jnp.take(t, jnp.asarray(dr.reshape(-1)), axis=2)
    t = t.reshape(DEPTH, C_HEADS, 3, C_QROWS, 3, C_QROWS, GRID_W, GRID_W)
    ok = row_ok[:, :, :, :, None, None] & col_ok[None, None, None, None, :, :]
    t = jnp.where(jnp.asarray(ok)[None, None], t, NEG_INF)
    t = jnp.transpose(t, (0, 2, 1, 3, 6, 4, 5, 7))
    return t.reshape(DEPTH, 3, C_HEADS, C_QBLOCK, 3 * C_QBLOCK)


def _fourier_kernel(u_ref, bc_ref, bs_ref, cl_ref, sl_ref, o_ref, zc_ref, zs_ref):
    @pl.when(pl.program_id(1) == 0)
    def _():
        u = u_ref[...]
        zc_ref[...] = jnp.dot(u, bc_ref[...].astype(BF16), preferred_element_type=F32).astype(BF16)
        zs_ref[...] = jnp.dot(u, bs_ref[...].astype(BF16), preferred_element_type=F32).astype(BF16)

    o = (jnp.dot(cl_ref[...].astype(BF16), zc_ref[...], preferred_element_type=F32)
         - jnp.dot(sl_ref[...].astype(BF16), zs_ref[...], preferred_element_type=F32))
    o_ref[...] = o.astype(BF16)


def _dft_tables(n):
    k = np.arange(n)
    ang = 2.0 * np.pi * ((k[:, None] * k[None, :]) % n) / n
    return np.cos(ang) / np.sqrt(n), np.sin(ang) / np.sqrt(n)


def _channel_dft_tables():
    c, s = _dft_tables(B_GROUP_DIM)
    eye = np.eye(B_GROUPS)
    return np.kron(eye, c).astype(np.float32), np.kron(eye, s).astype(np.float32)


def _fourier(fb, n_batch, seq, row0, tr):
    cl, sl = (jnp.asarray(a.astype(np.float32)) for a in _dft_tables(seq))
    bc, bs = (jnp.asarray(a) for a in _channel_dft_tables())
    nt = seq // tr
    const = pl.BlockSpec((B_WIDTH, B_WIDTH), lambda b, t: (0, 0))
    return pl.pallas_call(
        _fourier_kernel, grid=(n_batch, nt),
        in_specs=[pl.BlockSpec((seq, B_WIDTH), lambda b, t: (row0 // seq + b, 0)), const, const,
                  pl.BlockSpec((tr, seq), lambda b, t: (t, 0)), pl.BlockSpec((tr, seq), lambda b, t: (t, 0))],
        out_specs=pl.BlockSpec((tr, B_WIDTH), lambda b, t: (b * nt + t, 0)),
        out_shape=jax.ShapeDtypeStruct((n_batch * seq, B_WIDTH), BF16),
        scratch_shapes=[pltpu.VMEM((seq, B_WIDTH), BF16), pltpu.VMEM((seq, B_WIDTH), BF16)],
        compiler_params=_cparams(("parallel", "arbitrary")), name=f"fourier_{seq}",
    )(fb, bc, bs, cl, sl)


def _merge_kernel(oa_ref, ob_ref, oc_ref, gt_ref, x_ref, wa_ref, wb_ref, wc_ref, wo_ref, g1_ref, o_ref):
    ya = jnp.dot(oa_ref[...], wa_ref[...].astype(BF16), preferred_element_type=F32)
    yb = jnp.dot(ob_ref[...], wb_ref[...].astype(BF16), preferred_element_type=F32)
    yc = jnp.dot(oc_ref[...], wc_ref[...].astype(BF16), preferred_element_type=F32)
    d = D_MODEL
    m = (gt_ref[:, 0:d].astype(F32) * ya + gt_ref[:, d:2 * d].astype(F32) * yb
         + gt_ref[:, 2 * d:3 * d].astype(F32) * yc)
    y = jnp.dot(m.astype(BF16), wo_ref[...].astype(BF16), preferred_element_type=F32)
    o_ref[...] = x_ref[...] + g1_ref[...] * y


def _merge(oa, ob, oc, gates, x, wa, wb, wc, wo, mod, l):
    tm = 512
    row = lambda w: pl.BlockSpec((tm, w), lambda i: (i, 0))
    const = lambda r, c: pl.BlockSpec((r, c), lambda i: (0, 0))
    return pl.pallas_call(
        _merge_kernel, grid=(N_TOK // tm,),
        in_specs=[row(A_Q), row(B_WIDTH), row(C_W), row(GATE_WIDTH), row(D_MODEL),
                  const(A_Q, D_MODEL), const(B_WIDTH, D_MODEL), const(C_W, D_MODEL), const(D_MODEL, D_MODEL),
                  pl.BlockSpec((None, None, 1, D_MODEL), lambda i: (l, _cond_row(i, tm), 0, 2))],
        out_specs=row(D_MODEL),
        out_shape=jax.ShapeDtypeStruct((N_TOK, D_MODEL), F32),
        compiler_params=_cparams(("parallel",)), name="merge",
    )(oa, ob, oc, gates, x, wa, wb, wc, wo, mod)


def _ffn_kernel(h_ref, x_ref, wg_ref, wu_ref, wd_ref, g2_ref, o_ref, acc_ref):
    f = pl.program_id(1)

    @pl.when(f == 0)
    def _():
        acc_ref[...] = jnp.zeros_like(acc_ref)

    h = h_ref[...]
    g = jnp.dot(h, wg_ref[...].astype(BF16), preferred_element_type=F32)
    u = jnp.dot(h, wu_ref[...].astype(BF16), preferred_element_type=F32)
    a = (g * _sigmoid(g) * u).astype(BF16)
    acc_ref[...] += jnp.dot(a, wd_ref[...].astype(BF16), preferred_element_type=F32)

    @pl.when(f == pl.num_programs(1) - 1)
    def _():
        o_ref[...] = x_ref[...] + g2_ref[...] * acc_ref[...]


def _ffn(h, x, wg, wu, wd, mod, l):
    tm, tf = 1024, 256
    return pl.pallas_call(
        _ffn_kernel, grid=(N_TOK // tm, D_FF // tf),
        in_specs=[pl.BlockSpec((tm, D_MODEL), lambda i, f: (i, 0)),
                  pl.BlockSpec((tm, D_MODEL), lambda i, f: (i, 0)),
                  pl.BlockSpec((D_MODEL, tf), lambda i, f: (0, f)),
                  pl.BlockSpec((D_MODEL, tf), lambda i, f: (0, f)),
                  pl.BlockSpec((tf, D_MODEL), lambda i, f: (f, 0)),
                  pl.BlockSpec((None, None, 1, D_MODEL), lambda i, f: (l, _cond_row(i, tm), 0, 5))],
        out_specs=pl.BlockSpec((tm, D_MODEL), lambda i, f: (i, 0)),
        out_shape=jax.ShapeDtypeStruct((N_TOK, D_MODEL), F32),
        scratch_shapes=[pltpu.VMEM((tm, D_MODEL), F32)],
        compiler_params=_cparams(("parallel", "arbitrary")), name="ffn",
    )(h, x, wg, wu, wd, mod)


def _top2_weights(logits):
    lane = lax.broadcasted_iota(jnp.int32, logits.shape, 1).astype(F32)
    lg = jnp.where(lane < N_EXPERTS, logits, -jnp.inf)
    m1 = lg.max(axis=-1, keepdims=True)
    i1 = jnp.where(lg == m1, lane, float(LANES)).min(axis=-1, keepdims=True)
    rest = jnp.where(lane == i1, -jnp.inf, lg)
    m2 = rest.max(axis=-1, keepdims=True)
    i2 = jnp.where(rest == m2, lane, float(LANES)).min(axis=-1, keepdims=True)
    e2 = jnp.exp(m2 - m1)
    w1 = 1.0 / (1.0 + e2)
    w2 = e2 / (1.0 + e2)
    return jnp.where(lane == i1, w1, 0.0) + jnp.where(lane == i2, w2, 0.0)


def _moe_kernel(h_ref, x_ref, lg_ref, wg_ref, wu_ref, wd_ref, g2_ref, o_ref, acc_ref, comb_ref, col_ref):
    e = pl.program_id(1)
    f = pl.program_id(2)

    @pl.when((e == 0) & (f == 0))
    def _():
        acc_ref[...] = jnp.zeros_like(acc_ref)
        comb_ref[...] = _top2_weights(lg_ref[...])

    @pl.when(f == 0)
    def _():
        lane = lax.broadcasted_iota(jnp.int32, comb_ref.shape, 1)
        col_ref[...] = jnp.sum(jnp.where(lane == e, comb_ref[...], 0.0), axis=-1, keepdims=True)

    h = h_ref[...]
    g = jnp.dot(h, wg_ref[...].astype(BF16), preferred_element_type=F32)
    u = jnp.dot(h, wu_ref[...].astype(BF16), preferred_element_type=F32)
    a = (col_ref[...] * (g * _sigmoid(g) * u)).astype(BF16)
    acc_ref[...] += jnp.dot(a, wd_ref[...].astype(BF16), preferred_element_type=F32)

    @pl.when((e == pl.num_programs(1) - 1) & (f == pl.num_programs(2) - 1))
    def _():
        o_ref[...] = x_ref[...] + g2_ref[...] * acc_ref[...]


def _moe(h, x, logits, wg, wu, wd, mod, l):
    tm, tf = 1024, 256
    return pl.pallas_call(
        _moe_kernel, grid=(N_TOK // tm, N_EXPERTS, D_FF_EXPERT // tf),
        in_specs=[pl.BlockSpec((tm, D_MODEL), lambda i, e, f: (i, 0)),
                  pl.BlockSpec((tm, D_MODEL), lambda i, e, f: (i, 0)),
                  pl.BlockSpec((tm, LANES), lambda i, e, f: (i, 0)),
                  pl.BlockSpec((None, D_MODEL, tf), lambda i, e, f: (e, 0, f)),
                  pl.BlockSpec((None, D_MODEL, tf), lambda i, e, f: (e, 0, f)),
                  pl.BlockSpec((None, tf, D_MODEL), lambda i, e, f: (e, f, 0)),
                  pl.BlockSpec((None, None, 1, D_MODEL), lambda i, e, f: (l, _cond_row(i, tm), 0, 5))],
        out_specs=pl.BlockSpec((tm, D_MODEL), lambda i, e, f: (i, 0)),
        out_shape=jax.ShapeDtypeStruct((N_TOK, D_MODEL), F32),
        scratch_shapes=[pltpu.VMEM((tm, D_MODEL), F32), pltpu.VMEM((tm, LANES), F32), pltpu.VMEM((tm, 1), F32)],
        compiler_params=_cparams(("parallel", "arbitrary", "arbitrary")), name="moe",
    )(h, x, logits, wg, wu, wd, mod)


def _rope_tables():
    t = jnp.arange(DEC_SEQ)
    row = (t // GRID_W).astype(F32)
    col = (t % GRID_W).astype(F32)
    n_freq = HEAD_DIM // 4
    inv = ROPE_BASE ** (-jnp.arange(n_freq, dtype=F32) / n_freq)
    ang = jnp.concatenate([row[:, None] * inv, col[:, None] * inv], axis=-1)
    cos, sin = jnp.cos(ang), jnp.sin(ang)
    cos_h = jnp.concatenate([cos, cos], axis=-1)
    sin_h = jnp.concatenate([-sin, sin], axis=-1)
    cos_l = jnp.tile(jnp.concatenate([cos_h, cos_h], axis=-1), (DEC_BATCH, 1))
    sin_l = jnp.tile(jnp.concatenate([sin_h, sin_h], axis=-1), (DEC_BATCH, 1))
    cos_t = jnp.concatenate([jnp.ones((N_PROMPT, LANES), F32), cos_l], axis=0)
    sin_t = jnp.concatenate([jnp.zeros((N_PROMPT, LANES), F32), sin_l], axis=0)
    return cos_t, sin_t


def kernel(x_prompt, x_sample, cache_a_k, cache_a_v, cache_c_k, cache_c_v, c, c_ctx, w_mod, b_mod, norm1_g, norm2_g, w_in, qk_norm_a, qk_norm_c, sink_a, rpb_c, w_branch_a, w_branch_b, w_branch_c, w_out, w_ff_gate, w_ff_up, w_ff_down, w_router, w_exp_gate, w_exp_up, w_exp_down):
    x = jnp.concatenate([x_prompt.reshape(N_PROMPT, D_MODEL), x_sample.reshape(N_SAMPLE, D_MODEL)], axis=0)
    cond = jnp.concatenate([c_ctx[None, :], c], axis=0)
    cond_t = jnp.broadcast_to(cond[:, :, None], (N_COND, D_MODEL, LANES))
    mod = _modulation(cond_t, w_mod, b_mod)
    cos_t, sin_t = _rope_tables()
    bias = _nbr_bias_tables(rpb_c)
    ck_a = cache_a_k.reshape(DEC_BATCH, DEPTH, PAST_LEN, A_KV)
    cv_a = cache_a_v.reshape(DEC_BATCH, DEPTH, PAST_LEN, A_KV)
    ck_c = cache_c_k.reshape(DEC_BATCH, DEPTH, PAST_LEN, C_W)
    cv_c = cache_c_v.reshape(DEC_BATCH, DEPTH, PAST_LEN, C_W)

    new_ak, new_av, new_ck, new_cv = [], [], [], []
    for l in range(DEPTH):
        h = _adaln(x, norm1_g[l], mod, l, 0)
        p = _proj(h, w_in[l], QKV_WIDTH, QKV_WIDTH // 2, F32, gate=False)
        gates = _proj(h, w_in[l][:, QKV_WIDTH:], GATE_WIDTH, D_MODEL, BF16, gate=True)
        qa, ka, va, fb, qc, kc, vc = _qk_post(p, cos_t, sin_t, qk_norm_a[l], qk_norm_c[l])
        oa_p, oc_p = _ctx_attn(sink_a, qa, ka, va, qc, kc, vc, l)
        oa_s = _win_attn(sink_a, qa, ka, va, ck_a, cv_a, l)
        oc_s = _nbr_attn(qc, kc, vc, ck_c, cv_c, bias, l)
        ob_p = _fourier(fb, BATCH, SEQ, 0, SEQ)
        ob_s = _fourier(fb, DEC_BATCH, DEC_SEQ, N_PROMPT, 512)
        oa = jnp.concatenate([oa_p, oa_s], axis=0)
        ob = jnp.concatenate([ob_p, ob_s], axis=0)
        oc = jnp.concatenate([oc_p, oc_s], axis=0)
        x = _merge(oa, ob, oc, gates, x, w_branch_a[l], w_branch_b[l], w_branch_c[l], w_out[l], mod, l)
        i = l // 2
        if l % 2 == 0:
            h2 = _adaln(x, norm2_g[l], mod, l, 3)
            x = _ffn(h2, x, w_ff_gate[i], w_ff_up[i], w_ff_down[i], mod, l)
        else:
            h2, logits = _adaln(x, norm2_g[l], mod, l, 3, w_router=w_router[i])
            x = _moe(h2, x, logits, w_exp_gate[i], w_exp_up[i], w_exp_down[i], mod, l)
        new_ak.append(ka[:N_PROMPT].reshape(BATCH, SEQ, A_KV_HEADS, HEAD_DIM))
        new_av.append(va[:N_PROMPT].reshape(BATCH, SEQ, A_KV_HEADS, HEAD_DIM))
        new_ck.append(kc[:N_PROMPT].reshape(BATCH, SEQ, C_HEADS, HEAD_DIM))
        new_cv.append(vc[:N_PROMPT].reshape(BATCH, SEQ, C_HEADS, HEAD_DIM))

    xp = x[:N_PROMPT].reshape(BATCH, SEQ, D_MODEL)
    xs = x[N_PROMPT:].reshape(DEC_BATCH, DEC_SEQ, D_MODEL)
    return (xp, xs, jnp.stack(new_ak, axis=1), jnp.stack(new_av, axis=1),
            jnp.stack(new_ck, axis=1), jnp.stack(new_cv, axis=1))
```

```python
import functools

import numpy as np
import jax
import jax.numpy as jnp
from jax import lax
from jax.experimental import pallas as pl
from jax.experimental.pallas import tpu as pltpu

F32 = jnp.float32
BF16 = jnp.bfloat16

D_MODEL = 1024
BATCH = 16
SEQ = 256
DEPTH = 2
DEC_BATCH = 2
DEC_SEQ = 2048
PAST_LEN = 512
GRID_W = 64
HEAD_DIM = 64
SCALE = HEAD_DIM ** -0.5
A_HEADS = 8
A_KV_HEADS = 2
A_GROUP = A_HEADS // A_KV_HEADS
A_WINDOW = 128
A_BLOCK = 128
B_GROUPS = 8
B_GROUP_DIM = 64
B_WIDTH = B_GROUPS * B_GROUP_DIM
C_HEADS = 8
C_WIN_ROWS = 8
C_WIN_COLS = 16
A_Q = A_HEADS * HEAD_DIM
A_KV = A_KV_HEADS * HEAD_DIM
C_W = C_HEADS * HEAD_DIM
QKV_WIDTH = A_Q + 2 * A_KV + B_WIDTH + 3 * C_W
GATE_WIDTH = 3 * D_MODEL
D_FF = 2816
N_EXPERTS = 8
D_FF_EXPERT = 3584
ROPE_BASE = 10000.0
RMS_EPS = 1e-6
NEG_INF = -1e30

N_PROMPT = BATCH * SEQ
N_SAMPLE = DEC_BATCH * DEC_SEQ
N_TOK = N_PROMPT + N_SAMPLE
N_COND = 1 + DEC_BATCH
LANES = 128
C_QROWS = 4
C_QBLOCK = C_QROWS * GRID_W
VMEM_LIMIT = 56 * 1024 * 1024


def _cparams(sem):
    return pltpu.CompilerParams(dimension_semantics=sem, vmem_limit_bytes=VMEM_LIMIT)


def _sigmoid(x):
    return 1.0 / (1.0 + jnp.exp(-x))


def _cond_row(tile, tm):
    return jnp.maximum(tile * tm // DEC_SEQ - 1, 0)


def _mod_kernel(ct_ref, w_ref, b_ref, o_ref):
    tn = w_ref.shape[1]
    for r in range(N_COND):
        cb = ct_ref[r]
        s = cb * _sigmoid(cb)
        for cc in range(tn // LANES):
            sl = slice(cc * LANES, (cc + 1) * LANES)
            o_ref[r, :, sl] = jnp.sum(w_ref[:, sl] * s, axis=0, keepdims=True) + b_ref[:, sl]


def _modulation(cond_t, w_mod, b_mod):
    tn = 512
    n = 6 * D_MODEL
    return pl.pallas_call(
        _mod_kernel,
        grid=(DEPTH, n // tn),
        in_specs=[
            pl.BlockSpec((N_COND, D_MODEL, LANES), lambda l, j: (0, 0, 0)),
            pl.BlockSpec((None, D_MODEL, tn), lambda l, j: (l, 0, j)),
            pl.BlockSpec((None, 1, tn), lambda l, j: (l, 0, j)),
        ],
        out_specs=pl.BlockSpec((None, N_COND, 1, tn), lambda l, j: (l, 0, 0, j)),
        out_shape=jax.ShapeDtypeStruct((DEPTH, N_COND, 1, n), F32),
        compiler_params=_cparams(("parallel", "parallel")),
        name="modulation",
    )(cond_t, w_mod, b_mod.reshape(DEPTH, 1, n))


def _adaln_math(x, g, sh, sc):
    ms = jnp.mean(x * x, axis=-1, keepdims=True)
    y = x * lax.rsqrt(ms + RMS_EPS) * g
    return y * (1.0 + sc) + sh


def _adaln_kernel(x_ref, g_ref, sh_ref, sc_ref, o_ref):
    o_ref[...] = _adaln_math(x_ref[...], g_ref[...], sh_ref[...], sc_ref[...]).astype(BF16)


def _adaln_router_kernel(x_ref, g_ref, sh_ref, sc_ref, wr_ref, o_ref, lg_ref):
    h = _adaln_math(x_ref[...], g_ref[...], sh_ref[...], sc_ref[...])
    o_ref[...] = h
    lg_ref[...] = jnp.dot(h, wr_ref[...], precision=lax.Precision.HIGHEST, preferred_element_type=F32)


def _adaln(x, g, mod, l, shift_idx, w_router=None):
    tm = 512
    mspec = lambda which: pl.BlockSpec((None, None, 1, D_MODEL),
                                       lambda i: (l, _cond_row(i, tm), 0, which))
    in_specs = [pl.BlockSpec((tm, D_MODEL), lambda i: (i, 0)),
                pl.BlockSpec((1, D_MODEL), lambda i: (0, 0)),
                mspec(shift_idx), mspec(shift_idx + 1)]
    h_spec = pl.BlockSpec((tm, D_MODEL), lambda i: (i, 0))
    h_shape = jax.ShapeDtypeStruct((N_TOK, D_MODEL), BF16)
    if w_router is None:
        return pl.pallas_call(
            _adaln_kernel, grid=(N_TOK // tm,), in_specs=in_specs, out_specs=h_spec, out_shape=h_shape,
            compiler_params=_cparams(("parallel",)), name="adaln",
        )(x, g.reshape(1, D_MODEL), mod, mod)
    wr = jnp.pad(w_router, ((0, 0), (0, LANES - N_EXPERTS)))
    return pl.pallas_call(
        _adaln_router_kernel, grid=(N_TOK // tm,),
        in_specs=in_specs + [pl.BlockSpec((D_MODEL, LANES), lambda i: (0, 0))],
        out_specs=[h_spec, pl.BlockSpec((tm, LANES), lambda i: (i, 0))],
        out_shape=[jax.ShapeDtypeStruct((N_TOK, D_MODEL), F32), jax.ShapeDtypeStruct((N_TOK, LANES), F32)],
        compiler_params=_cparams(("parallel",)), name="adaln_router",
    )(x, g.reshape(1, D_MODEL), mod, mod, wr)


def _proj_kernel(a_ref, w_ref, o_ref, *, gate):
    acc = jnp.dot(a_ref[...], w_ref[...].astype(BF16), preferred_element_type=F32)
    if gate:
        acc = _sigmoid(acc)
    o_ref[...] = acc.astype(o_ref.dtype)


def _proj(h, w, n_out, tn, out_dtype, gate):
    tm = 1024
    return pl.pallas_call(
        functools.partial(_proj_kernel, gate=gate),
        grid=(N_TOK // tm, n_out // tn),
        in_specs=[pl.BlockSpec((tm, D_MODEL), lambda i, j: (i, 0)),
                  pl.BlockSpec((D_MODEL, tn), lambda i, j: (0, j))],
        out_specs=pl.BlockSpec((tm, tn), lambda i, j: (i, j)),
        out_shape=jax.ShapeDtypeStruct((N_TOK, n_out), out_dtype),
        compiler_params=_cparams(("parallel", "parallel")),
        name="proj_gate" if gate else "proj_qkv",
    )(h, w)


def _head_norm(x, gain, bd):
    sq = x * x
    hi = sq.astype(BF16)
    lo = (sq - hi.astype(F32)).astype(BF16)
    ms = jnp.dot(hi, bd, preferred_element_type=F32) + jnp.dot(lo, bd, preferred_element_type=F32)
    return x * lax.rsqrt(ms + RMS_EPS) * gain


def _rope(x, cos, sin_signed, first_half):
    swapped = jnp.where(first_half, pltpu.roll(x, LANES - HEAD_DIM // 2, 1), pltpu.roll(x, HEAD_DIM // 2, 1))
    return x * cos + swapped * sin_signed


def _qk_post_kernel(p_ref, cos_ref, sin_ref, ga_ref, gc_ref,
                    qa_ref, ka_ref, va_ref, fb_ref, qc_ref, kc_ref, vc_ref):
    r = lax.broadcasted_iota(jnp.int32, (LANES, LANES), 0) // HEAD_DIM
    c = lax.broadcasted_iota(jnp.int32, (LANES, LANES), 1) // HEAD_DIM
    bd = jnp.where(r == c, 1.0 / HEAD_DIM, 0.0).astype(BF16)
    lane = lax.broadcasted_iota(jnp.int32, (1, LANES), 1)
    first_half = (lane % HEAD_DIM) < HEAD_DIM // 2
    cos = cos_ref[...]
    sin = sin_ref[...]
    gqa, gka = ga_ref[0:1, :], ga_ref[1:2, :]
    gqc, gkc = gc_ref[0:1, :], gc_ref[1:2, :]
    for s in range(A_Q // LANES):
        sl = slice(s * LANES, (s + 1) * LANES)
        qa_ref[:, sl] = _rope(_head_norm(p_ref[:, sl], gqa, bd), cos, sin, first_half).astype(BF16)
    off = A_Q
    ka_ref[...] = _rope(_head_norm(p_ref[:, off:off + A_KV], gka, bd), cos, sin, first_half)
    off += A_KV
    va_ref[...] = p_ref[:, off:off + A_KV]
    off += A_KV
    fb_ref[...] = p_ref[:, off:off + B_WIDTH].astype(BF16)
    off += B_WIDTH
    for s in range(C_W // LANES):
        sl = slice(s * LANES, (s + 1) * LANES)
        qc_ref[:, sl] = _head_norm(p_ref[:, off + s * LANES:off + (s + 1) * LANES], gqc, bd).astype(BF16)
    off += C_W
    for s in range(C_W // LANES):
        sl = slice(s * LANES, (s + 1) * LANES)
        kc_ref[:, sl] = _head_norm(p_ref[:, off + s * LANES:off + (s + 1) * LANES], gkc, bd)
    off += C_W
    vc_ref[...] = p_ref[:, off:off + C_W]


def _qk_post(p, cos_t, sin_t, qk_a, qk_c):
    tm = 512
    row = lambda w: pl.BlockSpec((tm, w), lambda i: (i, 0))
    widths = (A_Q, A_KV, A_KV, B_WIDTH, C_W, C_W, C_W)
    dtypes = (BF16, F32, F32, BF16, BF16, F32, F32)
    return pl.pallas_call(
        _qk_post_kernel, grid=(N_TOK // tm,),
        in_specs=[row(QKV_WIDTH), row(LANES), row(LANES),
                  pl.BlockSpec((2, LANES), lambda i: (0, 0)), pl.BlockSpec((2, LANES), lambda i: (0, 0))],
        out_specs=[row(w) for w in widths],
        out_shape=[jax.ShapeDtypeStruct((N_TOK, w), d) for w, d in zip(widths, dtypes)],
        compiler_params=_cparams(("parallel",)), name="qk_post",
    )(p, cos_t, sin_t, jnp.tile(qk_a, (1, 2)), jnp.tile(qk_c, (1, 2)))


def _nt_dot(a, b):
    return lax.dot_general(a, b, (((1,), (1,)), ((), ())), preferred_element_type=F32)


def _softmax_pv(parts, sink):
    m = parts[0][0].max(axis=-1, keepdims=True)
    for s, _ in parts[1:]:
        m = jnp.maximum(m, s.max(axis=-1, keepdims=True))
    if sink is not None:
        m = jnp.maximum(m, sink)
    den = jnp.exp(sink - m) if sink is not None else 0.0
    o = None
    for s, v in parts:
        e = jnp.exp(s - m)
        den = den + e.sum(axis=-1, keepdims=True)
        pv = jnp.dot(e.astype(BF16), v, preferred_element_type=F32)
        o = pv if o is None else o + pv
    return o / den


def _ctx_attn_kernel(sink_ref, qa_ref, ka_ref, va_ref, qc_ref, kc_ref, vc_ref, oa_ref, oc_ref, *, l):
    for h in range(A_HEADS):
        g = h // A_GROUP
        q = qa_ref[:, h * HEAD_DIM:(h + 1) * HEAD_DIM] * SCALE
        k = ka_ref[:, g * HEAD_DIM:(g + 1) * HEAD_DIM].astype(BF16)
        v = va_ref[:, g * HEAD_DIM:(g + 1) * HEAD_DIM].astype(BF16)
        o = _softmax_pv([(_nt_dot(q, k), v)], sink_ref[l, h])
        oa_ref[:, h * HEAD_DIM:(h + 1) * HEAD_DIM] = o.astype(BF16)
    for h in range(C_HEADS):
        sl = slice(h * HEAD_DIM, (h + 1) * HEAD_DIM)
        o = _softmax_pv([(_nt_dot(qc_ref[:, sl] * SCALE, kc_ref[:, sl].astype(BF16)), vc_ref[:, sl].astype(BF16))], None)
        oc_ref[:, sl] = o.astype(BF16)


def _ctx_attn(sink_a, qa, ka, va, qc, kc, vc, l):
    blk = lambda w: pl.BlockSpec((SEQ, w), lambda b: (b, 0))
    return pl.pallas_call(
        functools.partial(_ctx_attn_kernel, l=l), grid=(BATCH,),
        in_specs=[pl.BlockSpec(memory_space=pltpu.SMEM),
                  blk(A_Q), blk(A_KV), blk(A_KV), blk(C_W), blk(C_W), blk(C_W)],
        out_specs=[blk(A_Q), blk(C_W)],
        out_shape=[jax.ShapeDtypeStruct((N_PROMPT, A_Q), BF16), jax.ShapeDtypeStruct((N_PROMPT, C_W), BF16)],
        compiler_params=_cparams(("parallel",)), name="ctx_attn",
    )(sink_a, qa, ka, va, qc, kc, vc)


def _win_attn_kernel(sink_ref, q_ref, kp_ref, kc_ref, kn_ref, vp_ref, vc_ref, vn_ref, ck_ref, cv_ref, o_ref, *, l):
    t = pl.program_id(1)
    rows = A_GROUP * A_BLOCK
    qi = lax.broadcasted_iota(jnp.int32, (rows, 3 * A_BLOCK), 0) % A_BLOCK
    kj = lax.broadcasted_iota(jnp.int32, (rows, 3 * A_BLOCK), 1) - A_BLOCK
    kpos = t * A_BLOCK + kj
    valid = (jnp.abs(kj - qi) <= A_WINDOW) & (kpos >= 0) & (kpos < DEC_SEQ)
    for g in range(A_KV_HEADS):
        sl = slice(g * HEAD_DIM, (g + 1) * HEAD_DIM)
        q = jnp.concatenate([q_ref[:, (g * A_GROUP + i) * HEAD_DIM:(g * A_GROUP + i + 1) * HEAD_DIM]
                             for i in range(A_GROUP)], axis=0) * SCALE
        sink = jnp.concatenate([jnp.full((A_BLOCK, 1), sink_ref[l, g * A_GROUP + i], F32)
                                for i in range(A_GROUP)], axis=0)
        k_loc = jnp.concatenate([kp_ref[:, sl], kc_ref[:, sl], kn_ref[:, sl]], axis=0).astype(BF16)
        v_loc = jnp.concatenate([vp_ref[:, sl], vc_ref[:, sl], vn_ref[:, sl]], axis=0).astype(BF16)
        s_loc = jnp.where(valid, _nt_dot(q, k_loc), NEG_INF)
        s_ctx = _nt_dot(q, ck_ref[:, sl].astype(BF16))
        o = _softmax_pv([(s_loc, v_loc), (s_ctx, cv_ref[:, sl].astype(BF16))], sink)
        for i in range(A_GROUP):
            h = g * A_GROUP + i
            o_ref[:, h * HEAD_DIM:(h + 1) * HEAD_DIM] = o[i * A_BLOCK:(i + 1) * A_BLOCK].astype(BF16)


def _win_attn(sink_a, qa, ka, va, cache_k, cache_v, l):
    nb = DEC_SEQ // A_BLOCK
    base = N_PROMPT // A_BLOCK

    def nbr(d):
        return lambda b, t: (base + b * nb + jnp.clip(t + d, 0, nb - 1), 0)

    kv = lambda d: pl.BlockSpec((A_BLOCK, A_KV), nbr(d))
    cache = pl.BlockSpec((None, None, PAST_LEN, A_KV), lambda b, t: (b, l, 0, 0))
    return pl.pallas_call(
        functools.partial(_win_attn_kernel, l=l), grid=(DEC_BATCH, nb),
        in_specs=[pl.BlockSpec(memory_space=pltpu.SMEM),
                  pl.BlockSpec((A_BLOCK, A_Q), nbr(0)),
                  kv(-1), kv(0), kv(1), kv(-1), kv(0), kv(1), cache, cache],
        out_specs=pl.BlockSpec((A_BLOCK, A_Q), lambda b, t: (b * nb + t, 0)),
        out_shape=jax.ShapeDtypeStruct((N_SAMPLE, A_Q), BF16),
        compiler_params=_cparams(("parallel", "parallel")), name="win_attn",
    )(sink_a, qa, ka, ka, ka, va, va, va, cache_k, cache_v)


def _nbr_attn_kernel(q_ref, kp_ref, kc_ref, kn_ref, vp_ref, vc_ref, vn_ref, ck_ref, cv_ref, bias_ref, o_ref):
    for h in range(C_HEADS):
        sl = slice(h * HEAD_DIM, (h + 1) * HEAD_DIM)
        q = q_ref[:, sl] * SCALE
        k_loc = jnp.concatenate([kp_ref[:, sl], kc_ref[:, sl], kn_ref[:, sl]], axis=0).astype(BF16)
        v_loc = jnp.concatenate([vp_ref[:, sl], vc_ref[:, sl], vn_ref[:, sl]], axis=0).astype(BF16)
        s_loc = _nt_dot(q, k_loc) + bias_ref[h]
        s_ctx = _nt_dot(q, ck_ref[:, sl].astype(BF16))
        o = _softmax_pv([(s_loc, v_loc), (s_ctx, cv_ref[:, sl].astype(BF16))], None)
        o_ref[:, sl] = o.astype(BF16)


def _nbr_attn(qc, kc, vc, cache_k, cache_v, bias, l):
    nb = DEC_SEQ // C_QBLOCK
    base = N_PROMPT // C_QBLOCK

    def nbr(d):
        return lambda j, b: (base + b * nb + jnp.clip(j + d, 0, nb - 1), 0)

    kv = lambda d: pl.BlockSpec((C_QBLOCK, C_W), nbr(d))
    cache = pl.BlockSpec((None, None, PAST_LEN, C_W), lambda j, b: (b, l, 0, 0))
    edge = lambda j, b: (l, jnp.where(j == 0, 0, jnp.where(j == nb - 1, 2, 1)), 0, 0, 0)
    return pl.pallas_call(
        _nbr_attn_kernel, grid=(nb, DEC_BATCH),
        in_specs=[kv(0), kv(-1), kv(0), kv(1), kv(-1), kv(0), kv(1), cache, cache,
                  pl.BlockSpec((None, None, C_HEADS, C_QBLOCK, 3 * C_QBLOCK), edge)],
        out_specs=pl.BlockSpec((C_QBLOCK, C_W), lambda j, b: (b * nb + j, 0)),
        out_shape=jax.ShapeDtypeStruct((N_SAMPLE, C_W), BF16),
        compiler_params=_cparams(("parallel", "parallel")), name="nbr_attn",
    )(qc, kc, kc, kc, vc, vc, vc, cache_k, cache_v, bias)


def _nbr_bias_tables(rpb):
    rows = DEC_SEQ // GRID_W
    qcol = np.arange(GRID_W)
    qcs = np.clip(qcol - C_WIN_COLS // 2, 0, GRID_W - C_WIN_COLS)
    kcol = np.arange(GRID_W)
    col_ok = (kcol[None, :] >= qcs[:, None]) & (kcol[None, :] < qcs[:, None] + C_WIN_COLS)
    dc = np.clip(kcol[None, :] - qcol[:, None], -(C_WIN_COLS - 1), C_WIN_COLS - 1) + C_WIN_COLS - 1
    onehot_dc = (dc.reshape(-1)[None, :] == np.arange(2 * C_WIN_COLS - 1)[:, None]).astype(np.float32)
    block_of_class = (0, 3, rows // C_QROWS - 1)
    dr = np.zeros((3, C_QROWS, 3, C_QROWS), np.int32)
    row_ok = np.zeros((3, C_QROWS, 3, C_QROWS), bool)
    for cls, j in enumerate(block_of_class):
        for qr in range(C_QROWS):
            r = C_QROWS * j + qr
            rs = min(max(r - C_WIN_ROWS // 2, 0), rows - C_WIN_ROWS)
            for kb in range(3):
                for kr in range(C_QROWS):
                    kabs = C_QROWS * (j - 1 + kb) + kr
                    ok = rs <= kabs < rs + C_WIN_ROWS
                    row_ok[cls, qr, kb, kr] = ok
                    dr[cls, qr, kb, kr] = min(max(kabs - r + C_WIN_ROWS - 1, 0), 2 * C_WIN_ROWS - 2)
    t = jnp.einsum('lhab,bx->lhax', rpb, jnp.asarray(onehot_dc), precision=lax.Precision.HIGHEST)
    t = jnp.take(t, jnp.asarray(dr.reshape(-1)), axis=2)
    t = t.reshape(DEPTH, C_HEADS, 3, C_QROWS, 3, C_QROWS, GRID_W, GRID_W)
    ok = row_ok[:, :, :, :, None, None] & col_ok[None, None, None, None, :, :]
    t = jnp.where(jnp.asarray(ok)[None, None], t, NEG_INF)
    t = jnp.transpose(t, (0, 2, 1, 3, 6, 4, 5, 7))
    return t.reshape(DEPTH, 3, C_HEADS, C_QBLOCK, 3 * C_QBLOCK)


def _fourier_kernel(u_ref, bc_ref, bs_ref, cl_ref, sl_ref, o_ref, zc_ref, zs_ref):
    @pl.when(pl.program_id(1) == 0)
    def _():
        u = u_ref[...]
        zc_ref[...] = jnp.dot(u, bc_ref[...].astype(BF16), preferred_element_type=F32).astype(BF16)
        zs_ref[...] = jnp.dot(u, bs_ref[...].astype(BF16), preferred_element_type=F32).astype(BF16)

    o = (jnp.dot(cl_ref[...].astype(BF16), zc_ref[...], preferred_element_type=F32)
         - jnp.dot(sl_ref[...].astype(BF16), zs_ref[...], preferred_element_type=F32))
    o_ref[...] = o.astype(BF16)


def _dft_tables(n):
    k = np.arange(n)
    ang = 2.0 * np.pi * ((k[:, None] * k[None, :]) % n) / n
    return np.cos(ang) / np.sqrt(n), np.sin(ang) / np.sqrt(n)


def _channel_dft_tables():
    c, s = _dft_tables(B_GROUP_DIM)
    eye = np.eye(B_GROUPS)
    return np.kron(eye, c).astype(np.float32), np.kron(eye, s).astype(np.float32)


def _fourier(fb, n_batch, seq, row0, tr):
    cl, sl = (jnp.asarray(a.astype(np.float32)) for a in _dft_tables(seq))
    bc, bs = (jnp.asarray(a) for a in _channel_dft_tables())
    nt = seq // tr
    const = pl.BlockSpec((B_WIDTH, B_WIDTH), lambda b, t: (0, 0))
    return pl.pallas_call(
        _fourier_kernel, grid=(n_batch, nt),
        in_specs=[pl.BlockSpec((seq, B_WIDTH), lambda b, t: (row0 // seq + b, 0)), const, const,
                  pl.BlockSpec((tr, seq), lambda b, t: (t, 0)), pl.BlockSpec((tr, seq), lambda b, t: (t, 0))],
        out_specs=pl.BlockSpec((tr, B_WIDTH), lambda b, t: (b * nt + t, 0)),
        out_shape=jax.ShapeDtypeStruct((n_batch * seq, B_WIDTH), BF16),
        scratch_shapes=[pltpu.VMEM((seq, B_WIDTH), BF16), pltpu.VMEM((seq, B_WIDTH), BF16)],
        compiler_params=_cparams(("parallel", "arbitrary")), name=f"fourier_{seq}",
    )(fb, bc, bs, cl, sl)


def _merge_kernel(oa_ref, ob_ref, oc_ref, gt_ref, x_ref, wa_ref, wb_ref, wc_ref, wo_ref, g1_ref, o_ref):
    ya = jnp.dot(oa_ref[...], wa_ref[...].astype(BF16), preferred_element_type=F32)
    yb = jnp.dot(ob_ref[...], wb_ref[...].astype(BF16), preferred_element_type=F32)
    yc = jnp.dot(oc_ref[...], wc_ref[...].astype(BF16), preferred_element_type=F32)
    d = D_MODEL
    m = (gt_ref[:, 0:d].astype(F32) * ya + gt_ref[:, d:2 * d].astype(F32) * yb
         + gt_ref[:, 2 * d:3 * d].astype(F32) * yc)
    y = jnp.dot(m.astype(BF16), wo_ref[...].astype(BF16), preferred_element_type=F32)
    o_ref[...] = x_ref[...] + g1_ref[...] * y


def _merge(oa, ob, oc, gates, x, wa, wb, wc, wo, mod, l):
    tm = 512
    row = lambda w: pl.BlockSpec((tm, w), lambda i: (i, 0))
    const = lambda r, c: pl.BlockSpec((r, c), lambda i: (0, 0))
    return pl.pallas_call(
        _merge_kernel, grid=(N_TOK // tm,),
        in_specs=[row(A_Q), row(B_WIDTH), row(C_W), row(GATE_WIDTH), row(D_MODEL),
                  const(A_Q, D_MODEL), const(B_WIDTH, D_MODEL), const(C_W, D_MODEL), const(D_MODEL, D_MODEL),
                  pl.BlockSpec((None, None, 1, D_MODEL), lambda i: (l, _cond_row(i, tm), 0, 2))],
        out_specs=row(D_MODEL),
        out_shape=jax.ShapeDtypeStruct((N_TOK, D_MODEL), F32),
        compiler_params=_cparams(("parallel",)), name="merge",
    )(oa, ob, oc, gates, x, wa, wb, wc, wo, mod)


def _ffn_kernel(h_ref, x_ref, wg_ref, wu_ref, wd_ref, g2_ref, o_ref, acc_ref):
    f = pl.program_id(1)

    @pl.when(f == 0)
    def _():
        acc_ref[...] = jnp.zeros_like(acc_ref)

    h = h_ref[...]
    g = jnp.dot(h, wg_ref[...].astype(BF16), preferred_element_type=F32)
    u = jnp.dot(h, wu_ref[...].astype(BF16), preferred_element_type=F32)
    a = (g * _sigmoid(g) * u).astype(BF16)
    acc_ref[...] += jnp.dot(a, wd_ref[...].astype(BF16), preferred_element_type=F32)

    @pl.when(f == pl.num_programs(1) - 1)
    def _():
        o_ref[...] = x_ref[...] + g2_ref[...] * acc_ref[...]


def _ffn(h, x, wg, wu, wd, mod, l):
    tm, tf = 1024, 256
    return pl.pallas_call(
        _ffn_kernel, grid=(N_TOK // tm, D_FF // tf),
        in_specs=[pl.BlockSpec((tm, D_MODEL), lambda i, f: (i, 0)),
                  pl.BlockSpec((tm, D_MODEL), lambda i, f: (i, 0)),
                  pl.BlockSpec((D_MODEL, tf), lambda i, f: (0, f)),
                  pl.BlockSpec((D_MODEL, tf), lambda i, f: (0, f)),
                  pl.BlockSpec((tf, D_MODEL), lambda i, f: (f, 0)),
                  pl.BlockSpec((None, None, 1, D_MODEL), lambda i, f: (l, _cond_row(i, tm), 0, 5))],
        out_specs=pl.BlockSpec((tm, D_MODEL), lambda i, f: (i, 0)),
        out_shape=jax.ShapeDtypeStruct((N_TOK, D_MODEL), F32),
        scratch_shapes=[pltpu.VMEM((tm, D_MODEL), F32)],
        compiler_params=_cparams(("parallel", "arbitrary")), name="ffn",
    )(h, x, wg, wu, wd, mod)


TOP_K = 2
MOE_TILE = 256
MOE_TILES = TOP_K * N_TOK // MOE_TILE + N_EXPERTS
MOE_ROWS = MOE_TILES * MOE_TILE
MOE_WINDOW = 8
MOE_TF = 896
ROUTE_TM = 512
DISPATCH_CHUNK = 256
COMBINE_TM = 256


def _route_kernel(lg_ref, o_ref, cnt_ref, base_ref, tri_ref):
    tm = lg_ref.shape[0]

    @pl.when(pl.program_id(0) == 0)
    def _():
        base_ref[...] = jnp.zeros_like(base_ref)
        r = lax.broadcasted_iota(jnp.int32, (tm, tm), 0)
        c = lax.broadcasted_iota(jnp.int32, (tm, tm), 1)
        tri_ref[...] = jnp.where(r > c, 1.0, 0.0).astype(BF16)

    lane = lax.broadcasted_iota(jnp.int32, lg_ref.shape, 1).astype(F32)
    lg = jnp.where(lane < N_EXPERTS, lg_ref[...], -jnp.inf)
    m1 = lg.max(axis=-1, keepdims=True)
    i1 = jnp.where(lg == m1, lane, float(LANES)).min(axis=-1, keepdims=True)
    rest = jnp.where(lane == i1, -jnp.inf, lg)
    m2 = rest.max(axis=-1, keepdims=True)
    i2 = jnp.where(rest == m2, lane, float(LANES)).min(axis=-1, keepdims=True)
    e2 = jnp.exp(m2 - m1)
    w1 = 1.0 / (1.0 + e2)
    w2 = e2 / (1.0 + e2)

    oh1 = jnp.where(lane == i1, 1.0, 0.0)
    oh2 = jnp.where(lane == i2, 1.0, 0.0)
    pre1 = jnp.dot(tri_ref[...], oh1.astype(BF16), preferred_element_type=F32)
    pre2 = jnp.dot(tri_ref[...], oh2.astype(BF16), preferred_element_type=F32)
    c1 = jnp.sum(oh1, axis=0, keepdims=True)
    c2 = jnp.sum(oh2, axis=0, keepdims=True)
    base = base_ref[...]
    rank1 = jnp.sum(oh1 * (base + pre1), axis=-1, keepdims=True)
    rank2 = jnp.sum(oh2 * (base + c1 + pre2), axis=-1, keepdims=True)
    base_ref[...] = base + c1 + c2

    cols = (i1, i2, rank1, rank2, w1, w2)
    out = jnp.zeros(lg_ref.shape, F32)
    for j, col in enumerate(cols):
        out = jnp.where(lane == float(j), col, out)
    o_ref[...] = out
    cnt_ref[...] = jnp.broadcast_to(base + c1 + c2, cnt_ref.shape)


def _dispatch_kernel(pos_ref, last_ref, nt_ref, h_hbm, xs_hbm, zero_ref, zsem, sem):
    zero_ref[...] = jnp.zeros_like(zero_ref)

    def zero_copy(tile):
        row0 = pl.multiple_of(tile * MOE_TILE, MOE_TILE)
        return pltpu.make_async_copy(zero_ref, xs_hbm.at[pl.ds(row0, MOE_TILE)], zsem)

    def for_zeroed_tiles(fn):
        for e in range(N_EXPERTS):
            @pl.when(last_ref[e] >= 0)
            def _():
                fn(zero_copy(last_ref[e]))

            tail = MOE_TILES - 1 - e

            @pl.when(tail >= nt_ref[0])
            def _():
                fn(zero_copy(tail))

    for_zeroed_tiles(lambda cp: cp.start())
    for_zeroed_tiles(lambda cp: cp.wait())

    def row_copy(t, k):
        return pltpu.make_async_copy(h_hbm.at[pl.ds(t, 1)], xs_hbm.at[pl.ds(pos_ref[TOP_K * t + k], 1)], sem)

    def issue(c):
        def body(t, carry):
            for k in range(TOP_K):
                row_copy(t, k).start()
            return carry
        lax.fori_loop(c * DISPATCH_CHUNK, (c + 1) * DISPATCH_CHUNK, body, 0)

    def drain(c):
        def body(t, carry):
            for k in range(TOP_K):
                row_copy(t, k).wait()
            return carry
        lax.fori_loop(c * DISPATCH_CHUNK, (c + 1) * DISPATCH_CHUNK, body, 0)

    issue(0)

    def chunk(c, carry):
        issue(c)
        drain(c - 1)
        return carry

    n_chunks = N_TOK // DISPATCH_CHUNK
    lax.fori_loop(1, n_chunks, chunk, 0)
    drain(n_chunks - 1)


def _expert_kernel(te_ref, nt_ref, x_ref, wg_ref, wu_ref, wd_ref, y_hbm, acc_ref, wgb_ref, wub_ref, wdb_ref, osem):
    f = pl.program_id(1)
    r = pl.program_id(2)
    last_f = pl.num_programs(1) - 1
    t = pl.program_id(0) * MOE_WINDOW + r
    valid = t < nt_ref[0]
    tc = jnp.minimum(t, nt_ref[0] - 1)
    same_weights = (r > 0) & (te_ref[tc] == te_ref[jnp.maximum(tc - 1, 0)])

    @pl.when(valid & jnp.logical_not(same_weights))
    def _():
        wgb_ref[...] = wg_ref[...].astype(BF16)
        wub_ref[...] = wu_ref[...].astype(BF16)
        wdb_ref[...] = wd_ref[...].astype(BF16)

    @pl.when(valid)
    def _():
        x = x_ref[...].astype(BF16)
        g = jnp.dot(x, wgb_ref[...], preferred_element_type=F32)
        u = jnp.dot(x, wub_ref[...], preferred_element_type=F32)
        a = (g * _sigmoid(g) * u).astype(BF16)
        d = jnp.dot(a, wdb_ref[...], preferred_element_type=F32)

        @pl.when(f == 0)
        def _():
            acc_ref[r] = d

        @pl.when(f > 0)
        def _():
            acc_ref[r] += d

    def out_copy(slot, tile):
        row0 = pl.multiple_of(tile * MOE_TILE, MOE_TILE)
        return pltpu.make_async_copy(acc_ref.at[slot], y_hbm.at[pl.ds(row0, MOE_TILE)], osem)

    @pl.when(f == last_f)
    def _():
        @pl.when(jnp.logical_not(valid))
        def _():
            acc_ref[r] = jnp.zeros((MOE_TILE, D_MODEL), F32)

        @pl.when(r > 0)
        def _():
            out_copy(r - 1, t - 1).wait()

        out_copy(r, t).start()

        @pl.when(r == MOE_WINDOW - 1)
        def _():
            out_copy(r, t).wait()


def _combine_kernel(pos_ref, route_ref, x_ref, g2_ref, y_hbm, o_ref, buf_ref, sem):
    i = pl.program_id(0)
    n = pl.num_programs(0)
    tm = x_ref.shape[0]
    slot = i % 2

    def row_copy(step, s, t, k):
        src = y_hbm.at[pl.ds(pos_ref[TOP_K * (step * tm + t) + k], 1)]
        return pltpu.make_async_copy(src, buf_ref.at[s, k, pl.ds(t, 1)], sem.at[s])

    def issue(step, s):
        def body(t, carry):
            for k in range(TOP_K):
                row_copy(step, s, t, k).start()
            return carry
        lax.fori_loop(0, tm, body, 0)

    @pl.when(i == 0)
    def _():
        issue(0, 0)

    @pl.when(i + 1 < n)
    def _():
        issue(i + 1, 1 - slot)

    def wait_body(t, carry):
        for k in range(TOP_K):
            row_copy(i, slot, t, k).wait()
        return carry

    lax.fori_loop(0, tm, wait_body, 0)

    lane = lax.broadcasted_iota(jnp.int32, route_ref.shape, 1)
    rt = route_ref[...]
    w1 = jnp.sum(jnp.where(lane == 2 * TOP_K, rt, 0.0), axis=-1, keepdims=True)
    w2 = jnp.sum(jnp.where(lane == 2 * TOP_K + 1, rt, 0.0), axis=-1, keepdims=True)
    o_ref[...] = x_ref[...] + g2_ref[...] * (w1 * buf_ref[slot, 0] + w2 * buf_ref[slot, 1])


def _moe(h, x, logits, wg, wu, wd, mod, l):
    route, cnt = pl.pallas_call(
        _route_kernel, grid=(N_TOK // ROUTE_TM,),
        in_specs=[pl.BlockSpec((ROUTE_TM, LANES), lambda i: (i, 0))],
        out_specs=[pl.BlockSpec((ROUTE_TM, LANES), lambda i: (i, 0)), pl.BlockSpec((8, LANES), lambda i: (0, 0))],
        out_shape=[jax.ShapeDtypeStruct((N_TOK, LANES), F32), jax.ShapeDtypeStruct((8, LANES), F32)],
        scratch_shapes=[pltpu.VMEM((1, LANES), F32), pltpu.VMEM((ROUTE_TM, ROUTE_TM), BF16)],
        compiler_params=_cparams(("arbitrary",)), name="route",
    )(logits)

    expert = route[:, 0:TOP_K].astype(jnp.int32)
    rank = route[:, TOP_K:2 * TOP_K].astype(jnp.int32)
    n_sub = (cnt[0, :N_EXPERTS].astype(jnp.int32) + MOE_TILE - 1) // MOE_TILE
    end = jnp.cumsum(n_sub)
    start = end - n_sub
    start_of = jnp.sum(jnp.where(expert[:, :, None] == jnp.arange(N_EXPERTS), start, 0), axis=-1)
    pos = (start_of * MOE_TILE + rank).reshape(-1)
    tile_expert = jnp.minimum(jnp.sum(jnp.arange(MOE_TILES)[:, None] >= end[None, :], axis=1), N_EXPERTS - 1)
    tile_expert = tile_expert.astype(jnp.int32)
    n_tiles = end[N_EXPERTS - 1:]
    last_tile = jnp.where(n_sub > 0, end - 1, -1).astype(jnp.int32)

    xs = pl.pallas_call(
        _dispatch_kernel,
        grid_spec=pltpu.PrefetchScalarGridSpec(
            num_scalar_prefetch=3, grid=(1,),
            in_specs=[pl.BlockSpec(memory_space=pl.ANY)],
            out_specs=pl.BlockSpec(memory_space=pl.ANY),
            scratch_shapes=[pltpu.VMEM((MOE_TILE, D_MODEL), F32), pltpu.SemaphoreType.DMA(()),
                            pltpu.SemaphoreType.DMA(())]),
        out_shape=jax.ShapeDtypeStruct((MOE_ROWS, D_MODEL), F32),
        compiler_params=_cparams(("arbitrary",)), name="moe_dispatch",
    )(pos, last_tile, n_tiles, h)

    nf = D_FF_EXPERT // MOE_TF

    def tile(w, r, nt):
        return jnp.minimum(w * MOE_WINDOW + r, nt[0] - 1)

    y = pl.pallas_call(
        _expert_kernel,
        grid_spec=pltpu.PrefetchScalarGridSpec(
            num_scalar_prefetch=2, grid=(MOE_TILES // MOE_WINDOW, nf, MOE_WINDOW),
            in_specs=[pl.BlockSpec((MOE_TILE, D_MODEL), lambda w, f, r, te, nt: (tile(w, r, nt), 0)),
                      pl.BlockSpec((None, D_MODEL, MOE_TF), lambda w, f, r, te, nt: (te[tile(w, r, nt)], 0, f)),
                      pl.BlockSpec((None, D_MODEL, MOE_TF), lambda w, f, r, te, nt: (te[tile(w, r, nt)], 0, f)),
                      pl.BlockSpec((None, MOE_TF, D_MODEL), lambda w, f, r, te, nt: (te[tile(w, r, nt)], f, 0))],
            out_specs=pl.BlockSpec(memory_space=pl.ANY),
            scratch_shapes=[pltpu.VMEM((MOE_WINDOW, MOE_TILE, D_MODEL), F32),
                            pltpu.VMEM((D_MODEL, MOE_TF), BF16), pltpu.VMEM((D_MODEL, MOE_TF), BF16),
                            pltpu.VMEM((MOE_TF, D_MODEL), BF16), pltpu.SemaphoreType.DMA(())]),
        out_shape=jax.ShapeDtypeStruct((MOE_ROWS, D_MODEL), F32),
        compiler_params=_cparams(("arbitrary", "arbitrary", "arbitrary")), name="moe_experts",
    )(tile_expert, n_tiles, xs, wg, wu, wd)

    tm = COMBINE_TM
    return pl.pallas_call(
        _combine_kernel,
        grid_spec=pltpu.PrefetchScalarGridSpec(
            num_scalar_prefetch=1, grid=(N_TOK // tm,),
            in_specs=[pl.BlockSpec((tm, LANES), lambda i, p: (i, 0)),
                      pl.BlockSpec((tm, D_MODEL), lambda i, p: (i, 0)),
                      pl.BlockSpec((None, None, 1, D_MODEL), lambda i, p: (l, _cond_row(i, tm), 0, 5)),
                      pl.BlockSpec(memory_space=pl.ANY)],
            out_specs=pl.BlockSpec((tm, D_MODEL), lambda i, p: (i, 0)),
            scratch_shapes=[pltpu.VMEM((2, TOP_K, tm, D_MODEL), F32), pltpu.SemaphoreType.DMA((2,))]),
        out_shape=jax.ShapeDtypeStruct((N_TOK, D_MODEL), F32),
        compiler_params=_cparams(("arbitrary",)), name="moe_combine",
    )(pos, route, x, mod, y)


def _rope_tables():
    t = jnp.arange(DEC_SEQ)
    row = (t // GRID_W).astype(F32)
    col = (t % GRID_W).astype(F32)
    n_freq = HEAD_DIM // 4
    inv = ROPE_BASE ** (-jnp.arange(n_freq, dtype=F32) / n_freq)
    ang = jnp.concatenate([row[:, None] * inv, col[:, None] * inv], axis=-1)
    cos, sin = jnp.cos(ang), jnp.sin(ang)
    cos_h = jnp.concatenate([cos, cos], axis=-1)
    sin_h = jnp.concatenate([-sin, sin], axis=-1)
    cos_l = jnp.tile(jnp.concatenate([cos_h, cos_h], axis=-1), (DEC_BATCH, 1))
    sin_l = jnp.tile(jnp.concatenate([sin_h, sin_h], axis=-1), (DEC_BATCH, 1))
    cos_t = jnp.concatenate([jnp.ones((N_PROMPT, LANES), F32), cos_l], axis=0)
    sin_t = jnp.concatenate([jnp.zeros((N_PROMPT, LANES), F32), sin_l], axis=0)
    return cos_t, sin_t


def kernel(x_prompt, x_sample, cache_a_k, cache_a_v, cache_c_k, cache_c_v, c, c_ctx, w_mod, b_mod, norm1_g, norm2_g, w_in, qk_norm_a, qk_norm_c, sink_a, rpb_c, w_branch_a, w_branch_b, w_branch_c, w_out, w_ff_gate, w_ff_up, w_ff_down, w_router, w_exp_gate, w_exp_up, w_exp_down):
    x = jnp.concatenate([x_prompt.reshape(N_PROMPT, D_MODEL), x_sample.reshape(N_SAMPLE, D_MODEL)], axis=0)
    cond = jnp.concatenate([c_ctx[None, :], c], axis=0)
    cond_t = jnp.broadcast_to(cond[:, :, None], (N_COND, D_MODEL, LANES))
    mod = _modulation(cond_t, w_mod, b_mod)
    cos_t, sin_t = _rope_tables()
    bias = _nbr_bias_tables(rpb_c)
    ck_a = cache_a_k.reshape(DEC_BATCH, DEPTH, PAST_LEN, A_KV)
    cv_a = cache_a_v.reshape(DEC_BATCH, DEPTH, PAST_LEN, A_KV)
    ck_c = cache_c_k.reshape(DEC_BATCH, DEPTH, PAST_LEN, C_W)
    cv_c = cache_c_v.reshape(DEC_BATCH, DEPTH, PAST_LEN, C_W)

    new_ak, new_av, new_ck, new_cv = [], [], [], []
    for l in range(DEPTH):
        h = _adaln(x, norm1_g[l], mod, l, 0)
        p = _proj(h, w_in[l], QKV_WIDTH, QKV_WIDTH // 2, F32, gate=False)
        gates = _proj(h, w_in[l][:, QKV_WIDTH:], GATE_WIDTH, D_MODEL, BF16, gate=True)
        qa, ka, va, fb, qc, kc, vc = _qk_post(p, cos_t, sin_t, qk_norm_a[l], qk_norm_c[l])
        oa_p, oc_p = _ctx_attn(sink_a, qa, ka, va, qc, kc, vc, l)
        oa_s = _win_attn(sink_a, qa, ka, va, ck_a, cv_a, l)
        oc_s = _nbr_attn(qc, kc, vc, ck_c, cv_c, bias, l)
        ob_p = _fourier(fb, BATCH, SEQ, 0, SEQ)
        ob_s = _fourier(fb, DEC_BATCH, DEC_SEQ, N_PROMPT, 512)
        oa = jnp.concatenate([oa_p, oa_s], axis=0)
        ob = jnp.concatenate([ob_p, ob_s], axis=0)
        oc = jnp.concatenate([oc_p, oc_s], axis=0)
        x = _merge(oa, ob, oc, gates, x, w_branch_a[l], w_branch_b[l], w_branch_c[l], w_out[l], mod, l)
        i = l // 2
        if l % 2 == 0:
            h2 = _adaln(x, norm2_g[l], mod, l, 3)
            x = _ffn(h2, x, w_ff_gate[i], w_ff_up[i], w_ff_down[i], mod, l)
        else:
            h2, logits = _adaln(x, norm2_g[l], mod, l, 3, w_router=w_router[i])
            x = _moe(h2, x, logits, w_exp_gate[i], w_exp_up[i], w_exp_down[i], mod, l)
        new_ak.append(ka[:N_PROMPT].reshape(BATCH, SEQ, A_KV_HEADS, HEAD_DIM))
        new_av.append(va[:N_PROMPT].reshape(BATCH, SEQ, A_KV_HEADS, HEAD_DIM))
        new_ck.append(kc[:N_PROMPT].reshape(BATCH, SEQ, C_HEADS, HEAD_DIM))
        new_cv.append(vc[:N_PROMPT].reshape(BATCH, SEQ, C_HEADS, HEAD_DIM))

    xp = x[:N_PROMPT].reshape(BATCH, SEQ, D_MODEL)
    xs = x[N_PROMPT:].reshape(DEC_BATCH, DEC_SEQ, D_MODEL)
    return (xp, xs, jnp.stack(new_ak, axis=1), jnp.stack(new_av, axis=1),
            jnp.stack(new_ck, axis=1), jnp.stack(new_cv, axis=1))
```

```python
import functools

import numpy as np
import jax
import jax.numpy as jnp
from jax import lax
from jax.experimental import pallas as pl
from jax.experimental.pallas import tpu as pltpu

F32 = jnp.float32
BF16 = jnp.bfloat16

D_MODEL = 1024
BATCH = 16
SEQ = 256
DEPTH = 2
DEC_BATCH = 2
DEC_SEQ = 2048
PAST_LEN = 512
GRID_W = 64
HEAD_DIM = 64
SCALE = HEAD_DIM ** -0.5
A_HEADS = 8
A_KV_HEADS = 2
A_GROUP = A_HEADS // A_KV_HEADS
A_WINDOW = 128
A_BLOCK = 128
B_GROUPS = 8
B_GROUP_DIM = 64
B_WIDTH = B_GROUPS * B_GROUP_DIM
C_HEADS = 8
C_WIN_ROWS = 8
C_WIN_COLS = 16
A_Q = A_HEADS * HEAD_DIM
A_KV = A_KV_HEADS * HEAD_DIM
C_W = C_HEADS * HEAD_DIM
QKV_WIDTH = A_Q + 2 * A_KV + B_WIDTH + 3 * C_W
GATE_WIDTH = 3 * D_MODEL
D_FF = 2816
N_EXPERTS = 8
D_FF_EXPERT = 3584
ROPE_BASE = 10000.0
RMS_EPS = 1e-6
NEG_INF = -1e30

N_PROMPT = BATCH * SEQ
N_SAMPLE = DEC_BATCH * DEC_SEQ
N_TOK = N_PROMPT + N_SAMPLE
N_COND = 1 + DEC_BATCH
LANES = 128
C_QROWS = 4
C_QBLOCK = C_QROWS * GRID_W
C_DR_SLOTS = 2 * C_WIN_ROWS
VMEM_LIMIT = 56 * 1024 * 1024


def _cparams(sem):
    return pltpu.CompilerParams(dimension_semantics=sem, vmem_limit_bytes=VMEM_LIMIT)


def _sigmoid(x):
    return 1.0 / (1.0 + jnp.exp(-x))


def _cond_row(tile, tm):
    return jnp.maximum(tile * tm // DEC_SEQ - 1, 0)


def _mod_kernel(ct_ref, w_ref, b_ref, o_ref):
    tn = w_ref.shape[1]
    for r in range(N_COND):
        cb = ct_ref[r]
        s = cb * _sigmoid(cb)
        for cc in range(tn // LANES):
            sl = slice(cc * LANES, (cc + 1) * LANES)
            o_ref[r, :, sl] = jnp.sum(w_ref[:, sl] * s, axis=0, keepdims=True) + b_ref[:, sl]


def _modulation(cond_t, w_mod, b_mod):
    tn = 512
    n = 6 * D_MODEL
    return pl.pallas_call(
        _mod_kernel,
        grid=(DEPTH, n // tn),
        in_specs=[
            pl.BlockSpec((N_COND, D_MODEL, LANES), lambda l, j: (0, 0, 0)),
            pl.BlockSpec((None, D_MODEL, tn), lambda l, j: (l, 0, j)),
            pl.BlockSpec((None, 1, tn), lambda l, j: (l, 0, j)),
        ],
        out_specs=pl.BlockSpec((None, N_COND, 1, tn), lambda l, j: (l, 0, 0, j)),
        out_shape=jax.ShapeDtypeStruct((DEPTH, N_COND, 1, n), F32),
        compiler_params=_cparams(("parallel", "parallel")),
        name="modulation",
    )(cond_t, w_mod, b_mod.reshape(DEPTH, 1, n))


def _adaln_math(x, g, sh, sc):
    ms = jnp.mean(x * x, axis=-1, keepdims=True)
    y = x * lax.rsqrt(ms + RMS_EPS) * g
    return y * (1.0 + sc) + sh


def _adaln_kernel(x_ref, g_ref, sh_ref, sc_ref, o_ref):
    o_ref[...] = _adaln_math(x_ref[...], g_ref[...], sh_ref[...], sc_ref[...]).astype(BF16)


def _adaln_router_kernel(x_ref, g_ref, sh_ref, sc_ref, wr_ref, o_ref, lg_ref):
    h = _adaln_math(x_ref[...], g_ref[...], sh_ref[...], sc_ref[...])
    o_ref[...] = h
    lg_ref[...] = jnp.dot(h, wr_ref[...], precision=lax.Precision.HIGHEST, preferred_element_type=F32)


def _adaln(x, g, mod, l, shift_idx, w_router=None):
    tm = 512
    mspec = lambda which: pl.BlockSpec((None, None, 1, D_MODEL),
                                       lambda i: (l, _cond_row(i, tm), 0, which))
    in_specs = [pl.BlockSpec((tm, D_MODEL), lambda i: (i, 0)),
                pl.BlockSpec((1, D_MODEL), lambda i: (0, 0)),
                mspec(shift_idx), mspec(shift_idx + 1)]
    h_spec = pl.BlockSpec((tm, D_MODEL), lambda i: (i, 0))
    h_shape = jax.ShapeDtypeStruct((N_TOK, D_MODEL), BF16)
    if w_router is None:
        return pl.pallas_call(
            _adaln_kernel, grid=(N_TOK // tm,), in_specs=in_specs, out_specs=h_spec, out_shape=h_shape,
            compiler_params=_cparams(("parallel",)), name="adaln",
        )(x, g.reshape(1, D_MODEL), mod, mod)
    wr = jnp.pad(w_router, ((0, 0), (0, LANES - N_EXPERTS)))
    return pl.pallas_call(
        _adaln_router_kernel, grid=(N_TOK // tm,),
        in_specs=in_specs + [pl.BlockSpec((D_MODEL, LANES), lambda i: (0, 0))],
        out_specs=[h_spec, pl.BlockSpec((tm, LANES), lambda i: (i, 0))],
        out_shape=[jax.ShapeDtypeStruct((N_TOK, D_MODEL), F32), jax.ShapeDtypeStruct((N_TOK, LANES), F32)],
        compiler_params=_cparams(("parallel",)), name="adaln_router",
    )(x, g.reshape(1, D_MODEL), mod, mod, wr)


def _proj_kernel(a_ref, w_ref, o_ref, *, gate):
    acc = jnp.dot(a_ref[...], w_ref[0].astype(BF16), preferred_element_type=F32)
    if gate:
        acc = _sigmoid(acc)
    o_ref[...] = acc.astype(o_ref.dtype)


def _proj(h, w_in, l, col0, n_out, tn, out_dtype, gate):
    tm = 1024
    return pl.pallas_call(
        functools.partial(_proj_kernel, gate=gate),
        grid=(N_TOK // tm, n_out // tn),
        in_specs=[pl.BlockSpec((tm, D_MODEL), lambda i, j: (i, 0)),
                  pl.BlockSpec((pl.Element(1), pl.Element(D_MODEL), pl.Element(tn)),
                               lambda i, j: (l, 0, pl.multiple_of(col0 + j * tn, LANES)))],
        out_specs=pl.BlockSpec((tm, tn), lambda i, j: (i, j)),
        out_shape=jax.ShapeDtypeStruct((N_TOK, n_out), out_dtype),
        compiler_params=_cparams(("parallel", "parallel")),
        name="proj_gate" if gate else "proj_qkv",
    )(h, w_in)


def _head_norm(x, gain, bd):
    sq = x * x
    hi = sq.astype(BF16)
    lo = (sq - hi.astype(F32)).astype(BF16)
    ms = jnp.dot(hi, bd, preferred_element_type=F32) + jnp.dot(lo, bd, preferred_element_type=F32)
    return x * lax.rsqrt(ms + RMS_EPS) * gain


def _rope(x, cos, sin_signed, first_half):
    swapped = jnp.where(first_half, pltpu.roll(x, LANES - HEAD_DIM // 2, 1), pltpu.roll(x, HEAD_DIM // 2, 1))
    return x * cos + swapped * sin_signed


def _qk_post_kernel(p_ref, cos_ref, sin_ref, ga_ref, gc_ref,
                    qa_ref, ka_ref, va_ref, fb_ref, qc_ref, kc_ref, vc_ref):
    r = lax.broadcasted_iota(jnp.int32, (LANES, LANES), 0) // HEAD_DIM
    c = lax.broadcasted_iota(jnp.int32, (LANES, LANES), 1) // HEAD_DIM
    bd = jnp.where(r == c, 1.0 / HEAD_DIM, 0.0).astype(BF16)
    lane = lax.broadcasted_iota(jnp.int32, (1, LANES), 1)
    first_half = (lane % HEAD_DIM) < HEAD_DIM // 2
    cos = cos_ref[...]
    sin = sin_ref[...]
    gqa, gka = ga_ref[0:1, :], ga_ref[1:2, :]
    gqc, gkc = gc_ref[0:1, :], gc_ref[1:2, :]
    for s in range(A_Q // LANES):
        sl = slice(s * LANES, (s + 1) * LANES)
        qa_ref[:, sl] = _rope(_head_norm(p_ref[:, sl], gqa, bd), cos, sin, first_half).astype(BF16)
    off = A_Q
    ka_ref[...] = _rope(_head_norm(p_ref[:, off:off + A_KV], gka, bd), cos, sin, first_half)
    off += A_KV
    va_ref[...] = p_ref[:, off:off + A_KV]
    off += A_KV
    fb_ref[...] = p_ref[:, off:off + B_WIDTH].astype(BF16)
    off += B_WIDTH
    for s in range(C_W // LANES):
        sl = slice(s * LANES, (s + 1) * LANES)
        qc_ref[:, sl] = _head_norm(p_ref[:, off + s * LANES:off + (s + 1) * LANES], gqc, bd).astype(BF16)
    off += C_W
    for s in range(C_W // LANES):
        sl = slice(s * LANES, (s + 1) * LANES)
        kc_ref[:, sl] = _head_norm(p_ref[:, off + s * LANES:off + (s + 1) * LANES], gkc, bd)
    off += C_W
    vc_ref[...] = p_ref[:, off:off + C_W]


def _qk_post(p, cos_t, sin_t, qk_a, qk_c):
    tm = 512
    row = lambda w: pl.BlockSpec((tm, w), lambda i: (i, 0))
    widths = (A_Q, A_KV, A_KV, B_WIDTH, C_W, C_W, C_W)
    dtypes = (BF16, F32, F32, BF16, BF16, F32, F32)
    return pl.pallas_call(
        _qk_post_kernel, grid=(N_TOK // tm,),
        in_specs=[row(QKV_WIDTH), row(LANES), row(LANES),
                  pl.BlockSpec((2, LANES), lambda i: (0, 0)), pl.BlockSpec((2, LANES), lambda i: (0, 0))],
        out_specs=[row(w) for w in widths],
        out_shape=[jax.ShapeDtypeStruct((N_TOK, w), d) for w, d in zip(widths, dtypes)],
        compiler_params=_cparams(("parallel",)), name="qk_post",
    )(p, cos_t, sin_t, jnp.tile(qk_a, (1, 2)), jnp.tile(qk_c, (1, 2)))


def _nt_dot(a, b):
    return lax.dot_general(a, b, (((1,), (1,)), ((), ())), preferred_element_type=F32)


def _softmax_pv(parts, sink):
    m = parts[0][0].max(axis=-1, keepdims=True)
    for s, _ in parts[1:]:
        m = jnp.maximum(m, s.max(axis=-1, keepdims=True))
    if sink is not None:
        m = jnp.maximum(m, sink)
    den = jnp.exp(sink - m) if sink is not None else 0.0
    o = None
    for s, v in parts:
        e = jnp.exp(s - m)
        den = den + e.sum(axis=-1, keepdims=True)
        pv = jnp.dot(e.astype(BF16), v, preferred_element_type=F32)
        o = pv if o is None else o + pv
    return o / den


def _ctx_attn_kernel(sink_ref, qa_ref, ka_ref, va_ref, qc_ref, kc_ref, vc_ref, oa_ref, oc_ref, *, l):
    for h in range(A_HEADS):
        g = h // A_GROUP
        q = qa_ref[:, h * HEAD_DIM:(h + 1) * HEAD_DIM] * SCALE
        k = ka_ref[:, g * HEAD_DIM:(g + 1) * HEAD_DIM].astype(BF16)
        v = va_ref[:, g * HEAD_DIM:(g + 1) * HEAD_DIM].astype(BF16)
        o = _softmax_pv([(_nt_dot(q, k), v)], sink_ref[l, h])
        oa_ref[:, h * HEAD_DIM:(h + 1) * HEAD_DIM] = o.astype(BF16)
    for h in range(C_HEADS):
        sl = slice(h * HEAD_DIM, (h + 1) * HEAD_DIM)
        o = _softmax_pv([(_nt_dot(qc_ref[:, sl] * SCALE, kc_ref[:, sl].astype(BF16)), vc_ref[:, sl].astype(BF16))], None)
        oc_ref[:, sl] = o.astype(BF16)


def _ctx_attn(sink_a, qa, ka, va, qc, kc, vc, l):
    blk = lambda w: pl.BlockSpec((SEQ, w), lambda b: (b, 0))
    return pl.pallas_call(
        functools.partial(_ctx_attn_kernel, l=l), grid=(BATCH,),
        in_specs=[pl.BlockSpec(memory_space=pltpu.SMEM),
                  blk(A_Q), blk(A_KV), blk(A_KV), blk(C_W), blk(C_W), blk(C_W)],
        out_specs=[blk(A_Q), blk(C_W)],
        out_shape=[jax.ShapeDtypeStruct((N_PROMPT, A_Q), BF16), jax.ShapeDtypeStruct((N_PROMPT, C_W), BF16)],
        compiler_params=_cparams(("parallel",)), name="ctx_attn",
    )(sink_a, qa, ka, va, qc, kc, vc)


def _win_attn_kernel(sink_ref, q_ref, kp_ref, kc_ref, kn_ref, vp_ref, vc_ref, vn_ref, ck_ref, cv_ref, o_ref, *, l):
    t = pl.program_id(1)
    rows = A_GROUP * A_BLOCK
    qi = lax.broadcasted_iota(jnp.int32, (rows, 3 * A_BLOCK), 0) % A_BLOCK
    kj = lax.broadcasted_iota(jnp.int32, (rows, 3 * A_BLOCK), 1) - A_BLOCK
    kpos = t * A_BLOCK + kj
    valid = (jnp.abs(kj - qi) <= A_WINDOW) & (kpos >= 0) & (kpos < DEC_SEQ)
    for g in range(A_KV_HEADS):
        sl = slice(g * HEAD_DIM, (g + 1) * HEAD_DIM)
        q = jnp.concatenate([q_ref[:, (g * A_GROUP + i) * HEAD_DIM:(g * A_GROUP + i + 1) * HEAD_DIM]
                             for i in range(A_GROUP)], axis=0) * SCALE
        sink = jnp.concatenate([jnp.full((A_BLOCK, 1), sink_ref[l, g * A_GROUP + i], F32)
                                for i in range(A_GROUP)], axis=0)
        k_loc = jnp.concatenate([kp_ref[:, sl], kc_ref[:, sl], kn_ref[:, sl]], axis=0).astype(BF16)
        v_loc = jnp.concatenate([vp_ref[:, sl], vc_ref[:, sl], vn_ref[:, sl]], axis=0).astype(BF16)
        s_loc = jnp.where(valid, _nt_dot(q, k_loc), NEG_INF)
        s_ctx = _nt_dot(q, ck_ref[:, sl].astype(BF16))
        o = _softmax_pv([(s_loc, v_loc), (s_ctx, cv_ref[:, sl].astype(BF16))], sink)
        for i in range(A_GROUP):
            h = g * A_GROUP + i
            o_ref[:, h * HEAD_DIM:(h + 1) * HEAD_DIM] = o[i * A_BLOCK:(i + 1) * A_BLOCK].astype(BF16)


def _win_attn(sink_a, qa, ka, va, cache_k, cache_v, l):
    nb = DEC_SEQ // A_BLOCK
    base = N_PROMPT // A_BLOCK

    def nbr(d):
        return lambda b, t: (base + b * nb + jnp.clip(t + d, 0, nb - 1), 0)

    kv = lambda d: pl.BlockSpec((A_BLOCK, A_KV), nbr(d))
    cache = pl.BlockSpec((None, None, PAST_LEN, A_KV), lambda b, t: (b, l, 0, 0))
    return pl.pallas_call(
        functools.partial(_win_attn_kernel, l=l), grid=(DEC_BATCH, nb),
        in_specs=[pl.BlockSpec(memory_space=pltpu.SMEM),
                  pl.BlockSpec((A_BLOCK, A_Q), nbr(0)),
                  kv(-1), kv(0), kv(1), kv(-1), kv(0), kv(1), cache, cache],
        out_specs=pl.BlockSpec((A_BLOCK, A_Q), lambda b, t: (b * nb + t, 0)),
        out_shape=jax.ShapeDtypeStruct((N_SAMPLE, A_Q), BF16),
        compiler_params=_cparams(("parallel", "parallel")), name="win_attn",
    )(sink_a, qa, ka, ka, ka, va, va, va, cache_k, cache_v)


def _nbr_attn_kernel(q_ref, kp_ref, kc_ref, kn_ref, vp_ref, vc_ref, vn_ref, ck_ref, cv_ref, tab_ref, o_ref,
                     bias_ref):
    j = pl.program_id(0)
    nb = pl.num_programs(0)
    slots = _nbr_row_slots()

    def build(cls):
        for h in range(C_HEADS):
            for qr in range(C_QROWS):
                for kk in range(3 * C_QROWS):
                    bias_ref[h, qr * GRID_W:(qr + 1) * GRID_W, kk * GRID_W:(kk + 1) * GRID_W] = (
                        tab_ref[h, slots[cls][qr][kk]])

    first_of_batch = pl.program_id(1) == 0
    for cls, at in enumerate((0, 1, nb - 1)):
        @pl.when(first_of_batch & (j == at))
        def _():
            build(cls)

    for h in range(C_HEADS):
        sl = slice(h * HEAD_DIM, (h + 1) * HEAD_DIM)
        q = q_ref[:, sl] * SCALE
        k_loc = jnp.concatenate([kp_ref[:, sl], kc_ref[:, sl], kn_ref[:, sl]], axis=0).astype(BF16)
        v_loc = jnp.concatenate([vp_ref[:, sl], vc_ref[:, sl], vn_ref[:, sl]], axis=0).astype(BF16)
        s_loc = _nt_dot(q, k_loc) + bias_ref[h]
        s_ctx = _nt_dot(q, ck_ref[:, sl].astype(BF16))
        o = _softmax_pv([(s_loc, v_loc), (s_ctx, cv_ref[:, sl].astype(BF16))], None)
        o_ref[:, sl] = o.astype(BF16)


def _nbr_attn(qc, kc, vc, cache_k, cache_v, bias, l):
    nb = DEC_SEQ // C_QBLOCK
    base = N_PROMPT // C_QBLOCK

    def nbr(d):
        return lambda j, b: (base + b * nb + jnp.clip(j + d, 0, nb - 1), 0)

    kv = lambda d: pl.BlockSpec((C_QBLOCK, C_W), nbr(d))
    cache = pl.BlockSpec((None, None, PAST_LEN, C_W), lambda j, b: (b, l, 0, 0))
    return pl.pallas_call(
        _nbr_attn_kernel, grid=(nb, DEC_BATCH),
        in_specs=[kv(0), kv(-1), kv(0), kv(1), kv(-1), kv(0), kv(1), cache, cache,
                  pl.BlockSpec((None, C_HEADS, C_DR_SLOTS, GRID_W, GRID_W), lambda j, b: (l, 0, 0, 0, 0))],
        out_specs=pl.BlockSpec((C_QBLOCK, C_W), lambda j, b: (b * nb + j, 0)),
        out_shape=jax.ShapeDtypeStruct((N_SAMPLE, C_W), BF16),
        scratch_shapes=[pltpu.VMEM((C_HEADS, C_QBLOCK, 3 * C_QBLOCK), F32)],
        compiler_params=_cparams(("arbitrary", "arbitrary")), name="nbr_attn",
    )(qc, kc, kc, kc, vc, vc, vc, cache_k, cache_v, bias)


def _nbr_bias_tables(rpb):
    qcol = np.arange(GRID_W)
    qcs = np.clip(qcol - C_WIN_COLS // 2, 0, GRID_W - C_WIN_COLS)
    kcol = np.arange(GRID_W)
    col_ok = (kcol[None, :] >= qcs[:, None]) & (kcol[None, :] < qcs[:, None] + C_WIN_COLS)
    dc = np.clip(kcol[None, :] - qcol[:, None], -(C_WIN_COLS - 1), C_WIN_COLS - 1) + C_WIN_COLS - 1
    onehot_dc = (dc.reshape(-1)[None, :] == np.arange(2 * C_WIN_COLS - 1)[:, None]).astype(np.float32)
    t = jnp.einsum('lhab,bx->lhax', rpb, jnp.asarray(onehot_dc), precision=lax.Precision.HIGHEST)
    t = jnp.where(jnp.asarray(col_ok.reshape(-1)), t, NEG_INF)
    t = jnp.concatenate([t, jnp.full((DEPTH, C_HEADS, 1, GRID_W * GRID_W), NEG_INF, F32)], axis=2)
    return t.reshape(DEPTH, C_HEADS, C_DR_SLOTS, GRID_W, GRID_W)


def _nbr_row_slots():
    rows = DEC_SEQ // GRID_W
    slots = []
    for j in (0, 3, rows // C_QROWS - 1):
        per_q = []
        for qr in range(C_QROWS):
            r = C_QROWS * j + qr
            rs = min(max(r - C_WIN_ROWS // 2, 0), rows - C_WIN_ROWS)
            per_k = []
            for kk in range(3 * C_QROWS):
                kabs = C_QROWS * (j - 1) + kk
                per_k.append(kabs - r + C_WIN_ROWS - 1 if rs <= kabs < rs + C_WIN_ROWS else C_DR_SLOTS - 1)
            per_q.append(per_k)
        slots.append(per_q)
    return slots


def _fourier_kernel(u_ref, bc_ref, bs_ref, cl_ref, sl_ref, o_ref, zc_ref, zs_ref):
    @pl.when(pl.program_id(1) == 0)
    def _():
        u = u_ref[...]
        zc_ref[...] = jnp.dot(u, bc_ref[...].astype(BF16), preferred_element_type=F32).astype(BF16)
        zs_ref[...] = jnp.dot(u, bs_ref[...].astype(BF16), preferred_element_type=F32).astype(BF16)

    o = (jnp.dot(cl_ref[...].astype(BF16), zc_ref[...], preferred_element_type=F32)
         - jnp.dot(sl_ref[...].astype(BF16), zs_ref[...], preferred_element_type=F32))
    o_ref[...] = o.astype(BF16)


def _dft_tables(n):
    k = np.arange(n)
    ang = 2.0 * np.pi * ((k[:, None] * k[None, :]) % n) / n
    return np.cos(ang) / np.sqrt(n), np.sin(ang) / np.sqrt(n)


def _channel_dft_tables():
    c, s = _dft_tables(B_GROUP_DIM)
    eye = np.eye(B_GROUPS)
    return np.kron(eye, c).astype(np.float32), np.kron(eye, s).astype(np.float32)


def _fourier(fb, n_batch, seq, row0, tr):
    cl, sl = (jnp.asarray(a.astype(np.float32)) for a in _dft_tables(seq))
    bc, bs = (jnp.asarray(a) for a in _channel_dft_tables())
    nt = seq // tr
    const = pl.BlockSpec((B_WIDTH, B_WIDTH), lambda b, t: (0, 0))
    return pl.pallas_call(
        _fourier_kernel, grid=(n_batch, nt),
        in_specs=[pl.BlockSpec((seq, B_WIDTH), lambda b, t: (row0 // seq + b, 0)), const, const,
                  pl.BlockSpec((tr, seq), lambda b, t: (t, 0)), pl.BlockSpec((tr, seq), lambda b, t: (t, 0))],
        out_specs=pl.BlockSpec((tr, B_WIDTH), lambda b, t: (b * nt + t, 0)),
        out_shape=jax.ShapeDtypeStruct((n_batch * seq, B_WIDTH), BF16),
        scratch_shapes=[pltpu.VMEM((seq, B_WIDTH), BF16), pltpu.VMEM((seq, B_WIDTH), BF16)],
        compiler_params=_cparams(("parallel", "arbitrary")), name=f"fourier_{seq}",
    )(fb, bc, bs, cl, sl)


def _merge_kernel(oap_ref, obp_ref, ocp_ref, oas_ref, obs_ref, ocs_ref, gt_ref, x_ref,
                  wa_ref, wb_ref, wc_ref, wo_ref, g1_ref, o_ref):
    ctx = pl.program_id(0) < N_PROMPT // x_ref.shape[0]
    oa = jnp.where(ctx, oap_ref[...], oas_ref[...])
    ob = jnp.where(ctx, obp_ref[...], obs_ref[...])
    oc = jnp.where(ctx, ocp_ref[...], ocs_ref[...])
    ya = jnp.dot(oa, wa_ref[...].astype(BF16), preferred_element_type=F32)
    yb = jnp.dot(ob, wb_ref[...].astype(BF16), preferred_element_type=F32)
    yc = jnp.dot(oc, wc_ref[...].astype(BF16), preferred_element_type=F32)
    d = D_MODEL
    m = (gt_ref[:, 0:d].astype(F32) * ya + gt_ref[:, d:2 * d].astype(F32) * yb
         + gt_ref[:, 2 * d:3 * d].astype(F32) * yc)
    y = jnp.dot(m.astype(BF16), wo_ref[...].astype(BF16), preferred_element_type=F32)
    o_ref[...] = x_ref[...] + g1_ref[...] * y


def _merge(branches_p, branches_s, gates, x, wa, wb, wc, wo, mod, l):
    tm = 512
    n_p = N_PROMPT // tm
    row = lambda w: pl.BlockSpec((tm, w), lambda i: (i, 0))
    row_p = lambda w: pl.BlockSpec((tm, w), lambda i: (jnp.minimum(i, n_p - 1), 0))
    row_s = lambda w: pl.BlockSpec((tm, w), lambda i: (jnp.maximum(i - n_p, 0), 0))
    const = lambda r, c: pl.BlockSpec((r, c), lambda i: (0, 0))
    return pl.pallas_call(
        _merge_kernel, grid=(N_TOK // tm,),
        in_specs=[row_p(A_Q), row_p(B_WIDTH), row_p(C_W), row_s(A_Q), row_s(B_WIDTH), row_s(C_W),
                  row(GATE_WIDTH), row(D_MODEL),
                  const(A_Q, D_MODEL), const(B_WIDTH, D_MODEL), const(C_W, D_MODEL), const(D_MODEL, D_MODEL),
                  pl.BlockSpec((None, None, 1, D_MODEL), lambda i: (l, _cond_row(i, tm), 0, 2))],
        out_specs=row(D_MODEL),
        out_shape=jax.ShapeDtypeStruct((N_TOK, D_MODEL), F32),
        compiler_params=_cparams(("parallel",)), name="merge",
    )(*branches_p, *branches_s, gates, x, wa, wb, wc, wo, mod)


def _ffn_kernel(h_ref, x_ref, wg_ref, wu_ref, wd_ref, g2_ref, o_ref, acc_ref):
    f = pl.program_id(1)

    @pl.when(f == 0)
    def _():
        acc_ref[...] = jnp.zeros_like(acc_ref)

    h = h_ref[...]
    g = jnp.dot(h, wg_ref[...].astype(BF16), preferred_element_type=F32)
    u = jnp.dot(h, wu_ref[...].astype(BF16), preferred_element_type=F32)
    a = (g * _sigmoid(g) * u).astype(BF16)
    acc_ref[...] += jnp.dot(a, wd_ref[...].astype(BF16), preferred_element_type=F32)

    @pl.when(f == pl.num_programs(1) - 1)
    def _():
        o_ref[...] = x_ref[...] + g2_ref[...] * acc_ref[...]


def _ffn(h, x, wg, wu, wd, mod, l):
    tm, tf = 1024, 256
    return pl.pallas_call(
        _ffn_kernel, grid=(N_TOK // tm, D_FF // tf),
        in_specs=[pl.BlockSpec((tm, D_MODEL), lambda i, f: (i, 0)),
                  pl.BlockSpec((tm, D_MODEL), lambda i, f: (i, 0)),
                  pl.BlockSpec((D_MODEL, tf), lambda i, f: (0, f)),
                  pl.BlockSpec((D_MODEL, tf), lambda i, f: (0, f)),
                  pl.BlockSpec((tf, D_MODEL), lambda i, f: (f, 0)),
                  pl.BlockSpec((None, None, 1, D_MODEL), lambda i, f: (l, _cond_row(i, tm), 0, 5))],
        out_specs=pl.BlockSpec((tm, D_MODEL), lambda i, f: (i, 0)),
        out_shape=jax.ShapeDtypeStruct((N_TOK, D_MODEL), F32),
        scratch_shapes=[pltpu.VMEM((tm, D_MODEL), F32)],
        compiler_params=_cparams(("parallel", "arbitrary")), name="ffn",
    )(h, x, wg, wu, wd, mod)


TOP_K = 2
MOE_TILE = 256
MOE_TILES = TOP_K * N_TOK // MOE_TILE + N_EXPERTS
MOE_ROWS = MOE_TILES * MOE_TILE
MOE_WINDOW = 8
MOE_TF = 896
ROUTE_TM = 512
DISPATCH_TM = 256
COMBINE_TM = 256


def _route_kernel(lg_ref, o_ref, cnt_ref, base_ref, tri_ref):
    tm = lg_ref.shape[0]

    @pl.when(pl.program_id(0) == 0)
    def _():
        base_ref[...] = jnp.zeros_like(base_ref)
        r = lax.broadcasted_iota(jnp.int32, (tm, tm), 0)
        c = lax.broadcasted_iota(jnp.int32, (tm, tm), 1)
        tri_ref[...] = jnp.where(r > c, 1.0, 0.0).astype(BF16)

    lane = lax.broadcasted_iota(jnp.int32, lg_ref.shape, 1).astype(F32)
    lg = jnp.where(lane < N_EXPERTS, lg_ref[...], -jnp.inf)
    m1 = lg.max(axis=-1, keepdims=True)
    i1 = jnp.where(lg == m1, lane, float(LANES)).min(axis=-1, keepdims=True)
    rest = jnp.where(lane == i1, -jnp.inf, lg)
    m2 = rest.max(axis=-1, keepdims=True)
    i2 = jnp.where(rest == m2, lane, float(LANES)).min(axis=-1, keepdims=True)
    e2 = jnp.exp(m2 - m1)
    w1 = 1.0 / (1.0 + e2)
    w2 = e2 / (1.0 + e2)

    oh1 = jnp.where(lane == i1, 1.0, 0.0)
    oh2 = jnp.where(lane == i2, 1.0, 0.0)
    pre1 = jnp.dot(tri_ref[...], oh1.astype(BF16), preferred_element_type=F32)
    pre2 = jnp.dot(tri_ref[...], oh2.astype(BF16), preferred_element_type=F32)
    c1 = jnp.sum(oh1, axis=0, keepdims=True)
    c2 = jnp.sum(oh2, axis=0, keepdims=True)
    base = base_ref[...]
    rank1 = jnp.sum(oh1 * (base + pre1), axis=-1, keepdims=True)
    rank2 = jnp.sum(oh2 * (base + c1 + pre2), axis=-1, keepdims=True)
    base_ref[...] = base + c1 + c2

    cols = (i1, i2, rank1, rank2, w1, w2)
    out = jnp.zeros(lg_ref.shape, F32)
    for j, col in enumerate(cols):
        out = jnp.where(lane == float(j), col, out)
    o_ref[...] = out
    cnt_ref[...] = jnp.broadcast_to(base + c1 + c2, cnt_ref.shape)


def _dispatch_kernel(pos_ref, last_ref, nt_ref, h_ref, xs_hbm, zero_ref, zsem, sem):
    i = pl.program_id(0)
    tm = h_ref.shape[0]

    @pl.when(i == 0)
    def _():
        zero_ref[...] = jnp.zeros_like(zero_ref)

        def zero_copy(tile):
            row0 = pl.multiple_of(tile * MOE_TILE, MOE_TILE)
            return pltpu.make_async_copy(zero_ref, xs_hbm.at[pl.ds(row0, MOE_TILE)], zsem)

        def for_zeroed_tiles(fn):
            for e in range(N_EXPERTS):
                @pl.when(last_ref[e] >= 0)
                def _():
                    fn(zero_copy(last_ref[e]))

                tail = MOE_TILES - 1 - e

                @pl.when(tail >= nt_ref[0])
                def _():
                    fn(zero_copy(tail))

        for_zeroed_tiles(lambda cp: cp.start())
        for_zeroed_tiles(lambda cp: cp.wait())

    def row_copy(t, k):
        dst = xs_hbm.at[pl.ds(pos_ref[TOP_K * (i * tm + t) + k], 1)]
        return pltpu.make_async_copy(h_ref.at[pl.ds(t, 1)], dst, sem)

    def issue(t, carry):
        for k in range(TOP_K):
            row_copy(t, k).start()
        return carry

    def drain(t, carry):
        for k in range(TOP_K):
            row_copy(t, k).wait()
        return carry

    lax.fori_loop(0, tm, issue, 0)
    lax.fori_loop(0, tm, drain, 0)


def _expert_kernel(te_ref, nt_ref, x_ref, wg_ref, wu_ref, wd_ref, y_hbm, acc_ref, wgb_ref, wub_ref, wdb_ref, osem):
    f = pl.program_id(1)
    r = pl.program_id(2)
    last_f = pl.num_programs(1) - 1
    t = pl.program_id(0) * MOE_WINDOW + r
    valid = t < nt_ref[0]
    tc = jnp.minimum(t, nt_ref[0] - 1)
    same_weights = (r > 0) & (te_ref[tc] == te_ref[jnp.maximum(tc - 1, 0)])

    @pl.when(valid & jnp.logical_not(same_weights))
    def _():
        wgb_ref[...] = wg_ref[...].astype(BF16)
        wub_ref[...] = wu_ref[...].astype(BF16)
        wdb_ref[...] = wd_ref[...].astype(BF16)

    @pl.when(valid)
    def _():
        x = x_ref[...].astype(BF16)
        g = jnp.dot(x, wgb_ref[...], preferred_element_type=F32)
        u = jnp.dot(x, wub_ref[...], preferred_element_type=F32)
        a = (g * _sigmoid(g) * u).astype(BF16)
        d = jnp.dot(a, wdb_ref[...], preferred_element_type=F32)

        @pl.when(f == 0)
        def _():
            acc_ref[r] = d

        @pl.when(f > 0)
        def _():
            acc_ref[r] += d

    def out_copy(slot, tile):
        row0 = pl.multiple_of(tile * MOE_TILE, MOE_TILE)
        return pltpu.make_async_copy(acc_ref.at[slot], y_hbm.at[pl.ds(row0, MOE_TILE)], osem)

    @pl.when(f == last_f)
    def _():
        @pl.when(jnp.logical_not(valid))
        def _():
            acc_ref[r] = jnp.zeros((MOE_TILE, D_MODEL), F32)

        @pl.when(r > 0)
        def _():
            out_copy(r - 1, t - 1).wait()

        out_copy(r, t).start()

        @pl.when(r == MOE_WINDOW - 1)
        def _():
            out_copy(r, t).wait()


def _combine_kernel(pos_ref, route_ref, x_ref, g2_ref, y_hbm, op_ref, os_ref, buf_ref, sem):
    i = pl.program_id(0)
    n = pl.num_programs(0)
    tm = x_ref.shape[0]
    slot = i % 2

    def row_copy(step, s, t, k):
        src = y_hbm.at[pl.ds(pos_ref[TOP_K * (step * tm + t) + k], 1)]
        return pltpu.make_async_copy(src, buf_ref.at[s, k, pl.ds(t, 1)], sem.at[s])

    def issue(step, s):
        def body(t, carry):
            for k in range(TOP_K):
                row_copy(step, s, t, k).start()
            return carry
        lax.fori_loop(0, tm, body, 0)

    @pl.when(i == 0)
    def _():
        issue(0, 0)

    @pl.when(i + 1 < n)
    def _():
        issue(i + 1, 1 - slot)

    def wait_body(t, carry):
        for k in range(TOP_K):
            row_copy(i, slot, t, k).wait()
        return carry

    lax.fori_loop(0, tm, wait_body, 0)

    lane = lax.broadcasted_iota(jnp.int32, route_ref.shape, 1)
    rt = route_ref[...]
    w1 = jnp.sum(jnp.where(lane == 2 * TOP_K, rt, 0.0), axis=-1, keepdims=True)
    w2 = jnp.sum(jnp.where(lane == 2 * TOP_K + 1, rt, 0.0), axis=-1, keepdims=True)
    out = x_ref[...] + g2_ref[...] * (w1 * buf_ref[slot, 0] + w2 * buf_ref[slot, 1])

    @pl.when(i < N_PROMPT // tm)
    def _():
        op_ref[...] = out

    @pl.when(i >= N_PROMPT // tm)
    def _():
        os_ref[...] = out


def _moe(h, x, logits, wg, wu, wd, mod, l):
    route, cnt = pl.pallas_call(
        _route_kernel, grid=(N_TOK // ROUTE_TM,),
        in_specs=[pl.BlockSpec((ROUTE_TM, LANES), lambda i: (i, 0))],
        out_specs=[pl.BlockSpec((ROUTE_TM, LANES), lambda i: (i, 0)), pl.BlockSpec((8, LANES), lambda i: (0, 0))],
        out_shape=[jax.ShapeDtypeStruct((N_TOK, LANES), F32), jax.ShapeDtypeStruct((8, LANES), F32)],
        scratch_shapes=[pltpu.VMEM((1, LANES), F32), pltpu.VMEM((ROUTE_TM, ROUTE_TM), BF16)],
        compiler_params=_cparams(("arbitrary",)), name="route",
    )(logits)

    expert = route[:, 0:TOP_K].astype(jnp.int32)
    rank = route[:, TOP_K:2 * TOP_K].astype(jnp.int32)
    n_sub = (cnt[0, :N_EXPERTS].astype(jnp.int32) + MOE_TILE - 1) // MOE_TILE
    end = jnp.cumsum(n_sub)
    start = end - n_sub
    start_of = jnp.sum(jnp.where(expert[:, :, None] == jnp.arange(N_EXPERTS), start, 0), axis=-1)
    pos = (start_of * MOE_TILE + rank).reshape(-1)
    tile_expert = jnp.minimum(jnp.sum(jnp.arange(MOE_TILES)[:, None] >= end[None, :], axis=1), N_EXPERTS - 1)
    tile_expert = tile_expert.astype(jnp.int32)
    n_tiles = end[N_EXPERTS - 1:]
    last_tile = jnp.where(n_sub > 0, end - 1, -1).astype(jnp.int32)

    xs = pl.pallas_call(
        _dispatch_kernel,
        grid_spec=pltpu.PrefetchScalarGridSpec(
            num_scalar_prefetch=3, grid=(N_TOK // DISPATCH_TM,),
            in_specs=[pl.BlockSpec((DISPATCH_TM, D_MODEL), lambda i, p, lt, nt: (i, 0))],
            out_specs=pl.BlockSpec(memory_space=pl.ANY),
            scratch_shapes=[pltpu.VMEM((MOE_TILE, D_MODEL), F32), pltpu.SemaphoreType.DMA(()),
                            pltpu.SemaphoreType.DMA(())]),
        out_shape=jax.ShapeDtypeStruct((MOE_ROWS, D_MODEL), F32),
        compiler_params=_cparams(("arbitrary",)), name="moe_dispatch",
    )(pos, last_tile, n_tiles, h)

    nf = D_FF_EXPERT // MOE_TF

    def tile(w, r, nt):
        return jnp.minimum(w * MOE_WINDOW + r, nt[0] - 1)

    y = pl.pallas_call(
        _expert_kernel,
        grid_spec=pltpu.PrefetchScalarGridSpec(
            num_scalar_prefetch=2, grid=(MOE_TILES // MOE_WINDOW, nf, MOE_WINDOW),
            in_specs=[pl.BlockSpec((MOE_TILE, D_MODEL), lambda w, f, r, te, nt: (tile(w, r, nt), 0)),
                      pl.BlockSpec((None, D_MODEL, MOE_TF), lambda w, f, r, te, nt: (te[tile(w, r, nt)], 0, f)),
                      pl.BlockSpec((None, D_MODEL, MOE_TF), lambda w, f, r, te, nt: (te[tile(w, r, nt)], 0, f)),
                      pl.BlockSpec((None, MOE_TF, D_MODEL), lambda w, f, r, te, nt: (te[tile(w, r, nt)], f, 0))],
            out_specs=pl.BlockSpec(memory_space=pl.ANY),
            scratch_shapes=[pltpu.VMEM((MOE_WINDOW, MOE_TILE, D_MODEL), F32),
                            pltpu.VMEM((D_MODEL, MOE_TF), BF16), pltpu.VMEM((D_MODEL, MOE_TF), BF16),
                            pltpu.VMEM((MOE_TF, D_MODEL), BF16), pltpu.SemaphoreType.DMA(())]),
        out_shape=jax.ShapeDtypeStruct((MOE_ROWS, D_MODEL), F32),
        compiler_params=_cparams(("arbitrary", "arbitrary", "arbitrary")), name="moe_experts",
    )(tile_expert, n_tiles, xs, wg, wu, wd)

    tm = COMBINE_TM
    return pl.pallas_call(
        _combine_kernel,
        grid_spec=pltpu.PrefetchScalarGridSpec(
            num_scalar_prefetch=1, grid=(N_TOK // tm,),
            in_specs=[pl.BlockSpec((tm, LANES), lambda i, p: (i, 0)),
                      pl.BlockSpec((tm, D_MODEL), lambda i, p: (i, 0)),
                      pl.BlockSpec((None, None, 1, D_MODEL), lambda i, p: (l, _cond_row(i, tm), 0, 5)),
                      pl.BlockSpec(memory_space=pl.ANY)],
            out_specs=[pl.BlockSpec((tm, D_MODEL), lambda i, p: (jnp.minimum(i, N_PROMPT // tm - 1), 0)),
                       pl.BlockSpec((tm, D_MODEL), lambda i, p: (jnp.maximum(i - N_PROMPT // tm, 0), 0))],
            scratch_shapes=[pltpu.VMEM((2, TOP_K, tm, D_MODEL), F32), pltpu.SemaphoreType.DMA((2,))]),
        out_shape=[jax.ShapeDtypeStruct((N_PROMPT, D_MODEL), F32), jax.ShapeDtypeStruct((N_SAMPLE, D_MODEL), F32)],
        compiler_params=_cparams(("arbitrary",)), name="moe_combine",
    )(pos, route, x, mod, y)


def _rope_tables():
    t = jnp.arange(DEC_SEQ)
    row = (t // GRID_W).astype(F32)
    col = (t % GRID_W).astype(F32)
    n_freq = HEAD_DIM // 4
    inv = ROPE_BASE ** (-jnp.arange(n_freq, dtype=F32) / n_freq)
    ang = jnp.concatenate([row[:, None] * inv, col[:, None] * inv], axis=-1)
    cos, sin = jnp.cos(ang), jnp.sin(ang)
    cos_h = jnp.concatenate([cos, cos], axis=-1)
    sin_h = jnp.concatenate([-sin, sin], axis=-1)
    cos_l = jnp.tile(jnp.concatenate([cos_h, cos_h], axis=-1), (DEC_BATCH, 1))
    sin_l = jnp.tile(jnp.concatenate([sin_h, sin_h], axis=-1), (DEC_BATCH, 1))
    cos_t = jnp.concatenate([jnp.ones((N_PROMPT, LANES), F32), cos_l], axis=0)
    sin_t = jnp.concatenate([jnp.zeros((N_PROMPT, LANES), F32), sin_l], axis=0)
    return cos_t, sin_t


def kernel(x_prompt, x_sample, cache_a_k, cache_a_v, cache_c_k, cache_c_v, c, c_ctx, w_mod, b_mod, norm1_g, norm2_g, w_in, qk_norm_a, qk_norm_c, sink_a, rpb_c, w_branch_a, w_branch_b, w_branch_c, w_out, w_ff_gate, w_ff_up, w_ff_down, w_router, w_exp_gate, w_exp_up, w_exp_down):
    x = jnp.concatenate([x_prompt.reshape(N_PROMPT, D_MODEL), x_sample.reshape(N_SAMPLE, D_MODEL)], axis=0)
    cond = jnp.concatenate([c_ctx[None, :], c], axis=0)
    cond_t = jnp.broadcast_to(cond[:, :, None], (N_COND, D_MODEL, LANES))
    mod = _modulation(cond_t, w_mod, b_mod)
    cos_t, sin_t = _rope_tables()
    bias = _nbr_bias_tables(rpb_c)
    ck_a = cache_a_k.reshape(DEC_BATCH, DEPTH, PAST_LEN, A_KV)
    cv_a = cache_a_v.reshape(DEC_BATCH, DEPTH, PAST_LEN, A_KV)
    ck_c = cache_c_k.reshape(DEC_BATCH, DEPTH, PAST_LEN, C_W)
    cv_c = cache_c_v.reshape(DEC_BATCH, DEPTH, PAST_LEN, C_W)

    new_ak, new_av, new_ck, new_cv = [], [], [], []
    for l in range(DEPTH):
        h = _adaln(x, norm1_g[l], mod, l, 0)
        p = _proj(h, w_in, l, 0, QKV_WIDTH, QKV_WIDTH // 2, F32, gate=False)
        gates = _proj(h, w_in, l, QKV_WIDTH, GATE_WIDTH, D_MODEL, BF16, gate=True)
        qa, ka, va, fb, qc, kc, vc = _qk_post(p, cos_t, sin_t, qk_norm_a[l], qk_norm_c[l])
        oa_p, oc_p = _ctx_attn(sink_a, qa, ka, va, qc, kc, vc, l)
        oa_s = _win_attn(sink_a, qa, ka, va, ck_a, cv_a, l)
        oc_s = _nbr_attn(qc, kc, vc, ck_c, cv_c, bias, l)
        ob_p = _fourier(fb, BATCH, SEQ, 0, SEQ)
        ob_s = _fourier(fb, DEC_BATCH, DEC_SEQ, N_PROMPT, 512)
        x = _merge((oa_p, ob_p, oc_p), (oa_s, ob_s, oc_s), gates, x,
                   w_branch_a[l], w_branch_b[l], w_branch_c[l], w_out[l], mod, l)
        i = l // 2
        if l % 2 == 0:
            h2 = _adaln(x, norm2_g[l], mod, l, 3)
            x = _ffn(h2, x, w_ff_gate[i], w_ff_up[i], w_ff_down[i], mod, l)
        else:
            h2, logits = _adaln(x, norm2_g[l], mod, l, 3, w_router=w_router[i])
            xp, xs = _moe(h2, x, logits, w_exp_gate[i], w_exp_up[i], w_exp_down[i], mod, l)
        new_ak.append(ka[:N_PROMPT].reshape(BATCH, SEQ, A_KV_HEADS, HEAD_DIM))
        new_av.append(va[:N_PROMPT].reshape(BATCH, SEQ, A_KV_HEADS, HEAD_DIM))
        new_ck.append(kc[:N_PROMPT].reshape(BATCH, SEQ, C_HEADS, HEAD_DIM))
        new_cv.append(vc[:N_PROMPT].reshape(BATCH, SEQ, C_HEADS, HEAD_DIM))

    return (xp.reshape(BATCH, SEQ, D_MODEL), xs.reshape(DEC_BATCH, DEC_SEQ, D_MODEL),
            jnp.stack(new_ak, axis=1), jnp.stack(new_av, axis=1),
            jnp.stack(new_ck, axis=1), jnp.stack(new_cv, axis=1))
```

```python
import functools

import numpy as np
import jax
import jax.numpy as jnp
from jax import lax
from jax.experimental import pallas as pl
from jax.experimental.pallas import tpu as pltpu

F32 = jnp.float32
BF16 = jnp.bfloat16

D_MODEL = 1024
BATCH = 16
SEQ = 256
DEPTH = 2
DEC_BATCH = 2
DEC_SEQ = 2048
PAST_LEN = 512
GRID_W = 64
HEAD_DIM = 64
SCALE = HEAD_DIM ** -0.5
A_HEADS = 8
A_KV_HEADS = 2
A_GROUP = A_HEADS // A_KV_HEADS
A_WINDOW = 128
A_BLOCK = 128
B_GROUPS = 8
B_GROUP_DIM = 64
B_WIDTH = B_GROUPS * B_GROUP_DIM
C_HEADS = 8
C_WIN_ROWS = 8
C_WIN_COLS = 16
A_Q = A_HEADS * HEAD_DIM
A_KV = A_KV_HEADS * HEAD_DIM
C_W = C_HEADS * HEAD_DIM
QKV_WIDTH = A_Q + 2 * A_KV + B_WIDTH + 3 * C_W
GATE_WIDTH = 3 * D_MODEL
D_FF = 2816
N_EXPERTS = 8
D_FF_EXPERT = 3584
ROPE_BASE = 10000.0
RMS_EPS = 1e-6
NEG_INF = -1e30

N_PROMPT = BATCH * SEQ
N_SAMPLE = DEC_BATCH * DEC_SEQ
N_TOK = N_PROMPT + N_SAMPLE
N_COND = 1 + DEC_BATCH
LANES = 128
C_QROWS = 4
C_QBLOCK = C_QROWS * GRID_W
C_DR_SLOTS = 2 * C_WIN_ROWS
VMEM_LIMIT = 56 * 1024 * 1024


def _cparams(sem):
    return pltpu.CompilerParams(dimension_semantics=sem, vmem_limit_bytes=VMEM_LIMIT)


def _sigmoid(x):
    return 1.0 / (1.0 + jnp.exp(-x))


def _cond_row(tile, tm):
    return jnp.maximum(tile * tm // DEC_SEQ - 1, 0)


def _mod_kernel(ct_ref, w_ref, b_ref, o_ref):
    tn = w_ref.shape[1]
    for r in range(N_COND):
        cb = ct_ref[r]
        s = cb * _sigmoid(cb)
        for cc in range(tn // LANES):
            sl = slice(cc * LANES, (cc + 1) * LANES)
            o_ref[r, :, sl] = jnp.sum(w_ref[:, sl] * s, axis=0, keepdims=True) + b_ref[:, sl]


def _modulation(cond_t, w_mod, b_mod):
    tn = 512
    n = 6 * D_MODEL
    return pl.pallas_call(
        _mod_kernel,
        grid=(DEPTH, n // tn),
        in_specs=[
            pl.BlockSpec((N_COND, D_MODEL, LANES), lambda l, j: (0, 0, 0)),
            pl.BlockSpec((None, D_MODEL, tn), lambda l, j: (l, 0, j)),
            pl.BlockSpec((None, 1, tn), lambda l, j: (l, 0, j)),
        ],
        out_specs=pl.BlockSpec((None, N_COND, 1, tn), lambda l, j: (l, 0, 0, j)),
        out_shape=jax.ShapeDtypeStruct((DEPTH, N_COND, 1, n), F32),
        compiler_params=_cparams(("parallel", "parallel")),
        name="modulation",
    )(cond_t, w_mod, b_mod.reshape(DEPTH, 1, n))


def _adaln_math(x, g, sh, sc):
    ms = jnp.mean(x * x, axis=-1, keepdims=True)
    y = x * lax.rsqrt(ms + RMS_EPS) * g
    return y * (1.0 + sc) + sh


def _adaln_kernel(x_ref, g_ref, sh_ref, sc_ref, o_ref):
    o_ref[...] = _adaln_math(x_ref[...], g_ref[...], sh_ref[...], sc_ref[...]).astype(BF16)


def _adaln_router_kernel(x_ref, g_ref, sh_ref, sc_ref, wr_ref, o_ref, lg_ref):
    h = _adaln_math(x_ref[...], g_ref[...], sh_ref[...], sc_ref[...])
    o_ref[...] = h
    lg_ref[...] = jnp.dot(h, wr_ref[...], precision=lax.Precision.HIGHEST, preferred_element_type=F32)


def _adaln(x, g, mod, l, shift_idx, w_router=None):
    tm = 512
    mspec = lambda which: pl.BlockSpec((None, None, 1, D_MODEL),
                                       lambda i: (l, _cond_row(i, tm), 0, which))
    in_specs = [pl.BlockSpec((tm, D_MODEL), lambda i: (i, 0)),
                pl.BlockSpec((1, D_MODEL), lambda i: (0, 0)),
                mspec(shift_idx), mspec(shift_idx + 1)]
    h_spec = pl.BlockSpec((tm, D_MODEL), lambda i: (i, 0))
    h_shape = jax.ShapeDtypeStruct((N_TOK, D_MODEL), BF16)
    if w_router is None:
        return pl.pallas_call(
            _adaln_kernel, grid=(N_TOK // tm,), in_specs=in_specs, out_specs=h_spec, out_shape=h_shape,
            compiler_params=_cparams(("parallel",)), name="adaln",
        )(x, g.reshape(1, D_MODEL), mod, mod)
    wr = jnp.pad(w_router, ((0, 0), (0, LANES - N_EXPERTS)))
    return pl.pallas_call(
        _adaln_router_kernel, grid=(N_TOK // tm,),
        in_specs=in_specs + [pl.BlockSpec((D_MODEL, LANES), lambda i: (0, 0))],
        out_specs=[h_spec, pl.BlockSpec((tm, LANES), lambda i: (i, 0))],
        out_shape=[jax.ShapeDtypeStruct((N_TOK, D_MODEL), F32), jax.ShapeDtypeStruct((N_TOK, LANES), F32)],
        compiler_params=_cparams(("parallel",)), name="adaln_router",
    )(x, g.reshape(1, D_MODEL), mod, mod, wr)


def _proj_kernel(a_ref, w_ref, o_ref, *, gate):
    acc = jnp.dot(a_ref[...], w_ref[0].astype(BF16), preferred_element_type=F32)
    if gate:
        acc = _sigmoid(acc)
    o_ref[...] = acc.astype(o_ref.dtype)


def _proj(h, w_in, l, col0, n_out, tn, out_dtype, gate):
    tm = 2048 if gate else 1024
    return pl.pallas_call(
        functools.partial(_proj_kernel, gate=gate),
        grid=(N_TOK // tm, n_out // tn),
        in_specs=[pl.BlockSpec((tm, D_MODEL), lambda i, j: (i, 0)),
                  pl.BlockSpec((pl.Element(1), pl.Element(D_MODEL), pl.Element(tn)),
                               lambda i, j: (l, 0, pl.multiple_of(col0 + j * tn, LANES)))],
        out_specs=pl.BlockSpec((tm, tn), lambda i, j: (i, j)),
        out_shape=jax.ShapeDtypeStruct((N_TOK, n_out), out_dtype),
        compiler_params=_cparams(("parallel", "parallel")),
        name="proj_gate" if gate else "proj_qkv",
    )(h, w_in)


def _head_norm(x, gain, bd):
    sq = x * x
    hi = sq.astype(BF16)
    lo = (sq - hi.astype(F32)).astype(BF16)
    ms = jnp.dot(hi, bd, preferred_element_type=F32) + jnp.dot(lo, bd, preferred_element_type=F32)
    return x * lax.rsqrt(ms + RMS_EPS) * gain


def _rope(x, cos, sin_signed, first_half):
    swapped = jnp.where(first_half, pltpu.roll(x, LANES - HEAD_DIM // 2, 1), pltpu.roll(x, HEAD_DIM // 2, 1))
    return x * cos + swapped * sin_signed


def _qk_post_kernel(p_ref, cos_ref, sin_ref, ga_ref, gc_ref,
                    qa_ref, ka_ref, va_ref, fb_ref, qc_ref, kc_ref, vc_ref):
    r = lax.broadcasted_iota(jnp.int32, (LANES, LANES), 0) // HEAD_DIM
    c = lax.broadcasted_iota(jnp.int32, (LANES, LANES), 1) // HEAD_DIM
    bd = jnp.where(r == c, 1.0 / HEAD_DIM, 0.0).astype(BF16)
    lane = lax.broadcasted_iota(jnp.int32, (1, LANES), 1)
    first_half = (lane % HEAD_DIM) < HEAD_DIM // 2
    cos = cos_ref[...]
    sin = sin_ref[...]
    gqa, gka = ga_ref[0:1, :], ga_ref[1:2, :]
    gqc, gkc = gc_ref[0:1, :], gc_ref[1:2, :]
    for s in range(A_Q // LANES):
        sl = slice(s * LANES, (s + 1) * LANES)
        qa_ref[:, sl] = _rope(_head_norm(p_ref[:, sl], gqa, bd), cos, sin, first_half).astype(BF16)
    off = A_Q
    ka_ref[...] = _rope(_head_norm(p_ref[:, off:off + A_KV], gka, bd), cos, sin, first_half)
    off += A_KV
    va_ref[...] = p_ref[:, off:off + A_KV]
    off += A_KV
    fb_ref[...] = p_ref[:, off:off + B_WIDTH].astype(BF16)
    off += B_WIDTH
    for s in range(C_W // LANES):
        sl = slice(s * LANES, (s + 1) * LANES)
        qc_ref[:, sl] = _head_norm(p_ref[:, off + s * LANES:off + (s + 1) * LANES], gqc, bd).astype(BF16)
    off += C_W
    for s in range(C_W // LANES):
        sl = slice(s * LANES, (s + 1) * LANES)
        kc_ref[:, sl] = _head_norm(p_ref[:, off + s * LANES:off + (s + 1) * LANES], gkc, bd)
    off += C_W
    vc_ref[...] = p_ref[:, off:off + C_W]


def _qk_post(p, cos_t, sin_t, qk_a, qk_c):
    tm = 512
    row = lambda w: pl.BlockSpec((tm, w), lambda i: (i, 0))
    widths = (A_Q, A_KV, A_KV, B_WIDTH, C_W, C_W, C_W)
    dtypes = (BF16, F32, F32, BF16, BF16, F32, F32)
    return pl.pallas_call(
        _qk_post_kernel, grid=(N_TOK // tm,),
        in_specs=[row(QKV_WIDTH), row(LANES), row(LANES),
                  pl.BlockSpec((2, LANES), lambda i: (0, 0)), pl.BlockSpec((2, LANES), lambda i: (0, 0))],
        out_specs=[row(w) for w in widths],
        out_shape=[jax.ShapeDtypeStruct((N_TOK, w), d) for w, d in zip(widths, dtypes)],
        compiler_params=_cparams(("parallel",)), name="qk_post",
    )(p, cos_t, sin_t, jnp.tile(qk_a, (1, 2)), jnp.tile(qk_c, (1, 2)))


def _nt_dot(a, b):
    return lax.dot_general(a, b, (((1,), (1,)), ((), ())), preferred_element_type=F32)


def _softmax_pv(parts, sink):
    m = parts[0][0].max(axis=-1, keepdims=True)
    for s, _ in parts[1:]:
        m = jnp.maximum(m, s.max(axis=-1, keepdims=True))
    if sink is not None:
        m = jnp.maximum(m, sink)
    den = jnp.exp(sink - m) if sink is not None else 0.0
    o = None
    for s, v in parts:
        e = jnp.exp(s - m)
        den = den + e.sum(axis=-1, keepdims=True)
        pv = jnp.dot(e.astype(BF16), v, preferred_element_type=F32)
        o = pv if o is None else o + pv
    return o / den


def _ctx_attn_kernel(sink_ref, qa_ref, ka_ref, va_ref, qc_ref, kc_ref, vc_ref, oa_ref, oc_ref, *, l):
    for h in range(A_HEADS):
        g = h // A_GROUP
        q = qa_ref[:, h * HEAD_DIM:(h + 1) * HEAD_DIM] * SCALE
        k = ka_ref[:, g * HEAD_DIM:(g + 1) * HEAD_DIM].astype(BF16)
        v = va_ref[:, g * HEAD_DIM:(g + 1) * HEAD_DIM].astype(BF16)
        o = _softmax_pv([(_nt_dot(q, k), v)], sink_ref[l, h])
        oa_ref[:, h * HEAD_DIM:(h + 1) * HEAD_DIM] = o.astype(BF16)
    for h in range(C_HEADS):
        sl = slice(h * HEAD_DIM, (h + 1) * HEAD_DIM)
        o = _softmax_pv([(_nt_dot(qc_ref[:, sl] * SCALE, kc_ref[:, sl].astype(BF16)), vc_ref[:, sl].astype(BF16))], None)
        oc_ref[:, sl] = o.astype(BF16)


def _ctx_attn(sink_a, qa, ka, va, qc, kc, vc, l):
    blk = lambda w: pl.BlockSpec((SEQ, w), lambda b: (b, 0))
    return pl.pallas_call(
        functools.partial(_ctx_attn_kernel, l=l), grid=(BATCH,),
        in_specs=[pl.BlockSpec(memory_space=pltpu.SMEM),
                  blk(A_Q), blk(A_KV), blk(A_KV), blk(C_W), blk(C_W), blk(C_W)],
        out_specs=[blk(A_Q), blk(C_W)],
        out_shape=[jax.ShapeDtypeStruct((N_PROMPT, A_Q), BF16), jax.ShapeDtypeStruct((N_PROMPT, C_W), BF16)],
        compiler_params=_cparams(("parallel",)), name="ctx_attn",
    )(sink_a, qa, ka, va, qc, kc, vc)


def _win_attn_kernel(sink_ref, q_ref, kp_ref, kc_ref, kn_ref, vp_ref, vc_ref, vn_ref, ck_ref, cv_ref, o_ref, *, l):
    t = pl.program_id(1)
    rows = A_GROUP * A_BLOCK
    qi = lax.broadcasted_iota(jnp.int32, (rows, 3 * A_BLOCK), 0) % A_BLOCK
    kj = lax.broadcasted_iota(jnp.int32, (rows, 3 * A_BLOCK), 1) - A_BLOCK
    kpos = t * A_BLOCK + kj
    valid = (jnp.abs(kj - qi) <= A_WINDOW) & (kpos >= 0) & (kpos < DEC_SEQ)
    for g in range(A_KV_HEADS):
        sl = slice(g * HEAD_DIM, (g + 1) * HEAD_DIM)
        q = jnp.concatenate([q_ref[:, (g * A_GROUP + i) * HEAD_DIM:(g * A_GROUP + i + 1) * HEAD_DIM]
                             for i in range(A_GROUP)], axis=0) * SCALE
        sink = jnp.concatenate([jnp.full((A_BLOCK, 1), sink_ref[l, g * A_GROUP + i], F32)
                                for i in range(A_GROUP)], axis=0)
        k_loc = jnp.concatenate([kp_ref[:, sl], kc_ref[:, sl], kn_ref[:, sl]], axis=0).astype(BF16)
        v_loc = jnp.concatenate([vp_ref[:, sl], vc_ref[:, sl], vn_ref[:, sl]], axis=0).astype(BF16)
        s_loc = jnp.where(valid, _nt_dot(q, k_loc), NEG_INF)
        s_ctx = _nt_dot(q, ck_ref[:, sl].astype(BF16))
        o = _softmax_pv([(s_loc, v_loc), (s_ctx, cv_ref[:, sl].astype(BF16))], sink)
        for i in range(A_GROUP):
            h = g * A_GROUP + i
            o_ref[:, h * HEAD_DIM:(h + 1) * HEAD_DIM] = o[i * A_BLOCK:(i + 1) * A_BLOCK].astype(BF16)


def _win_attn(sink_a, qa, ka, va, cache_k, cache_v, l):
    nb = DEC_SEQ // A_BLOCK
    base = N_PROMPT // A_BLOCK

    def nbr(d):
        return lambda b, t: (base + b * nb + jnp.clip(t + d, 0, nb - 1), 0)

    kv = lambda d: pl.BlockSpec((A_BLOCK, A_KV), nbr(d))
    cache = pl.BlockSpec((None, None, PAST_LEN, A_KV), lambda b, t: (b, l, 0, 0))
    return pl.pallas_call(
        functools.partial(_win_attn_kernel, l=l), grid=(DEC_BATCH, nb),
        in_specs=[pl.BlockSpec(memory_space=pltpu.SMEM),
                  pl.BlockSpec((A_BLOCK, A_Q), nbr(0)),
                  kv(-1), kv(0), kv(1), kv(-1), kv(0), kv(1), cache, cache],
        out_specs=pl.BlockSpec((A_BLOCK, A_Q), lambda b, t: (b * nb + t, 0)),
        out_shape=jax.ShapeDtypeStruct((N_SAMPLE, A_Q), BF16),
        compiler_params=_cparams(("parallel", "parallel")), name="win_attn",
    )(sink_a, qa, ka, ka, ka, va, va, va, cache_k, cache_v)


def _nbr_attn_kernel(q_ref, kp_ref, kc_ref, kn_ref, vp_ref, vc_ref, vn_ref, ck_ref, cv_ref, tab_ref, o_ref,
                     bias_ref):
    j = pl.program_id(0)
    nb = pl.num_programs(0)
    slots = _nbr_row_slots()

    def build(cls):
        for h in range(C_HEADS):
            for qr in range(C_QROWS):
                for kk in range(3 * C_QROWS):
                    bias_ref[h, qr * GRID_W:(qr + 1) * GRID_W, kk * GRID_W:(kk + 1) * GRID_W] = (
                        tab_ref[h, slots[cls][qr][kk]])

    first_of_batch = pl.program_id(1) == 0
    for cls, at in enumerate((0, 1, nb - 1)):
        @pl.when(first_of_batch & (j == at))
        def _():
            build(cls)

    for h in range(C_HEADS):
        sl = slice(h * HEAD_DIM, (h + 1) * HEAD_DIM)
        q = q_ref[:, sl] * SCALE
        k_loc = jnp.concatenate([kp_ref[:, sl], kc_ref[:, sl], kn_ref[:, sl]], axis=0).astype(BF16)
        v_loc = jnp.concatenate([vp_ref[:, sl], vc_ref[:, sl], vn_ref[:, sl]], axis=0).astype(BF16)
        s_loc = _nt_dot(q, k_loc) + bias_ref[h]
        s_ctx = _nt_dot(q, ck_ref[:, sl].astype(BF16))
        o = _softmax_pv([(s_loc, v_loc), (s_ctx, cv_ref[:, sl].astype(BF16))], None)
        o_ref[:, sl] = o.astype(BF16)


def _nbr_attn(qc, kc, vc, cache_k, cache_v, bias, l):
    nb = DEC_SEQ // C_QBLOCK
    base = N_PROMPT // C_QBLOCK

    def nbr(d):
        return lambda j, b: (base + b * nb + jnp.clip(j + d, 0, nb - 1), 0)

    kv = lambda d: pl.BlockSpec((C_QBLOCK, C_W), nbr(d))
    cache = pl.BlockSpec((None, None, PAST_LEN, C_W), lambda j, b: (b, l, 0, 0))
    return pl.pallas_call(
        _nbr_attn_kernel, grid=(nb, DEC_BATCH),
        in_specs=[kv(0), kv(-1), kv(0), kv(1), kv(-1), kv(0), kv(1), cache, cache,
                  pl.BlockSpec((None, C_HEADS, C_DR_SLOTS, GRID_W, GRID_W), lambda j, b: (l, 0, 0, 0, 0))],
        out_specs=pl.BlockSpec((C_QBLOCK, C_W), lambda j, b: (b * nb + j, 0)),
        out_shape=jax.ShapeDtypeStruct((N_SAMPLE, C_W), BF16),
        scratch_shapes=[pltpu.VMEM((C_HEADS, C_QBLOCK, 3 * C_QBLOCK), F32)],
        compiler_params=_cparams(("arbitrary", "arbitrary")), name="nbr_attn",
    )(qc, kc, kc, kc, vc, vc, vc, cache_k, cache_v, bias)


def _nbr_bias_tables(rpb):
    qcol = np.arange(GRID_W)
    qcs = np.clip(qcol - C_WIN_COLS // 2, 0, GRID_W - C_WIN_COLS)
    kcol = np.arange(GRID_W)
    col_ok = (kcol[None, :] >= qcs[:, None]) & (kcol[None, :] < qcs[:, None] + C_WIN_COLS)
    dc = np.clip(kcol[None, :] - qcol[:, None], -(C_WIN_COLS - 1), C_WIN_COLS - 1) + C_WIN_COLS - 1
    onehot_dc = (dc.reshape(-1)[None, :] == np.arange(2 * C_WIN_COLS - 1)[:, None]).astype(np.float32)
    t = jnp.einsum('lhab,bx->lhax', rpb, jnp.asarray(onehot_dc), precision=lax.Precision.HIGHEST)
    t = jnp.where(jnp.asarray(col_ok.reshape(-1)), t, NEG_INF)
    t = jnp.concatenate([t, jnp.full((DEPTH, C_HEADS, 1, GRID_W * GRID_W), NEG_INF, F32)], axis=2)
    return t.reshape(DEPTH, C_HEADS, C_DR_SLOTS, GRID_W, GRID_W)


def _nbr_row_slots():
    rows = DEC_SEQ // GRID_W
    slots = []
    for j in (0, 3, rows // C_QROWS - 1):
        per_q = []
        for qr in range(C_QROWS):
            r = C_QROWS * j + qr
            rs = min(max(r - C_WIN_ROWS // 2, 0), rows - C_WIN_ROWS)
            per_k = []
            for kk in range(3 * C_QROWS):
                kabs = C_QROWS * (j - 1) + kk
                per_k.append(kabs - r + C_WIN_ROWS - 1 if rs <= kabs < rs + C_WIN_ROWS else C_DR_SLOTS - 1)
            per_q.append(per_k)
        slots.append(per_q)
    return slots


def _fourier_kernel(u_ref, bc_ref, bs_ref, cl_ref, sl_ref, o_ref, zc_ref, zs_ref):
    @pl.when(pl.program_id(1) == 0)
    def _():
        u = u_ref[...]
        zc_ref[...] = jnp.dot(u, bc_ref[...].astype(BF16), preferred_element_type=F32).astype(BF16)
        zs_ref[...] = jnp.dot(u, bs_ref[...].astype(BF16), preferred_element_type=F32).astype(BF16)

    o = (jnp.dot(cl_ref[...].astype(BF16), zc_ref[...], preferred_element_type=F32)
         - jnp.dot(sl_ref[...].astype(BF16), zs_ref[...], preferred_element_type=F32))
    o_ref[...] = o.astype(BF16)


def _dft_tables(n):
    k = np.arange(n)
    ang = 2.0 * np.pi * ((k[:, None] * k[None, :]) % n) / n
    return np.cos(ang) / np.sqrt(n), np.sin(ang) / np.sqrt(n)


def _channel_dft_tables():
    c, s = _dft_tables(B_GROUP_DIM)
    eye = np.eye(B_GROUPS)
    return np.kron(eye, c).astype(np.float32), np.kron(eye, s).astype(np.float32)


def _fourier(fb, n_batch, seq, row0, tr):
    cl, sl = (jnp.asarray(a.astype(np.float32)) for a in _dft_tables(seq))
    bc, bs = (jnp.asarray(a) for a in _channel_dft_tables())
    nt = seq // tr
    const = pl.BlockSpec((B_WIDTH, B_WIDTH), lambda b, t: (0, 0))
    return pl.pallas_call(
        _fourier_kernel, grid=(n_batch, nt),
        in_specs=[pl.BlockSpec((seq, B_WIDTH), lambda b, t: (row0 // seq + b, 0)), const, const,
                  pl.BlockSpec((tr, seq), lambda b, t: (t, 0)), pl.BlockSpec((tr, seq), lambda b, t: (t, 0))],
        out_specs=pl.BlockSpec((tr, B_WIDTH), lambda b, t: (b * nt + t, 0)),
        out_shape=jax.ShapeDtypeStruct((n_batch * seq, B_WIDTH), BF16),
        scratch_shapes=[pltpu.VMEM((seq, B_WIDTH), BF16), pltpu.VMEM((seq, B_WIDTH), BF16)],
        compiler_params=_cparams(("parallel", "arbitrary")), name=f"fourier_{seq}",
    )(fb, bc, bs, cl, sl)


def _merge_kernel(oap_ref, obp_ref, ocp_ref, oas_ref, obs_ref, ocs_ref, gt_ref, x_ref,
                  wa_ref, wb_ref, wc_ref, wo_ref, g1_ref, o_ref):
    ctx = pl.program_id(0) < N_PROMPT // x_ref.shape[0]
    oa = jnp.where(ctx, oap_ref[...], oas_ref[...])
    ob = jnp.where(ctx, obp_ref[...], obs_ref[...])
    oc = jnp.where(ctx, ocp_ref[...], ocs_ref[...])
    ya = jnp.dot(oa, wa_ref[...].astype(BF16), preferred_element_type=F32)
    yb = jnp.dot(ob, wb_ref[...].astype(BF16), preferred_element_type=F32)
    yc = jnp.dot(oc, wc_ref[...].astype(BF16), preferred_element_type=F32)
    d = D_MODEL
    m = (gt_ref[:, 0:d].astype(F32) * ya + gt_ref[:, d:2 * d].astype(F32) * yb
         + gt_ref[:, 2 * d:3 * d].astype(F32) * yc)
    y = jnp.dot(m.astype(BF16), wo_ref[...].astype(BF16), preferred_element_type=F32)
    o_ref[...] = x_ref[...] + g1_ref[...] * y


def _merge(branches_p, branches_s, gates, x, wa, wb, wc, wo, mod, l):
    tm = 512
    n_p = N_PROMPT // tm
    row = lambda w: pl.BlockSpec((tm, w), lambda i: (i, 0))
    row_p = lambda w: pl.BlockSpec((tm, w), lambda i: (jnp.minimum(i, n_p - 1), 0))
    row_s = lambda w: pl.BlockSpec((tm, w), lambda i: (jnp.maximum(i - n_p, 0), 0))
    const = lambda r, c: pl.BlockSpec((r, c), lambda i: (0, 0))
    return pl.pallas_call(
        _merge_kernel, grid=(N_TOK // tm,),
        in_specs=[row_p(A_Q), row_p(B_WIDTH), row_p(C_W), row_s(A_Q), row_s(B_WIDTH), row_s(C_W),
                  row(GATE_WIDTH), row(D_MODEL),
                  const(A_Q, D_MODEL), const(B_WIDTH, D_MODEL), const(C_W, D_MODEL), const(D_MODEL, D_MODEL),
                  pl.BlockSpec((None, None, 1, D_MODEL), lambda i: (l, _cond_row(i, tm), 0, 2))],
        out_specs=row(D_MODEL),
        out_shape=jax.ShapeDtypeStruct((N_TOK, D_MODEL), F32),
        compiler_params=_cparams(("parallel",)), name="merge",
    )(*branches_p, *branches_s, gates, x, wa, wb, wc, wo, mod)


def _ffn_kernel(h_ref, x_ref, wg_ref, wu_ref, wd_ref, g2_ref, o_ref):
    f = pl.program_id(1)
    h = h_ref[...]
    g = jnp.dot(h, wg_ref[...].astype(BF16), preferred_element_type=F32)
    u = jnp.dot(h, wu_ref[...].astype(BF16), preferred_element_type=F32)
    a = (g * _sigmoid(g) * u).astype(BF16)
    d = jnp.dot(a, wd_ref[...].astype(BF16), preferred_element_type=F32)

    @pl.when(f == 0)
    def _():
        o_ref[...] = d

    @pl.when(f > 0)
    def _():
        o_ref[...] += d

    @pl.when(f == pl.num_programs(1) - 1)
    def _():
        o_ref[...] = x_ref[...] + g2_ref[...] * o_ref[...]


def _ffn(h, x, wg, wu, wd, mod, l):
    tm, tf = 2048, 256
    return pl.pallas_call(
        _ffn_kernel, grid=(N_TOK // tm, D_FF // tf),
        in_specs=[pl.BlockSpec((tm, D_MODEL), lambda i, f: (i, 0)),
                  pl.BlockSpec((tm, D_MODEL), lambda i, f: (i, 0), pipeline_mode=pl.Buffered(1)),
                  pl.BlockSpec((D_MODEL, tf), lambda i, f: (0, f)),
                  pl.BlockSpec((D_MODEL, tf), lambda i, f: (0, f)),
                  pl.BlockSpec((tf, D_MODEL), lambda i, f: (f, 0)),
                  pl.BlockSpec((None, None, 1, D_MODEL), lambda i, f: (l, _cond_row(i, tm), 0, 5))],
        out_specs=pl.BlockSpec((tm, D_MODEL), lambda i, f: (i, 0)),
        out_shape=jax.ShapeDtypeStruct((N_TOK, D_MODEL), F32),
        compiler_params=_cparams(("parallel", "arbitrary")), name="ffn",
    )(h, x, wg, wu, wd, mod)


TOP_K = 2
MOE_TILE = 256
MOE_TILES = TOP_K * N_TOK // MOE_TILE + N_EXPERTS
MOE_ROWS = MOE_TILES * MOE_TILE
MOE_CHUNK = 10
MOE_CHUNKS = MOE_TILES // MOE_CHUNK + N_EXPERTS
MOE_TF = 896
ROUTE_TM = 512
DISPATCH_TM = 256
COMBINE_TM = 256
ROW_COPY_UNROLL = 8


def _route_kernel(lg_ref, o_ref, cnt_ref, base_ref, tri_ref):
    tm = lg_ref.shape[0]

    @pl.when(pl.program_id(0) == 0)
    def _():
        base_ref[...] = jnp.zeros_like(base_ref)
        r = lax.broadcasted_iota(jnp.int32, (tm, tm), 0)
        c = lax.broadcasted_iota(jnp.int32, (tm, tm), 1)
        tri_ref[...] = jnp.where(r > c, 1.0, 0.0).astype(BF16)

    lane = lax.broadcasted_iota(jnp.int32, lg_ref.shape, 1).astype(F32)
    lg = jnp.where(lane < N_EXPERTS, lg_ref[...], -jnp.inf)
    m1 = lg.max(axis=-1, keepdims=True)
    i1 = jnp.where(lg == m1, lane, float(LANES)).min(axis=-1, keepdims=True)
    rest = jnp.where(lane == i1, -jnp.inf, lg)
    m2 = rest.max(axis=-1, keepdims=True)
    i2 = jnp.where(rest == m2, lane, float(LANES)).min(axis=-1, keepdims=True)
    e2 = jnp.exp(m2 - m1)
    w1 = 1.0 / (1.0 + e2)
    w2 = e2 / (1.0 + e2)

    oh1 = jnp.where(lane == i1, 1.0, 0.0)
    oh2 = jnp.where(lane == i2, 1.0, 0.0)
    pre1 = jnp.dot(tri_ref[...], oh1.astype(BF16), preferred_element_type=F32)
    pre2 = jnp.dot(tri_ref[...], oh2.astype(BF16), preferred_element_type=F32)
    c1 = jnp.sum(oh1, axis=0, keepdims=True)
    c2 = jnp.sum(oh2, axis=0, keepdims=True)
    base = base_ref[...]
    rank1 = jnp.sum(oh1 * (base + pre1), axis=-1, keepdims=True)
    rank2 = jnp.sum(oh2 * (base + c1 + pre2), axis=-1, keepdims=True)
    base_ref[...] = base + c1 + c2

    cols = (i1, i2, rank1, rank2, w1, w2)
    out = jnp.zeros(lg_ref.shape, F32)
    for j, col in enumerate(cols):
        out = jnp.where(lane == float(j), col, out)
    o_ref[...] = out
    cnt_ref[...] = jnp.broadcast_to(base + c1 + c2, cnt_ref.shape)


def _dispatch_kernel(pos_ref, last_ref, nt_ref, h_ref, xs_hbm, zero_ref, zsem, sem):
    i = pl.program_id(0)
    tm = h_ref.shape[0]

    @pl.when(i == 0)
    def _():
        zero_ref[...] = jnp.zeros_like(zero_ref)

        def zero_copy(tile):
            row0 = pl.multiple_of(tile * MOE_TILE, MOE_TILE)
            return pltpu.make_async_copy(zero_ref, xs_hbm.at[pl.ds(row0, MOE_TILE)], zsem)

        def for_zeroed_tiles(fn):
            for e in range(N_EXPERTS):
                @pl.when(last_ref[e] >= 0)
                def _():
                    fn(zero_copy(last_ref[e]))

                tail = MOE_TILES - 1 - e

                @pl.when(tail >= nt_ref[0])
                def _():
                    fn(zero_copy(tail))

        for_zeroed_tiles(lambda cp: cp.start())
        for_zeroed_tiles(lambda cp: cp.wait())

    def row_copy(t, k):
        dst = xs_hbm.at[pl.ds(pos_ref[TOP_K * (i * tm + t) + k], 1)]
        return pltpu.make_async_copy(h_ref.at[pl.ds(t, 1)], dst, sem)

    def issue(t, carry):
        for k in range(TOP_K):
            row_copy(t, k).start()
        return carry

    def drain(t, carry):
        for k in range(TOP_K):
            row_copy(t, k).wait()
        return carry

    lax.fori_loop(0, tm, issue, 0, unroll=ROW_COPY_UNROLL)
    lax.fori_loop(0, tm, drain, 0, unroll=ROW_COPY_UNROLL)


def _expert_kernel(ce_ref, ct_ref, cn_ref, nch_ref, nt_ref, xs_hbm, wg_ref, wu_ref, wd_ref, y_hbm,
                   acc_ref, xbuf_ref, wgb_ref, wub_ref, wdb_ref, xsem, osem):
    c = pl.program_id(0)
    f = pl.program_id(1)
    last_f = pl.num_programs(1) - 1

    def tile_rows(tile):
        return pl.ds(pl.multiple_of(tile * MOE_TILE, MOE_TILE), MOE_TILE)

    def out_copy(slot, tile):
        return pltpu.make_async_copy(acc_ref.at[slot], y_hbm.at[tile_rows(tile)], osem)

    @pl.when((c == 0) & (f == 0))
    def _():
        acc_ref[0] = jnp.zeros((MOE_TILE, D_MODEL), F32)
        for e in range(N_EXPERTS):
            tail = MOE_TILES - 1 - e

            @pl.when(tail >= nt_ref[0])
            def _():
                cp = out_copy(0, tail)
                cp.start()
                cp.wait()

    @pl.when(c < nch_ref[0])
    def _():
        n = cn_ref[c]
        t0 = ct_ref[c]

        def x_copy(j):
            return pltpu.make_async_copy(xs_hbm.at[tile_rows(t0 + j)], xbuf_ref.at[j], xsem.at[j])

        @pl.when(f == 0)
        def _():
            def fetch(j, carry):
                x_copy(j).start()
                return carry
            lax.fori_loop(0, n, fetch, 0)

        wgb_ref[...] = wg_ref[...].astype(BF16)
        wub_ref[...] = wu_ref[...].astype(BF16)
        wdb_ref[...] = wd_ref[...].astype(BF16)

        def tile_step(j, carry):
            @pl.when(f == 0)
            def _():
                x_copy(j).wait()

            x = xbuf_ref[j].astype(BF16)
            g = jnp.dot(x, wgb_ref[...], preferred_element_type=F32)
            u = jnp.dot(x, wub_ref[...], preferred_element_type=F32)
            a = (g * _sigmoid(g) * u).astype(BF16)
            d = jnp.dot(a, wdb_ref[...], preferred_element_type=F32)

            @pl.when(f == 0)
            def _():
                acc_ref[j] = d

            @pl.when(f > 0)
            def _():
                acc_ref[j] += d

            @pl.when(f == last_f)
            def _():
                @pl.when(j > 0)
                def _():
                    out_copy(j - 1, t0 + j - 1).wait()

                out_copy(j, t0 + j).start()

            return carry

        lax.fori_loop(0, n, tile_step, 0)

        @pl.when(f == last_f)
        def _():
            out_copy(n - 1, t0 + n - 1).wait()


def _combine_kernel(pos_ref, route_ref, x_ref, g2_ref, y_hbm, op_ref, os_ref, buf_ref, sem):
    i = pl.program_id(0)
    n = pl.num_programs(0)
    tm = x_ref.shape[0]
    slot = i % 2

    def row_copy(step, s, t, k):
        src = y_hbm.at[pl.ds(pos_ref[TOP_K * (step * tm + t) + k], 1)]
        return pltpu.make_async_copy(src, buf_ref.at[s, k, pl.ds(t, 1)], sem.at[s])

    def issue(step, s):
        def body(t, carry):
            for k in range(TOP_K):
                row_copy(step, s, t, k).start()
            return carry
        lax.fori_loop(0, tm, body, 0, unroll=ROW_COPY_UNROLL)

    @pl.when(i == 0)
    def _():
        issue(0, 0)

    @pl.when(i + 1 < n)
    def _():
        issue(i + 1, 1 - slot)

    def wait_body(t, carry):
        for k in range(TOP_K):
            row_copy(i, slot, t, k).wait()
        return carry

    lax.fori_loop(0, tm, wait_body, 0, unroll=ROW_COPY_UNROLL)

    lane = lax.broadcasted_iota(jnp.int32, route_ref.shape, 1)
    rt = route_ref[...]
    w1 = jnp.sum(jnp.where(lane == 2 * TOP_K, rt, 0.0), axis=-1, keepdims=True)
    w2 = jnp.sum(jnp.where(lane == 2 * TOP_K + 1, rt, 0.0), axis=-1, keepdims=True)
    out = x_ref[...] + g2_ref[...] * (w1 * buf_ref[slot, 0] + w2 * buf_ref[slot, 1])

    @pl.when(i < N_PROMPT // tm)
    def _():
        op_ref[...] = out

    @pl.when(i >= N_PROMPT // tm)
    def _():
        os_ref[...] = out


def _moe(h, x, logits, wg, wu, wd, mod, l):
    route, cnt = pl.pallas_call(
        _route_kernel, grid=(N_TOK // ROUTE_TM,),
        in_specs=[pl.BlockSpec((ROUTE_TM, LANES), lambda i: (i, 0))],
        out_specs=[pl.BlockSpec((ROUTE_TM, LANES), lambda i: (i, 0)), pl.BlockSpec((8, LANES), lambda i: (0, 0))],
        out_shape=[jax.ShapeDtypeStruct((N_TOK, LANES), F32), jax.ShapeDtypeStruct((8, LANES), F32)],
        scratch_shapes=[pltpu.VMEM((1, LANES), F32), pltpu.VMEM((ROUTE_TM, ROUTE_TM), BF16)],
        compiler_params=_cparams(("arbitrary",)), name="route",
    )(logits)

    expert = route[:, 0:TOP_K].astype(jnp.int32)
    rank = route[:, TOP_K:2 * TOP_K].astype(jnp.int32)
    n_sub = (cnt[0, :N_EXPERTS].astype(jnp.int32) + MOE_TILE - 1) // MOE_TILE
    end = jnp.cumsum(n_sub)
    start = end - n_sub
    start_of = jnp.sum(jnp.where(expert[:, :, None] == jnp.arange(N_EXPERTS), start, 0), axis=-1)
    pos = (start_of * MOE_TILE + rank).reshape(-1)
    n_tiles = end[N_EXPERTS - 1:]
    last_tile = jnp.where(n_sub > 0, end - 1, -1).astype(jnp.int32)

    xs = pl.pallas_call(
        _dispatch_kernel,
        grid_spec=pltpu.PrefetchScalarGridSpec(
            num_scalar_prefetch=3, grid=(N_TOK // DISPATCH_TM,),
            in_specs=[pl.BlockSpec((DISPATCH_TM, D_MODEL), lambda i, p, lt, nt: (i, 0))],
            out_specs=pl.BlockSpec(memory_space=pl.ANY),
            scratch_shapes=[pltpu.VMEM((MOE_TILE, D_MODEL), F32), pltpu.SemaphoreType.DMA(()),
                            pltpu.SemaphoreType.DMA(())]),
        out_shape=jax.ShapeDtypeStruct((MOE_ROWS, D_MODEL), F32),
        compiler_params=_cparams(("arbitrary",)), name="moe_dispatch",
    )(pos, last_tile, n_tiles, h)

    n_chunk = (n_sub + MOE_CHUNK - 1) // MOE_CHUNK
    chunk_end = jnp.cumsum(n_chunk)
    cidx = jnp.arange(MOE_CHUNKS)
    c_expert = jnp.minimum(jnp.sum(cidx[:, None] >= chunk_end[None, :], axis=1), N_EXPERTS - 1).astype(jnp.int32)
    c_k = cidx - (chunk_end - n_chunk)[c_expert]
    c_tile0 = (start[c_expert] + c_k * MOE_CHUNK).astype(jnp.int32)
    c_ntiles = jnp.clip(n_sub[c_expert] - c_k * MOE_CHUNK, 0, MOE_CHUNK).astype(jnp.int32)
    n_chunks = chunk_end[N_EXPERTS - 1:]

    nf = D_FF_EXPERT // MOE_TF

    def w_idx(c, f, ce, nch):
        live = c < nch[0]
        return ce[jnp.minimum(c, nch[0] - 1)], jnp.where(live, f, nf - 1)

    def up_map(c, f, ce, ct, cn, nch, nt):
        e, ff = w_idx(c, f, ce, nch)
        return (e, 0, ff)

    def down_map(c, f, ce, ct, cn, nch, nt):
        e, ff = w_idx(c, f, ce, nch)
        return (e, ff, 0)

    y = pl.pallas_call(
        _expert_kernel,
        grid_spec=pltpu.PrefetchScalarGridSpec(
            num_scalar_prefetch=5, grid=(MOE_CHUNKS, nf),
            in_specs=[pl.BlockSpec(memory_space=pl.ANY),
                      pl.BlockSpec((None, D_MODEL, MOE_TF), up_map),
                      pl.BlockSpec((None, D_MODEL, MOE_TF), up_map),
                      pl.BlockSpec((None, MOE_TF, D_MODEL), down_map)],
            out_specs=pl.BlockSpec(memory_space=pl.ANY),
            scratch_shapes=[pltpu.VMEM((MOE_CHUNK, MOE_TILE, D_MODEL), F32),
                            pltpu.VMEM((MOE_CHUNK, MOE_TILE, D_MODEL), F32),
                            pltpu.VMEM((D_MODEL, MOE_TF), BF16), pltpu.VMEM((D_MODEL, MOE_TF), BF16),
                            pltpu.VMEM((MOE_TF, D_MODEL), BF16),
                            pltpu.SemaphoreType.DMA((MOE_CHUNK,)), pltpu.SemaphoreType.DMA(())]),
        out_shape=jax.ShapeDtypeStruct((MOE_ROWS, D_MODEL), F32),
        compiler_params=_cparams(("arbitrary", "arbitrary")), name="moe_experts",
    )(c_expert, c_tile0, c_ntiles, n_chunks, n_tiles, xs, wg, wu, wd)

    tm = COMBINE_TM
    return pl.pallas_call(
        _combine_kernel,
        grid_spec=pltpu.PrefetchScalarGridSpec(
            num_scalar_prefetch=1, grid=(N_TOK // tm,),
            in_specs=[pl.BlockSpec((tm, LANES), lambda i, p: (i, 0)),
                      pl.BlockSpec((tm, D_MODEL), lambda i, p: (i, 0)),
                      pl.BlockSpec((None, None, 1, D_MODEL), lambda i, p: (l, _cond_row(i, tm), 0, 5)),
                      pl.BlockSpec(memory_space=pl.ANY)],
            out_specs=[pl.BlockSpec((tm, D_MODEL), lambda i, p: (jnp.minimum(i, N_PROMPT // tm - 1), 0)),
                       pl.BlockSpec((tm, D_MODEL), lambda i, p: (jnp.maximum(i - N_PROMPT // tm, 0), 0))],
            scratch_shapes=[pltpu.VMEM((2, TOP_K, tm, D_MODEL), F32), pltpu.SemaphoreType.DMA((2,))]),
        out_shape=[jax.ShapeDtypeStruct((N_PROMPT, D_MODEL), F32), jax.ShapeDtypeStruct((N_SAMPLE, D_MODEL), F32)],
        compiler_params=_cparams(("arbitrary",)), name="moe_combine",
    )(pos, route, x, mod, y)


def _rope_tables():
    t = jnp.arange(DEC_SEQ)
    row = (t // GRID_W).astype(F32)
    col = (t % GRID_W).astype(F32)
    n_freq = HEAD_DIM // 4
    inv = ROPE_BASE ** (-jnp.arange(n_freq, dtype=F32) / n_freq)
    ang = jnp.concatenate([row[:, None] * inv, col[:, None] * inv], axis=-1)
    cos, sin = jnp.cos(ang), jnp.sin(ang)
    cos_h = jnp.concatenate([cos, cos], axis=-1)
    sin_h = jnp.concatenate([-sin, sin], axis=-1)
    cos_l = jnp.tile(jnp.concatenate([cos_h, cos_h], axis=-1), (DEC_BATCH, 1))
    sin_l = jnp.tile(jnp.concatenate([sin_h, sin_h], axis=-1), (DEC_BATCH, 1))
    cos_t = jnp.concatenate([jnp.ones((N_PROMPT, LANES), F32), cos_l], axis=0)
    sin_t = jnp.concatenate([jnp.zeros((N_PROMPT, LANES), F32), sin_l], axis=0)
    return cos_t, sin_t


def kernel(x_prompt, x_sample, cache_a_k, cache_a_v, cache_c_k, cache_c_v, c, c_ctx, w_mod, b_mod, norm1_g, norm2_g, w_in, qk_norm_a, qk_norm_c, sink_a, rpb_c, w_branch_a, w_branch_b, w_branch_c, w_out, w_ff_gate, w_ff_up, w_ff_down, w_router, w_exp_gate, w_exp_up, w_exp_down):
    x = jnp.concatenate([x_prompt.reshape(N_PROMPT, D_MODEL), x_sample.reshape(N_SAMPLE, D_MODEL)], axis=0)
    cond = jnp.concatenate([c_ctx[None, :], c], axis=0)
    cond_t = jnp.broadcast_to(cond[:, :, None], (N_COND, D_MODEL, LANES))
    mod = _modulation(cond_t, w_mod, b_mod)
    cos_t, sin_t = _rope_tables()
    bias = _nbr_bias_tables(rpb_c)
    ck_a = cache_a_k.reshape(DEC_BATCH, DEPTH, PAST_LEN, A_KV)
    cv_a = cache_a_v.reshape(DEC_BATCH, DEPTH, PAST_LEN, A_KV)
    ck_c = cache_c_k.reshape(DEC_BATCH, DEPTH, PAST_LEN, C_W)
    cv_c = cache_c_v.reshape(DEC_BATCH, DEPTH, PAST_LEN, C_W)

    new_ak, new_av, new_ck, new_cv = [], [], [], []
    for l in range(DEPTH):
        h = _adaln(x, norm1_g[l], mod, l, 0)
        p = _proj(h, w_in, l, 0, QKV_WIDTH, QKV_WIDTH // 2, F32, gate=False)
        gates = _proj(h, w_in, l, QKV_WIDTH, GATE_WIDTH, D_MODEL, BF16, gate=True)
        qa, ka, va, fb, qc, kc, vc = _qk_post(p, cos_t, sin_t, qk_norm_a[l], qk_norm_c[l])
        oa_p, oc_p = _ctx_attn(sink_a, qa, ka, va, qc, kc, vc, l)
        oa_s = _win_attn(sink_a, qa, ka, va, ck_a, cv_a, l)
        oc_s = _nbr_attn(qc, kc, vc, ck_c, cv_c, bias, l)
        ob_p = _fourier(fb, BATCH, SEQ, 0, SEQ)
        ob_s = _fourier(fb, DEC_BATCH, DEC_SEQ, N_PROMPT, 512)
        x = _merge((oa_p, ob_p, oc_p), (oa_s, ob_s, oc_s), gates, x,
                   w_branch_a[l], w_branch_b[l], w_branch_c[l], w_out[l], mod, l)
        i = l // 2
        if l % 2 == 0:
            h2 = _adaln(x, norm2_g[l], mod, l, 3)
            x = _ffn(h2, x, w_ff_gate[i], w_ff_up[i], w_ff_down[i], mod, l)
        else:
            h2, logits = _adaln(x, norm2_g[l], mod, l, 3, w_router=w_router[i])
            xp, xs = _moe(h2, x, logits, w_exp_gate[i], w_exp_up[i], w_exp_down[i], mod, l)
        new_ak.append(ka[:N_PROMPT].reshape(BATCH, SEQ, A_KV_HEADS, HEAD_DIM))
        new_av.append(va[:N_PROMPT].reshape(BATCH, SEQ, A_KV_HEADS, HEAD_DIM))
        new_ck.append(kc[:N_PROMPT].reshape(BATCH, SEQ, C_HEADS, HEAD_DIM))
        new_cv.append(vc[:N_PROMPT].reshape(BATCH, SEQ, C_HEADS, HEAD_DIM))

    return (xp.reshape(BATCH, SEQ, D_MODEL), xs.reshape(DEC_BATCH, DEC_SEQ, D_MODEL),
            jnp.stack(new_ak, axis=1), jnp.stack(new_av, axis=1),
            jnp.stack(new_ck, axis=1), jnp.stack(new_cv, axis=1))
```

```python
import functools

import numpy as np
import jax
import jax.numpy as jnp
from jax import lax
from jax.experimental import pallas as pl
from jax.experimental.pallas import tpu as pltpu

F32 = jnp.float32
BF16 = jnp.bfloat16

D_MODEL = 1024
BATCH = 16
SEQ = 256
DEPTH = 2
DEC_BATCH = 2
DEC_SEQ = 2048
PAST_LEN = 512
GRID_W = 64
HEAD_DIM = 64
SCALE = HEAD_DIM ** -0.5
A_HEADS = 8
A_KV_HEADS = 2
A_GROUP = A_HEADS // A_KV_HEADS
A_WINDOW = 128
A_BLOCK = 128
B_GROUPS = 8
B_GROUP_DIM = 64
B_WIDTH = B_GROUPS * B_GROUP_DIM
C_HEADS = 8
C_WIN_ROWS = 8
C_WIN_COLS = 16
A_Q = A_HEADS * HEAD_DIM
A_KV = A_KV_HEADS * HEAD_DIM
C_W = C_HEADS * HEAD_DIM
QKV_WIDTH = A_Q + 2 * A_KV + B_WIDTH + 3 * C_W
GATE_WIDTH = 3 * D_MODEL
D_FF = 2816
N_EXPERTS = 8
D_FF_EXPERT = 3584
ROPE_BASE = 10000.0
RMS_EPS = 1e-6
NEG_INF = -1e30

N_PROMPT = BATCH * SEQ
N_SAMPLE = DEC_BATCH * DEC_SEQ
N_TOK = N_PROMPT + N_SAMPLE
N_COND = 1 + DEC_BATCH
LANES = 128
C_QROWS = 4
C_QBLOCK = C_QROWS * GRID_W
C_DR_SLOTS = 2 * C_WIN_ROWS
VMEM_LIMIT = 56 * 1024 * 1024


def _cparams(sem):
    return pltpu.CompilerParams(dimension_semantics=sem, vmem_limit_bytes=VMEM_LIMIT)


def _sigmoid(x):
    return 1.0 / (1.0 + jnp.exp(-x))


def _cond_row(tile, tm):
    return jnp.maximum(tile * tm // DEC_SEQ - 1, 0)


def _mod_kernel(ct_ref, w_ref, b_ref, o_ref):
    tn = w_ref.shape[1]
    for r in range(N_COND):
        cb = ct_ref[r]
        s = cb * _sigmoid(cb)
        for cc in range(tn // LANES):
            sl = slice(cc * LANES, (cc + 1) * LANES)
            o_ref[r, :, sl] = jnp.sum(w_ref[:, sl] * s, axis=0, keepdims=True) + b_ref[:, sl]


def _modulation(cond_t, w_mod, b_mod):
    tn = 512
    n = 6 * D_MODEL
    return pl.pallas_call(
        _mod_kernel,
        grid=(DEPTH, n // tn),
        in_specs=[
            pl.BlockSpec((N_COND, D_MODEL, LANES), lambda l, j: (0, 0, 0)),
            pl.BlockSpec((None, D_MODEL, tn), lambda l, j: (l, 0, j)),
            pl.BlockSpec((None, 1, tn), lambda l, j: (l, 0, j)),
        ],
        out_specs=pl.BlockSpec((None, N_COND, 1, tn), lambda l, j: (l, 0, 0, j)),
        out_shape=jax.ShapeDtypeStruct((DEPTH, N_COND, 1, n), F32),
        compiler_params=_cparams(("parallel", "parallel")),
        name="modulation",
    )(cond_t, w_mod, b_mod.reshape(DEPTH, 1, n))


def _adaln_math(x, g, sh, sc):
    ms = jnp.mean(x * x, axis=-1, keepdims=True)
    y = x * lax.rsqrt(ms + RMS_EPS) * g
    return y * (1.0 + sc) + sh


def _split_rows(tm, width, joined=False):
    n_p = N_PROMPT // tm
    latent0 = n_p if joined else 0
    return (pl.BlockSpec((tm, width), lambda i: (jnp.minimum(i, n_p - 1), 0)),
            pl.BlockSpec((tm, width), lambda i: (jnp.maximum(i - n_p, 0) + latent0, 0)))


def _split_operands(x):
    return (x, False) if isinstance(x, tuple) else ((x, x), True)


def _read_split(p_ref, s_ref):
    is_ctx = pl.program_id(0) < N_PROMPT // p_ref.shape[0]
    return jnp.where(is_ctx, p_ref[...], s_ref[...])


def _adaln_kernel(xp_ref, xs_ref, g_ref, sh_ref, sc_ref, o_ref):
    o_ref[...] = _adaln_math(_read_split(xp_ref, xs_ref), g_ref[...], sh_ref[...], sc_ref[...]).astype(BF16)


def _adaln_router_kernel(xp_ref, xs_ref, g_ref, sh_ref, sc_ref, wr_ref, o_ref, lg_ref):
    h = _adaln_math(_read_split(xp_ref, xs_ref), g_ref[...], sh_ref[...], sc_ref[...])
    o_ref[...] = h
    lg_ref[...] = jnp.dot(h, wr_ref[...], precision=lax.Precision.HIGHEST, preferred_element_type=F32)


def _adaln(x, g, mod, l, shift_idx, w_router=None):
    tm = 512
    x, joined = _split_operands(x)
    mspec = lambda which: pl.BlockSpec((None, None, 1, D_MODEL),
                                       lambda i: (l, _cond_row(i, tm), 0, which))
    in_specs = [*_split_rows(tm, D_MODEL, joined),
                pl.BlockSpec((1, D_MODEL), lambda i: (0, 0)),
                mspec(shift_idx), mspec(shift_idx + 1)]
    h_spec = pl.BlockSpec((tm, D_MODEL), lambda i: (i, 0))
    h_shape = jax.ShapeDtypeStruct((N_TOK, D_MODEL), BF16)
    if w_router is None:
        return pl.pallas_call(
            _adaln_kernel, grid=(N_TOK // tm,), in_specs=in_specs, out_specs=h_spec, out_shape=h_shape,
            compiler_params=_cparams(("parallel",)), name="adaln",
        )(*x, g.reshape(1, D_MODEL), mod, mod)
    wr = jnp.pad(w_router, ((0, 0), (0, LANES - N_EXPERTS)))
    return pl.pallas_call(
        _adaln_router_kernel, grid=(N_TOK // tm,),
        in_specs=in_specs + [pl.BlockSpec((D_MODEL, LANES), lambda i: (0, 0))],
        out_specs=[h_spec, pl.BlockSpec((tm, LANES), lambda i: (i, 0))],
        out_shape=[jax.ShapeDtypeStruct((N_TOK, D_MODEL), F32), jax.ShapeDtypeStruct((N_TOK, LANES), F32)],
        compiler_params=_cparams(("parallel",)), name="adaln_router",
    )(*x, g.reshape(1, D_MODEL), mod, mod, wr)


def _proj_kernel(a_ref, w_ref, o_ref, *, gate):
    acc = jnp.dot(a_ref[...], w_ref[0].astype(BF16), preferred_element_type=F32)
    if gate:
        acc = _sigmoid(acc)
    o_ref[...] = acc.astype(o_ref.dtype)


def _proj(h, w_in, l, col0, n_out, tn, out_dtype, gate):
    tm = 2048 if gate else 1024
    return pl.pallas_call(
        functools.partial(_proj_kernel, gate=gate),
        grid=(N_TOK // tm, n_out // tn),
        in_specs=[pl.BlockSpec((tm, D_MODEL), lambda i, j: (i, 0)),
                  pl.BlockSpec((pl.Element(1), pl.Element(D_MODEL), pl.Element(tn)),
                               lambda i, j: (l, 0, pl.multiple_of(col0 + j * tn, LANES)))],
        out_specs=pl.BlockSpec((tm, tn), lambda i, j: (i, j)),
        out_shape=jax.ShapeDtypeStruct((N_TOK, n_out), out_dtype),
        compiler_params=_cparams(("parallel", "parallel")),
        name="proj_gate" if gate else "proj_qkv",
    )(h, w_in)


def _head_norm(x, gain, bd):
    sq = x * x
    hi = sq.astype(BF16)
    lo = (sq - hi.astype(F32)).astype(BF16)
    ms = jnp.dot(hi, bd, preferred_element_type=F32) + jnp.dot(lo, bd, preferred_element_type=F32)
    return x * lax.rsqrt(ms + RMS_EPS) * gain


def _rope(x, cos, sin_signed, first_half):
    swapped = jnp.where(first_half, pltpu.roll(x, LANES - HEAD_DIM // 2, 1), pltpu.roll(x, HEAD_DIM // 2, 1))
    return x * cos + swapped * sin_signed


def _qk_post_kernel(p_ref, cos_ref, sin_ref, ga_ref, gc_ref,
                    qa_ref, ka_ref, va_ref, fb_ref, qc_ref, kc_ref, vc_ref):
    r = lax.broadcasted_iota(jnp.int32, (LANES, LANES), 0) // HEAD_DIM
    c = lax.broadcasted_iota(jnp.int32, (LANES, LANES), 1) // HEAD_DIM
    bd = jnp.where(r == c, 1.0 / HEAD_DIM, 0.0).astype(BF16)
    lane = lax.broadcasted_iota(jnp.int32, (1, LANES), 1)
    first_half = (lane % HEAD_DIM) < HEAD_DIM // 2
    cos = cos_ref[...]
    sin = sin_ref[...]
    gqa, gka = ga_ref[0:1, :], ga_ref[1:2, :]
    gqc, gkc = gc_ref[0:1, :], gc_ref[1:2, :]
    for s in range(A_Q // LANES):
        sl = slice(s * LANES, (s + 1) * LANES)
        qa_ref[:, sl] = _rope(_head_norm(p_ref[:, sl], gqa, bd), cos, sin, first_half).astype(BF16)
    off = A_Q
    ka_ref[...] = _rope(_head_norm(p_ref[:, off:off + A_KV], gka, bd), cos, sin, first_half)
    off += A_KV
    va_ref[...] = p_ref[:, off:off + A_KV]
    off += A_KV
    fb_ref[...] = p_ref[:, off:off + B_WIDTH].astype(BF16)
    off += B_WIDTH
    for s in range(C_W // LANES):
        sl = slice(s * LANES, (s + 1) * LANES)
        qc_ref[:, sl] = _head_norm(p_ref[:, off + s * LANES:off + (s + 1) * LANES], gqc, bd).astype(BF16)
    off += C_W
    for s in range(C_W // LANES):
        sl = slice(s * LANES, (s + 1) * LANES)
        kc_ref[:, sl] = _head_norm(p_ref[:, off + s * LANES:off + (s + 1) * LANES], gkc, bd)
    off += C_W
    vc_ref[...] = p_ref[:, off:off + C_W]


def _qk_post(p, cos_t, sin_t, qk_a, qk_c):
    tm = 512
    row = lambda w: pl.BlockSpec((tm, w), lambda i: (i, 0))
    widths = (A_Q, A_KV, A_KV, B_WIDTH, C_W, C_W, C_W)
    dtypes = (BF16, F32, F32, BF16, BF16, F32, F32)
    return pl.pallas_call(
        _qk_post_kernel, grid=(N_TOK // tm,),
        in_specs=[row(QKV_WIDTH), row(LANES), row(LANES),
                  pl.BlockSpec((2, LANES), lambda i: (0, 0)), pl.BlockSpec((2, LANES), lambda i: (0, 0))],
        out_specs=[row(w) for w in widths],
        out_shape=[jax.ShapeDtypeStruct((N_TOK, w), d) for w, d in zip(widths, dtypes)],
        compiler_params=_cparams(("parallel",)), name="qk_post",
    )(p, cos_t, sin_t, jnp.tile(qk_a, (1, 2)), jnp.tile(qk_c, (1, 2)))


def _nt_dot(a, b):
    return lax.dot_general(a, b, (((1,), (1,)), ((), ())), preferred_element_type=F32)


def _head(x, h):
    return x[:, h * HEAD_DIM:(h + 1) * HEAD_DIM]


def _stacked_softmax(parts, sink):
    m = parts[0].max(axis=-1, keepdims=True)
    for s in parts[1:]:
        m = jnp.maximum(m, s.max(axis=-1, keepdims=True))
    if sink is not None:
        m = jnp.maximum(m, sink)
    den = jnp.exp(sink - m) if sink is not None else 0.0
    es = []
    for s in parts:
        e = jnp.exp(s - m)
        den = den + e.sum(axis=-1, keepdims=True)
        es.append(e.astype(BF16))
    return es, 1.0 / den


def _sink_column(sink_ref, l, rows_per_head):
    return jnp.concatenate([jnp.full((rows_per_head, 1), sink_ref[l, h], F32) for h in range(A_HEADS)], axis=0)


def _gqa_queries(qa, g):
    return jnp.concatenate([_head(qa, g * A_GROUP + i) for i in range(A_GROUP)], axis=0)


def _ctx_attn_kernel(sink_ref, qa_ref, ka_ref, va_ref, qc_ref, kc_ref, vc_ref, oa_ref, oc_ref, *, l):
    t = SEQ
    qa = qa_ref[...] * SCALE
    ka = ka_ref[...].astype(BF16)
    va = va_ref[...].astype(BF16)
    s = jnp.concatenate([_nt_dot(_gqa_queries(qa, g), _head(ka, g)) for g in range(A_KV_HEADS)], axis=0)
    (e,), inv = _stacked_softmax([s], _sink_column(sink_ref, l, t))
    outs = []
    for g in range(A_KV_HEADS):
        rows = slice(g * A_GROUP * t, (g + 1) * A_GROUP * t)
        o = jnp.dot(e[rows], _head(va, g), preferred_element_type=F32) * inv[rows]
        outs += [o[i * t:(i + 1) * t] for i in range(A_GROUP)]
    oa_ref[...] = jnp.concatenate(outs, axis=1).astype(BF16)

    qc = qc_ref[...] * SCALE
    kc = kc_ref[...].astype(BF16)
    vc = vc_ref[...].astype(BF16)
    s = jnp.concatenate([_nt_dot(_head(qc, h), _head(kc, h)) for h in range(C_HEADS)], axis=0)
    (e,), inv = _stacked_softmax([s], None)
    outs = [jnp.dot(e[h * t:(h + 1) * t], _head(vc, h), preferred_element_type=F32) * inv[h * t:(h + 1) * t]
            for h in range(C_HEADS)]
    oc_ref[...] = jnp.concatenate(outs, axis=1).astype(BF16)


def _ctx_attn(sink_a, qa, ka, va, qc, kc, vc, l):
    blk = lambda w: pl.BlockSpec((SEQ, w), lambda b: (b, 0))
    return pl.pallas_call(
        functools.partial(_ctx_attn_kernel, l=l), grid=(BATCH,),
        in_specs=[pl.BlockSpec(memory_space=pltpu.SMEM),
                  blk(A_Q), blk(A_KV), blk(A_KV), blk(C_W), blk(C_W), blk(C_W)],
        out_specs=[blk(A_Q), blk(C_W)],
        out_shape=[jax.ShapeDtypeStruct((N_PROMPT, A_Q), BF16), jax.ShapeDtypeStruct((N_PROMPT, C_W), BF16)],
        compiler_params=_cparams(("parallel",)), name="ctx_attn",
    )(sink_a, qa, ka, va, qc, kc, vc)


def _win_attn_kernel(sink_ref, q_ref, kp_ref, kc_ref, kn_ref, vp_ref, vc_ref, vn_ref, ck_ref, cv_ref, o_ref, *, l):
    t = pl.program_id(1)
    rows = A_GROUP * A_BLOCK
    qi = lax.broadcasted_iota(jnp.int32, (rows, 3 * A_BLOCK), 0) % A_BLOCK
    kj = lax.broadcasted_iota(jnp.int32, (rows, 3 * A_BLOCK), 1) - A_BLOCK
    kpos = t * A_BLOCK + kj
    valid = (jnp.abs(kj - qi) <= A_WINDOW) & (kpos >= 0) & (kpos < DEC_SEQ)
    sink = _sink_column(sink_ref, l, A_BLOCK)
    for g in range(A_KV_HEADS):
        sl = slice(g * HEAD_DIM, (g + 1) * HEAD_DIM)
        q = jnp.concatenate([q_ref[:, (g * A_GROUP + i) * HEAD_DIM:(g * A_GROUP + i + 1) * HEAD_DIM]
                             for i in range(A_GROUP)], axis=0) * SCALE
        k_loc = jnp.concatenate([kp_ref[:, sl], kc_ref[:, sl], kn_ref[:, sl]], axis=0).astype(BF16)
        v_loc = jnp.concatenate([vp_ref[:, sl], vc_ref[:, sl], vn_ref[:, sl]], axis=0).astype(BF16)
        s_loc = jnp.where(valid, _nt_dot(q, k_loc), NEG_INF)
        s_ctx = _nt_dot(q, ck_ref[:, sl].astype(BF16))
        (e_loc, e_ctx), inv = _stacked_softmax([s_loc, s_ctx], sink[g * rows:(g + 1) * rows])
        o = (jnp.dot(e_loc, v_loc, preferred_element_type=F32)
             + jnp.dot(e_ctx, cv_ref[:, sl].astype(BF16), preferred_element_type=F32)) * inv
        for i in range(A_GROUP):
            h = g * A_GROUP + i
            o_ref[:, h * HEAD_DIM:(h + 1) * HEAD_DIM] = o[i * A_BLOCK:(i + 1) * A_BLOCK].astype(BF16)


def _win_attn(sink_a, qa, ka, va, cache_k, cache_v, l):
    nb = DEC_SEQ // A_BLOCK
    base = N_PROMPT // A_BLOCK

    def nbr(d):
        return lambda b, t: (base + b * nb + jnp.clip(t + d, 0, nb - 1), 0)

    kv = lambda d: pl.BlockSpec((A_BLOCK, A_KV), nbr(d))
    cache = pl.BlockSpec((None, None, PAST_LEN, A_KV), lambda b, t: (b, l, 0, 0))
    return pl.pallas_call(
        functools.partial(_win_attn_kernel, l=l), grid=(DEC_BATCH, nb),
        in_specs=[pl.BlockSpec(memory_space=pltpu.SMEM),
                  pl.BlockSpec((A_BLOCK, A_Q), nbr(0)),
                  kv(-1), kv(0), kv(1), kv(-1), kv(0), kv(1), cache, cache],
        out_specs=pl.BlockSpec((A_BLOCK, A_Q), lambda b, t: (b * nb + t, 0)),
        out_shape=jax.ShapeDtypeStruct((N_SAMPLE, A_Q), BF16),
        compiler_params=_cparams(("parallel", "parallel")), name="win_attn",
    )(sink_a, qa, ka, ka, ka, va, va, va, cache_k, cache_v)


def _nbr_attn_kernel(q_ref, kp_ref, kc_ref, kn_ref, vp_ref, vc_ref, vn_ref, ck_ref, cv_ref, tab_ref, o_ref,
                     bias_ref):
    j = pl.program_id(0)
    nb = pl.num_programs(0)
    slots = _nbr_row_slots()

    def build(cls):
        for h in range(C_HEADS):
            for qr in range(C_QROWS):
                for kk in range(3 * C_QROWS):
                    bias_ref[h, qr * GRID_W:(qr + 1) * GRID_W, kk * GRID_W:(kk + 1) * GRID_W] = (
                        tab_ref[h, slots[cls][qr][kk]])

    first_of_batch = pl.program_id(1) == 0
    for cls, at in enumerate((0, 1, nb - 1)):
        @pl.when(first_of_batch & (j == at))
        def _():
            build(cls)

    tq = C_QBLOCK
    q = q_ref[...] * SCALE
    k_loc = jnp.concatenate([kp_ref[...], kc_ref[...], kn_ref[...]], axis=0).astype(BF16)
    v_loc = jnp.concatenate([vp_ref[...], vc_ref[...], vn_ref[...]], axis=0).astype(BF16)
    k_ctx = ck_ref[...].astype(BF16)
    v_ctx = cv_ref[...].astype(BF16)
    s_loc = jnp.concatenate([_nt_dot(_head(q, h), _head(k_loc, h)) for h in range(C_HEADS)], axis=0)
    s_loc = s_loc + bias_ref[...].reshape(C_HEADS * tq, 3 * tq)
    s_ctx = jnp.concatenate([_nt_dot(_head(q, h), _head(k_ctx, h)) for h in range(C_HEADS)], axis=0)
    (e_loc, e_ctx), inv = _stacked_softmax([s_loc, s_ctx], None)
    outs = []
    for h in range(C_HEADS):
        r = slice(h * tq, (h + 1) * tq)
        outs.append((jnp.dot(e_loc[r], _head(v_loc, h), preferred_element_type=F32)
                     + jnp.dot(e_ctx[r], _head(v_ctx, h), preferred_element_type=F32)) * inv[r])
    o_ref[...] = jnp.concatenate(outs, axis=1).astype(BF16)


def _nbr_attn(qc, kc, vc, cache_k, cache_v, bias, l):
    nb = DEC_SEQ // C_QBLOCK
    base = N_PROMPT // C_QBLOCK

    def nbr(d):
        return lambda j, b: (base + b * nb + jnp.clip(j + d, 0, nb - 1), 0)

    kv = lambda d: pl.BlockSpec((C_QBLOCK, C_W), nbr(d))
    cache = pl.BlockSpec((None, None, PAST_LEN, C_W), lambda j, b: (b, l, 0, 0))
    return pl.pallas_call(
        _nbr_attn_kernel, grid=(nb, DEC_BATCH),
        in_specs=[kv(0), kv(-1), kv(0), kv(1), kv(-1), kv(0), kv(1), cache, cache,
                  pl.BlockSpec((None, C_HEADS, C_DR_SLOTS, GRID_W, GRID_W), lambda j, b: (l, 0, 0, 0, 0))],
        out_specs=pl.BlockSpec((C_QBLOCK, C_W), lambda j, b: (b * nb + j, 0)),
        out_shape=jax.ShapeDtypeStruct((N_SAMPLE, C_W), BF16),
        scratch_shapes=[pltpu.VMEM((C_HEADS, C_QBLOCK, 3 * C_QBLOCK), F32)],
        compiler_params=_cparams(("arbitrary", "arbitrary")), name="nbr_attn",
    )(qc, kc, kc, kc, vc, vc, vc, cache_k, cache_v, bias)


def _nbr_bias_tables(rpb):
    qcol = np.arange(GRID_W)
    qcs = np.clip(qcol - C_WIN_COLS // 2, 0, GRID_W - C_WIN_COLS)
    kcol = np.arange(GRID_W)
    col_ok = (kcol[None, :] >= qcs[:, None]) & (kcol[None, :] < qcs[:, None] + C_WIN_COLS)
    dc = np.clip(kcol[None, :] - qcol[:, None], -(C_WIN_COLS - 1), C_WIN_COLS - 1) + C_WIN_COLS - 1
    onehot_dc = (dc.reshape(-1)[None, :] == np.arange(2 * C_WIN_COLS - 1)[:, None]).astype(np.float32)
    t = jnp.einsum('lhab,bx->lhax', rpb, jnp.asarray(onehot_dc), precision=lax.Precision.HIGHEST)
    t = jnp.where(jnp.asarray(col_ok.reshape(-1)), t, NEG_INF)
    t = jnp.concatenate([t, jnp.full((DEPTH, C_HEADS, 1, GRID_W * GRID_W), NEG_INF, F32)], axis=2)
    return t.reshape(DEPTH, C_HEADS, C_DR_SLOTS, GRID_W, GRID_W)


def _nbr_row_slots():
    rows = DEC_SEQ // GRID_W
    slots = []
    for j in (0, 3, rows // C_QROWS - 1):
        per_q = []
        for qr in range(C_QROWS):
            r = C_QROWS * j + qr
            rs = min(max(r - C_WIN_ROWS // 2, 0), rows - C_WIN_ROWS)
            per_k = []
            for kk in range(3 * C_QROWS):
                kabs = C_QROWS * (j - 1) + kk
                per_k.append(kabs - r + C_WIN_ROWS - 1 if rs <= kabs < rs + C_WIN_ROWS else C_DR_SLOTS - 1)
            per_q.append(per_k)
        slots.append(per_q)
    return slots


def _fourier_kernel(u_ref, bc_ref, bs_ref, cl_ref, sl_ref, o_ref, zc_ref, zs_ref):
    @pl.when(pl.program_id(1) == 0)
    def _():
        u = u_ref[...]
        zc_ref[...] = jnp.dot(u, bc_ref[...].astype(BF16), preferred_element_type=F32).astype(BF16)
        zs_ref[...] = jnp.dot(u, bs_ref[...].astype(BF16), preferred_element_type=F32).astype(BF16)

    o = (jnp.dot(cl_ref[...].astype(BF16), zc_ref[...], preferred_element_type=F32)
         - jnp.dot(sl_ref[...].astype(BF16), zs_ref[...], preferred_element_type=F32))
    o_ref[...] = o.astype(BF16)


def _dft_tables(n):
    k = np.arange(n)
    ang = 2.0 * np.pi * ((k[:, None] * k[None, :]) % n) / n
    return np.cos(ang) / np.sqrt(n), np.sin(ang) / np.sqrt(n)


def _channel_dft_tables():
    c, s = _dft_tables(B_GROUP_DIM)
    eye = np.eye(B_GROUPS)
    return np.kron(eye, c).astype(np.float32), np.kron(eye, s).astype(np.float32)


def _fourier(fb, n_batch, seq, row0, tr):
    cl, sl = (jnp.asarray(a.astype(np.float32)) for a in _dft_tables(seq))
    bc, bs = (jnp.asarray(a) for a in _channel_dft_tables())
    nt = seq // tr
    const = pl.BlockSpec((B_WIDTH, B_WIDTH), lambda b, t: (0, 0))
    return pl.pallas_call(
        _fourier_kernel, grid=(n_batch, nt),
        in_specs=[pl.BlockSpec((seq, B_WIDTH), lambda b, t: (row0 // seq + b, 0)), const, const,
                  pl.BlockSpec((tr, seq), lambda b, t: (t, 0)), pl.BlockSpec((tr, seq), lambda b, t: (t, 0))],
        out_specs=pl.BlockSpec((tr, B_WIDTH), lambda b, t: (b * nt + t, 0)),
        out_shape=jax.ShapeDtypeStruct((n_batch * seq, B_WIDTH), BF16),
        scratch_shapes=[pltpu.VMEM((seq, B_WIDTH), BF16), pltpu.VMEM((seq, B_WIDTH), BF16)],
        compiler_params=_cparams(("parallel", "arbitrary")), name=f"fourier_{seq}",
    )(fb, bc, bs, cl, sl)


def _merge_kernel(oap_ref, obp_ref, ocp_ref, oas_ref, obs_ref, ocs_ref, gt_ref, xp_ref, xs_ref,
                  wa_ref, wb_ref, wc_ref, wo_ref, g1_ref, o_ref, wab_ref, wbb_ref, wcb_ref, wob_ref):
    @pl.when(pl.program_id(0) == 0)
    def _():
        wab_ref[...] = wa_ref[...].astype(BF16)
        wbb_ref[...] = wb_ref[...].astype(BF16)
        wcb_ref[...] = wc_ref[...].astype(BF16)
        wob_ref[...] = wo_ref[...].astype(BF16)

    ya = jnp.dot(_read_split(oap_ref, oas_ref), wab_ref[...], preferred_element_type=F32)
    yb = jnp.dot(_read_split(obp_ref, obs_ref), wbb_ref[...], preferred_element_type=F32)
    yc = jnp.dot(_read_split(ocp_ref, ocs_ref), wcb_ref[...], preferred_element_type=F32)
    d = D_MODEL
    m = (gt_ref[:, 0:d].astype(F32) * ya + gt_ref[:, d:2 * d].astype(F32) * yb
         + gt_ref[:, 2 * d:3 * d].astype(F32) * yc)
    y = jnp.dot(m.astype(BF16), wob_ref[...], preferred_element_type=F32)
    o_ref[...] = _read_split(xp_ref, xs_ref) + g1_ref[...] * y


def _merge(branches_p, branches_s, gates, x, wa, wb, wc, wo, mod, l):
    tm = 512
    x, joined = _split_operands(x)
    row = lambda w: pl.BlockSpec((tm, w), lambda i: (i, 0))
    row_p = lambda w: _split_rows(tm, w)[0]
    row_s = lambda w: _split_rows(tm, w)[1]
    const = lambda r, c: pl.BlockSpec((None, r, c), lambda i: (l, 0, 0), pipeline_mode=pl.Buffered(1))
    return pl.pallas_call(
        _merge_kernel, grid=(N_TOK // tm,),
        in_specs=[row_p(A_Q), row_p(B_WIDTH), row_p(C_W), row_s(A_Q), row_s(B_WIDTH), row_s(C_W),
                  row(GATE_WIDTH), *_split_rows(tm, D_MODEL, joined),
                  const(A_Q, D_MODEL), const(B_WIDTH, D_MODEL), const(C_W, D_MODEL), const(D_MODEL, D_MODEL),
                  pl.BlockSpec((None, None, 1, D_MODEL), lambda i: (l, _cond_row(i, tm), 0, 2))],
        out_specs=row(D_MODEL),
        out_shape=jax.ShapeDtypeStruct((N_TOK, D_MODEL), F32),
        scratch_shapes=[pltpu.VMEM((A_Q, D_MODEL), BF16), pltpu.VMEM((B_WIDTH, D_MODEL), BF16),
                        pltpu.VMEM((C_W, D_MODEL), BF16), pltpu.VMEM((D_MODEL, D_MODEL), BF16)],
        compiler_params=_cparams(("arbitrary",)), name="merge",
    )(*branches_p, *branches_s, gates, *x, wa, wb, wc, wo, mod)


def _ffn_kernel(h_ref, x_ref, wg_ref, wu_ref, wd_ref, g2_ref, o_ref):
    f = pl.program_id(1)

    @pl.when(f == 0)
    def _():
        o_ref[...] = jnp.zeros_like(o_ref)

    h = h_ref[...]
    g = jnp.dot(h, wg_ref[...].astype(BF16), preferred_element_type=F32)
    u = jnp.dot(h, wu_ref[...].astype(BF16), preferred_element_type=F32)
    a = (g * _sigmoid(g) * u).astype(BF16)
    o_ref[...] += jnp.dot(a, wd_ref[...].astype(BF16), preferred_element_type=F32)

    @pl.when(f == pl.num_programs(1) - 1)
    def _():
        o_ref[...] = x_ref[...] + g2_ref[...] * o_ref[...]


def _ffn(h, x, wg, wu, wd, mod, l):
    tm, tf = 2048, 256
    return pl.pallas_call(
        _ffn_kernel, grid=(N_TOK // tm, D_FF // tf),
        in_specs=[pl.BlockSpec((tm, D_MODEL), lambda i, f: (i, 0)),
                  pl.BlockSpec((tm, D_MODEL), lambda i, f: (i, 0), pipeline_mode=pl.Buffered(1)),
                  pl.BlockSpec((D_MODEL, tf), lambda i, f: (0, f)),
                  pl.BlockSpec((D_MODEL, tf), lambda i, f: (0, f)),
                  pl.BlockSpec((tf, D_MODEL), lambda i, f: (f, 0)),
                  pl.BlockSpec((None, None, 1, D_MODEL), lambda i, f: (l, _cond_row(i, tm), 0, 5))],
        out_specs=pl.BlockSpec((tm, D_MODEL), lambda i, f: (i, 0)),
        out_shape=jax.ShapeDtypeStruct((N_TOK, D_MODEL), F32),
        compiler_params=_cparams(("parallel", "arbitrary")), name="ffn",
    )(h, x, wg, wu, wd, mod)


TOP_K = 2
MOE_TILE = 256
MOE_TILES = TOP_K * N_TOK // MOE_TILE + N_EXPERTS
MOE_ROWS = MOE_TILES * MOE_TILE
MOE_CHUNK = 10
MOE_CHUNKS = MOE_TILES // MOE_CHUNK + N_EXPERTS
MOE_TF = 896
ROUTE_TM = 512
DISPATCH_TM = 256
COMBINE_TM = 256
ROW_COPY_UNROLL = 16


def _route_kernel(lg_ref, o_ref, cnt_ref, base_ref, tri_ref):
    tm = lg_ref.shape[0]

    @pl.when(pl.program_id(0) == 0)
    def _():
        base_ref[...] = jnp.zeros_like(base_ref)
        r = lax.broadcasted_iota(jnp.int32, (tm, tm), 0)
        c = lax.broadcasted_iota(jnp.int32, (tm, tm), 1)
        tri_ref[...] = jnp.where(r > c, 1.0, 0.0).astype(BF16)

    lane = lax.broadcasted_iota(jnp.int32, lg_ref.shape, 1).astype(F32)
    lg = jnp.where(lane < N_EXPERTS, lg_ref[...], -jnp.inf)
    m1 = lg.max(axis=-1, keepdims=True)
    i1 = jnp.where(lg == m1, lane, float(LANES)).min(axis=-1, keepdims=True)
    rest = jnp.where(lane == i1, -jnp.inf, lg)
    m2 = rest.max(axis=-1, keepdims=True)
    i2 = jnp.where(rest == m2, lane, float(LANES)).min(axis=-1, keepdims=True)
    e2 = jnp.exp(m2 - m1)
    w1 = 1.0 / (1.0 + e2)
    w2 = e2 / (1.0 + e2)

    oh1 = jnp.where(lane == i1, 1.0, 0.0)
    oh2 = jnp.where(lane == i2, 1.0, 0.0)
    pre1 = jnp.dot(tri_ref[...], oh1.astype(BF16), preferred_element_type=F32)
    pre2 = jnp.dot(tri_ref[...], oh2.astype(BF16), preferred_element_type=F32)
    c1 = jnp.sum(oh1, axis=0, keepdims=True)
    c2 = jnp.sum(oh2, axis=0, keepdims=True)
    base = base_ref[...]
    rank1 = jnp.sum(oh1 * (base + pre1), axis=-1, keepdims=True)
    rank2 = jnp.sum(oh2 * (base + c1 + pre2), axis=-1, keepdims=True)
    base_ref[...] = base + c1 + c2

    cols = (i1, i2, rank1, rank2, w1, w2)
    out = jnp.zeros(lg_ref.shape, F32)
    for j, col in enumerate(cols):
        out = jnp.where(lane == float(j), col, out)
    o_ref[...] = out
    cnt_ref[...] = jnp.broadcast_to(base + c1 + c2, cnt_ref.shape)


def _dispatch_kernel(pos_ref, last_ref, nt_ref, h_ref, xs_hbm, zero_ref, zsem, sem):
    i = pl.program_id(0)
    tm = h_ref.shape[0]

    @pl.when(i == 0)
    def _():
        zero_ref[...] = jnp.zeros_like(zero_ref)

        def zero_copy(tile):
            row0 = pl.multiple_of(tile * MOE_TILE, MOE_TILE)
            return pltpu.make_async_copy(zero_ref, xs_hbm.at[pl.ds(row0, MOE_TILE)], zsem)

        def for_zeroed_tiles(fn):
            for e in range(N_EXPERTS):
                @pl.when(last_ref[e] >= 0)
                def _():
                    fn(zero_copy(last_ref[e]))

                tail = MOE_TILES - 1 - e

                @pl.when(tail >= nt_ref[0])
                def _():
                    fn(zero_copy(tail))

        for_zeroed_tiles(lambda cp: cp.start())
        for_zeroed_tiles(lambda cp: cp.wait())

    def row_copy(t, k):
        dst = xs_hbm.at[pl.ds(pos_ref[TOP_K * (i * tm + t) + k], 1)]
        return pltpu.make_async_copy(h_ref.at[pl.ds(t, 1)], dst, sem)

    def issue(t, carry):
        for k in range(TOP_K):
            row_copy(t, k).start()
        return carry

    def drain(t, carry):
        for k in range(TOP_K):
            row_copy(t, k).wait()
        return carry

    lax.fori_loop(0, tm, issue, 0, unroll=ROW_COPY_UNROLL)
    lax.fori_loop(0, tm, drain, 0, unroll=ROW_COPY_UNROLL)


def _expert_kernel(ce_ref, ct_ref, cn_ref, nch_ref, nt_ref, xs_hbm, wg_ref, wu_ref, wd_ref, y_hbm,
                   acc_ref, xbuf_ref, wgb_ref, wub_ref, wdb_ref, xsem, osem):
    c = pl.program_id(0)
    f = pl.program_id(1)
    last_f = pl.num_programs(1) - 1

    def tile_rows(tile):
        return pl.ds(pl.multiple_of(tile * MOE_TILE, MOE_TILE), MOE_TILE)

    def out_copy(slot, tile):
        return pltpu.make_async_copy(acc_ref.at[slot], y_hbm.at[tile_rows(tile)], osem)

    @pl.when((c == 0) & (f == 0))
    def _():
        acc_ref[0] = jnp.zeros((MOE_TILE, D_MODEL), F32)
        for e in range(N_EXPERTS):
            tail = MOE_TILES - 1 - e

            @pl.when(tail >= nt_ref[0])
            def _():
                cp = out_copy(0, tail)
                cp.start()
                cp.wait()

    @pl.when(c < nch_ref[0])
    def _():
        n = cn_ref[c]
        t0 = ct_ref[c]

        def x_copy(j):
            return pltpu.make_async_copy(xs_hbm.at[tile_rows(t0 + j)], xbuf_ref.at[j], xsem.at[j])

        @pl.when(f == 0)
        def _():
            def fetch(j, carry):
                x_copy(j).start()
                return carry
            lax.fori_loop(0, n, fetch, 0)

        wgb_ref[...] = wg_ref[...].astype(BF16)
        wub_ref[...] = wu_ref[...].astype(BF16)
        wdb_ref[...] = wd_ref[...].astype(BF16)

        def tile_step(j, carry):
            @pl.when(f == 0)
            def _():
                x_copy(j).wait()

            x = xbuf_ref[j].astype(BF16)
            g = jnp.dot(x, wgb_ref[...], preferred_element_type=F32)
            u = jnp.dot(x, wub_ref[...], preferred_element_type=F32)
            a = (g * _sigmoid(g) * u).astype(BF16)
            d = jnp.dot(a, wdb_ref[...], preferred_element_type=F32)

            @pl.when(f == 0)
            def _():
                acc_ref[j] = d

            @pl.when(f > 0)
            def _():
                acc_ref[j] += d

            @pl.when(f == last_f)
            def _():
                @pl.when(j > 0)
                def _():
                    out_copy(j - 1, t0 + j - 1).wait()

                out_copy(j, t0 + j).start()

            return carry

        lax.fori_loop(0, n, tile_step, 0)

        @pl.when(f == last_f)
        def _():
            out_copy(n - 1, t0 + n - 1).wait()


def _combine_kernel(pos_ref, route_ref, x_ref, g2_ref, y_hbm, op_ref, os_ref, buf_ref, sem):
    i = pl.program_id(0)
    n = pl.num_programs(0)
    tm = x_ref.shape[0]
    slot = i % 2

    def row_copy(step, s, t, k):
        src = y_hbm.at[pl.ds(pos_ref[TOP_K * (step * tm + t) + k], 1)]
        return pltpu.make_async_copy(src, buf_ref.at[s, k, pl.ds(t, 1)], sem.at[s])

    def issue(step, s):
        def body(t, carry):
            for k in range(TOP_K):
                row_copy(step, s, t, k).start()
            return carry
        lax.fori_loop(0, tm, body, 0, unroll=ROW_COPY_UNROLL)

    @pl.when(i == 0)
    def _():
        issue(0, 0)

    @pl.when(i + 1 < n)
    def _():
        issue(i + 1, 1 - slot)

    def wait_body(t, carry):
        for k in range(TOP_K):
            row_copy(i, slot, t, k).wait()
        return carry

    lax.fori_loop(0, tm, wait_body, 0, unroll=ROW_COPY_UNROLL)

    lane = lax.broadcasted_iota(jnp.int32, route_ref.shape, 1)
    rt = route_ref[...]
    w1 = jnp.sum(jnp.where(lane == 2 * TOP_K, rt, 0.0), axis=-1, keepdims=True)
    w2 = jnp.sum(jnp.where(lane == 2 * TOP_K + 1, rt, 0.0), axis=-1, keepdims=True)
    out = x_ref[...] + g2_ref[...] * (w1 * buf_ref[slot, 0] + w2 * buf_ref[slot, 1])

    @pl.when(i < N_PROMPT // tm)
    def _():
        op_ref[...] = out

    @pl.when(i >= N_PROMPT // tm)
    def _():
        os_ref[...] = out


def _moe(h, x, logits, wg, wu, wd, mod, l):
    route, cnt = pl.pallas_call(
        _route_kernel, grid=(N_TOK // ROUTE_TM,),
        in_specs=[pl.BlockSpec((ROUTE_TM, LANES), lambda i: (i, 0))],
        out_specs=[pl.BlockSpec((ROUTE_TM, LANES), lambda i: (i, 0)), pl.BlockSpec((8, LANES), lambda i: (0, 0))],
        out_shape=[jax.ShapeDtypeStruct((N_TOK, LANES), F32), jax.ShapeDtypeStruct((8, LANES), F32)],
        scratch_shapes=[pltpu.VMEM((1, LANES), F32), pltpu.VMEM((ROUTE_TM, ROUTE_TM), BF16)],
        compiler_params=_cparams(("arbitrary",)), name="route",
    )(logits)

    expert = route[:, 0:TOP_K].astype(jnp.int32)
    rank = route[:, TOP_K:2 * TOP_K].astype(jnp.int32)
    n_sub = (cnt[0, :N_EXPERTS].astype(jnp.int32) + MOE_TILE - 1) // MOE_TILE
    end = jnp.cumsum(n_sub)
    start = end - n_sub
    start_of = jnp.sum(jnp.where(expert[:, :, None] == jnp.arange(N_EXPERTS), start, 0), axis=-1)
    pos = (start_of * MOE_TILE + rank).reshape(-1)
    n_tiles = end[N_EXPERTS - 1:]
    last_tile = jnp.where(n_sub > 0, end - 1, -1).astype(jnp.int32)

    xs = pl.pallas_call(
        _dispatch_kernel,
        grid_spec=pltpu.PrefetchScalarGridSpec(
            num_scalar_prefetch=3, grid=(N_TOK // DISPATCH_TM,),
            in_specs=[pl.BlockSpec((DISPATCH_TM, D_MODEL), lambda i, p, lt, nt: (i, 0))],
            out_specs=pl.BlockSpec(memory_space=pl.ANY),
            scratch_shapes=[pltpu.VMEM((MOE_TILE, D_MODEL), F32), pltpu.SemaphoreType.DMA(()),
                            pltpu.SemaphoreType.DMA(())]),
        out_shape=jax.ShapeDtypeStruct((MOE_ROWS, D_MODEL), F32),
        compiler_params=_cparams(("arbitrary",)), name="moe_dispatch",
    )(pos, last_tile, n_tiles, h)

    n_chunk = (n_sub + MOE_CHUNK - 1) // MOE_CHUNK
    chunk_end = jnp.cumsum(n_chunk)
    cidx = jnp.arange(MOE_CHUNKS)
    c_expert = jnp.minimum(jnp.sum(cidx[:, None] >= chunk_end[None, :], axis=1), N_EXPERTS - 1).astype(jnp.int32)
    c_k = cidx - (chunk_end - n_chunk)[c_expert]
    c_tile0 = (start[c_expert] + c_k * MOE_CHUNK).astype(jnp.int32)
    c_ntiles = jnp.clip(n_sub[c_expert] - c_k * MOE_CHUNK, 0, MOE_CHUNK).astype(jnp.int32)
    n_chunks = chunk_end[N_EXPERTS - 1:]

    nf = D_FF_EXPERT // MOE_TF

    def w_idx(c, f, ce, nch):
        live = c < nch[0]
        return ce[jnp.minimum(c, nch[0] - 1)], jnp.where(live, f, nf - 1)

    def up_map(c, f, ce, ct, cn, nch, nt):
        e, ff = w_idx(c, f, ce, nch)
        return (e, 0, ff)

    def down_map(c, f, ce, ct, cn, nch, nt):
        e, ff = w_idx(c, f, ce, nch)
        return (e, ff, 0)

    y = pl.pallas_call(
        _expert_kernel,
        grid_spec=pltpu.PrefetchScalarGridSpec(
            num_scalar_prefetch=5, grid=(MOE_CHUNKS, nf),
            in_specs=[pl.BlockSpec(memory_space=pl.ANY),
                      pl.BlockSpec((None, D_MODEL, MOE_TF), up_map),
                      pl.BlockSpec((None, D_MODEL, MOE_TF), up_map),
                      pl.BlockSpec((None, MOE_TF, D_MODEL), down_map)],
            out_specs=pl.BlockSpec(memory_space=pl.ANY),
            scratch_shapes=[pltpu.VMEM((MOE_CHUNK, MOE_TILE, D_MODEL), F32),
                            pltpu.VMEM((MOE_CHUNK, MOE_TILE, D_MODEL), F32),
                            pltpu.VMEM((D_MODEL, MOE_TF), BF16), pltpu.VMEM((D_MODEL, MOE_TF), BF16),
                            pltpu.VMEM((MOE_TF, D_MODEL), BF16),
                            pltpu.SemaphoreType.DMA((MOE_CHUNK,)), pltpu.SemaphoreType.DMA(())]),
        out_shape=jax.ShapeDtypeStruct((MOE_ROWS, D_MODEL), F32),
        compiler_params=_cparams(("arbitrary", "arbitrary")), name="moe_experts",
    )(c_expert, c_tile0, c_ntiles, n_chunks, n_tiles, xs, wg, wu, wd)

    tm = COMBINE_TM
    return pl.pallas_call(
        _combine_kernel,
        grid_spec=pltpu.PrefetchScalarGridSpec(
            num_scalar_prefetch=1, grid=(N_TOK // tm,),
            in_specs=[pl.BlockSpec((tm, LANES), lambda i, p: (i, 0)),
                      pl.BlockSpec((tm, D_MODEL), lambda i, p: (i, 0)),
                      pl.BlockSpec((None, None, 1, D_MODEL), lambda i, p: (l, _cond_row(i, tm), 0, 5)),
                      pl.BlockSpec(memory_space=pl.ANY)],
            out_specs=[pl.BlockSpec((tm, D_MODEL), lambda i, p: (jnp.minimum(i, N_PROMPT // tm - 1), 0)),
                       pl.BlockSpec((tm, D_MODEL), lambda i, p: (jnp.maximum(i - N_PROMPT // tm, 0), 0))],
            scratch_shapes=[pltpu.VMEM((2, TOP_K, tm, D_MODEL), F32), pltpu.SemaphoreType.DMA((2,))]),
        out_shape=[jax.ShapeDtypeStruct((N_PROMPT, D_MODEL), F32), jax.ShapeDtypeStruct((N_SAMPLE, D_MODEL), F32)],
        compiler_params=_cparams(("arbitrary",)), name="moe_combine",
    )(pos, route, x, mod, y)


def _rope_tables():
    t = jnp.arange(DEC_SEQ)
    row = (t // GRID_W).astype(F32)
    col = (t % GRID_W).astype(F32)
    n_freq = HEAD_DIM // 4
    inv = ROPE_BASE ** (-jnp.arange(n_freq, dtype=F32) / n_freq)
    ang = jnp.concatenate([row[:, None] * inv, col[:, None] * inv], axis=-1)
    cos, sin = jnp.cos(ang), jnp.sin(ang)
    cos_h = jnp.concatenate([cos, cos], axis=-1)
    sin_h = jnp.concatenate([-sin, sin], axis=-1)
    cos_l = jnp.tile(jnp.concatenate([cos_h, cos_h], axis=-1), (DEC_BATCH, 1))
    sin_l = jnp.tile(jnp.concatenate([sin_h, sin_h], axis=-1), (DEC_BATCH, 1))
    cos_t = jnp.concatenate([jnp.ones((N_PROMPT, LANES), F32), cos_l], axis=0)
    sin_t = jnp.concatenate([jnp.zeros((N_PROMPT, LANES), F32), sin_l], axis=0)
    return cos_t, sin_t


def kernel(x_prompt, x_sample, cache_a_k, cache_a_v, cache_c_k, cache_c_v, c, c_ctx, w_mod, b_mod, norm1_g, norm2_g, w_in, qk_norm_a, qk_norm_c, sink_a, rpb_c, w_branch_a, w_branch_b, w_branch_c, w_out, w_ff_gate, w_ff_up, w_ff_down, w_router, w_exp_gate, w_exp_up, w_exp_down):
    x = (x_prompt.reshape(N_PROMPT, D_MODEL), x_sample.reshape(N_SAMPLE, D_MODEL))
    cond =jnp.concatenate([c_ctx[None, :], c], axis=0)
    cond_t = jnp.broadcast_to(cond[:, :, None], (N_COND, D_MODEL, LANES))
    mod = _modulation(cond_t, w_mod, b_mod)
    cos_t, sin_t = _rope_tables()
    bias = _nbr_bias_tables(rpb_c)
    ck_a = cache_a_k.reshape(DEC_BATCH, DEPTH, PAST_LEN, A_KV)
    cv_a = cache_a_v.reshape(DEC_BATCH, DEPTH, PAST_LEN, A_KV)
    ck_c = cache_c_k.reshape(DEC_BATCH, DEPTH, PAST_LEN, C_W)
    cv_c = cache_c_v.reshape(DEC_BATCH, DEPTH, PAST_LEN, C_W)

    new_ak, new_av, new_ck, new_cv = [], [], [], []
    for l in range(DEPTH):
        h = _adaln(x, norm1_g[l], mod, l, 0)
        p = _proj(h, w_in, l, 0, QKV_WIDTH, QKV_WIDTH // 2, F32, gate=False)
        gates = _proj(h, w_in, l, QKV_WIDTH, GATE_WIDTH, D_MODEL, BF16, gate=True)
        qa, ka, va, fb, qc, kc, vc = _qk_post(p, cos_t, sin_t, qk_norm_a[l], qk_norm_c[l])
        oa_p, oc_p = _ctx_attn(sink_a, qa, ka, va, qc, kc, vc, l)
        oa_s = _win_attn(sink_a, qa, ka, va, ck_a, cv_a, l)
        oc_s = _nbr_attn(qc, kc, vc, ck_c, cv_c, bias, l)
        ob_p = _fourier(fb, BATCH, SEQ, 0, SEQ)
        ob_s = _fourier(fb, DEC_BATCH, DEC_SEQ, N_PROMPT, 512)
        x = _merge((oa_p, ob_p, oc_p), (oa_s, ob_s, oc_s), gates, x,
                   w_branch_a, w_branch_b, w_branch_c, w_out, mod, l)
        i = l // 2
        if l % 2 == 0:
            h2 = _adaln(x, norm2_g[l], mod, l, 3)
            x = _ffn(h2, x, w_ff_gate[i], w_ff_up[i], w_ff_down[i], mod, l)
        else:
            h2, logits = _adaln(x, norm2_g[l], mod, l, 3, w_router=w_router[i])
            xp, xs = _moe(h2, x, logits, w_exp_gate[i], w_exp_up[i], w_exp_down[i], mod, l)
        new_ak.append(ka[:N_PROMPT].reshape(BATCH, SEQ, A_KV_HEADS, HEAD_DIM))
        new_av.append(va[:N_PROMPT].reshape(BATCH, SEQ, A_KV_HEADS, HEAD_DIM))
        new_ck.append(kc[:N_PROMPT].reshape(BATCH, SEQ, C_HEADS, HEAD_DIM))
        new_cv.append(vc[:N_PROMPT].reshape(BATCH, SEQ, C_HEADS, HEAD_DIM))

    return (xp.reshape(BATCH, SEQ, D_MODEL), xs.reshape(DEC_BATCH, DEC_SEQ, D_MODEL),
            jnp.stack(new_ak, axis=1), jnp.stack(new_av, axis=1),
            jnp.stack(new_ck, axis=1), jnp.stack(new_cv, axis=1))
```

```python
import functools

import numpy as np
import jax
import jax.numpy as jnp
from jax import lax
from jax.experimental import pallas as pl
from jax.experimental.pallas import tpu as pltpu

F32 = jnp.float32
BF16 = jnp.bfloat16

D_MODEL = 1024
BATCH = 16
SEQ = 256
DEPTH = 2
DEC_BATCH = 2
DEC_SEQ = 2048
PAST_LEN = 512
GRID_W = 64
HEAD_DIM = 64
SCALE = HEAD_DIM ** -0.5
A_HEADS = 8
A_KV_HEADS = 2
A_GROUP = A_HEADS // A_KV_HEADS
A_WINDOW = 128
A_BLOCK = 128
B_GROUPS = 8
B_GROUP_DIM = 64
B_WIDTH = B_GROUPS * B_GROUP_DIM
C_HEADS = 8
C_WIN_ROWS = 8
C_WIN_COLS = 16
A_Q = A_HEADS * HEAD_DIM
A_KV = A_KV_HEADS * HEAD_DIM
C_W = C_HEADS * HEAD_DIM
QKV_WIDTH = A_Q + 2 * A_KV + B_WIDTH + 3 * C_W
N_BRANCH = 3
GATE_WIDTH = N_BRANCH * D_MODEL
IN_WIDTH = QKV_WIDTH + GATE_WIDTH
D_FF = 2816
N_EXPERTS = 8
D_FF_EXPERT = 3584
ROPE_BASE = 10000.0
RMS_EPS = 1e-6
NEG_INF = -1e30

N_PROMPT = BATCH * SEQ
N_SAMPLE = DEC_BATCH * DEC_SEQ
N_TOK = N_PROMPT + N_SAMPLE
N_COND = 1 + DEC_BATCH
LANES = 128
C_QROWS = 4
C_QBLOCK = C_QROWS * GRID_W
C_DR_SLOTS = 2 * C_WIN_ROWS
VMEM_LIMIT = 56 * 1024 * 1024


def _cparams(sem):
    return pltpu.CompilerParams(dimension_semantics=sem, vmem_limit_bytes=VMEM_LIMIT)


def _sigmoid(x):
    return 1.0 / (1.0 + jnp.exp(-x))


def _cond_row(tile, tm):
    return jnp.maximum(tile * tm // DEC_SEQ - 1, 0)


def _mod_kernel(ct_ref, w_ref, b_ref, o_ref):
    tn = w_ref.shape[1]
    for r in range(N_COND):
        cb = ct_ref[r]
        s = cb * _sigmoid(cb)
        for cc in range(tn // LANES):
            sl = slice(cc * LANES, (cc + 1) * LANES)
            o_ref[r, :, sl] = jnp.sum(w_ref[:, sl] * s, axis=0, keepdims=True) + b_ref[:, sl]


def _modulation(cond_t, w_mod, b_mod):
    tn = 512
    n = 6 * D_MODEL
    return pl.pallas_call(
        _mod_kernel,
        grid=(DEPTH, n // tn),
        in_specs=[
            pl.BlockSpec((N_COND, D_MODEL, LANES), lambda l, j: (0, 0, 0)),
            pl.BlockSpec((None, D_MODEL, tn), lambda l, j: (l, 0, j)),
            pl.BlockSpec((None, 1, tn), lambda l, j: (l, 0, j)),
        ],
        out_specs=pl.BlockSpec((None, N_COND, 1, tn), lambda l, j: (l, 0, 0, j)),
        out_shape=jax.ShapeDtypeStruct((DEPTH, N_COND, 1, n), F32),
        compiler_params=_cparams(("parallel", "parallel")),
        name="modulation",
    )(cond_t, w_mod, b_mod.reshape(DEPTH, 1, n))


def _adaln_math(x, g, sh, sc):
    ms = jnp.mean(x * x, axis=-1, keepdims=True)
    y = x * lax.rsqrt(ms + RMS_EPS) * g
    return y * (1.0 + sc) + sh


def _split_rows(tm, width, joined=False):
    n_p = N_PROMPT // tm
    latent0 = n_p if joined else 0
    return (pl.BlockSpec((tm, width), lambda i: (jnp.minimum(i, n_p - 1), 0)),
            pl.BlockSpec((tm, width), lambda i: (jnp.maximum(i - n_p, 0) + latent0, 0)))


def _split_operands(x):
    return (x, False) if isinstance(x, tuple) else ((x, x), True)


def _read_split(p_ref, s_ref):
    is_ctx = pl.program_id(0) < N_PROMPT // p_ref.shape[0]
    return jnp.where(is_ctx, p_ref[...], s_ref[...])


def _adaln_kernel(xp_ref, xs_ref, g_ref, sh_ref, sc_ref, o_ref):
    o_ref[...] = _adaln_math(_read_split(xp_ref, xs_ref), g_ref[...], sh_ref[...], sc_ref[...]).astype(BF16)


def _adaln(x, g, mod, l, shift_idx):
    tm = 512
    x, joined = _split_operands(x)
    mspec = lambda which: pl.BlockSpec((None, None, 1, D_MODEL),
                                       lambda i: (l, _cond_row(i, tm), 0, which))
    return pl.pallas_call(
        _adaln_kernel, grid=(N_TOK // tm,),
        in_specs=[*_split_rows(tm, D_MODEL, joined), pl.BlockSpec((1, D_MODEL), lambda i: (0, 0)),
                  mspec(shift_idx), mspec(shift_idx + 1)],
        out_specs=pl.BlockSpec((tm, D_MODEL), lambda i: (i, 0)),
        out_shape=jax.ShapeDtypeStruct((N_TOK, D_MODEL), BF16),
        compiler_params=_cparams(("parallel",)), name="adaln",
    )(*x, g.reshape(1, D_MODEL), mod, mod)


WEIGHT_ROW_CHUNK = 128


def _load_weight_bf16(src_rows, dst_ref, stage_ref, sem):
    n = dst_ref.shape[0] // WEIGHT_ROW_CHUNK

    def chunk_copy(k):
        return pltpu.make_async_copy(src_rows(k), stage_ref.at[k % 2], sem.at[k % 2])

    chunk_copy(0).start()
    for k in range(n):
        if k + 1 < n:
            chunk_copy(k + 1).start()
        chunk_copy(k).wait()
        dst_ref[k * WEIGHT_ROW_CHUNK:(k + 1) * WEIGHT_ROW_CHUNK, :] = stage_ref[k % 2].astype(BF16)


def _row_chunk(k):
    return pl.ds(k * WEIGHT_ROW_CHUNK, WEIGHT_ROW_CHUNK)


def _head_norm(x, gain, bd):
    sq = x * x
    hi = sq.astype(BF16)
    lo = (sq - hi.astype(F32)).astype(BF16)
    ms = jnp.dot(hi, bd, preferred_element_type=F32) + jnp.dot(lo, bd, preferred_element_type=F32)
    return x * lax.rsqrt(ms + RMS_EPS) * gain


def _rope(x, cos, sin_signed, first_half):
    swapped = jnp.where(first_half, pltpu.roll(x, LANES - HEAD_DIM // 2, 1), pltpu.roll(x, HEAD_DIM // 2, 1))
    return x * cos + swapped * sin_signed


def _in_proj_kernel(h_ref, cos_ref, sin_ref, ga_ref, gc_ref, w_hbm,
                    qa_ref, ka_ref, va_ref, fb_ref, qc_ref, kc_ref, vc_ref, gt_ref,
                    wb_ref, stage_ref, sem, *, l):
    @pl.when(pl.program_id(0) == 0)
    def _():
        _load_weight_bf16(lambda k: w_hbm.at[l, _row_chunk(k)], wb_ref, stage_ref, sem)

    h = h_ref[...]

    def proj(off, width):
        return jnp.dot(h, wb_ref[:, off:off + width], preferred_element_type=F32)

    r = lax.broadcasted_iota(jnp.int32, (LANES, LANES), 0) // HEAD_DIM
    c = lax.broadcasted_iota(jnp.int32, (LANES, LANES), 1) // HEAD_DIM
    bd = jnp.where(r == c, 1.0 / HEAD_DIM, 0.0).astype(BF16)
    lane = lax.broadcasted_iota(jnp.int32, (1, LANES), 1)
    first_half = (lane % HEAD_DIM) < HEAD_DIM // 2
    cos = cos_ref[...]
    sin = sin_ref[...]
    gqa, gka = ga_ref[0:1, :], ga_ref[1:2, :]
    gqc, gkc = gc_ref[0:1, :], gc_ref[1:2, :]
    slabs = lambda p: [p[:, s * LANES:(s + 1) * LANES] for s in range(p.shape[1] // LANES)]
    cat = lambda parts: jnp.concatenate(parts, axis=1)

    off = 0
    qa_ref[...] = cat([_rope(_head_norm(x, gqa, bd), cos, sin, first_half) for x in slabs(proj(off, A_Q))]
                      ).astype(BF16)
    off += A_Q
    kv = proj(off, 2 * A_KV)
    ka_ref[...] = _rope(_head_norm(kv[:, :A_KV], gka, bd), cos, sin, first_half)
    va_ref[...] = kv[:, A_KV:]
    off += 2 * A_KV
    fb_ref[...] = proj(off, B_WIDTH).astype(BF16)
    off += B_WIDTH
    qc_ref[...] = cat([_head_norm(x, gqc, bd) for x in slabs(proj(off, C_W))]).astype(BF16)
    off += C_W
    kc_ref[...] = cat([_head_norm(x, gkc, bd) for x in slabs(proj(off, C_W))])
    off += C_W
    vc_ref[...] = proj(off, C_W)
    off += C_W
    for j in range(N_BRANCH):
        cols = slice(j * D_MODEL, (j + 1) * D_MODEL)
        gt_ref[:, cols] = _sigmoid(proj(off + j * D_MODEL, D_MODEL)).astype(BF16)


def _in_proj(h, w_in, cos_t, sin_t, qk_a, qk_c, l):
    tm = 512
    row = lambda w: pl.BlockSpec((tm, w), lambda i: (i, 0))
    widths = (A_Q, A_KV, A_KV, B_WIDTH, C_W, C_W, C_W, GATE_WIDTH)
    dtypes = (BF16, F32, F32, BF16, BF16, F32, F32, BF16)
    return pl.pallas_call(
        functools.partial(_in_proj_kernel, l=l), grid=(N_TOK // tm,),
        in_specs=[row(D_MODEL), row(LANES), row(LANES),
                  pl.BlockSpec((2, LANES), lambda i: (0, 0)), pl.BlockSpec((2, LANES), lambda i: (0, 0)),
                  pl.BlockSpec(memory_space=pl.ANY)],
        out_specs=[row(w) for w in widths],
        out_shape=[jax.ShapeDtypeStruct((N_TOK, w), d) for w, d in zip(widths, dtypes)],
        scratch_shapes=[pltpu.VMEM((D_MODEL, IN_WIDTH), BF16),
                        pltpu.VMEM((2, WEIGHT_ROW_CHUNK, IN_WIDTH), F32), pltpu.SemaphoreType.DMA((2,))],
        compiler_params=_cparams(("arbitrary",)), name="in_proj",
    )(h, cos_t, sin_t, jnp.tile(qk_a, (1, 2)), jnp.tile(qk_c, (1, 2)), w_in)


def _nt_dot(a, b):
    return lax.dot_general(a, b, (((1,), (1,)), ((), ())), preferred_element_type=F32)


def _head(x, h):
    return x[:, h * HEAD_DIM:(h + 1) * HEAD_DIM]


def _stacked_softmax(parts, sink):
    m = parts[0].max(axis=-1, keepdims=True)
    for s in parts[1:]:
        m = jnp.maximum(m, s.max(axis=-1, keepdims=True))
    if sink is not None:
        m = jnp.maximum(m, sink)
    den = jnp.exp(sink - m) if sink is not None else 0.0
    es = []
    for s in parts:
        e = jnp.exp(s - m)
        den = den + e.sum(axis=-1, keepdims=True)
        es.append(e.astype(BF16))
    return es, 1.0 / den


def _sink_column(sink_ref, l, rows_per_head):
    return jnp.concatenate([jnp.full((rows_per_head, 1), sink_ref[l, h], F32) for h in range(A_HEADS)], axis=0)


def _gqa_queries(qa, g):
    return jnp.concatenate([_head(qa, g * A_GROUP + i) for i in range(A_GROUP)], axis=0)


def _ctx_attn_kernel(sink_ref, qa_ref, ka_ref, va_ref, qc_ref, kc_ref, vc_ref, oa_ref, oc_ref, *, l):
    t = SEQ
    qa = qa_ref[...] * SCALE
    ka = ka_ref[...].astype(BF16)
    va = va_ref[...].astype(BF16)
    s = jnp.concatenate([_nt_dot(_gqa_queries(qa, g), _head(ka, g)) for g in range(A_KV_HEADS)], axis=0)
    (e,), inv = _stacked_softmax([s], _sink_column(sink_ref, l, t))
    outs = []
    for g in range(A_KV_HEADS):
        rows = slice(g * A_GROUP * t, (g + 1) * A_GROUP * t)
        o = jnp.dot(e[rows], _head(va, g), preferred_element_type=F32) * inv[rows]
        outs += [o[i * t:(i + 1) * t] for i in range(A_GROUP)]
    oa_ref[...] = jnp.concatenate(outs, axis=1).astype(BF16)

    qc = qc_ref[...] * SCALE
    kc = kc_ref[...].astype(BF16)
    vc = vc_ref[...].astype(BF16)
    s = jnp.concatenate([_nt_dot(_head(qc, h), _head(kc, h)) for h in range(C_HEADS)], axis=0)
    (e,), inv = _stacked_softmax([s], None)
    outs = [jnp.dot(e[h * t:(h + 1) * t], _head(vc, h), preferred_element_type=F32) * inv[h * t:(h + 1) * t]
            for h in range(C_HEADS)]
    oc_ref[...] = jnp.concatenate(outs, axis=1).astype(BF16)


def _ctx_attn(sink_a, qa, ka, va, qc, kc, vc, l):
    blk = lambda w: pl.BlockSpec((SEQ, w), lambda b: (b, 0))
    return pl.pallas_call(
        functools.partial(_ctx_attn_kernel, l=l), grid=(BATCH,),
        in_specs=[pl.BlockSpec(memory_space=pltpu.SMEM),
                  blk(A_Q), blk(A_KV), blk(A_KV), blk(C_W), blk(C_W), blk(C_W)],
        out_specs=[blk(A_Q), blk(C_W)],
        out_shape=[jax.ShapeDtypeStruct((N_PROMPT, A_Q), BF16), jax.ShapeDtypeStruct((N_PROMPT, C_W), BF16)],
        compiler_params=_cparams(("parallel",)), name="ctx_attn",
    )(sink_a, qa, ka, va, qc, kc, vc)


def _win_attn_kernel(sink_ref, q_ref, kp_ref, kc_ref, kn_ref, vp_ref, vc_ref, vn_ref, ck_ref, cv_ref, o_ref, *, l):
    t = pl.program_id(1)
    rows = A_GROUP * A_BLOCK
    qi = lax.broadcasted_iota(jnp.int32, (rows, 3 * A_BLOCK), 0) % A_BLOCK
    kj = lax.broadcasted_iota(jnp.int32, (rows, 3 * A_BLOCK), 1) - A_BLOCK
    kpos = t * A_BLOCK + kj
    valid = (jnp.abs(kj - qi) <= A_WINDOW) & (kpos >= 0) & (kpos < DEC_SEQ)
    sink = _sink_column(sink_ref, l, A_BLOCK)
    for g in range(A_KV_HEADS):
        sl = slice(g * HEAD_DIM, (g + 1) * HEAD_DIM)
        q = jnp.concatenate([q_ref[:, (g * A_GROUP + i) * HEAD_DIM:(g * A_GROUP + i + 1) * HEAD_DIM]
                             for i in range(A_GROUP)], axis=0) * SCALE
        k_loc = jnp.concatenate([kp_ref[:, sl], kc_ref[:, sl], kn_ref[:, sl]], axis=0).astype(BF16)
        v_loc = jnp.concatenate([vp_ref[:, sl], vc_ref[:, sl], vn_ref[:, sl]], axis=0).astype(BF16)
        s_loc = jnp.where(valid, _nt_dot(q, k_loc), NEG_INF)
        s_ctx = _nt_dot(q, ck_ref[:, sl].astype(BF16))
        (e_loc, e_ctx), inv = _stacked_softmax([s_loc, s_ctx], sink[g * rows:(g + 1) * rows])
        o = (jnp.dot(e_loc, v_loc, preferred_element_type=F32)
             + jnp.dot(e_ctx, cv_ref[:, sl].astype(BF16), preferred_element_type=F32)) * inv
        for i in range(A_GROUP):
            h = g * A_GROUP + i
            o_ref[:, h * HEAD_DIM:(h + 1) * HEAD_DIM] = o[i * A_BLOCK:(i + 1) * A_BLOCK].astype(BF16)


def _win_attn(sink_a, qa, ka, va, cache_k, cache_v, l):
    nb = DEC_SEQ // A_BLOCK
    base = N_PROMPT // A_BLOCK

    def nbr(d):
        return lambda b, t: (base + b * nb + jnp.clip(t + d, 0, nb - 1), 0)

    kv = lambda d: pl.BlockSpec((A_BLOCK, A_KV), nbr(d))
    cache = pl.BlockSpec((None, None, PAST_LEN, A_KV), lambda b, t: (b, l, 0, 0))
    return pl.pallas_call(
        functools.partial(_win_attn_kernel, l=l), grid=(DEC_BATCH, nb),
        in_specs=[pl.BlockSpec(memory_space=pltpu.SMEM),
                  pl.BlockSpec((A_BLOCK, A_Q), nbr(0)),
                  kv(-1), kv(0), kv(1), kv(-1), kv(0), kv(1), cache, cache],
        out_specs=pl.BlockSpec((A_BLOCK, A_Q), lambda b, t: (b * nb + t, 0)),
        out_shape=jax.ShapeDtypeStruct((N_SAMPLE, A_Q), BF16),
        compiler_params=_cparams(("parallel", "parallel")), name="win_attn",
    )(sink_a, qa, ka, ka, ka, va, va, va, cache_k, cache_v)


def _nbr_attn_kernel(q_ref, kp_ref, kc_ref, kn_ref, vp_ref, vc_ref, vn_ref, ck_ref, cv_ref, tab_ref, o_ref,
                     bias_ref):
    j = pl.program_id(0)
    nb = pl.num_programs(0)
    slots = _nbr_row_slots()

    def build(cls):
        for h in range(C_HEADS):
            for qr in range(C_QROWS):
                for kk in range(3 * C_QROWS):
                    bias_ref[h, qr * GRID_W:(qr + 1) * GRID_W, kk * GRID_W:(kk + 1) * GRID_W] = (
                        tab_ref[h, slots[cls][qr][kk]])

    first_of_batch = pl.program_id(1) == 0
    for cls, at in enumerate((0, 1, nb - 1)):
        @pl.when(first_of_batch & (j == at))
        def _():
            build(cls)

    tq = C_QBLOCK
    q = q_ref[...] * SCALE
    k_loc = jnp.concatenate([kp_ref[...], kc_ref[...], kn_ref[...]], axis=0).astype(BF16)
    v_loc = jnp.concatenate([vp_ref[...], vc_ref[...], vn_ref[...]], axis=0).astype(BF16)
    k_ctx = ck_ref[...].astype(BF16)
    v_ctx = cv_ref[...].astype(BF16)
    s_loc = jnp.concatenate([_nt_dot(_head(q, h), _head(k_loc, h)) for h in range(C_HEADS)], axis=0)
    s_loc = s_loc + bias_ref[...].reshape(C_HEADS * tq, 3 * tq)
    s_ctx = jnp.concatenate([_nt_dot(_head(q, h), _head(k_ctx, h)) for h in range(C_HEADS)], axis=0)
    (e_loc, e_ctx), inv = _stacked_softmax([s_loc, s_ctx], None)
    outs = []
    for h in range(C_HEADS):
        r = slice(h * tq, (h + 1) * tq)
        outs.append((jnp.dot(e_loc[r], _head(v_loc, h), preferred_element_type=F32)
                     + jnp.dot(e_ctx[r], _head(v_ctx, h), preferred_element_type=F32)) * inv[r])
    o_ref[...] = jnp.concatenate(outs, axis=1).astype(BF16)


def _nbr_attn(qc, kc, vc, cache_k, cache_v, bias, l):
    nb = DEC_SEQ // C_QBLOCK
    base = N_PROMPT // C_QBLOCK

    def nbr(d):
        return lambda j, b: (base + b * nb + jnp.clip(j + d, 0, nb - 1), 0)

    kv = lambda d: pl.BlockSpec((C_QBLOCK, C_W), nbr(d))
    cache = pl.BlockSpec((None, None, PAST_LEN, C_W), lambda j, b: (b, l, 0, 0))
    return pl.pallas_call(
        _nbr_attn_kernel, grid=(nb, DEC_BATCH),
        in_specs=[kv(0), kv(-1), kv(0), kv(1), kv(-1), kv(0), kv(1), cache, cache,
                  pl.BlockSpec((None, C_HEADS, C_DR_SLOTS, GRID_W, GRID_W), lambda j, b: (l, 0, 0, 0, 0))],
        out_specs=pl.BlockSpec((C_QBLOCK, C_W), lambda j, b: (b * nb + j, 0)),
        out_shape=jax.ShapeDtypeStruct((N_SAMPLE, C_W), BF16),
        scratch_shapes=[pltpu.VMEM((C_HEADS, C_QBLOCK, 3 * C_QBLOCK), F32)],
        compiler_params=_cparams(("arbitrary", "arbitrary")), name="nbr_attn",
    )(qc, kc, kc, kc, vc, vc, vc, cache_k, cache_v, bias)


def _nbr_bias_tables(rpb):
    qcol = np.arange(GRID_W)
    qcs = np.clip(qcol - C_WIN_COLS // 2, 0, GRID_W - C_WIN_COLS)
    kcol = np.arange(GRID_W)
    col_ok = (kcol[None, :] >= qcs[:, None]) & (kcol[None, :] < qcs[:, None] + C_WIN_COLS)
    dc = np.clip(kcol[None, :] - qcol[:, None], -(C_WIN_COLS - 1), C_WIN_COLS - 1) + C_WIN_COLS - 1
    onehot_dc = (dc.reshape(-1)[None, :] == np.arange(2 * C_WIN_COLS - 1)[:, None]).astype(np.float32)
    t = jnp.einsum('lhab,bx->lhax', rpb, jnp.asarray(onehot_dc), precision=lax.Precision.HIGHEST)
    t = jnp.where(jnp.asarray(col_ok.reshape(-1)), t, NEG_INF)
    t = jnp.concatenate([t, jnp.full((DEPTH, C_HEADS, 1, GRID_W * GRID_W), NEG_INF, F32)], axis=2)
    return t.reshape(DEPTH, C_HEADS, C_DR_SLOTS, GRID_W, GRID_W)


def _nbr_row_slots():
    rows = DEC_SEQ // GRID_W
    slots = []
    for j in (0, 3, rows // C_QROWS - 1):
        per_q = []
        for qr in range(C_QROWS):
            r = C_QROWS * j + qr
            rs = min(max(r - C_WIN_ROWS // 2, 0), rows - C_WIN_ROWS)
            per_k = []
            for kk in range(3 * C_QROWS):
                kabs = C_QROWS * (j - 1) + kk
                per_k.append(kabs - r + C_WIN_ROWS - 1 if rs <= kabs < rs + C_WIN_ROWS else C_DR_SLOTS - 1)
            per_q.append(per_k)
        slots.append(per_q)
    return slots


def _fourier_kernel(u_ref, bc_ref, bs_ref, cl_ref, sl_ref, o_ref, zc_ref, zs_ref):
    @pl.when(pl.program_id(1) == 0)
    def _():
        u = u_ref[...]
        zc_ref[...] = jnp.dot(u, bc_ref[...].astype(BF16), preferred_element_type=F32).astype(BF16)
        zs_ref[...] = jnp.dot(u, bs_ref[...].astype(BF16), preferred_element_type=F32).astype(BF16)

    o = (jnp.dot(cl_ref[...].astype(BF16), zc_ref[...], preferred_element_type=F32)
         - jnp.dot(sl_ref[...].astype(BF16), zs_ref[...], preferred_element_type=F32))
    o_ref[...] = o.astype(BF16)


def _dft_tables(n):
    k = np.arange(n)
    ang = 2.0 * np.pi * ((k[:, None] * k[None, :]) % n) / n
    return np.cos(ang) / np.sqrt(n), np.sin(ang) / np.sqrt(n)


def _channel_dft_tables():
    c, s = _dft_tables(B_GROUP_DIM)
    eye = np.eye(B_GROUPS)
    return np.kron(eye, c).astype(np.float32), np.kron(eye, s).astype(np.float32)


def _fourier(fb, n_batch, seq, row0, tr):
    cl, sl = (jnp.asarray(a.astype(np.float32)) for a in _dft_tables(seq))
    bc, bs = (jnp.asarray(a) for a in _channel_dft_tables())
    nt = seq // tr
    const = pl.BlockSpec((B_WIDTH, B_WIDTH), lambda b, t: (0, 0))
    return pl.pallas_call(
        _fourier_kernel, grid=(n_batch, nt),
        in_specs=[pl.BlockSpec((seq, B_WIDTH), lambda b, t: (row0 // seq + b, 0)), const, const,
                  pl.BlockSpec((tr, seq), lambda b, t: (t, 0)), pl.BlockSpec((tr, seq), lambda b, t: (t, 0))],
        out_specs=pl.BlockSpec((tr, B_WIDTH), lambda b, t: (b * nt + t, 0)),
        out_shape=jax.ShapeDtypeStruct((n_batch * seq, B_WIDTH), BF16),
        scratch_shapes=[pltpu.VMEM((seq, B_WIDTH), BF16), pltpu.VMEM((seq, B_WIDTH), BF16)],
        compiler_params=_cparams(("parallel", "arbitrary")), name=f"fourier_{seq}",
    )(fb, bc, bs, cl, sl)


def _merge_kernel(oap_ref, obp_ref, ocp_ref, oas_ref, obs_ref, ocs_ref, gt_ref, xp_ref, xs_ref,
                  wa_ref, wb_ref, wc_ref, wo_ref, g1_ref, ng_ref, sh_ref, sc_ref, *rest, router):
    if router:
        wr_ref, o_ref, h2_ref, lg_ref, wab_ref, wbb_ref, wcb_ref, wob_ref = rest
    else:
        o_ref, h2_ref, wab_ref, wbb_ref, wcb_ref, wob_ref = rest

    @pl.when(pl.program_id(0) == 0)
    def _():
        wab_ref[...] = wa_ref[...].astype(BF16)
        wbb_ref[...] = wb_ref[...].astype(BF16)
        wcb_ref[...] = wc_ref[...].astype(BF16)
        wob_ref[...] = wo_ref[...].astype(BF16)

    ya = jnp.dot(_read_split(oap_ref, oas_ref), wab_ref[...], preferred_element_type=F32)
    yb = jnp.dot(_read_split(obp_ref, obs_ref), wbb_ref[...], preferred_element_type=F32)
    yc = jnp.dot(_read_split(ocp_ref, ocs_ref), wcb_ref[...], preferred_element_type=F32)
    d = D_MODEL
    m = (gt_ref[:, 0:d].astype(F32) * ya + gt_ref[:, d:2 * d].astype(F32) * yb
         + gt_ref[:, 2 * d:3 * d].astype(F32) * yc)
    y = jnp.dot(m.astype(BF16), wob_ref[...], preferred_element_type=F32)
    x_new = _read_split(xp_ref, xs_ref) + g1_ref[...] * y
    o_ref[...] = x_new
    h2 = _adaln_math(x_new, ng_ref[...], sh_ref[...], sc_ref[...])
    h2_ref[...] = h2.astype(h2_ref.dtype)
    if router:
        w = wr_ref[...]
        w_hi = w.astype(BF16)
        w_lo = (w - w_hi.astype(F32)).astype(BF16)
        h_hi = h2.astype(BF16)
        h_lo = (h2 - h_hi.astype(F32)).astype(BF16)
        lg_ref[...] = (jnp.dot(h_hi, w_hi, preferred_element_type=F32)
                       + jnp.dot(h_lo, w_hi, preferred_element_type=F32)
                       + jnp.dot(h_hi, w_lo, preferred_element_type=F32))


def _merge(branches_p, branches_s, gates, x, wa, wb, wc, wo, mod, l, norm_g, w_router=None):
    tm = 512
    router = w_router is not None
    x, joined = _split_operands(x)
    row = lambda w: pl.BlockSpec((tm, w), lambda i: (i, 0))
    row_p = lambda w: _split_rows(tm, w)[0]
    row_s = lambda w: _split_rows(tm, w)[1]
    const = lambda r, c: pl.BlockSpec((None, r, c), lambda i: (l, 0, 0), pipeline_mode=pl.Buffered(1))
    mspec = lambda which: pl.BlockSpec((None, None, 1, D_MODEL), lambda i: (l, _cond_row(i, tm), 0, which))
    in_specs = [row_p(A_Q), row_p(B_WIDTH), row_p(C_W), row_s(A_Q), row_s(B_WIDTH), row_s(C_W),
                row(GATE_WIDTH), *_split_rows(tm, D_MODEL, joined),
                const(A_Q, D_MODEL), const(B_WIDTH, D_MODEL), const(C_W, D_MODEL), const(D_MODEL, D_MODEL),
                mspec(2), pl.BlockSpec((1, D_MODEL), lambda i: (0, 0)), mspec(3), mspec(4)]
    operands = [*branches_p, *branches_s, gates, *x, wa, wb, wc, wo, mod, norm_g.reshape(1, D_MODEL), mod, mod]
    out_specs = [row(D_MODEL), row(D_MODEL)]
    out_shape = [jax.ShapeDtypeStruct((N_TOK, D_MODEL), F32),
                 jax.ShapeDtypeStruct((N_TOK, D_MODEL), F32 if router else BF16)]
    if router:
        in_specs.append(pl.BlockSpec((D_MODEL, LANES), lambda i: (0, 0)))
        operands.append(jnp.pad(w_router, ((0, 0), (0, LANES - N_EXPERTS))))
        out_specs.append(row(LANES))
        out_shape.append(jax.ShapeDtypeStruct((N_TOK, LANES), F32))
    return pl.pallas_call(
        functools.partial(_merge_kernel, router=router), grid=(N_TOK // tm,),
        in_specs=in_specs, out_specs=out_specs, out_shape=out_shape,
        scratch_shapes=[pltpu.VMEM((A_Q, D_MODEL), BF16), pltpu.VMEM((B_WIDTH, D_MODEL), BF16),
                        pltpu.VMEM((C_W, D_MODEL), BF16), pltpu.VMEM((D_MODEL, D_MODEL), BF16)],
        compiler_params=_cparams(("arbitrary",)), name="merge_router" if router else "merge",
    )(*operands)


FFN_COL_CHUNK = D_FF // 2


def _ffn_kernel(h_ref, x_ref, g2_ref, ng_ref, sh_ref, sc_ref, wg_hbm, wu_hbm, wd_hbm, o_ref, hn_ref,
                wgb_ref, wub_ref, wdb_ref, stage_up_ref, stage_dn_ref, sem, *, i_dense):
    @pl.when(pl.program_id(0) == 0)
    def _():
        _load_weight_bf16(lambda k: wg_hbm.at[i_dense, _row_chunk(k)], wgb_ref, stage_up_ref, sem)
        _load_weight_bf16(lambda k: wu_hbm.at[i_dense, _row_chunk(k)], wub_ref, stage_up_ref, sem)
        _load_weight_bf16(lambda k: wd_hbm.at[i_dense, _row_chunk(k)], wdb_ref, stage_dn_ref, sem)

    h = h_ref[...]
    acc = None
    for c in range(D_FF // FFN_COL_CHUNK):
        cols = slice(c * FFN_COL_CHUNK, (c + 1) * FFN_COL_CHUNK)
        g = jnp.dot(h, wgb_ref[:, cols], preferred_element_type=F32)
        u = jnp.dot(h, wub_ref[:, cols], preferred_element_type=F32)
        a = (g * _sigmoid(g) * u).astype(BF16)
        d = jnp.dot(a, wdb_ref[cols, :], preferred_element_type=F32)
        acc = d if acc is None else acc + d
    x_new = x_ref[...] + g2_ref[...] * acc
    o_ref[...] = x_new
    hn_ref[...] = _adaln_math(x_new, ng_ref[...], sh_ref[...], sc_ref[...]).astype(BF16)


def _ffn(h, x, wg, wu, wd, mod, l, next_norm_g):
    tm = 512
    row = lambda dt: pl.BlockSpec((tm, D_MODEL), lambda i: (i, 0))
    mspec = lambda layer, which: pl.BlockSpec((None, None, 1, D_MODEL),
                                              lambda i: (layer, _cond_row(i, tm), 0, which))
    hbm = pl.BlockSpec(memory_space=pl.ANY)
    return pl.pallas_call(
        functools.partial(_ffn_kernel, i_dense=l // 2), grid=(N_TOK // tm,),
        in_specs=[row(BF16), row(F32), mspec(l, 5),
                  pl.BlockSpec((1, D_MODEL), lambda i: (0, 0)), mspec(l + 1, 0), mspec(l + 1, 1),
                  hbm, hbm, hbm],
        out_specs=[row(F32), row(BF16)],
        out_shape=[jax.ShapeDtypeStruct((N_TOK, D_MODEL), F32), jax.ShapeDtypeStruct((N_TOK, D_MODEL), BF16)],
        scratch_shapes=[pltpu.VMEM((D_MODEL, D_FF), BF16), pltpu.VMEM((D_MODEL, D_FF), BF16),
                        pltpu.VMEM((D_FF, D_MODEL), BF16),
                        pltpu.VMEM((2, WEIGHT_ROW_CHUNK, D_FF), F32),
                        pltpu.VMEM((2, WEIGHT_ROW_CHUNK, D_MODEL), F32), pltpu.SemaphoreType.DMA((2,))],
        compiler_params=_cparams(("arbitrary",)), name="ffn",
    )(h, x, mod, next_norm_g.reshape(1, D_MODEL), mod, mod, wg, wu, wd)


TOP_K = 2
MOE_TILE = 256
MOE_TILES = TOP_K * N_TOK // MOE_TILE + N_EXPERTS
MOE_ROWS = MOE_TILES * MOE_TILE
MOE_CHUNK = 10
MOE_CHUNKS = MOE_TILES // MOE_CHUNK + N_EXPERTS
MOE_TF = 896
ROUTE_TM = 512
DISPATCH_TM = 256
COMBINE_TM = 256
ROW_COPY_UNROLL = 16


def _route_kernel(lg_ref, o_ref, cnt_ref, base_ref, tri_ref):
    tm = lg_ref.shape[0]

    @pl.when(pl.program_id(0) == 0)
    def _():
        base_ref[...] = jnp.zeros_like(base_ref)
        r = lax.broadcasted_iota(jnp.int32, (tm, tm), 0)
        c = lax.broadcasted_iota(jnp.int32, (tm, tm), 1)
        tri_ref[...] = jnp.where(r > c, 1.0, 0.0).astype(BF16)

    lane = lax.broadcasted_iota(jnp.int32, lg_ref.shape, 1).astype(F32)
    lg = jnp.where(lane < N_EXPERTS, lg_ref[...], -jnp.inf)
    m1 = lg.max(axis=-1, keepdims=True)
    i1 = jnp.where(lg == m1, lane, float(LANES)).min(axis=-1, keepdims=True)
    rest = jnp.where(lane == i1, -jnp.inf, lg)
    m2 = rest.max(axis=-1, keepdims=True)
    i2 = jnp.where(rest == m2, lane, float(LANES)).min(axis=-1, keepdims=True)
    e2 = jnp.exp(m2 - m1)
    w1 = 1.0 / (1.0 + e2)
    w2 = e2 / (1.0 + e2)

    oh1 = jnp.where(lane == i1, 1.0, 0.0)
    oh2 = jnp.where(lane == i2, 1.0, 0.0)
    pre1 = jnp.dot(tri_ref[...], oh1.astype(BF16), preferred_element_type=F32)
    pre2 = jnp.dot(tri_ref[...], oh2.astype(BF16), preferred_element_type=F32)
    c1 = jnp.sum(oh1, axis=0, keepdims=True)
    c2 = jnp.sum(oh2, axis=0, keepdims=True)
    base = base_ref[...]
    rank1 = jnp.sum(oh1 * (base + pre1), axis=-1, keepdims=True)
    rank2 = jnp.sum(oh2 * (base + c1 + pre2), axis=-1, keepdims=True)
    base_ref[...] = base + c1 + c2

    cols = (i1, i2, rank1, rank2, w1, w2)
    out = jnp.zeros(lg_ref.shape, F32)
    for j, col in enumerate(cols):
        out = jnp.where(lane == float(j), col, out)
    o_ref[...] = out
    cnt_ref[...] = jnp.broadcast_to(base + c1 + c2, cnt_ref.shape)


def _dispatch_kernel(pos_ref, last_ref, nt_ref, h_ref, xs_hbm, zero_ref, zsem, sem):
    i = pl.program_id(0)
    tm = h_ref.shape[0]

    @pl.when(i == 0)
    def _():
        zero_ref[...] = jnp.zeros_like(zero_ref)

        def zero_copy(tile):
            row0 = pl.multiple_of(tile * MOE_TILE, MOE_TILE)
            return pltpu.make_async_copy(zero_ref, xs_hbm.at[pl.ds(row0, MOE_TILE)], zsem)

        def for_zeroed_tiles(fn):
            for e in range(N_EXPERTS):
                @pl.when(last_ref[e] >= 0)
                def _():
                    fn(zero_copy(last_ref[e]))

                tail = MOE_TILES - 1 - e

                @pl.when(tail >= nt_ref[0])
                def _():
                    fn(zero_copy(tail))

        for_zeroed_tiles(lambda cp: cp.start())
        for_zeroed_tiles(lambda cp: cp.wait())

    def row_copy(t, k):
        dst = xs_hbm.at[pl.ds(pos_ref[TOP_K * (i * tm + t) + k], 1)]
        return pltpu.make_async_copy(h_ref.at[pl.ds(t, 1)], dst, sem)

    def issue(t, carry):
        for k in range(TOP_K):
            row_copy(t, k).start()
        return carry

    def drain(t, carry):
        for k in range(TOP_K):
            row_copy(t, k).wait()
        return carry

    lax.fori_loop(0, tm, issue, 0, unroll=ROW_COPY_UNROLL)
    lax.fori_loop(0, tm, drain, 0, unroll=ROW_COPY_UNROLL)


def _expert_kernel(ce_ref, ct_ref, cn_ref, nch_ref, nt_ref, xs_hbm, wg_ref, wu_ref, wd_ref, y_hbm,
                   acc_ref, xbuf_ref, wgb_ref, wub_ref, wdb_ref, xsem, osem):
    c = pl.program_id(0)
    f = pl.program_id(1)
    last_f = pl.num_programs(1) - 1

    def tile_rows(tile):
        return pl.ds(pl.multiple_of(tile * MOE_TILE, MOE_TILE), MOE_TILE)

    def out_copy(slot, tile):
        return pltpu.make_async_copy(acc_ref.at[slot], y_hbm.at[tile_rows(tile)], osem)

    @pl.when((c == 0) & (f == 0))
    def _():
        acc_ref[0] = jnp.zeros((MOE_TILE, D_MODEL), F32)
        for e in range(N_EXPERTS):
            tail = MOE_TILES - 1 - e

            @pl.when(tail >= nt_ref[0])
            def _():
                cp = out_copy(0, tail)
                cp.start()
                cp.wait()

    @pl.when(c < nch_ref[0])
    def _():
        n = cn_ref[c]
        t0 = ct_ref[c]

        def x_copy(j):
            return pltpu.make_async_copy(xs_hbm.at[tile_rows(t0 + j)], xbuf_ref.at[j], xsem.at[j])

        @pl.when(f == 0)
        def _():
            def fetch(j, carry):
                x_copy(j).start()
                return carry
            lax.fori_loop(0, n, fetch, 0)

        wgb_ref[...] = wg_ref[...].astype(BF16)
        wub_ref[...] = wu_ref[...].astype(BF16)
        wdb_ref[...] = wd_ref[...].astype(BF16)

        def tile_step(j, carry):
            @pl.when(f == 0)
            def _():
                x_copy(j).wait()

            x = xbuf_ref[j].astype(BF16)
            g = jnp.dot(x, wgb_ref[...], preferred_element_type=F32)
            u = jnp.dot(x, wub_ref[...], preferred_element_type=F32)
            a = (g * _sigmoid(g) * u).astype(BF16)
            d = jnp.dot(a, wdb_ref[...], preferred_element_type=F32)

            @pl.when(f == 0)
            def _():
                acc_ref[j] = d

            @pl.when(f > 0)
            def _():
                acc_ref[j] += d

            @pl.when(f == last_f)
            def _():
                @pl.when(j > 0)
                def _():
                    out_copy(j - 1, t0 + j - 1).wait()

                out_copy(j, t0 + j).start()

            return carry

        lax.fori_loop(0, n, tile_step, 0)

        @pl.when(f == last_f)
        def _():
            out_copy(n - 1, t0 + n - 1).wait()


def _combine_kernel(pos_ref, route_ref, x_ref, g2_ref, y_hbm, op_ref, os_ref, buf_ref, sem):
    i = pl.program_id(0)
    n = pl.num_programs(0)
    tm = x_ref.shape[0]
    slot = i % 2

    def row_copy(step, s, t, k):
        src = y_hbm.at[pl.ds(pos_ref[TOP_K * (step * tm + t) + k], 1)]
        return pltpu.make_async_copy(src, buf_ref.at[s, k, pl.ds(t, 1)], sem.at[s])

    def issue(step, s):
        def body(t, carry):
            for k in range(TOP_K):
                row_copy(step, s, t, k).start()
            return carry
        lax.fori_loop(0, tm, body, 0, unroll=ROW_COPY_UNROLL)

    @pl.when(i == 0)
    def _():
        issue(0, 0)

    @pl.when(i + 1 < n)
    def _():
        issue(i + 1, 1 - slot)

    def wait_body(t, carry):
        for k in range(TOP_K):
            row_copy(i, slot, t, k).wait()
        return carry

    lax.fori_loop(0, tm, wait_body, 0, unroll=ROW_COPY_UNROLL)

    lane = lax.broadcasted_iota(jnp.int32, route_ref.shape, 1)
    rt = route_ref[...]
    w1 = jnp.sum(jnp.where(lane == 2 * TOP_K, rt, 0.0), axis=-1, keepdims=True)
    w2 = jnp.sum(jnp.where(lane == 2 * TOP_K + 1, rt, 0.0), axis=-1, keepdims=True)
    out = x_ref[...] + g2_ref[...] * (w1 * buf_ref[slot, 0] + w2 * buf_ref[slot, 1])

    @pl.when(i < N_PROMPT // tm)
    def _():
        op_ref[...] = out

    @pl.when(i >= N_PROMPT // tm)
    def _():
        os_ref[...] = out


def _moe(h, x, logits, wg, wu, wd, mod, l):
    route, cnt = pl.pallas_call(
        _route_kernel, grid=(N_TOK // ROUTE_TM,),
        in_specs=[pl.BlockSpec((ROUTE_TM, LANES), lambda i: (i, 0))],
        out_specs=[pl.BlockSpec((ROUTE_TM, LANES), lambda i: (i, 0)), pl.BlockSpec((8, LANES), lambda i: (0, 0))],
        out_shape=[jax.ShapeDtypeStruct((N_TOK, LANES), F32), jax.ShapeDtypeStruct((8, LANES), F32)],
        scratch_shapes=[pltpu.VMEM((1, LANES), F32), pltpu.VMEM((ROUTE_TM, ROUTE_TM), BF16)],
        compiler_params=_cparams(("arbitrary",)), name="route",
    )(logits)

    expert = route[:, 0:TOP_K].astype(jnp.int32)
    rank = route[:, TOP_K:2 * TOP_K].astype(jnp.int32)
    n_sub = (cnt[0, :N_EXPERTS].astype(jnp.int32) + MOE_TILE - 1) // MOE_TILE
    end = jnp.cumsum(n_sub)
    start = end - n_sub
    start_of = jnp.sum(jnp.where(expert[:, :, None] == jnp.arange(N_EXPERTS), start, 0), axis=-1)
    pos = (start_of * MOE_TILE + rank).reshape(-1)
    n_tiles = end[N_EXPERTS - 1:]
    last_tile = jnp.where(n_sub > 0, end - 1, -1).astype(jnp.int32)

    xs = pl.pallas_call(
        _dispatch_kernel,
        grid_spec=pltpu.PrefetchScalarGridSpec(
            num_scalar_prefetch=3, grid=(N_TOK // DISPATCH_TM,),
            in_specs=[pl.BlockSpec((DISPATCH_TM, D_MODEL), lambda i, p, lt, nt: (i, 0))],
            out_specs=pl.BlockSpec(memory_space=pl.ANY),
            scratch_shapes=[pltpu.VMEM((MOE_TILE, D_MODEL), F32), pltpu.SemaphoreType.DMA(()),
                            pltpu.SemaphoreType.DMA(())]),
        out_shape=jax.ShapeDtypeStruct((MOE_ROWS, D_MODEL), F32),
        compiler_params=_cparams(("arbitrary",)), name="moe_dispatch",
    )(pos, last_tile, n_tiles, h)

    n_chunk = (n_sub + MOE_CHUNK - 1) // MOE_CHUNK
    chunk_end = jnp.cumsum(n_chunk)
    cidx = jnp.arange(MOE_CHUNKS)
    c_expert = jnp.minimum(jnp.sum(cidx[:, None] >= chunk_end[None, :], axis=1), N_EXPERTS - 1).astype(jnp.int32)
    c_k = cidx - (chunk_end - n_chunk)[c_expert]
    c_tile0 = (start[c_expert] + c_k * MOE_CHUNK).astype(jnp.int32)
    c_ntiles = jnp.clip(n_sub[c_expert] - c_k * MOE_CHUNK, 0, MOE_CHUNK).astype(jnp.int32)
    n_chunks = chunk_end[N_EXPERTS - 1:]

    nf = D_FF_EXPERT // MOE_TF

    def w_idx(c, f, ce, nch):
        live = c < nch[0]
        return ce[jnp.minimum(c, nch[0] - 1)], jnp.where(live, f, nf - 1)

    def up_map(c, f, ce, ct, cn, nch, nt):
        e, ff = w_idx(c, f, ce, nch)
        return (e, 0, ff)

    def down_map(c, f, ce, ct, cn, nch, nt):
        e, ff = w_idx(c, f, ce, nch)
        return (e, ff, 0)

    y = pl.pallas_call(
        _expert_kernel,
        grid_spec=pltpu.PrefetchScalarGridSpec(
            num_scalar_prefetch=5, grid=(MOE_CHUNKS, nf),
            in_specs=[pl.BlockSpec(memory_space=pl.ANY),
                      pl.BlockSpec((None, D_MODEL, MOE_TF), up_map),
                      pl.BlockSpec((None, D_MODEL, MOE_TF), up_map),
                      pl.BlockSpec((None, MOE_TF, D_MODEL), down_map)],
            out_specs=pl.BlockSpec(memory_space=pl.ANY),
            scratch_shapes=[pltpu.VMEM((MOE_CHUNK, MOE_TILE, D_MODEL), F32),
                            pltpu.VMEM((MOE_CHUNK, MOE_TILE, D_MODEL), F32),
                            pltpu.VMEM((D_MODEL, MOE_TF), BF16), pltpu.VMEM((D_MODEL, MOE_TF), BF16),
                            pltpu.VMEM((MOE_TF, D_MODEL), BF16),
                            pltpu.SemaphoreType.DMA((MOE_CHUNK,)), pltpu.SemaphoreType.DMA(())]),
        out_shape=jax.ShapeDtypeStruct((MOE_ROWS, D_MODEL), F32),
        compiler_params=_cparams(("arbitrary", "arbitrary")), name="moe_experts",
    )(c_expert, c_tile0, c_ntiles, n_chunks, n_tiles, xs, wg, wu, wd)

    tm = COMBINE_TM
    return pl.pallas_call(
        _combine_kernel,
        grid_spec=pltpu.PrefetchScalarGridSpec(
            num_scalar_prefetch=1, grid=(N_TOK // tm,),
            in_specs=[pl.BlockSpec((tm, LANES), lambda i, p: (i, 0)),
                      pl.BlockSpec((tm, D_MODEL), lambda i, p: (i, 0)),
                      pl.BlockSpec((None, None, 1, D_MODEL), lambda i, p: (l, _cond_row(i, tm), 0, 5)),
                      pl.BlockSpec(memory_space=pl.ANY)],
            out_specs=[pl.BlockSpec((tm, D_MODEL), lambda i, p: (jnp.minimum(i, N_PROMPT // tm - 1), 0)),
                       pl.BlockSpec((tm, D_MODEL), lambda i, p: (jnp.maximum(i - N_PROMPT // tm, 0), 0))],
            scratch_shapes=[pltpu.VMEM((2, TOP_K, tm, D_MODEL), F32), pltpu.SemaphoreType.DMA((2,))]),
        out_shape=[jax.ShapeDtypeStruct((N_PROMPT, D_MODEL), F32), jax.ShapeDtypeStruct((N_SAMPLE, D_MODEL), F32)],
        compiler_params=_cparams(("arbitrary",)), name="moe_combine",
    )(pos, route, x, mod, y)


def _rope_tables():
    t = jnp.arange(DEC_SEQ)
    row = (t // GRID_W).astype(F32)
    col = (t % GRID_W).astype(F32)
    n_freq = HEAD_DIM // 4
    inv = ROPE_BASE ** (-jnp.arange(n_freq, dtype=F32) / n_freq)
    ang = jnp.concatenate([row[:, None] * inv, col[:, None] * inv], axis=-1)
    cos, sin = jnp.cos(ang), jnp.sin(ang)
    cos_h = jnp.concatenate([cos, cos], axis=-1)
    sin_h = jnp.concatenate([-sin, sin], axis=-1)
    cos_l = jnp.tile(jnp.concatenate([cos_h, cos_h], axis=-1), (DEC_BATCH, 1))
    sin_l = jnp.tile(jnp.concatenate([sin_h, sin_h], axis=-1), (DEC_BATCH, 1))
    cos_t = jnp.concatenate([jnp.ones((N_PROMPT, LANES), F32), cos_l], axis=0)
    sin_t = jnp.concatenate([jnp.zeros((N_PROMPT, LANES), F32), sin_l], axis=0)
    return cos_t, sin_t


def kernel(x_prompt, x_sample, cache_a_k, cache_a_v, cache_c_k, cache_c_v, c, c_ctx, w_mod, b_mod, norm1_g, norm2_g, w_in, qk_norm_a, qk_norm_c, sink_a, rpb_c, w_branch_a, w_branch_b, w_branch_c, w_out, w_ff_gate, w_ff_up, w_ff_down, w_router, w_exp_gate, w_exp_up, w_exp_down):
    x = (x_prompt.reshape(N_PROMPT, D_MODEL), x_sample.reshape(N_SAMPLE, D_MODEL))
    cond =jnp.concatenate([c_ctx[None, :], c], axis=0)
    cond_t = jnp.broadcast_to(cond[:, :, None], (N_COND, D_MODEL, LANES))
    mod = _modulation(cond_t, w_mod, b_mod)
    cos_t, sin_t = _rope_tables()
    bias = _nbr_bias_tables(rpb_c)
    ck_a = cache_a_k.reshape(DEC_BATCH, DEPTH, PAST_LEN, A_KV)
    cv_a = cache_a_v.reshape(DEC_BATCH, DEPTH, PAST_LEN, A_KV)
    ck_c = cache_c_k.reshape(DEC_BATCH, DEPTH, PAST_LEN, C_W)
    cv_c = cache_c_v.reshape(DEC_BATCH, DEPTH, PAST_LEN, C_W)

    new_ak, new_av, new_ck, new_cv = [], [], [], []
    h = _adaln(x, norm1_g[0], mod, 0, 0)
    for l in range(DEPTH):
        qa, ka, va, fb, qc, kc, vc, gates = _in_proj(h, w_in, cos_t, sin_t, qk_norm_a[l], qk_norm_c[l], l)
        oa_p, oc_p = _ctx_attn(sink_a, qa, ka, va, qc, kc, vc, l)
        oa_s = _win_attn(sink_a, qa, ka, va, ck_a, cv_a, l)
        oc_s = _nbr_attn(qc, kc, vc, ck_c, cv_c, bias, l)
        ob_p = _fourier(fb, BATCH, SEQ, 0, SEQ)
        ob_s = _fourier(fb, DEC_BATCH, DEC_SEQ, N_PROMPT, 512)
        branches = ((oa_p, ob_p, oc_p), (oa_s, ob_s, oc_s), gates, x,
                    w_branch_a, w_branch_b, w_branch_c, w_out, mod, l, norm2_g[l])
        i = l // 2
        if l % 2 == 0:
            x, h2 = _merge(*branches)
            x, h = _ffn(h2, x, w_ff_gate, w_ff_up, w_ff_down, mod, l, norm1_g[l + 1])
        else:
            x, h2, logits = _merge(*branches, w_router=w_router[i])
            xp, xs = _moe(h2, x, logits, w_exp_gate[i], w_exp_up[i], w_exp_down[i], mod, l)
        new_ak.append(ka[:N_PROMPT].reshape(BATCH, SEQ, A_KV_HEADS, HEAD_DIM))
        new_av.append(va[:N_PROMPT].reshape(BATCH, SEQ, A_KV_HEADS, HEAD_DIM))
        new_ck.append(kc[:N_PROMPT].reshape(BATCH, SEQ, C_HEADS, HEAD_DIM))
        new_cv.append(vc[:N_PROMPT].reshape(BATCH, SEQ, C_HEADS, HEAD_DIM))

    return (xp.reshape(BATCH, SEQ, D_MODEL), xs.reshape(DEC_BATCH, DEC_SEQ, D_MODEL),
            jnp.stack(new_ak, axis=1), jnp.stack(new_av, axis=1),
            jnp.stack(new_ck, axis=1), jnp.stack(new_cv, axis=1))
```

```python
import functools

import numpy as np
import jax
import jax.numpy as jnp
from jax import lax
from jax.experimental import pallas as pl
from jax.experimental.pallas import tpu as pltpu

F32 = jnp.float32
BF16 = jnp.bfloat16

D_MODEL = 1024
BATCH = 16
SEQ = 256
DEPTH = 2
DEC_BATCH = 2
DEC_SEQ = 2048
PAST_LEN = 512
GRID_W = 64
HEAD_DIM = 64
SCALE = HEAD_DIM ** -0.5
A_HEADS = 8
A_KV_HEADS = 2
A_GROUP = A_HEADS // A_KV_HEADS
A_WINDOW = 128
A_BLOCK = 128
B_GROUPS = 8
B_GROUP_DIM = 64
B_WIDTH = B_GROUPS * B_GROUP_DIM
C_HEADS = 8
C_WIN_ROWS = 8
C_WIN_COLS = 16
A_Q = A_HEADS * HEAD_DIM
A_KV = A_KV_HEADS * HEAD_DIM
C_W = C_HEADS * HEAD_DIM
QKV_WIDTH = A_Q + 2 * A_KV + B_WIDTH + 3 * C_W
N_BRANCH = 3
GATE_WIDTH = N_BRANCH * D_MODEL
IN_WIDTH = QKV_WIDTH + GATE_WIDTH
D_FF = 2816
N_EXPERTS = 8
D_FF_EXPERT = 3584
ROPE_BASE = 10000.0
RMS_EPS = 1e-6
NEG_INF = -1e30

N_PROMPT = BATCH * SEQ
N_SAMPLE = DEC_BATCH * DEC_SEQ
N_TOK = N_PROMPT + N_SAMPLE
N_COND = 1 + DEC_BATCH
LANES = 128
NORM_SLAB = 256
C_QROWS = 4
C_QBLOCK = C_QROWS * GRID_W
C_DR_SLOTS = 2 * C_WIN_ROWS
VMEM_LIMIT = 56 * 1024 * 1024


def _cparams(sem):
    return pltpu.CompilerParams(dimension_semantics=sem, vmem_limit_bytes=VMEM_LIMIT)


def _sigmoid(x):
    return 1.0 / (1.0 + jnp.exp(-x))


def _cond_row(tile, tm):
    return jnp.maximum(tile * tm // DEC_SEQ - 1, 0)


def _mod_kernel(ct_ref, w_ref, b_ref, o_ref, silu_ref):
    @pl.when((pl.program_id(0) == 0) & (pl.program_id(1) == 0))
    def _():
        cb = ct_ref[...]
        silu_ref[...] = cb * _sigmoid(cb)

    tn = w_ref.shape[1]
    for r in range(N_COND):
        s = silu_ref[r]
        for cc in range(tn // LANES):
            sl = slice(cc * LANES, (cc + 1) * LANES)
            o_ref[r, :, sl] = jnp.sum(w_ref[:, sl] * s, axis=0, keepdims=True) + b_ref[:, sl]


def _modulation(cond_t, w_mod, b_mod):
    tn = 1024
    n = 6 * D_MODEL
    return pl.pallas_call(
        _mod_kernel,
        grid=(DEPTH, n // tn),
        in_specs=[
            pl.BlockSpec((N_COND, D_MODEL, LANES), lambda l, j: (0, 0, 0)),
            pl.BlockSpec((None, D_MODEL, tn), lambda l, j: (l, 0, j)),
            pl.BlockSpec((None, 1, tn), lambda l, j: (l, 0, j)),
        ],
        out_specs=pl.BlockSpec((None, N_COND, 1, tn), lambda l, j: (l, 0, 0, j)),
        out_shape=jax.ShapeDtypeStruct((DEPTH, N_COND, 1, n), F32),
        scratch_shapes=[pltpu.VMEM((N_COND, D_MODEL, LANES), F32)],
        compiler_params=_cparams(("arbitrary", "arbitrary")),
        name="modulation",
    )(cond_t, w_mod, b_mod.reshape(DEPTH, 1, n))


def _adaln_math(x, g, sh, sc):
    ms = jnp.mean(x * x, axis=-1, keepdims=True)
    y = x * lax.rsqrt(ms + RMS_EPS) * g
    return y * (1.0 + sc) + sh


def _split_rows(tm, width, joined=False):
    n_p = N_PROMPT // tm
    latent0 = n_p if joined else 0
    return (pl.BlockSpec((tm, width), lambda i: (jnp.minimum(i, n_p - 1), 0)),
            pl.BlockSpec((tm, width), lambda i: (jnp.maximum(i - n_p, 0) + latent0, 0)))


def _split_operands(x):
    return (x, False) if isinstance(x, tuple) else ((x, x), True)


def _read_split(p_ref, s_ref):
    is_ctx = pl.program_id(0) < N_PROMPT // p_ref.shape[0]
    return jnp.where(is_ctx, p_ref[...], s_ref[...])


def _adaln_kernel(xp_ref, xs_ref, g_ref, sh_ref, sc_ref, o_ref):
    o_ref[...] = _adaln_math(_read_split(xp_ref, xs_ref), g_ref[...], sh_ref[...], sc_ref[...]).astype(BF16)


def _adaln(x, g, mod, l, shift_idx):
    tm = 512
    x, joined = _split_operands(x)
    mspec = lambda which: pl.BlockSpec((None, None, 1, D_MODEL),
                                       lambda i: (l, _cond_row(i, tm), 0, which))
    return pl.pallas_call(
        _adaln_kernel, grid=(N_TOK // tm,),
        in_specs=[*_split_rows(tm, D_MODEL, joined), pl.BlockSpec((1, D_MODEL), lambda i: (0, 0)),
                  mspec(shift_idx), mspec(shift_idx + 1)],
        out_specs=pl.BlockSpec((tm, D_MODEL), lambda i: (i, 0)),
        out_shape=jax.ShapeDtypeStruct((N_TOK, D_MODEL), BF16),
        compiler_params=_cparams(("parallel",)), name="adaln",
    )(*x, g.reshape(1, D_MODEL), mod, mod)


WEIGHT_ROW_CHUNK = 128


def _load_weight_bf16(src_rows, dst_ref, stage_ref, sem):
    n = dst_ref.shape[0] // WEIGHT_ROW_CHUNK

    def chunk_copy(k):
        return pltpu.make_async_copy(src_rows(k), stage_ref.at[k % 2], sem.at[k % 2])

    chunk_copy(0).start()
    for k in range(n):
        if k + 1 < n:
            chunk_copy(k + 1).start()
        chunk_copy(k).wait()
        dst_ref[k * WEIGHT_ROW_CHUNK:(k + 1) * WEIGHT_ROW_CHUNK, :] = stage_ref[k % 2].astype(BF16)


def _row_chunk(k):
    return pl.ds(k * WEIGHT_ROW_CHUNK, WEIGHT_ROW_CHUNK)


def _head_norm(x, gain, bd):
    sq = x * x
    hi = sq.astype(BF16)
    lo = (sq - hi.astype(F32)).astype(BF16)
    ms = jnp.dot(hi, bd, preferred_element_type=F32) + jnp.dot(lo, bd, preferred_element_type=F32)
    return x * lax.rsqrt(ms + RMS_EPS) * gain


def _rope(x, cos, sin_signed, first_half):
    half = HEAD_DIM // 2
    swapped = jnp.where(first_half, pltpu.roll(x, x.shape[1] - half, 1), pltpu.roll(x, half, 1))
    return x * cos + swapped * sin_signed


def _in_proj_kernel(h_ref, cos_ref, sin_ref, ga_ref, gc_ref, w_hbm,
                    qa_ref, ka_ref, va_ref, fb_ref, qc_ref, kc_ref, vc_ref, gt_ref,
                    wb_ref, stage_ref, sem, *, l):
    @pl.when(pl.program_id(0) == 0)
    def _():
        _load_weight_bf16(lambda k: w_hbm.at[l, _row_chunk(k)], wb_ref, stage_ref, sem)

    h = h_ref[...]

    def proj(off, width):
        return jnp.dot(h, wb_ref[:, off:off + width], preferred_element_type=F32)

    r = lax.broadcasted_iota(jnp.int32, (NORM_SLAB, NORM_SLAB), 0) // HEAD_DIM
    c = lax.broadcasted_iota(jnp.int32, (NORM_SLAB, NORM_SLAB), 1) // HEAD_DIM
    bd = jnp.where(r == c, 1.0 / HEAD_DIM, 0.0).astype(BF16)
    lane = lax.broadcasted_iota(jnp.int32, (1, NORM_SLAB), 1)
    first_half = (lane % HEAD_DIM) < HEAD_DIM // 2
    cos = jnp.concatenate([cos_ref[...]] * (NORM_SLAB // LANES), axis=1)
    sin = jnp.concatenate([sin_ref[...]] * (NORM_SLAB // LANES), axis=1)
    gqa, gka = ga_ref[0:1, :], ga_ref[1:2, :]
    gqc, gkc = gc_ref[0:1, :], gc_ref[1:2, :]
    slabs = lambda p: [p[:, s * NORM_SLAB:(s + 1) * NORM_SLAB] for s in range(p.shape[1] // NORM_SLAB)]
    cat = lambda parts: jnp.concatenate(parts, axis=1)

    off = 0
    qa_ref[...] = cat([_rope(_head_norm(x, gqa, bd), cos, sin, first_half) for x in slabs(proj(off, A_Q))]
                      ).astype(BF16)
    off += A_Q
    kv = proj(off, 2 * A_KV)
    ka_ref[...] = _rope(_head_norm(kv, gka, bd), cos, sin, first_half)[:, :A_KV]
    va_ref[...] = kv[:, A_KV:]
    off += 2 * A_KV
    fb_ref[...] = proj(off, B_WIDTH).astype(BF16)
    off += B_WIDTH
    qc_ref[...] = cat([_head_norm(x, gqc, bd) for x in slabs(proj(off, C_W))]).astype(BF16)
    off += C_W
    kc_ref[...] = cat([_head_norm(x, gkc, bd) for x in slabs(proj(off, C_W))])
    off += C_W
    vc_ref[...] = proj(off, C_W)
    off += C_W
    for j in range(N_BRANCH):
        cols = slice(j * D_MODEL, (j + 1) * D_MODEL)
        gt_ref[:, cols] = _sigmoid(proj(off + j * D_MODEL, D_MODEL)).astype(BF16)


def _in_proj(h, w_in, cos_t, sin_t, qk_a, qk_c, l):
    tm = 512
    row = lambda w: pl.BlockSpec((tm, w), lambda i: (i, 0))
    widths = (A_Q, A_KV, A_KV, B_WIDTH, C_W, C_W, C_W, GATE_WIDTH)
    dtypes = (BF16, F32, F32, BF16, BF16, F32, F32, BF16)
    return pl.pallas_call(
        functools.partial(_in_proj_kernel, l=l), grid=(N_TOK // tm,),
        in_specs=[row(D_MODEL), row(LANES), row(LANES),
                  pl.BlockSpec((2, NORM_SLAB), lambda i: (0, 0)), pl.BlockSpec((2, NORM_SLAB), lambda i: (0, 0)),
                  pl.BlockSpec(memory_space=pl.ANY)],
        out_specs=[row(w) for w in widths],
        out_shape=[jax.ShapeDtypeStruct((N_TOK, w), d) for w, d in zip(widths, dtypes)],
        scratch_shapes=[pltpu.VMEM((D_MODEL, IN_WIDTH), BF16),
                        pltpu.VMEM((2, WEIGHT_ROW_CHUNK, IN_WIDTH), F32), pltpu.SemaphoreType.DMA((2,))],
        compiler_params=_cparams(("arbitrary",)), name="in_proj",
    )(h, cos_t, sin_t, jnp.tile(qk_a, (1, NORM_SLAB // HEAD_DIM)), jnp.tile(qk_c, (1, NORM_SLAB // HEAD_DIM)), w_in)


def _nt_dot(a, b):
    return lax.dot_general(a, b, (((1,), (1,)), ((), ())), preferred_element_type=F32)


def _head(x, h):
    return x[:, h * HEAD_DIM:(h + 1) * HEAD_DIM]


def _stacked_softmax(parts, sink):
    m = parts[0].max(axis=-1, keepdims=True)
    for s in parts[1:]:
        m = jnp.maximum(m, s.max(axis=-1, keepdims=True))
    if sink is not None:
        m = jnp.maximum(m, sink)
    den = jnp.exp(sink - m) if sink is not None else 0.0
    es = []
    for s in parts:
        e = jnp.exp(s - m)
        den = den + e.sum(axis=-1, keepdims=True)
        es.append(e.astype(BF16))
    return es, 1.0 / den


def _sink_column(sink_ref, l, rows_per_head):
    return jnp.concatenate([jnp.full((rows_per_head, 1), sink_ref[l, h], F32) for h in range(A_HEADS)], axis=0)


def _gqa_queries(qa, g):
    return jnp.concatenate([_head(qa, g * A_GROUP + i) for i in range(A_GROUP)], axis=0)


def _ctx_attn_kernel(sink_ref, qa_ref, ka_ref, va_ref, qc_ref, kc_ref, vc_ref, oa_ref, oc_ref, *, l):
    t = SEQ
    qa = qa_ref[...] * SCALE
    ka = ka_ref[...].astype(BF16)
    va = va_ref[...].astype(BF16)
    s = jnp.concatenate([_nt_dot(_gqa_queries(qa, g), _head(ka, g)) for g in range(A_KV_HEADS)], axis=0)
    (e,), inv = _stacked_softmax([s], _sink_column(sink_ref, l, t))
    outs = []
    for g in range(A_KV_HEADS):
        rows = slice(g * A_GROUP * t, (g + 1) * A_GROUP * t)
        o = jnp.dot(e[rows], _head(va, g), preferred_element_type=F32) * inv[rows]
        outs += [o[i * t:(i + 1) * t] for i in range(A_GROUP)]
    oa_ref[...] = jnp.concatenate(outs, axis=1).astype(BF16)

    qc = qc_ref[...] * SCALE
    kc = kc_ref[...].astype(BF16)
    vc = vc_ref[...].astype(BF16)
    s = jnp.concatenate([_nt_dot(_head(qc, h), _head(kc, h)) for h in range(C_HEADS)], axis=0)
    (e,), inv = _stacked_softmax([s], None)
    outs = [jnp.dot(e[h * t:(h + 1) * t], _head(vc, h), preferred_element_type=F32) * inv[h * t:(h + 1) * t]
            for h in range(C_HEADS)]
    oc_ref[...] = jnp.concatenate(outs, axis=1).astype(BF16)


def _ctx_attn(sink_a, qa, ka, va, qc, kc, vc, l):
    blk = lambda w: pl.BlockSpec((SEQ, w), lambda b: (b, 0))
    return pl.pallas_call(
        functools.partial(_ctx_attn_kernel, l=l), grid=(BATCH,),
        in_specs=[pl.BlockSpec(memory_space=pltpu.SMEM),
                  blk(A_Q), blk(A_KV), blk(A_KV), blk(C_W), blk(C_W), blk(C_W)],
        out_specs=[blk(A_Q), blk(C_W)],
        out_shape=[jax.ShapeDtypeStruct((N_PROMPT, A_Q), BF16), jax.ShapeDtypeStruct((N_PROMPT, C_W), BF16)],
        compiler_params=_cparams(("parallel",)), name="ctx_attn",
    )(sink_a, qa, ka, va, qc, kc, vc)


def _win_attn_kernel(sink_ref, q_ref, kp_ref, kc_ref, kn_ref, vp_ref, vc_ref, vn_ref, ck_ref, cv_ref, o_ref, *, l):
    t = pl.program_id(1)
    rows = A_GROUP * A_BLOCK
    qi = lax.broadcasted_iota(jnp.int32, (rows, 3 * A_BLOCK), 0) % A_BLOCK
    kj = lax.broadcasted_iota(jnp.int32, (rows, 3 * A_BLOCK), 1) - A_BLOCK
    kpos = t * A_BLOCK + kj
    valid = (jnp.abs(kj - qi) <= A_WINDOW) & (kpos >= 0) & (kpos < DEC_SEQ)
    sink = _sink_column(sink_ref, l, A_BLOCK)
    for g in range(A_KV_HEADS):
        sl = slice(g * HEAD_DIM, (g + 1) * HEAD_DIM)
        q = jnp.concatenate([q_ref[:, (g * A_GROUP + i) * HEAD_DIM:(g * A_GROUP + i + 1) * HEAD_DIM]
                             for i in range(A_GROUP)], axis=0) * SCALE
        k_loc = jnp.concatenate([kp_ref[:, sl], kc_ref[:, sl], kn_ref[:, sl]], axis=0).astype(BF16)
        v_loc = jnp.concatenate([vp_ref[:, sl], vc_ref[:, sl], vn_ref[:, sl]], axis=0).astype(BF16)
        s_loc = jnp.where(valid, _nt_dot(q, k_loc), NEG_INF)
        s_ctx = _nt_dot(q, ck_ref[:, sl].astype(BF16))
        (e_loc, e_ctx), inv = _stacked_softmax([s_loc, s_ctx], sink[g * rows:(g + 1) * rows])
        o = (jnp.dot(e_loc, v_loc, preferred_element_type=F32)
             + jnp.dot(e_ctx, cv_ref[:, sl].astype(BF16), preferred_element_type=F32)) * inv
        for i in range(A_GROUP):
            h = g * A_GROUP + i
            o_ref[:, h * HEAD_DIM:(h + 1) * HEAD_DIM] = o[i * A_BLOCK:(i + 1) * A_BLOCK].astype(BF16)


def _win_attn(sink_a, qa, ka, va, cache_k, cache_v, l):
    nb = DEC_SEQ // A_BLOCK
    base = N_PROMPT // A_BLOCK

    def nbr(d):
        return lambda b, t: (base + b * nb + jnp.clip(t + d, 0, nb - 1), 0)

    kv = lambda d: pl.BlockSpec((A_BLOCK, A_KV), nbr(d))
    cache = pl.BlockSpec((None, None, PAST_LEN, A_KV), lambda b, t: (b, l, 0, 0))
    return pl.pallas_call(
        functools.partial(_win_attn_kernel, l=l), grid=(DEC_BATCH, nb),
        in_specs=[pl.BlockSpec(memory_space=pltpu.SMEM),
                  pl.BlockSpec((A_BLOCK, A_Q), nbr(0)),
                  kv(-1), kv(0), kv(1), kv(-1), kv(0), kv(1), cache, cache],
        out_specs=pl.BlockSpec((A_BLOCK, A_Q), lambda b, t: (b * nb + t, 0)),
        out_shape=jax.ShapeDtypeStruct((N_SAMPLE, A_Q), BF16),
        compiler_params=_cparams(("parallel", "parallel")), name="win_attn",
    )(sink_a, qa, ka, ka, ka, va, va, va, cache_k, cache_v)


def _nbr_attn_kernel(q_ref, kp_ref, kc_ref, kn_ref, vp_ref, vc_ref, vn_ref, ck_ref, cv_ref, tab_ref, o_ref,
                     bias_ref):
    j = pl.program_id(0)
    nb = pl.num_programs(0)
    slots = _nbr_row_slots()

    def build(cls):
        for h in range(C_HEADS):
            for qr in range(C_QROWS):
                for kk in range(3 * C_QROWS):
                    bias_ref[h, qr * GRID_W:(qr + 1) * GRID_W, kk * GRID_W:(kk + 1) * GRID_W] = (
                        tab_ref[h, slots[cls][qr][kk]])

    first_of_batch = pl.program_id(1) == 0
    for cls, at in enumerate((0, 1, nb - 1)):
        @pl.when(first_of_batch & (j == at))
        def _():
            build(cls)

    tq = C_QBLOCK
    q = q_ref[...] * SCALE
    k_loc = jnp.concatenate([kp_ref[...], kc_ref[...], kn_ref[...]], axis=0).astype(BF16)
    v_loc = jnp.concatenate([vp_ref[...], vc_ref[...], vn_ref[...]], axis=0).astype(BF16)
    k_ctx = ck_ref[...].astype(BF16)
    v_ctx = cv_ref[...].astype(BF16)
    s_loc = jnp.concatenate([_nt_dot(_head(q, h), _head(k_loc, h)) for h in range(C_HEADS)], axis=0)
    s_loc = s_loc + bias_ref[...].reshape(C_HEADS * tq, 3 * tq)
    s_ctx = jnp.concatenate([_nt_dot(_head(q, h), _head(k_ctx, h)) for h in range(C_HEADS)], axis=0)
    (e_loc, e_ctx), inv = _stacked_softmax([s_loc, s_ctx], None)
    outs = []
    for h in range(C_HEADS):
        r = slice(h * tq, (h + 1) * tq)
        outs.append((jnp.dot(e_loc[r], _head(v_loc, h), preferred_element_type=F32)
                     + jnp.dot(e_ctx[r], _head(v_ctx, h), preferred_element_type=F32)) * inv[r])
    o_ref[...] = jnp.concatenate(outs, axis=1).astype(BF16)


def _nbr_attn(qc, kc, vc, cache_k, cache_v, bias, l):
    nb = DEC_SEQ // C_QBLOCK
    base = N_PROMPT // C_QBLOCK

    def nbr(d):
        return lambda j, b: (base + b * nb + jnp.clip(j + d, 0, nb - 1), 0)

    kv = lambda d: pl.BlockSpec((C_QBLOCK, C_W), nbr(d))
    cache = pl.BlockSpec((None, None, PAST_LEN, C_W), lambda j, b: (b, l, 0, 0))
    return pl.pallas_call(
        _nbr_attn_kernel, grid=(nb, DEC_BATCH),
        in_specs=[kv(0), kv(-1), kv(0), kv(1), kv(-1), kv(0), kv(1), cache, cache,
                  pl.BlockSpec((None, C_HEADS, C_DR_SLOTS, GRID_W, GRID_W), lambda j, b: (l, 0, 0, 0, 0))],
        out_specs=pl.BlockSpec((C_QBLOCK, C_W), lambda j, b: (b * nb + j, 0)),
        out_shape=jax.ShapeDtypeStruct((N_SAMPLE, C_W), BF16),
        scratch_shapes=[pltpu.VMEM((C_HEADS, C_QBLOCK, 3 * C_QBLOCK), F32)],
        compiler_params=_cparams(("arbitrary", "arbitrary")), name="nbr_attn",
    )(qc, kc, kc, kc, vc, vc, vc, cache_k, cache_v, bias)


def _nbr_bias_tables(rpb):
    qcol = np.arange(GRID_W)
    qcs = np.clip(qcol - C_WIN_COLS // 2, 0, GRID_W - C_WIN_COLS)
    kcol = np.arange(GRID_W)
    col_ok = (kcol[None, :] >= qcs[:, None]) & (kcol[None, :] < qcs[:, None] + C_WIN_COLS)
    dc = np.clip(kcol[None, :] - qcol[:, None], -(C_WIN_COLS - 1), C_WIN_COLS - 1) + C_WIN_COLS - 1
    onehot_dc = (dc.reshape(-1)[None, :] == np.arange(2 * C_WIN_COLS - 1)[:, None]).astype(np.float32)
    t = jnp.einsum('lhab,bx->lhax', rpb, jnp.asarray(onehot_dc), precision=lax.Precision.HIGHEST)
    t = jnp.where(jnp.asarray(col_ok.reshape(-1)), t, NEG_INF)
    t = jnp.concatenate([t, jnp.full((DEPTH, C_HEADS, 1, GRID_W * GRID_W), NEG_INF, F32)], axis=2)
    return t.reshape(DEPTH, C_HEADS, C_DR_SLOTS, GRID_W, GRID_W)


def _nbr_row_slots():
    rows = DEC_SEQ // GRID_W
    slots = []
    for j in (0, 3, rows // C_QROWS - 1):
        per_q = []
        for qr in range(C_QROWS):
            r = C_QROWS * j + qr
            rs = min(max(r - C_WIN_ROWS // 2, 0), rows - C_WIN_ROWS)
            per_k = []
            for kk in range(3 * C_QROWS):
                kabs = C_QROWS * (j - 1) + kk
                per_k.append(kabs - r + C_WIN_ROWS - 1 if rs <= kabs < rs + C_WIN_ROWS else C_DR_SLOTS - 1)
            per_q.append(per_k)
        slots.append(per_q)
    return slots


def _fourier_kernel(u_ref, bc_ref, bs_ref, cl_ref, sl_ref, o_ref, zc_ref, zs_ref):
    @pl.when(pl.program_id(1) == 0)
    def _():
        u = u_ref[...]
        zc_ref[...] = jnp.dot(u, bc_ref[...].astype(BF16), preferred_element_type=F32).astype(BF16)
        zs_ref[...] = jnp.dot(u, bs_ref[...].astype(BF16), preferred_element_type=F32).astype(BF16)

    o = (jnp.dot(cl_ref[...].astype(BF16), zc_ref[...], preferred_element_type=F32)
         - jnp.dot(sl_ref[...].astype(BF16), zs_ref[...], preferred_element_type=F32))
    o_ref[...] = o.astype(BF16)


def _dft_tables(n):
    k = np.arange(n)
    ang = 2.0 * np.pi * ((k[:, None] * k[None, :]) % n) / n
    return np.cos(ang) / np.sqrt(n), np.sin(ang) / np.sqrt(n)


def _channel_dft_tables():
    c, s = _dft_tables(B_GROUP_DIM)
    eye = np.eye(B_GROUPS)
    return np.kron(eye, c).astype(np.float32), np.kron(eye, s).astype(np.float32)


def _fourier(fb, n_batch, seq, row0, tr):
    cl, sl = (jnp.asarray(a.astype(np.float32)) for a in _dft_tables(seq))
    bc, bs = (jnp.asarray(a) for a in _channel_dft_tables())
    nt = seq // tr
    const = pl.BlockSpec((B_WIDTH, B_WIDTH), lambda b, t: (0, 0))
    return pl.pallas_call(
        _fourier_kernel, grid=(n_batch, nt),
        in_specs=[pl.BlockSpec((seq, B_WIDTH), lambda b, t: (row0 // seq + b, 0)), const, const,
                  pl.BlockSpec((tr, seq), lambda b, t: (t, 0)), pl.BlockSpec((tr, seq), lambda b, t: (t, 0))],
        out_specs=pl.BlockSpec((tr, B_WIDTH), lambda b, t: (b * nt + t, 0)),
        out_shape=jax.ShapeDtypeStruct((n_batch * seq, B_WIDTH), BF16),
        scratch_shapes=[pltpu.VMEM((seq, B_WIDTH), BF16), pltpu.VMEM((seq, B_WIDTH), BF16)],
        compiler_params=_cparams(("parallel", "arbitrary")), name=f"fourier_{seq}",
    )(fb, bc, bs, cl, sl)


def _merge_kernel(oap_ref, obp_ref, ocp_ref, oas_ref, obs_ref, ocs_ref, gt_ref, xp_ref, xs_ref,
                  wa_ref, wb_ref, wc_ref, wo_ref, g1_ref, ng_ref, sh_ref, sc_ref, *rest, router):
    if router:
        wr_ref, o_ref, h2_ref, lg_ref, wab_ref, wbb_ref, wcb_ref, wob_ref = rest
    else:
        o_ref, h2_ref, wab_ref, wbb_ref, wcb_ref, wob_ref = rest

    @pl.when(pl.program_id(0) == 0)
    def _():
        wab_ref[...] = wa_ref[...].astype(BF16)
        wbb_ref[...] = wb_ref[...].astype(BF16)
        wcb_ref[...] = wc_ref[...].astype(BF16)
        wob_ref[...] = wo_ref[...].astype(BF16)

    ya = jnp.dot(_read_split(oap_ref, oas_ref), wab_ref[...], preferred_element_type=F32)
    yb = jnp.dot(_read_split(obp_ref, obs_ref), wbb_ref[...], preferred_element_type=F32)
    yc = jnp.dot(_read_split(ocp_ref, ocs_ref), wcb_ref[...], preferred_element_type=F32)
    d = D_MODEL
    m = (gt_ref[:, 0:d].astype(F32) * ya + gt_ref[:, d:2 * d].astype(F32) * yb
         + gt_ref[:, 2 * d:3 * d].astype(F32) * yc)
    y = jnp.dot(m.astype(BF16), wob_ref[...], preferred_element_type=F32)
    x_new = _read_split(xp_ref, xs_ref) + g1_ref[...] * y
    o_ref[...] = x_new
    h2 = _adaln_math(x_new, ng_ref[...], sh_ref[...], sc_ref[...])
    h2_ref[...] = h2.astype(h2_ref.dtype)
    if router:
        w = wr_ref[...]
        w_hi = w.astype(BF16)
        w_lo = (w - w_hi.astype(F32)).astype(BF16)
        h_hi = h2.astype(BF16)
        h_lo = (h2 - h_hi.astype(F32)).astype(BF16)
        hi_terms = jnp.dot(h_hi, jnp.concatenate([w_hi, w_lo], axis=1), preferred_element_type=F32)
        lg_ref[...] = (hi_terms[:, :LANES] + jnp.dot(h_lo, w_hi, preferred_element_type=F32)
                       + hi_terms[:, LANES:])


def _merge(branches_p, branches_s, gates, x, wa, wb, wc, wo, mod, l, norm_g, w_router=None):
    tm = 512
    router = w_router is not None
    x, joined = _split_operands(x)
    row = lambda w: pl.BlockSpec((tm, w), lambda i: (i, 0))
    row_p = lambda w: _split_rows(tm, w)[0]
    row_s = lambda w: _split_rows(tm, w)[1]
    const = lambda r, c: pl.BlockSpec((None, r, c), lambda i: (l, 0, 0), pipeline_mode=pl.Buffered(1))
    mspec = lambda which: pl.BlockSpec((None, None, 1, D_MODEL), lambda i: (l, _cond_row(i, tm), 0, which))
    in_specs = [row_p(A_Q), row_p(B_WIDTH), row_p(C_W), row_s(A_Q), row_s(B_WIDTH), row_s(C_W),
                row(GATE_WIDTH), *_split_rows(tm, D_MODEL, joined),
                const(A_Q, D_MODEL), const(B_WIDTH, D_MODEL), const(C_W, D_MODEL), const(D_MODEL, D_MODEL),
                mspec(2), pl.BlockSpec((1, D_MODEL), lambda i: (0, 0)), mspec(3), mspec(4)]
    operands = [*branches_p, *branches_s, gates, *x, wa, wb, wc, wo, mod, norm_g.reshape(1, D_MODEL), mod, mod]
    out_specs = [row(D_MODEL), row(D_MODEL)]
    out_shape = [jax.ShapeDtypeStruct((N_TOK, D_MODEL), F32),
                 jax.ShapeDtypeStruct((N_TOK, D_MODEL), F32 if router else BF16)]
    if router:
        in_specs.append(pl.BlockSpec((D_MODEL, LANES), lambda i: (0, 0)))
        operands.append(jnp.pad(w_router, ((0, 0), (0, LANES - N_EXPERTS))))
        out_specs.append(row(LANES))
        out_shape.append(jax.ShapeDtypeStruct((N_TOK, LANES), F32))
    return pl.pallas_call(
        functools.partial(_merge_kernel, router=router), grid=(N_TOK // tm,),
        in_specs=in_specs, out_specs=out_specs, out_shape=out_shape,
        scratch_shapes=[pltpu.VMEM((A_Q, D_MODEL), BF16), pltpu.VMEM((B_WIDTH, D_MODEL), BF16),
                        pltpu.VMEM((C_W, D_MODEL), BF16), pltpu.VMEM((D_MODEL, D_MODEL), BF16)],
        compiler_params=_cparams(("arbitrary",)), name="merge_router" if router else "merge",
    )(*operands)


FFN_COL_CHUNK = D_FF // 2


def _ffn_kernel(h_ref, x_ref, g2_ref, ng_ref, sh_ref, sc_ref, wg_hbm, wu_hbm, wd_hbm, o_ref, hn_ref,
                wgb_ref, wub_ref, wdb_ref, stage_up_ref, stage_dn_ref, sem, *, i_dense):
    @pl.when(pl.program_id(0) == 0)
    def _():
        _load_weight_bf16(lambda k: wg_hbm.at[i_dense, _row_chunk(k)], wgb_ref, stage_up_ref, sem)
        _load_weight_bf16(lambda k: wu_hbm.at[i_dense, _row_chunk(k)], wub_ref, stage_up_ref, sem)
        _load_weight_bf16(lambda k: wd_hbm.at[i_dense, _row_chunk(k)], wdb_ref, stage_dn_ref, sem)

    h = h_ref[...]
    acc = None
    for c in range(D_FF // FFN_COL_CHUNK):
        cols = slice(c * FFN_COL_CHUNK, (c + 1) * FFN_COL_CHUNK)
        g = jnp.dot(h, wgb_ref[:, cols], preferred_element_type=F32)
        u = jnp.dot(h, wub_ref[:, cols], preferred_element_type=F32)
        a = (g * _sigmoid(g) * u).astype(BF16)
        d = jnp.dot(a, wdb_ref[cols, :], preferred_element_type=F32)
        acc = d if acc is None else acc + d
    x_new = x_ref[...] + g2_ref[...] * acc
    o_ref[...] = x_new
    hn_ref[...] = _adaln_math(x_new, ng_ref[...], sh_ref[...], sc_ref[...]).astype(BF16)


def _ffn(h, x, wg, wu, wd, mod, l, next_norm_g):
    tm = 512
    row = lambda dt: pl.BlockSpec((tm, D_MODEL), lambda i: (i, 0))
    mspec = lambda layer, which: pl.BlockSpec((None, None, 1, D_MODEL),
                                              lambda i: (layer, _cond_row(i, tm), 0, which))
    hbm = pl.BlockSpec(memory_space=pl.ANY)
    return pl.pallas_call(
        functools.partial(_ffn_kernel, i_dense=l // 2), grid=(N_TOK // tm,),
        in_specs=[row(BF16), row(F32), mspec(l, 5),
                  pl.BlockSpec((1, D_MODEL), lambda i: (0, 0)), mspec(l + 1, 0), mspec(l + 1, 1),
                  hbm, hbm, hbm],
        out_specs=[row(F32), row(BF16)],
        out_shape=[jax.ShapeDtypeStruct((N_TOK, D_MODEL), F32), jax.ShapeDtypeStruct((N_TOK, D_MODEL), BF16)],
        scratch_shapes=[pltpu.VMEM((D_MODEL, D_FF), BF16), pltpu.VMEM((D_MODEL, D_FF), BF16),
                        pltpu.VMEM((D_FF, D_MODEL), BF16),
                        pltpu.VMEM((2, WEIGHT_ROW_CHUNK, D_FF), F32),
                        pltpu.VMEM((2, WEIGHT_ROW_CHUNK, D_MODEL), F32), pltpu.SemaphoreType.DMA((2,))],
        compiler_params=_cparams(("arbitrary",)), name="ffn",
    )(h, x, mod, next_norm_g.reshape(1, D_MODEL), mod, mod, wg, wu, wd)


TOP_K = 2
MOE_TILE = 256
MOE_TILES = TOP_K * N_TOK // MOE_TILE + N_EXPERTS
MOE_ROWS = MOE_TILES * MOE_TILE
MOE_CHUNK = 10
MOE_CHUNKS = MOE_TILES // MOE_CHUNK + N_EXPERTS
MOE_TF = 896
ROUTE_TM = 512
DISPATCH_TM = 256
COMBINE_TM = 256
ROW_COPY_UNROLL = 16


def _route_kernel(lg_ref, o_ref, cnt_ref, base_ref, tri_ref):
    tm = lg_ref.shape[0]

    @pl.when(pl.program_id(0) == 0)
    def _():
        base_ref[...] = jnp.zeros_like(base_ref)
        r = lax.broadcasted_iota(jnp.int32, (tm, tm), 0)
        c = lax.broadcasted_iota(jnp.int32, (tm, tm), 1)
        tri_ref[...] = jnp.where(r > c, 1.0, 0.0).astype(BF16)

    lane = lax.broadcasted_iota(jnp.int32, lg_ref.shape, 1).astype(F32)
    lg = jnp.where(lane < N_EXPERTS, lg_ref[...], -jnp.inf)
    m1 = lg.max(axis=-1, keepdims=True)
    i1 = jnp.where(lg == m1, lane, float(LANES)).min(axis=-1, keepdims=True)
    rest = jnp.where(lane == i1, -jnp.inf, lg)
    m2 = rest.max(axis=-1, keepdims=True)
    i2 = jnp.where(rest == m2, lane, float(LANES)).min(axis=-1, keepdims=True)
    e2 = jnp.exp(m2 - m1)
    w1 = 1.0 / (1.0 + e2)
    w2 = e2 / (1.0 + e2)

    oh1 = jnp.where(lane == i1, 1.0, 0.0)
    oh2 = jnp.where(lane == i2, 1.0, 0.0)
    pre1 = jnp.dot(tri_ref[...], oh1.astype(BF16), preferred_element_type=F32)
    pre2 = jnp.dot(tri_ref[...], oh2.astype(BF16), preferred_element_type=F32)
    c1 = jnp.sum(oh1, axis=0, keepdims=True)
    c2 = jnp.sum(oh2, axis=0, keepdims=True)
    base = base_ref[...]
    rank1 = jnp.sum(oh1 * (base + pre1), axis=-1, keepdims=True)
    rank2 = jnp.sum(oh2 * (base + c1 + pre2), axis=-1, keepdims=True)
    base_ref[...] = base + c1 + c2

    cols = (i1, i2, rank1, rank2, w1, w2)
    out = jnp.zeros(lg_ref.shape, F32)
    for j, col in enumerate(cols):
        out = jnp.where(lane == float(j), col, out)
    o_ref[...] = out
    cnt_ref[...] = jnp.broadcast_to(base + c1 + c2, cnt_ref.shape)


def _dispatch_kernel(pos_ref, last_ref, nt_ref, h_ref, xs_hbm, zero_ref, zsem, sem):
    i = pl.program_id(0)
    tm = h_ref.shape[0]

    @pl.when(i == 0)
    def _():
        zero_ref[...] = jnp.zeros_like(zero_ref)

        def zero_copy(tile):
            row0 = pl.multiple_of(tile * MOE_TILE, MOE_TILE)
            return pltpu.make_async_copy(zero_ref, xs_hbm.at[pl.ds(row0, MOE_TILE)], zsem)

        def for_zeroed_tiles(fn):
            for e in range(N_EXPERTS):
                @pl.when(last_ref[e] >= 0)
                def _():
                    fn(zero_copy(last_ref[e]))

                tail = MOE_TILES - 1 - e

                @pl.when(tail >= nt_ref[0])
                def _():
                    fn(zero_copy(tail))

        for_zeroed_tiles(lambda cp: cp.start())
        for_zeroed_tiles(lambda cp: cp.wait())

    def row_copy(t, k):
        dst = xs_hbm.at[pl.ds(pos_ref[TOP_K * (i * tm + t) + k], 1)]
        return pltpu.make_async_copy(h_ref.at[pl.ds(t, 1)], dst, sem)

    def issue(t, carry):
        for k in range(TOP_K):
            row_copy(t, k).start()
        return carry

    def drain(t, carry):
        for k in range(TOP_K):
            row_copy(t, k).wait()
        return carry

    lax.fori_loop(0, tm, issue, 0, unroll=ROW_COPY_UNROLL)
    lax.fori_loop(0, tm, drain, 0, unroll=ROW_COPY_UNROLL)


def _expert_kernel(ce_ref, ct_ref, cn_ref, nch_ref, nt_ref, xs_hbm, wg_ref, wu_ref, wd_ref, y_hbm,
                   acc_ref, xbuf_ref, wgb_ref, wub_ref, wdb_ref, xsem, osem):
    c = pl.program_id(0)
    f = pl.program_id(1)
    last_f = pl.num_programs(1) - 1

    def tile_rows(tile):
        return pl.ds(pl.multiple_of(tile * MOE_TILE, MOE_TILE), MOE_TILE)

    def out_copy(slot, tile):
        return pltpu.make_async_copy(acc_ref.at[slot], y_hbm.at[tile_rows(tile)], osem)

    @pl.when((c == 0) & (f == 0))
    def _():
        acc_ref[0] = jnp.zeros((MOE_TILE, D_MODEL), F32)
        for e in range(N_EXPERTS):
            tail = MOE_TILES - 1 - e

            @pl.when(tail >= nt_ref[0])
            def _():
                cp = out_copy(0, tail)
                cp.start()
                cp.wait()

    @pl.when(c < nch_ref[0])
    def _():
        n = cn_ref[c]
        t0 = ct_ref[c]

        def x_copy(j):
            return pltpu.make_async_copy(xs_hbm.at[tile_rows(t0 + j)], xbuf_ref.at[j], xsem.at[j])

        @pl.when(f == 0)
        def _():
            def fetch(j, carry):
                x_copy(j).start()
                return carry
            lax.fori_loop(0, n, fetch, 0)

        wgb_ref[...] = wg_ref[...].astype(BF16)
        wub_ref[...] = wu_ref[...].astype(BF16)
        wdb_ref[...] = wd_ref[...].astype(BF16)

        def tile_step(j, carry):
            @pl.when(f == 0)
            def _():
                x_copy(j).wait()

                acc_ref[j] = jnp.zeros((MOE_TILE, D_MODEL), F32)

            x = xbuf_ref[j].astype(BF16)
            g = jnp.dot(x, wgb_ref[...], preferred_element_type=F32)
            u = jnp.dot(x, wub_ref[...], preferred_element_type=F32)
            a = (g * _sigmoid(g) * u).astype(BF16)
            acc_ref[j] += jnp.dot(a, wdb_ref[...], preferred_element_type=F32)

            @pl.when(f == last_f)
            def _():
                @pl.when(j > 0)
                def _():
                    out_copy(j - 1, t0 + j - 1).wait()

                out_copy(j, t0 + j).start()

            return carry

        lax.fori_loop(0, n, tile_step, 0)

        @pl.when(f == last_f)
        def _():
            out_copy(n - 1, t0 + n - 1).wait()


def _combine_kernel(pos_ref, route_ref, x_ref, g2_ref, y_hbm, op_ref, os_ref, buf_ref, sem):
    i = pl.program_id(0)
    n = pl.num_programs(0)
    tm = x_ref.shape[0]
    slot = i % 2

    def row_copy(step, s, t, k):
        src = y_hbm.at[pl.ds(pos_ref[TOP_K * (step * tm + t) + k], 1)]
        return pltpu.make_async_copy(src, buf_ref.at[s, k, pl.ds(t, 1)], sem.at[s])

    def issue(step, s):
        def body(t, carry):
            for k in range(TOP_K):
                row_copy(step, s, t, k).start()
            return carry
        lax.fori_loop(0, tm, body, 0, unroll=ROW_COPY_UNROLL)

    @pl.when(i == 0)
    def _():
        issue(0, 0)

    @pl.when(i + 1 < n)
    def _():
        issue(i + 1, 1 - slot)

    def wait_body(t, carry):
        for k in range(TOP_K):
            row_copy(i, slot, t, k).wait()
        return carry

    lax.fori_loop(0, tm, wait_body, 0, unroll=ROW_COPY_UNROLL)

    lane = lax.broadcasted_iota(jnp.int32, route_ref.shape, 1)
    rt = route_ref[...]
    w1 = jnp.sum(jnp.where(lane == 2 * TOP_K, rt, 0.0), axis=-1, keepdims=True)
    w2 = jnp.sum(jnp.where(lane == 2 * TOP_K + 1, rt, 0.0), axis=-1, keepdims=True)
    out = x_ref[...] + g2_ref[...] * (w1 * buf_ref[slot, 0] + w2 * buf_ref[slot, 1])

    @pl.when(i < N_PROMPT // tm)
    def _():
        op_ref[...] = out

    @pl.when(i >= N_PROMPT // tm)
    def _():
        os_ref[...] = out


def _moe(h, x, logits, wg, wu, wd, mod, l):
    route, cnt = pl.pallas_call(
        _route_kernel, grid=(N_TOK // ROUTE_TM,),
        in_specs=[pl.BlockSpec((ROUTE_TM, LANES), lambda i: (i, 0))],
        out_specs=[pl.BlockSpec((ROUTE_TM, LANES), lambda i: (i, 0)), pl.BlockSpec((8, LANES), lambda i: (0, 0))],
        out_shape=[jax.ShapeDtypeStruct((N_TOK, LANES), F32), jax.ShapeDtypeStruct((8, LANES), F32)],
        scratch_shapes=[pltpu.VMEM((1, LANES), F32), pltpu.VMEM((ROUTE_TM, ROUTE_TM), BF16)],
        compiler_params=_cparams(("arbitrary",)), name="route",
    )(logits)

    expert = route[:, 0:TOP_K].astype(jnp.int32)
    rank = route[:, TOP_K:2 * TOP_K].astype(jnp.int32)
    n_sub = (cnt[0, :N_EXPERTS].astype(jnp.int32) + MOE_TILE - 1) // MOE_TILE
    end = jnp.cumsum(n_sub)
    start = end - n_sub
    start_of = jnp.sum(jnp.where(expert[:, :, None] == jnp.arange(N_EXPERTS), start, 0), axis=-1)
    pos = (start_of * MOE_TILE + rank).reshape(-1)
    n_tiles = end[N_EXPERTS - 1:]
    last_tile = jnp.where(n_sub > 0, end - 1, -1).astype(jnp.int32)

    xs = pl.pallas_call(
        _dispatch_kernel,
        grid_spec=pltpu.PrefetchScalarGridSpec(
            num_scalar_prefetch=3, grid=(N_TOK // DISPATCH_TM,),
            in_specs=[pl.BlockSpec((DISPATCH_TM, D_MODEL), lambda i, p, lt, nt: (i, 0))],
            out_specs=pl.BlockSpec(memory_space=pl.ANY),
            scratch_shapes=[pltpu.VMEM((MOE_TILE, D_MODEL), F32), pltpu.SemaphoreType.DMA(()),
                            pltpu.SemaphoreType.DMA(())]),
        out_shape=jax.ShapeDtypeStruct((MOE_ROWS, D_MODEL), F32),
        compiler_params=_cparams(("arbitrary",)), name="moe_dispatch",
    )(pos, last_tile, n_tiles, h)

    n_chunk = (n_sub + MOE_CHUNK - 1) // MOE_CHUNK
    chunk_end = jnp.cumsum(n_chunk)
    cidx = jnp.arange(MOE_CHUNKS)
    c_expert = jnp.minimum(jnp.sum(cidx[:, None] >= chunk_end[None, :], axis=1), N_EXPERTS - 1).astype(jnp.int32)
    c_k = cidx - (chunk_end - n_chunk)[c_expert]
    c_tile0 = (start[c_expert] + c_k * MOE_CHUNK).astype(jnp.int32)
    c_ntiles = jnp.clip(n_sub[c_expert] - c_k * MOE_CHUNK, 0, MOE_CHUNK).astype(jnp.int32)
    n_chunks = chunk_end[N_EXPERTS - 1:]

    nf = D_FF_EXPERT // MOE_TF

    def w_idx(c, f, ce, nch):
        live = c < nch[0]
        return ce[jnp.minimum(c, nch[0] - 1)], jnp.where(live, f, nf - 1)

    def up_map(c, f, ce, ct, cn, nch, nt):
        e, ff = w_idx(c, f, ce, nch)
        return (e, 0, ff)

    def down_map(c, f, ce, ct, cn, nch, nt):
        e, ff = w_idx(c, f, ce, nch)
        return (e, ff, 0)

    y = pl.pallas_call(
        _expert_kernel,
        grid_spec=pltpu.PrefetchScalarGridSpec(
            num_scalar_prefetch=5, grid=(MOE_CHUNKS, nf),
            in_specs=[pl.BlockSpec(memory_space=pl.ANY),
                      pl.BlockSpec((None, D_MODEL, MOE_TF), up_map),
                      pl.BlockSpec((None, D_MODEL, MOE_TF), up_map),
                      pl.BlockSpec((None, MOE_TF, D_MODEL), down_map)],
            out_specs=pl.BlockSpec(memory_space=pl.ANY),
            scratch_shapes=[pltpu.VMEM((MOE_CHUNK, MOE_TILE, D_MODEL), F32),
                            pltpu.VMEM((MOE_CHUNK, MOE_TILE, D_MODEL), F32),
                            pltpu.VMEM((D_MODEL, MOE_TF), BF16), pltpu.VMEM((D_MODEL, MOE_TF), BF16),
                            pltpu.VMEM((MOE_TF, D_MODEL), BF16),
                            pltpu.SemaphoreType.DMA((MOE_CHUNK,)), pltpu.SemaphoreType.DMA(())]),
        out_shape=jax.ShapeDtypeStruct((MOE_ROWS, D_MODEL), F32),
        compiler_params=_cparams(("arbitrary", "arbitrary")), name="moe_experts",
    )(c_expert, c_tile0, c_ntiles, n_chunks, n_tiles, xs, wg, wu, wd)

    tm = COMBINE_TM
    return pl.pallas_call(
        _combine_kernel,
        grid_spec=pltpu.PrefetchScalarGridSpec(
            num_scalar_prefetch=1, grid=(N_TOK // tm,),
            in_specs=[pl.BlockSpec((tm, LANES), lambda i, p: (i, 0)),
                      pl.BlockSpec((tm, D_MODEL), lambda i, p: (i, 0)),
                      pl.BlockSpec((None, None, 1, D_MODEL), lambda i, p: (l, _cond_row(i, tm), 0, 5)),
                      pl.BlockSpec(memory_space=pl.ANY)],
            out_specs=[pl.BlockSpec((tm, D_MODEL), lambda i, p: (jnp.minimum(i, N_PROMPT // tm - 1), 0)),
                       pl.BlockSpec((tm, D_MODEL), lambda i, p: (jnp.maximum(i - N_PROMPT // tm, 0), 0))],
            scratch_shapes=[pltpu.VMEM((2, TOP_K, tm, D_MODEL), F32), pltpu.SemaphoreType.DMA((2,))]),
        out_shape=[jax.ShapeDtypeStruct((N_PROMPT, D_MODEL), F32), jax.ShapeDtypeStruct((N_SAMPLE, D_MODEL), F32)],
        compiler_params=_cparams(("arbitrary",)), name="moe_combine",
    )(pos, route, x, mod, y)


def _rope_tables():
    t = jnp.arange(DEC_SEQ)
    row = (t // GRID_W).astype(F32)
    col = (t % GRID_W).astype(F32)
    n_freq = HEAD_DIM // 4
    inv = ROPE_BASE ** (-jnp.arange(n_freq, dtype=F32) / n_freq)
    ang = jnp.concatenate([row[:, None] * inv, col[:, None] * inv], axis=-1)
    cos, sin = jnp.cos(ang), jnp.sin(ang)
    cos_h = jnp.concatenate([cos, cos], axis=-1)
    sin_h = jnp.concatenate([-sin, sin], axis=-1)
    cos_l = jnp.tile(jnp.concatenate([cos_h, cos_h], axis=-1), (DEC_BATCH, 1))
    sin_l = jnp.tile(jnp.concatenate([sin_h, sin_h], axis=-1), (DEC_BATCH, 1))
    cos_t = jnp.concatenate([jnp.ones((N_PROMPT, LANES), F32), cos_l], axis=0)
    sin_t = jnp.concatenate([jnp.zeros((N_PROMPT, LANES), F32), sin_l], axis=0)
    return cos_t, sin_t


def kernel(x_prompt, x_sample, cache_a_k, cache_a_v, cache_c_k, cache_c_v, c, c_ctx, w_mod, b_mod, norm1_g, norm2_g, w_in, qk_norm_a, qk_norm_c, sink_a, rpb_c, w_branch_a, w_branch_b, w_branch_c, w_out, w_ff_gate, w_ff_up, w_ff_down, w_router, w_exp_gate, w_exp_up, w_exp_down):
    x = (x_prompt.reshape(N_PROMPT, D_MODEL), x_sample.reshape(N_SAMPLE, D_MODEL))
    cond =jnp.concatenate([c_ctx[None, :], c], axis=0)
    cond_t = jnp.broadcast_to(cond[:, :, None], (N_COND, D_MODEL, LANES))
    mod = _modulation(cond_t, w_mod, b_mod)
    cos_t, sin_t = _rope_tables()
    bias = _nbr_bias_tables(rpb_c)
    ck_a = cache_a_k.reshape(DEC_BATCH, DEPTH, PAST_LEN, A_KV)
    cv_a = cache_a_v.reshape(DEC_BATCH, DEPTH, PAST_LEN, A_KV)
    ck_c = cache_c_k.reshape(DEC_BATCH, DEPTH, PAST_LEN, C_W)
    cv_c = cache_c_v.reshape(DEC_BATCH, DEPTH, PAST_LEN, C_W)

    new_ak, new_av, new_ck, new_cv = [], [], [], []
    h = _adaln(x, norm1_g[0], mod, 0, 0)
    for l in range(DEPTH):
        qa, ka, va, fb, qc, kc, vc, gates = _in_proj(h, w_in, cos_t, sin_t, qk_norm_a[l], qk_norm_c[l], l)
        oa_p, oc_p = _ctx_attn(sink_a, qa, ka, va, qc, kc, vc, l)
        oa_s = _win_attn(sink_a, qa, ka, va, ck_a, cv_a, l)
        oc_s = _nbr_attn(qc, kc, vc, ck_c, cv_c, bias, l)
        ob_p = _fourier(fb, BATCH, SEQ, 0, SEQ)
        ob_s = _fourier(fb, DEC_BATCH, DEC_SEQ, N_PROMPT, 512)
        branches = ((oa_p, ob_p, oc_p), (oa_s, ob_s, oc_s), gates, x,
                    w_branch_a, w_branch_b, w_branch_c, w_out, mod, l, norm2_g[l])
        i = l // 2
        if l % 2 == 0:
            x, h2 = _merge(*branches)
            x, h = _ffn(h2, x, w_ff_gate, w_ff_up, w_ff_down, mod, l, norm1_g[l + 1])
        else:
            x, h2, logits = _merge(*branches, w_router=w_router[i])
            xp, xs = _moe(h2, x, logits, w_exp_gate[i], w_exp_up[i], w_exp_down[i], mod, l)
        new_ak.append(ka[:N_PROMPT].reshape(BATCH, SEQ, A_KV_HEADS, HEAD_DIM))
        new_av.append(va[:N_PROMPT].reshape(BATCH, SEQ, A_KV_HEADS, HEAD_DIM))
        new_ck.append(kc[:N_PROMPT].reshape(BATCH, SEQ, C_HEADS, HEAD_DIM))
        new_cv.append(vc[:N_PROMPT].reshape(BATCH, SEQ, C_HEADS, HEAD_DIM))

    return (xp.reshape(BATCH, SEQ, D_MODEL), xs.reshape(DEC_BATCH, DEC_SEQ, D_MODEL),
            jnp.stack(new_ak, axis=1), jnp.stack(new_av, axis=1),
            jnp.stack(new_ck, axis=1), jnp.stack(new_cv, axis=1))
```

```python
import functools

import numpy as np
import jax
import jax.numpy as jnp
from jax import lax
from jax.experimental import pallas as pl
from jax.experimental.pallas import tpu as pltpu

F32 = jnp.float32
BF16 = jnp.bfloat16

D_MODEL = 1024
BATCH = 16
SEQ = 256
DEPTH = 2
DEC_BATCH = 2
DEC_SEQ = 2048
PAST_LEN = 512
GRID_W = 64
HEAD_DIM = 64
SCALE = HEAD_DIM ** -0.5
A_HEADS = 8
A_KV_HEADS = 2
A_GROUP = A_HEADS // A_KV_HEADS
A_WINDOW = 128
A_BLOCK = 128
B_GROUPS = 8
B_GROUP_DIM = 64
B_WIDTH = B_GROUPS * B_GROUP_DIM
C_HEADS = 8
C_WIN_ROWS = 8
C_WIN_COLS = 16
A_Q = A_HEADS * HEAD_DIM
A_KV = A_KV_HEADS * HEAD_DIM
C_W = C_HEADS * HEAD_DIM
QKV_WIDTH = A_Q + 2 * A_KV + B_WIDTH + 3 * C_W
N_BRANCH = 3
GATE_WIDTH = N_BRANCH * D_MODEL
IN_WIDTH = QKV_WIDTH + GATE_WIDTH
D_FF = 2816
N_EXPERTS = 8
D_FF_EXPERT = 3584
ROPE_BASE = 10000.0
RMS_EPS = 1e-6
NEG_INF = -1e30

N_PROMPT = BATCH * SEQ
N_SAMPLE = DEC_BATCH * DEC_SEQ
N_TOK = N_PROMPT + N_SAMPLE
N_COND = 1 + DEC_BATCH
LANES = 128
NORM_SLAB = 256
C_QROWS = 4
C_QBLOCK = C_QROWS * GRID_W
C_DR_SLOTS = 2 * C_WIN_ROWS
VMEM_LIMIT = 56 * 1024 * 1024


def _cparams(sem):
    return pltpu.CompilerParams(dimension_semantics=sem, vmem_limit_bytes=VMEM_LIMIT)


def _sigmoid(x):
    return 1.0 / (1.0 + jnp.exp(-x))


def _cond_row(tile, tm):
    return jnp.maximum(tile * tm // DEC_SEQ - 1, 0)


def _mod_kernel(ct_ref, w_ref, b_ref, o_ref, silu_ref):
    @pl.when((pl.program_id(0) == 0) & (pl.program_id(1) == 0))
    def _():
        cb = ct_ref[...]
        silu_ref[...] = cb * _sigmoid(cb)

    tn = w_ref.shape[1]
    for r in range(N_COND):
        s = silu_ref[r]
        for cc in range(tn // LANES):
            sl = slice(cc * LANES, (cc + 1) * LANES)
            o_ref[r, :, sl] = jnp.sum(w_ref[:, sl] * s, axis=0, keepdims=True) + b_ref[:, sl]


def _modulation(cond_t, w_mod, b_mod):
    tn = 1024
    n = 6 * D_MODEL
    return pl.pallas_call(
        _mod_kernel,
        grid=(DEPTH, n // tn),
        in_specs=[
            pl.BlockSpec((N_COND, D_MODEL, LANES), lambda l, j: (0, 0, 0)),
            pl.BlockSpec((None, D_MODEL, tn), lambda l, j: (l, 0, j)),
            pl.BlockSpec((None, 1, tn), lambda l, j: (l, 0, j)),
        ],
        out_specs=pl.BlockSpec((None, N_COND, 1, tn), lambda l, j: (l, 0, 0, j)),
        out_shape=jax.ShapeDtypeStruct((DEPTH, N_COND, 1, n), F32),
        scratch_shapes=[pltpu.VMEM((N_COND, D_MODEL, LANES), F32)],
        compiler_params=_cparams(("arbitrary", "arbitrary")),
        name="modulation",
    )(cond_t, w_mod, b_mod.reshape(DEPTH, 1, n))


def _adaln_math(x, g, sh, sc):
    ms = jnp.mean(x * x, axis=-1, keepdims=True)
    y = x * lax.rsqrt(ms + RMS_EPS) * g
    return y * (1.0 + sc) + sh


def _split_rows(tm, width, joined=False):
    n_p = N_PROMPT // tm
    latent0 = n_p if joined else 0
    return (pl.BlockSpec((tm, width), lambda i: (jnp.minimum(i, n_p - 1), 0)),
            pl.BlockSpec((tm, width), lambda i: (jnp.maximum(i - n_p, 0) + latent0, 0)))


def _split_operands(x):
    return (x, False) if isinstance(x, tuple) else ((x, x), True)


def _read_split(p_ref, s_ref):
    is_ctx = pl.program_id(0) < N_PROMPT // p_ref.shape[0]
    return jnp.where(is_ctx, p_ref[...], s_ref[...])


WEIGHT_ROW_CHUNK = 128


def _load_weight_bf16(src_rows, dst_ref, stage_ref, sem):
    n = dst_ref.shape[0] // WEIGHT_ROW_CHUNK

    def chunk_copy(k):
        return pltpu.make_async_copy(src_rows(k), stage_ref.at[k % 2], sem.at[k % 2])

    chunk_copy(0).start()
    for k in range(n):
        if k + 1 < n:
            chunk_copy(k + 1).start()
        chunk_copy(k).wait()
        dst_ref[k * WEIGHT_ROW_CHUNK:(k + 1) * WEIGHT_ROW_CHUNK, :] = stage_ref[k % 2].astype(BF16)


def _row_chunk(k):
    return pl.ds(k * WEIGHT_ROW_CHUNK, WEIGHT_ROW_CHUNK)


def _head_norm(x, gain, bd):
    sq = x * x
    hi = sq.astype(BF16)
    lo = (sq - hi.astype(F32)).astype(BF16)
    ms = jnp.dot(hi, bd, preferred_element_type=F32) + jnp.dot(lo, bd, preferred_element_type=F32)
    return x * lax.rsqrt(ms + RMS_EPS) * gain


def _rope(x, cos, sin_signed, first_half):
    half = HEAD_DIM // 2
    swapped = jnp.where(first_half, pltpu.roll(x, x.shape[1] - half, 1), pltpu.roll(x, half, 1))
    return x * cos + swapped * sin_signed


def _in_proj_kernel(*refs, l, fused_norm):
    n_lead = 5 if fused_norm else 1
    (cos_ref, sin_ref, ga_ref, gc_ref, w_hbm, qa_ref, ka_ref, va_ref, fb_ref, qc_ref, kc_ref, vc_ref, gt_ref,
     wb_ref, stage_ref, sem) = refs[n_lead:]

    @pl.when(pl.program_id(0) == 0)
    def _():
        _load_weight_bf16(lambda k: w_hbm.at[l, _row_chunk(k)], wb_ref, stage_ref, sem)

    if fused_norm:
        xp_ref, xs_ref, ng_ref, sh_ref, sc_ref = refs[:n_lead]
        h = _adaln_math(_read_split(xp_ref, xs_ref), ng_ref[...], sh_ref[...], sc_ref[...]).astype(BF16)
    else:
        h = refs[0][...]

    def proj(off, width):
        return jnp.dot(h, wb_ref[:, off:off + width], preferred_element_type=F32)

    r = lax.broadcasted_iota(jnp.int32, (NORM_SLAB, NORM_SLAB), 0) // HEAD_DIM
    c = lax.broadcasted_iota(jnp.int32, (NORM_SLAB, NORM_SLAB), 1) // HEAD_DIM
    bd = jnp.where(r == c, 1.0 / HEAD_DIM, 0.0).astype(BF16)
    lane = lax.broadcasted_iota(jnp.int32, (1, NORM_SLAB), 1)
    first_half = (lane % HEAD_DIM) < HEAD_DIM // 2
    cos = jnp.concatenate([cos_ref[...]] * (NORM_SLAB // LANES), axis=1)
    sin = jnp.concatenate([sin_ref[...]] * (NORM_SLAB // LANES), axis=1)
    gqa, gka = ga_ref[0:1, :], ga_ref[1:2, :]
    gqc, gkc = gc_ref[0:1, :], gc_ref[1:2, :]
    slabs = lambda p: [p[:, s * NORM_SLAB:(s + 1) * NORM_SLAB] for s in range(p.shape[1] // NORM_SLAB)]
    cat = lambda parts: jnp.concatenate(parts, axis=1)

    off = 0
    qa_ref[...] = cat([_rope(_head_norm(x, gqa, bd), cos, sin, first_half) for x in slabs(proj(off, A_Q))]
                      ).astype(BF16)
    off += A_Q
    kv = proj(off, 2 * A_KV)
    ka_ref[...] = _rope(_head_norm(kv, gka, bd), cos, sin, first_half)[:, :A_KV]
    va_ref[...] = kv[:, A_KV:]
    off += 2 * A_KV
    fb_ref[...] = proj(off, B_WIDTH).astype(BF16)
    off += B_WIDTH
    qc_ref[...] = cat([_head_norm(x, gqc, bd) for x in slabs(proj(off, C_W))]).astype(BF16)
    off += C_W
    kc_ref[...] = cat([_head_norm(x, gkc, bd) for x in slabs(proj(off, C_W))])
    off += C_W
    vc_ref[...] = proj(off, C_W)
    off += C_W
    for j in range(N_BRANCH):
        cols = slice(j * D_MODEL, (j + 1) * D_MODEL)
        gt_ref[:, cols] = _sigmoid(proj(off + j * D_MODEL, D_MODEL)).astype(BF16)


def _in_proj(h, w_in, cos_t, sin_t, qk_a, qk_c, l, norm=None):
    tm = 512
    row = lambda w: pl.BlockSpec((tm, w), lambda i: (i, 0))
    widths = (A_Q, A_KV, A_KV, B_WIDTH, C_W, C_W, C_W, GATE_WIDTH)
    dtypes = (BF16, F32, F32, BF16, BF16, F32, F32, BF16)
    if norm is None:
        lead_specs, lead = [row(D_MODEL)], [h]
    else:
        norm_g, mod = norm
        x, joined = _split_operands(h)
        mspec = lambda which: pl.BlockSpec((None, None, 1, D_MODEL), lambda i: (l, _cond_row(i, tm), 0, which))
        lead_specs = [*_split_rows(tm, D_MODEL, joined), pl.BlockSpec((1, D_MODEL), lambda i: (0, 0)),
                      mspec(0), mspec(1)]
        lead = [*x, norm_g.reshape(1, D_MODEL), mod, mod]
    return pl.pallas_call(
        functools.partial(_in_proj_kernel, l=l, fused_norm=norm is not None), grid=(N_TOK // tm,),
        in_specs=[*lead_specs, row(LANES), row(LANES),
                  pl.BlockSpec((2, NORM_SLAB), lambda i: (0, 0)), pl.BlockSpec((2, NORM_SLAB), lambda i: (0, 0)),
                  pl.BlockSpec(memory_space=pl.ANY)],
        out_specs=[row(w) for w in widths],
        out_shape=[jax.ShapeDtypeStruct((N_TOK, w), d) for w, d in zip(widths, dtypes)],
        scratch_shapes=[pltpu.VMEM((D_MODEL, IN_WIDTH), BF16),
                        pltpu.VMEM((2, WEIGHT_ROW_CHUNK, IN_WIDTH), F32), pltpu.SemaphoreType.DMA((2,))],
        compiler_params=_cparams(("arbitrary",)), name="in_proj",
    )(*lead, cos_t, sin_t, jnp.tile(qk_a, (1, NORM_SLAB // HEAD_DIM)), jnp.tile(qk_c, (1, NORM_SLAB // HEAD_DIM)), w_in)


def _nt_dot(a, b):
    return lax.dot_general(a, b, (((1,), (1,)), ((), ())), preferred_element_type=F32)


def _head(x, h):
    return x[:, h * HEAD_DIM:(h + 1) * HEAD_DIM]


def _stacked_softmax(parts, sink):
    m = parts[0].max(axis=-1, keepdims=True)
    for s in parts[1:]:
        m = jnp.maximum(m, s.max(axis=-1, keepdims=True))
    if sink is not None:
        m = jnp.maximum(m, sink)
    den = jnp.exp(sink - m) if sink is not None else 0.0
    es = []
    for s in parts:
        e = jnp.exp(s - m)
        den = den + e.sum(axis=-1, keepdims=True)
        es.append(e.astype(BF16))
    return es, 1.0 / den


def _sink_column(sink_ref, l, rows_per_head):
    return jnp.concatenate([jnp.full((rows_per_head, 1), sink_ref[l, h], F32) for h in range(A_HEADS)], axis=0)


def _gqa_queries(qa, g):
    return jnp.concatenate([_head(qa, g * A_GROUP + i) for i in range(A_GROUP)], axis=0)


def _ctx_attn_kernel(sink_ref, qa_ref, ka_ref, va_ref, qc_ref, kc_ref, vc_ref, oa_ref, oc_ref, *, l):
    t = SEQ
    qa = qa_ref[...] * SCALE
    ka = ka_ref[...].astype(BF16)
    va = va_ref[...].astype(BF16)
    s = jnp.concatenate([_nt_dot(_gqa_queries(qa, g), _head(ka, g)) for g in range(A_KV_HEADS)], axis=0)
    (e,), inv = _stacked_softmax([s], _sink_column(sink_ref, l, t))
    outs = []
    for g in range(A_KV_HEADS):
        rows = slice(g * A_GROUP * t, (g + 1) * A_GROUP * t)
        o = jnp.dot(e[rows], _head(va, g), preferred_element_type=F32) * inv[rows]
        outs += [o[i * t:(i + 1) * t] for i in range(A_GROUP)]
    oa_ref[...] = jnp.concatenate(outs, axis=1).astype(BF16)

    qc = qc_ref[...] * SCALE
    kc = kc_ref[...].astype(BF16)
    vc = vc_ref[...].astype(BF16)
    s = jnp.concatenate([_nt_dot(_head(qc, h), _head(kc, h)) for h in range(C_HEADS)], axis=0)
    (e,), inv = _stacked_softmax([s], None)
    outs = [jnp.dot(e[h * t:(h + 1) * t], _head(vc, h), preferred_element_type=F32) * inv[h * t:(h + 1) * t]
            for h in range(C_HEADS)]
    oc_ref[...] = jnp.concatenate(outs, axis=1).astype(BF16)


def _ctx_attn(sink_a, qa, ka, va, qc, kc, vc, l):
    blk = lambda w: pl.BlockSpec((SEQ, w), lambda b: (b, 0))
    return pl.pallas_call(
        functools.partial(_ctx_attn_kernel, l=l), grid=(BATCH,),
        in_specs=[pl.BlockSpec(memory_space=pltpu.SMEM),
                  blk(A_Q), blk(A_KV), blk(A_KV), blk(C_W), blk(C_W), blk(C_W)],
        out_specs=[blk(A_Q), blk(C_W)],
        out_shape=[jax.ShapeDtypeStruct((N_PROMPT, A_Q), BF16), jax.ShapeDtypeStruct((N_PROMPT, C_W), BF16)],
        compiler_params=_cparams(("parallel",)), name="ctx_attn",
    )(sink_a, qa, ka, va, qc, kc, vc)


def _win_attn_kernel(sink_ref, q_ref, kp_ref, kc_ref, kn_ref, vp_ref, vc_ref, vn_ref, ck_ref, cv_ref, o_ref, *, l):
    t = pl.program_id(1)
    rows = A_GROUP * A_BLOCK
    qi = lax.broadcasted_iota(jnp.int32, (rows, 3 * A_BLOCK), 0) % A_BLOCK
    kj = lax.broadcasted_iota(jnp.int32, (rows, 3 * A_BLOCK), 1) - A_BLOCK
    kpos = t * A_BLOCK + kj
    valid = (jnp.abs(kj - qi) <= A_WINDOW) & (kpos >= 0) & (kpos < DEC_SEQ)
    sink = _sink_column(sink_ref, l, A_BLOCK)
    for g in range(A_KV_HEADS):
        sl = slice(g * HEAD_DIM, (g + 1) * HEAD_DIM)
        q = jnp.concatenate([q_ref[:, (g * A_GROUP + i) * HEAD_DIM:(g * A_GROUP + i + 1) * HEAD_DIM]
                             for i in range(A_GROUP)], axis=0) * SCALE
        k_loc = jnp.concatenate([kp_ref[:, sl], kc_ref[:, sl], kn_ref[:, sl]], axis=0).astype(BF16)
        v_loc = jnp.concatenate([vp_ref[:, sl], vc_ref[:, sl], vn_ref[:, sl]], axis=0).astype(BF16)
        s_loc = jnp.where(valid, _nt_dot(q, k_loc), NEG_INF)
        s_ctx = _nt_dot(q, ck_ref[:, sl].astype(BF16))
        (e_loc, e_ctx), inv = _stacked_softmax([s_loc, s_ctx], sink[g * rows:(g + 1) * rows])
        o = (jnp.dot(e_loc, v_loc, preferred_element_type=F32)
             + jnp.dot(e_ctx, cv_ref[:, sl].astype(BF16), preferred_element_type=F32)) * inv
        for i in range(A_GROUP):
            h = g * A_GROUP + i
            o_ref[:, h * HEAD_DIM:(h + 1) * HEAD_DIM] = o[i * A_BLOCK:(i + 1) * A_BLOCK].astype(BF16)


def _win_attn(sink_a, qa, ka, va, cache_k, cache_v, l):
    nb = DEC_SEQ // A_BLOCK
    base = N_PROMPT // A_BLOCK

    def nbr(d):
        return lambda b, t: (base + b * nb + jnp.clip(t + d, 0, nb - 1), 0)

    kv = lambda d: pl.BlockSpec((A_BLOCK, A_KV), nbr(d))
    cache = pl.BlockSpec((None, None, PAST_LEN, A_KV), lambda b, t: (b, l, 0, 0))
    return pl.pallas_call(
        functools.partial(_win_attn_kernel, l=l), grid=(DEC_BATCH, nb),
        in_specs=[pl.BlockSpec(memory_space=pltpu.SMEM),
                  pl.BlockSpec((A_BLOCK, A_Q), nbr(0)),
                  kv(-1), kv(0), kv(1), kv(-1), kv(0), kv(1), cache, cache],
        out_specs=pl.BlockSpec((A_BLOCK, A_Q), lambda b, t: (b * nb + t, 0)),
        out_shape=jax.ShapeDtypeStruct((N_SAMPLE, A_Q), BF16),
        compiler_params=_cparams(("parallel", "parallel")), name="win_attn",
    )(sink_a, qa, ka, ka, ka, va, va, va, cache_k, cache_v)


def _nbr_attn_kernel(q_ref, kp_ref, kc_ref, kn_ref, vp_ref, vc_ref, vn_ref, ck_ref, cv_ref, tab_ref, o_ref,
                     bias_ref):
    j = pl.program_id(0)
    nb = pl.num_programs(0)
    slots = _nbr_row_slots()

    def build(cls):
        for h in range(C_HEADS):
            for qr in range(C_QROWS):
                for kk in range(3 * C_QROWS):
                    bias_ref[h, qr * GRID_W:(qr + 1) * GRID_W, kk * GRID_W:(kk + 1) * GRID_W] = (
                        tab_ref[h, slots[cls][qr][kk]])

    first_of_batch = pl.program_id(1) == 0
    for cls, at in enumerate((0, 1, nb - 1)):
        @pl.when(first_of_batch & (j == at))
        def _():
            build(cls)

    tq = C_QBLOCK
    q = q_ref[...] * SCALE
    k_loc = jnp.concatenate([kp_ref[...], kc_ref[...], kn_ref[...]], axis=0).astype(BF16)
    v_loc = jnp.concatenate([vp_ref[...], vc_ref[...], vn_ref[...]], axis=0).astype(BF16)
    k_ctx = ck_ref[...].astype(BF16)
    v_ctx = cv_ref[...].astype(BF16)
    s_loc = jnp.concatenate([_nt_dot(_head(q, h), _head(k_loc, h)) for h in range(C_HEADS)], axis=0)
    s_loc = s_loc + bias_ref[...].reshape(C_HEADS * tq, 3 * tq)
    s_ctx = jnp.concatenate([_nt_dot(_head(q, h), _head(k_ctx, h)) for h in range(C_HEADS)], axis=0)
    (e_loc, e_ctx), inv = _stacked_softmax([s_loc, s_ctx], None)
    outs = []
    for h in range(C_HEADS):
        r = slice(h * tq, (h + 1) * tq)
        outs.append((jnp.dot(e_loc[r], _head(v_loc, h), preferred_element_type=F32)
                     + jnp.dot(e_ctx[r], _head(v_ctx, h), preferred_element_type=F32)) * inv[r])
    o_ref[...] = jnp.concatenate(outs, axis=1).astype(BF16)


def _nbr_attn(qc, kc, vc, cache_k, cache_v, bias, l):
    nb = DEC_SEQ // C_QBLOCK
    base = N_PROMPT // C_QBLOCK

    def nbr(d):
        return lambda j, b: (base + b * nb + jnp.clip(j + d, 0, nb - 1), 0)

    kv = lambda d: pl.BlockSpec((C_QBLOCK, C_W), nbr(d))
    cache = pl.BlockSpec((None, None, PAST_LEN, C_W), lambda j, b: (b, l, 0, 0))
    return pl.pallas_call(
        _nbr_attn_kernel, grid=(nb, DEC_BATCH),
        in_specs=[kv(0), kv(-1), kv(0), kv(1), kv(-1), kv(0), kv(1), cache, cache,
                  pl.BlockSpec((None, C_HEADS, C_DR_SLOTS, GRID_W, GRID_W), lambda j, b: (l, 0, 0, 0, 0))],
        out_specs=pl.BlockSpec((C_QBLOCK, C_W), lambda j, b: (b * nb + j, 0)),
        out_shape=jax.ShapeDtypeStruct((N_SAMPLE, C_W), BF16),
        scratch_shapes=[pltpu.VMEM((C_HEADS, C_QBLOCK, 3 * C_QBLOCK), F32)],
        compiler_params=_cparams(("arbitrary", "arbitrary")), name="nbr_attn",
    )(qc, kc, kc, kc, vc, vc, vc, cache_k, cache_v, bias)


def _nbr_bias_tables(rpb):
    qcol = np.arange(GRID_W)
    qcs = np.clip(qcol - C_WIN_COLS // 2, 0, GRID_W - C_WIN_COLS)
    kcol = np.arange(GRID_W)
    col_ok = (kcol[None, :] >= qcs[:, None]) & (kcol[None, :] < qcs[:, None] + C_WIN_COLS)
    dc = np.clip(kcol[None, :] - qcol[:, None], -(C_WIN_COLS - 1), C_WIN_COLS - 1) + C_WIN_COLS - 1
    onehot_dc = (dc.reshape(-1)[None, :] == np.arange(2 * C_WIN_COLS - 1)[:, None]).astype(np.float32)
    t = jnp.einsum('lhab,bx->lhax', rpb, jnp.asarray(onehot_dc), precision=lax.Precision.HIGHEST)
    t = jnp.where(jnp.asarray(col_ok.reshape(-1)), t, NEG_INF)
    t = jnp.concatenate([t, jnp.full((DEPTH, C_HEADS, 1, GRID_W * GRID_W), NEG_INF, F32)], axis=2)
    return t.reshape(DEPTH, C_HEADS, C_DR_SLOTS, GRID_W, GRID_W)


def _nbr_row_slots():
    rows = DEC_SEQ // GRID_W
    slots = []
    for j in (0, 3, rows // C_QROWS - 1):
        per_q = []
        for qr in range(C_QROWS):
            r = C_QROWS * j + qr
            rs = min(max(r - C_WIN_ROWS // 2, 0), rows - C_WIN_ROWS)
            per_k = []
            for kk in range(3 * C_QROWS):
                kabs = C_QROWS * (j - 1) + kk
                per_k.append(kabs - r + C_WIN_ROWS - 1 if rs <= kabs < rs + C_WIN_ROWS else C_DR_SLOTS - 1)
            per_q.append(per_k)
        slots.append(per_q)
    return slots


def _fourier_kernel(u_ref, bc_ref, bs_ref, cl_ref, sl_ref, o_ref, zc_ref, zs_ref):
    @pl.when(pl.program_id(1) == 0)
    def _():
        u = u_ref[...]
        zc_ref[...] = jnp.dot(u, bc_ref[...].astype(BF16), preferred_element_type=F32).astype(BF16)
        zs_ref[...] = jnp.dot(u, bs_ref[...].astype(BF16), preferred_element_type=F32).astype(BF16)

    o = (jnp.dot(cl_ref[...].astype(BF16), zc_ref[...], preferred_element_type=F32)
         - jnp.dot(sl_ref[...].astype(BF16), zs_ref[...], preferred_element_type=F32))
    o_ref[...] = o.astype(BF16)


def _dft_tables(n):
    k = np.arange(n)
    ang = 2.0 * np.pi * ((k[:, None] * k[None, :]) % n) / n
    return np.cos(ang) / np.sqrt(n), np.sin(ang) / np.sqrt(n)


def _channel_dft_tables():
    c, s = _dft_tables(B_GROUP_DIM)
    eye = np.eye(B_GROUPS)
    return np.kron(eye, c).astype(np.float32), np.kron(eye, s).astype(np.float32)


def _fourier(fb, n_batch, seq, row0, tr):
    cl, sl = (jnp.asarray(a.astype(np.float32)) for a in _dft_tables(seq))
    bc, bs = (jnp.asarray(a) for a in _channel_dft_tables())
    nt = seq // tr
    const = pl.BlockSpec((B_WIDTH, B_WIDTH), lambda b, t: (0, 0))
    return pl.pallas_call(
        _fourier_kernel, grid=(n_batch, nt),
        in_specs=[pl.BlockSpec((seq, B_WIDTH), lambda b, t: (row0 // seq + b, 0)), const, const,
                  pl.BlockSpec((tr, seq), lambda b, t: (t, 0)), pl.BlockSpec((tr, seq), lambda b, t: (t, 0))],
        out_specs=pl.BlockSpec((tr, B_WIDTH), lambda b, t: (b * nt + t, 0)),
        out_shape=jax.ShapeDtypeStruct((n_batch * seq, B_WIDTH), BF16),
        scratch_shapes=[pltpu.VMEM((seq, B_WIDTH), BF16), pltpu.VMEM((seq, B_WIDTH), BF16)],
        compiler_params=_cparams(("parallel", "arbitrary")), name=f"fourier_{seq}",
    )(fb, bc, bs, cl, sl)


def _merge_kernel(oap_ref, obp_ref, ocp_ref, oas_ref, obs_ref, ocs_ref, gt_ref, xp_ref, xs_ref,
                  wa_ref, wb_ref, wc_ref, wo_ref, g1_ref, ng_ref, sh_ref, sc_ref, *rest, router):
    if router:
        wr_ref, o_ref, h2_ref, lg_ref, wab_ref, wbb_ref, wcb_ref, wob_ref = rest
    else:
        o_ref, h2_ref, wab_ref, wbb_ref, wcb_ref, wob_ref = rest

    @pl.when(pl.program_id(0) == 0)
    def _():
        wab_ref[...] = wa_ref[...].astype(BF16)
        wbb_ref[...] = wb_ref[...].astype(BF16)
        wcb_ref[...] = wc_ref[...].astype(BF16)
        wob_ref[...] = wo_ref[...].astype(BF16)

    ya = jnp.dot(_read_split(oap_ref, oas_ref), wab_ref[...], preferred_element_type=F32)
    yb = jnp.dot(_read_split(obp_ref, obs_ref), wbb_ref[...], preferred_element_type=F32)
    yc = jnp.dot(_read_split(ocp_ref, ocs_ref), wcb_ref[...], preferred_element_type=F32)
    d = D_MODEL
    m = (gt_ref[:, 0:d].astype(F32) * ya + gt_ref[:, d:2 * d].astype(F32) * yb
         + gt_ref[:, 2 * d:3 * d].astype(F32) * yc)
    y = jnp.dot(m.astype(BF16), wob_ref[...], preferred_element_type=F32)
    x_new = _read_split(xp_ref, xs_ref) + g1_ref[...] * y
    o_ref[...] = x_new
    h2 = _adaln_math(x_new, ng_ref[...], sh_ref[...], sc_ref[...])
    h2_ref[...] = h2.astype(h2_ref.dtype)
    if router:
        w = wr_ref[...]
        w_hi = w.astype(BF16)
        w_lo = (w - w_hi.astype(F32)).astype(BF16)
        h_hi = h2.astype(BF16)
        h_lo = (h2 - h_hi.astype(F32)).astype(BF16)
        hi_terms = jnp.dot(h_hi, jnp.concatenate([w_hi, w_lo], axis=1), preferred_element_type=F32)
        lg_ref[...] = (hi_terms[:, :LANES] + jnp.dot(h_lo, w_hi, preferred_element_type=F32)
                       + hi_terms[:, LANES:])


def _merge(branches_p, branches_s, gates, x, wa, wb, wc, wo, mod, l, norm_g, w_router=None):
    tm = 512
    router = w_router is not None
    x, joined = _split_operands(x)
    row = lambda w: pl.BlockSpec((tm, w), lambda i: (i, 0))
    row_p = lambda w: _split_rows(tm, w)[0]
    row_s = lambda w: _split_rows(tm, w)[1]
    const = lambda r, c: pl.BlockSpec((None, r, c), lambda i: (l, 0, 0), pipeline_mode=pl.Buffered(1))
    mspec = lambda which: pl.BlockSpec((None, None, 1, D_MODEL), lambda i: (l, _cond_row(i, tm), 0, which))
    in_specs = [row_p(A_Q), row_p(B_WIDTH), row_p(C_W), row_s(A_Q), row_s(B_WIDTH), row_s(C_W),
                row(GATE_WIDTH), *_split_rows(tm, D_MODEL, joined),
                const(A_Q, D_MODEL), const(B_WIDTH, D_MODEL), const(C_W, D_MODEL), const(D_MODEL, D_MODEL),
                mspec(2), pl.BlockSpec((1, D_MODEL), lambda i: (0, 0)), mspec(3), mspec(4)]
    operands = [*branches_p, *branches_s, gates, *x, wa, wb, wc, wo, mod, norm_g.reshape(1, D_MODEL), mod, mod]
    out_specs = [row(D_MODEL), row(D_MODEL)]
    out_shape = [jax.ShapeDtypeStruct((N_TOK, D_MODEL), F32),
                 jax.ShapeDtypeStruct((N_TOK, D_MODEL), F32 if router else BF16)]
    if router:
        in_specs.append(pl.BlockSpec((D_MODEL, LANES), lambda i: (0, 0)))
        operands.append(jnp.pad(w_router, ((0, 0), (0, LANES - N_EXPERTS))))
        out_specs.append(row(LANES))
        out_shape.append(jax.ShapeDtypeStruct((N_TOK, LANES), F32))
    return pl.pallas_call(
        functools.partial(_merge_kernel, router=router), grid=(N_TOK // tm,),
        in_specs=in_specs, out_specs=out_specs, out_shape=out_shape,
        scratch_shapes=[pltpu.VMEM((A_Q, D_MODEL), BF16), pltpu.VMEM((B_WIDTH, D_MODEL), BF16),
                        pltpu.VMEM((C_W, D_MODEL), BF16), pltpu.VMEM((D_MODEL, D_MODEL), BF16)],
        compiler_params=_cparams(("arbitrary",)), name="merge_router" if router else "merge",
    )(*operands)


FFN_COL_CHUNK = D_FF // 2


def _ffn_kernel(h_ref, x_ref, g2_ref, ng_ref, sh_ref, sc_ref, wg_hbm, wu_hbm, wd_hbm, o_ref, hn_ref,
                wgb_ref, wub_ref, wdb_ref, stage_up_ref, stage_dn_ref, sem, *, i_dense):
    @pl.when(pl.program_id(0) == 0)
    def _():
        _load_weight_bf16(lambda k: wg_hbm.at[i_dense, _row_chunk(k)], wgb_ref, stage_up_ref, sem)
        _load_weight_bf16(lambda k: wu_hbm.at[i_dense, _row_chunk(k)], wub_ref, stage_up_ref, sem)
        _load_weight_bf16(lambda k: wd_hbm.at[i_dense, _row_chunk(k)], wdb_ref, stage_dn_ref, sem)

    h = h_ref[...]
    acc = None
    for c in range(D_FF // FFN_COL_CHUNK):
        cols = slice(c * FFN_COL_CHUNK, (c + 1) * FFN_COL_CHUNK)
        g = jnp.dot(h, wgb_ref[:, cols], preferred_element_type=F32)
        u = jnp.dot(h, wub_ref[:, cols], preferred_element_type=F32)
        a = (g * _sigmoid(g) * u).astype(BF16)
        d = jnp.dot(a, wdb_ref[cols, :], preferred_element_type=F32)
        acc = d if acc is None else acc + d
    x_new = x_ref[...] + g2_ref[...] * acc
    o_ref[...] = x_new
    hn_ref[...] = _adaln_math(x_new, ng_ref[...], sh_ref[...], sc_ref[...]).astype(BF16)


def _ffn(h, x, wg, wu, wd, mod, l, next_norm_g):
    tm = 512
    row = lambda dt: pl.BlockSpec((tm, D_MODEL), lambda i: (i, 0))
    mspec = lambda layer, which: pl.BlockSpec((None, None, 1, D_MODEL),
                                              lambda i: (layer, _cond_row(i, tm), 0, which))
    hbm = pl.BlockSpec(memory_space=pl.ANY)
    return pl.pallas_call(
        functools.partial(_ffn_kernel, i_dense=l // 2), grid=(N_TOK // tm,),
        in_specs=[row(BF16), row(F32), mspec(l, 5),
                  pl.BlockSpec((1, D_MODEL), lambda i: (0, 0)), mspec(l + 1, 0), mspec(l + 1, 1),
                  hbm, hbm, hbm],
        out_specs=[row(F32), row(BF16)],
        out_shape=[jax.ShapeDtypeStruct((N_TOK, D_MODEL), F32), jax.ShapeDtypeStruct((N_TOK, D_MODEL), BF16)],
        scratch_shapes=[pltpu.VMEM((D_MODEL, D_FF), BF16), pltpu.VMEM((D_MODEL, D_FF), BF16),
                        pltpu.VMEM((D_FF, D_MODEL), BF16),
                        pltpu.VMEM((2, WEIGHT_ROW_CHUNK, D_FF), F32),
                        pltpu.VMEM((2, WEIGHT_ROW_CHUNK, D_MODEL), F32), pltpu.SemaphoreType.DMA((2,))],
        compiler_params=_cparams(("arbitrary",)), name="ffn",
    )(h, x, mod, next_norm_g.reshape(1, D_MODEL), mod, mod, wg, wu, wd)


TOP_K = 2
MOE_TILE = 256
MOE_TILES = TOP_K * N_TOK // MOE_TILE + N_EXPERTS
MOE_ROWS = MOE_TILES * MOE_TILE
MOE_CHUNK = 10
MOE_CHUNKS = MOE_TILES // MOE_CHUNK + N_EXPERTS
MOE_TF = 896
ROUTE_TM = 512
DISPATCH_TM = 256
COMBINE_TM = 256
ROW_COPY_UNROLL = 16


def _route_kernel(lg_ref, o_ref, cnt_ref, base_ref, tri_ref):
    tm = lg_ref.shape[0]

    @pl.when(pl.program_id(0) == 0)
    def _():
        base_ref[...] = jnp.zeros_like(base_ref)
        r = lax.broadcasted_iota(jnp.int32, (tm, tm), 0)
        c = lax.broadcasted_iota(jnp.int32, (tm, tm), 1)
        tri_ref[...] = jnp.where(r > c, 1.0, 0.0).astype(BF16)

    lane = lax.broadcasted_iota(jnp.int32, lg_ref.shape, 1).astype(F32)
    lg = jnp.where(lane < N_EXPERTS, lg_ref[...], -jnp.inf)
    m1 = lg.max(axis=-1, keepdims=True)
    i1 = jnp.where(lg == m1, lane, float(LANES)).min(axis=-1, keepdims=True)
    rest = jnp.where(lane == i1, -jnp.inf, lg)
    m2 = rest.max(axis=-1, keepdims=True)
    i2 = jnp.where(rest == m2, lane, float(LANES)).min(axis=-1, keepdims=True)
    e2 = jnp.exp(m2 - m1)
    w1 = 1.0 / (1.0 + e2)
    w2 = e2 / (1.0 + e2)

    oh1 = jnp.where(lane == i1, 1.0, 0.0)
    oh2 = jnp.where(lane == i2, 1.0, 0.0)
    pre1 = jnp.dot(tri_ref[...], oh1.astype(BF16), preferred_element_type=F32)
    pre2 = jnp.dot(tri_ref[...], oh2.astype(BF16), preferred_element_type=F32)
    c1 = jnp.sum(oh1, axis=0, keepdims=True)
    c2 = jnp.sum(oh2, axis=0, keepdims=True)
    base = base_ref[...]
    rank1 = jnp.sum(oh1 * (base + pre1), axis=-1, keepdims=True)
    rank2 = jnp.sum(oh2 * (base + c1 + pre2), axis=-1, keepdims=True)
    base_ref[...] = base + c1 + c2

    cols = (i1, i2, rank1, rank2, w1, w2)
    out = jnp.zeros(lg_ref.shape, F32)
    for j, col in enumerate(cols):
        out = jnp.where(lane == float(j), col, out)
    o_ref[...] = out
    cnt_ref[...] = jnp.broadcast_to(base + c1 + c2, cnt_ref.shape)


def _dispatch_kernel(pos_ref, last_ref, nt_ref, h_ref, xs_hbm, zero_ref, stage_ref, zsem, sem):
    i = pl.program_id(0)
    tm = h_ref.shape[0]

    @pl.when(i == 0)
    def _():
        zero_ref[...] = jnp.zeros_like(zero_ref)

        def zero_copy(tile):
            row0 = pl.multiple_of(tile * MOE_TILE, MOE_TILE)
            return pltpu.make_async_copy(zero_ref, xs_hbm.at[pl.ds(row0, MOE_TILE)], zsem)

        def for_zeroed_tiles(fn):
            for e in range(N_EXPERTS):
                @pl.when(last_ref[e] >= 0)
                def _():
                    fn(zero_copy(last_ref[e]))

                tail = MOE_TILES - 1 - e

                @pl.when(tail >= nt_ref[0])
                def _():
                    fn(zero_copy(tail))

        for_zeroed_tiles(lambda cp: cp.start())
        for_zeroed_tiles(lambda cp: cp.wait())

    slot = i % 2
    stage_ref[slot] = h_ref[...]

    def row_copy(step, s, t, k):
        dst = xs_hbm.at[pl.ds(pos_ref[TOP_K * (step * tm + t) + k], 1)]
        return pltpu.make_async_copy(stage_ref.at[s, pl.ds(t, 1)], dst, sem.at[s])

    def issue(t, carry):
        for k in range(TOP_K):
            row_copy(i, slot, t, k).start()
        return carry

    def drain_of(step, s):
        def drain(t, carry):
            for k in range(TOP_K):
                row_copy(step, s, t, k).wait()
            return carry
        lax.fori_loop(0, tm, drain, 0, unroll=ROW_COPY_UNROLL)

    lax.fori_loop(0, tm, issue, 0, unroll=ROW_COPY_UNROLL)

    @pl.when(i > 0)
    def _():
        drain_of(i - 1, 1 - slot)

    @pl.when(i == pl.num_programs(0) - 1)
    def _():
        drain_of(i, slot)


def _expert_kernel(ce_ref, ct_ref, cn_ref, nch_ref, nt_ref, xs_hbm, wg_ref, wu_ref, wd_ref, y_hbm,
                   acc_ref, xbuf_ref, wgb_ref, wub_ref, wdb_ref, xsem, osem):
    c = pl.program_id(0)
    f = pl.program_id(1)
    last_f = pl.num_programs(1) - 1

    def tile_rows(tile):
        return pl.ds(pl.multiple_of(tile * MOE_TILE, MOE_TILE), MOE_TILE)

    def out_copy(slot, tile):
        return pltpu.make_async_copy(acc_ref.at[slot], y_hbm.at[tile_rows(tile)], osem)

    @pl.when((c == 0) & (f == 0))
    def _():
        acc_ref[0] = jnp.zeros((MOE_TILE, D_MODEL), F32)
        for e in range(N_EXPERTS):
            tail = MOE_TILES - 1 - e

            @pl.when(tail >= nt_ref[0])
            def _():
                cp = out_copy(0, tail)
                cp.start()
                cp.wait()

    @pl.when(c < nch_ref[0])
    def _():
        n = cn_ref[c]
        t0 = ct_ref[c]

        def x_copy(j):
            return pltpu.make_async_copy(xs_hbm.at[tile_rows(t0 + j)], xbuf_ref.at[j], xsem.at[j])

        @pl.when(f == 0)
        def _():
            def fetch(j, carry):
                x_copy(j).start()
                return carry
            lax.fori_loop(0, n, fetch, 0)

        wgb_ref[...] = wg_ref[...].astype(BF16)
        wub_ref[...] = wu_ref[...].astype(BF16)
        wdb_ref[...] = wd_ref[...].astype(BF16)

        def tile_step(j, carry):
            @pl.when(f == 0)
            def _():
                x_copy(j).wait()

                acc_ref[j] = jnp.zeros((MOE_TILE, D_MODEL), F32)

            x = xbuf_ref[j].astype(BF16)
            g = jnp.dot(x, wgb_ref[...], preferred_element_type=F32)
            u = jnp.dot(x, wub_ref[...], preferred_element_type=F32)
            a = (g * _sigmoid(g) * u).astype(BF16)
            acc_ref[j] += jnp.dot(a, wdb_ref[...], preferred_element_type=F32)

            @pl.when(f == last_f)
            def _():
                @pl.when(j > 0)
                def _():
                    out_copy(j - 1, t0 + j - 1).wait()

                out_copy(j, t0 + j).start()

            return carry

        lax.fori_loop(0, n, tile_step, 0)

        @pl.when(f == last_f)
        def _():
            out_copy(n - 1, t0 + n - 1).wait()


def _combine_kernel(pos_ref, route_ref, x_ref, g2_ref, y_hbm, op_ref, os_ref, buf_ref, sem):
    i = pl.program_id(0)
    n = pl.num_programs(0)
    tm = x_ref.shape[0]
    slot = i % 2

    def row_copy(step, s, t, k):
        src = y_hbm.at[pl.ds(pos_ref[TOP_K * (step * tm + t) + k], 1)]
        return pltpu.make_async_copy(src, buf_ref.at[s, k, pl.ds(t, 1)], sem.at[s])

    def issue(step, s):
        def body(t, carry):
            for k in range(TOP_K):
                row_copy(step, s, t, k).start()
            return carry
        lax.fori_loop(0, tm, body, 0, unroll=ROW_COPY_UNROLL)

    @pl.when(i == 0)
    def _():
        issue(0, 0)

    @pl.when(i + 1 < n)
    def _():
        issue(i + 1, 1 - slot)

    def wait_body(t, carry):
        for k in range(TOP_K):
            row_copy(i, slot, t, k).wait()
        return carry

    lax.fori_loop(0, tm, wait_body, 0, unroll=ROW_COPY_UNROLL)

    lane = lax.broadcasted_iota(jnp.int32, route_ref.shape, 1)
    rt = route_ref[...]
    w1 = jnp.sum(jnp.where(lane == 2 * TOP_K, rt, 0.0), axis=-1, keepdims=True)
    w2 = jnp.sum(jnp.where(lane == 2 * TOP_K + 1, rt, 0.0), axis=-1, keepdims=True)
    out = x_ref[...] + g2_ref[...] * (w1 * buf_ref[slot, 0] + w2 * buf_ref[slot, 1])

    @pl.when(i < N_PROMPT // tm)
    def _():
        op_ref[...] = out

    @pl.when(i >= N_PROMPT // tm)
    def _():
        os_ref[...] = out


def _moe(h, x, logits, wg, wu, wd, mod, l):
    route, cnt = pl.pallas_call(
        _route_kernel, grid=(N_TOK // ROUTE_TM,),
        in_specs=[pl.BlockSpec((ROUTE_TM, LANES), lambda i: (i, 0))],
        out_specs=[pl.BlockSpec((ROUTE_TM, LANES), lambda i: (i, 0)), pl.BlockSpec((8, LANES), lambda i: (0, 0))],
        out_shape=[jax.ShapeDtypeStruct((N_TOK, LANES), F32), jax.ShapeDtypeStruct((8, LANES), F32)],
        scratch_shapes=[pltpu.VMEM((1, LANES), F32), pltpu.VMEM((ROUTE_TM, ROUTE_TM), BF16)],
        compiler_params=_cparams(("arbitrary",)), name="route",
    )(logits)

    expert = route[:, 0:TOP_K].astype(jnp.int32)
    rank = route[:, TOP_K:2 * TOP_K].astype(jnp.int32)
    n_sub = (cnt[0, :N_EXPERTS].astype(jnp.int32) + MOE_TILE - 1) // MOE_TILE
    end = jnp.cumsum(n_sub)
    start = end - n_sub
    start_of = jnp.sum(jnp.where(expert[:, :, None] == jnp.arange(N_EXPERTS), start, 0), axis=-1)
    pos = (start_of * MOE_TILE + rank).reshape(-1)
    n_tiles = end[N_EXPERTS - 1:]
    last_tile = jnp.where(n_sub > 0, end - 1, -1).astype(jnp.int32)

    xs = pl.pallas_call(
        _dispatch_kernel,
        grid_spec=pltpu.PrefetchScalarGridSpec(
            num_scalar_prefetch=3, grid=(N_TOK // DISPATCH_TM,),
            in_specs=[pl.BlockSpec((DISPATCH_TM, D_MODEL), lambda i, p, lt, nt: (i, 0))],
            out_specs=pl.BlockSpec(memory_space=pl.ANY),
            scratch_shapes=[pltpu.VMEM((MOE_TILE, D_MODEL), F32), pltpu.VMEM((2, DISPATCH_TM, D_MODEL), F32),
                            pltpu.SemaphoreType.DMA(()), pltpu.SemaphoreType.DMA((2,))]),
        out_shape=jax.ShapeDtypeStruct((MOE_ROWS, D_MODEL), F32),
        compiler_params=_cparams(("arbitrary",)), name="moe_dispatch",
    )(pos, last_tile, n_tiles, h)

    n_chunk = (n_sub + MOE_CHUNK - 1) // MOE_CHUNK
    chunk_end = jnp.cumsum(n_chunk)
    cidx = jnp.arange(MOE_CHUNKS)
    c_expert = jnp.minimum(jnp.sum(cidx[:, None] >= chunk_end[None, :], axis=1), N_EXPERTS - 1).astype(jnp.int32)
    c_k = cidx - (chunk_end - n_chunk)[c_expert]
    c_tile0 = (start[c_expert] + c_k * MOE_CHUNK).astype(jnp.int32)
    c_ntiles = jnp.clip(n_sub[c_expert] - c_k * MOE_CHUNK, 0, MOE_CHUNK).astype(jnp.int32)
    n_chunks = chunk_end[N_EXPERTS - 1:]

    nf = D_FF_EXPERT // MOE_TF

    def w_idx(c, f, ce, nch):
        live = c < nch[0]
        return ce[jnp.minimum(c, nch[0] - 1)], jnp.where(live, f, nf - 1)

    def up_map(c, f, ce, ct, cn, nch, nt):
        e, ff = w_idx(c, f, ce, nch)
        return (e, 0, ff)

    def down_map(c, f, ce, ct, cn, nch, nt):
        e, ff = w_idx(c, f, ce, nch)
        return (e, ff, 0)

    y = pl.pallas_call(
        _expert_kernel,
        grid_spec=pltpu.PrefetchScalarGridSpec(
            num_scalar_prefetch=5, grid=(MOE_CHUNKS, nf),
            in_specs=[pl.BlockSpec(memory_space=pl.ANY),
                      pl.BlockSpec((None, D_MODEL, MOE_TF), up_map),
                      pl.BlockSpec((None, D_MODEL, MOE_TF), up_map),
                      pl.BlockSpec((None, MOE_TF, D_MODEL), down_map)],
            out_specs=pl.BlockSpec(memory_space=pl.ANY),
            scratch_shapes=[pltpu.VMEM((MOE_CHUNK, MOE_TILE, D_MODEL), F32),
                            pltpu.VMEM((MOE_CHUNK, MOE_TILE, D_MODEL), F32),
                            pltpu.VMEM((D_MODEL, MOE_TF), BF16), pltpu.VMEM((D_MODEL, MOE_TF), BF16),
                            pltpu.VMEM((MOE_TF, D_MODEL), BF16),
                            pltpu.SemaphoreType.DMA((MOE_CHUNK,)), pltpu.SemaphoreType.DMA(())]),
        out_shape=jax.ShapeDtypeStruct((MOE_ROWS, D_MODEL), F32),
        compiler_params=_cparams(("arbitrary", "arbitrary")), name="moe_experts",
    )(c_expert, c_tile0, c_ntiles, n_chunks, n_tiles, xs, wg, wu, wd)

    tm = COMBINE_TM
    return pl.pallas_call(
        _combine_kernel,
        grid_spec=pltpu.PrefetchScalarGridSpec(
            num_scalar_prefetch=1, grid=(N_TOK // tm,),
            in_specs=[pl.BlockSpec((tm, LANES), lambda i, p: (i, 0)),
                      pl.BlockSpec((tm, D_MODEL), lambda i, p: (i, 0)),
                      pl.BlockSpec((None, None, 1, D_MODEL), lambda i, p: (l, _cond_row(i, tm), 0, 5)),
                      pl.BlockSpec(memory_space=pl.ANY)],
            out_specs=[pl.BlockSpec((tm, D_MODEL), lambda i, p: (jnp.minimum(i, N_PROMPT // tm - 1), 0)),
                       pl.BlockSpec((tm, D_MODEL), lambda i, p: (jnp.maximum(i - N_PROMPT // tm, 0), 0))],
            scratch_shapes=[pltpu.VMEM((2, TOP_K, tm, D_MODEL), F32), pltpu.SemaphoreType.DMA((2,))]),
        out_shape=[jax.ShapeDtypeStruct((N_PROMPT, D_MODEL), F32), jax.ShapeDtypeStruct((N_SAMPLE, D_MODEL), F32)],
        compiler_params=_cparams(("arbitrary",)), name="moe_combine",
    )(pos, route, x, mod, y)


def _kv_leaf_kernel(*refs):
    ins, outs = refs[:4 * DEPTH], refs[4 * DEPTH:]
    for l in range(DEPTH):
        for j in range(4):
            outs[j][l] = ins[4 * l + j][...]


def _kv_leaves(per_layer):
    widths = (A_KV, A_KV, C_W, C_W)
    return pl.pallas_call(
        _kv_leaf_kernel, grid=(BATCH,),
        in_specs=[pl.BlockSpec((SEQ, w), lambda b: (b, 0)) for _ in range(DEPTH) for w in widths],
        out_specs=[pl.BlockSpec((None, DEPTH, SEQ, w), lambda b: (b, 0, 0, 0)) for w in widths],
        out_shape=[jax.ShapeDtypeStruct((BATCH, DEPTH, SEQ, w), F32) for w in widths],
        compiler_params=_cparams(("parallel",)), name="kv_leaves",
    )(*[a for layer in per_layer for a in layer])


def _rope_tables():
    t = jnp.arange(DEC_SEQ)
    row = (t // GRID_W).astype(F32)
    col = (t % GRID_W).astype(F32)
    n_freq = HEAD_DIM // 4
    inv = ROPE_BASE ** (-jnp.arange(n_freq, dtype=F32) / n_freq)
    ang = jnp.concatenate([row[:, None] * inv, col[:, None] * inv], axis=-1)
    cos, sin = jnp.cos(ang), jnp.sin(ang)
    cos_h = jnp.concatenate([cos, cos], axis=-1)
    sin_h = jnp.concatenate([-sin, sin], axis=-1)
    cos_l = jnp.tile(jnp.concatenate([cos_h, cos_h], axis=-1), (DEC_BATCH, 1))
    sin_l = jnp.tile(jnp.concatenate([sin_h, sin_h], axis=-1), (DEC_BATCH, 1))
    cos_t = jnp.concatenate([jnp.ones((N_PROMPT, LANES), F32), cos_l], axis=0)
    sin_t = jnp.concatenate([jnp.zeros((N_PROMPT, LANES), F32), sin_l], axis=0)
    return cos_t, sin_t


def kernel(x_prompt, x_sample, cache_a_k, cache_a_v, cache_c_k, cache_c_v, c, c_ctx, w_mod, b_mod, norm1_g, norm2_g, w_in, qk_norm_a, qk_norm_c, sink_a, rpb_c, w_branch_a, w_branch_b, w_branch_c, w_out, w_ff_gate, w_ff_up, w_ff_down, w_router, w_exp_gate, w_exp_up, w_exp_down):
    x = (x_prompt.reshape(N_PROMPT, D_MODEL), x_sample.reshape(N_SAMPLE, D_MODEL))
    cond =jnp.concatenate([c_ctx[None, :], c], axis=0)
    cond_t = jnp.broadcast_to(cond[:, :, None], (N_COND, D_MODEL, LANES))
    mod = _modulation(cond_t, w_mod, b_mod)
    cos_t, sin_t = _rope_tables()
    bias = _nbr_bias_tables(rpb_c)
    ck_a = cache_a_k.reshape(DEC_BATCH, DEPTH, PAST_LEN, A_KV)
    cv_a = cache_a_v.reshape(DEC_BATCH, DEPTH, PAST_LEN, A_KV)
    ck_c = cache_c_k.reshape(DEC_BATCH, DEPTH, PAST_LEN, C_W)
    cv_c = cache_c_v.reshape(DEC_BATCH, DEPTH, PAST_LEN, C_W)

    new_kv = []
    h = x
    for l in range(DEPTH):
        qa, ka, va, fb, qc, kc, vc, gates = _in_proj(h, w_in, cos_t, sin_t, qk_norm_a[l], qk_norm_c[l], l,
                                                     norm=(norm1_g[l], mod) if l == 0 else None)
        oa_p, oc_p = _ctx_attn(sink_a, qa, ka, va, qc, kc, vc, l)
        oa_s = _win_attn(sink_a, qa, ka, va, ck_a, cv_a, l)
        oc_s = _nbr_attn(qc, kc, vc, ck_c, cv_c, bias, l)
        ob_p = _fourier(fb, BATCH, SEQ, 0, SEQ)
        ob_s = _fourier(fb, DEC_BATCH, DEC_SEQ, N_PROMPT, 512)
        branches = ((oa_p, ob_p, oc_p), (oa_s, ob_s, oc_s), gates, x,
                    w_branch_a, w_branch_b, w_branch_c, w_out, mod, l, norm2_g[l])
        i = l // 2
        if l % 2 == 0:
            x, h2 = _merge(*branches)
            x, h = _ffn(h2, x, w_ff_gate, w_ff_up, w_ff_down, mod, l, norm1_g[l + 1])
        else:
            x, h2, logits = _merge(*branches, w_router=w_router[i])
            xp, xs = _moe(h2, x, logits, w_exp_gate[i], w_exp_up[i], w_exp_down[i], mod, l)
        new_kv.append((ka, va, kc, vc))

    new_ak, new_av, new_ck, new_cv = _kv_leaves(new_kv)
    return (xp.reshape(BATCH, SEQ, D_MODEL), xs.reshape(DEC_BATCH, DEC_SEQ, D_MODEL),
            new_ak.reshape(BATCH, DEPTH, SEQ, A_KV_HEADS, HEAD_DIM),
            new_av.reshape(BATCH, DEPTH, SEQ, A_KV_HEADS, HEAD_DIM),
            new_ck.reshape(BATCH, DEPTH, SEQ, C_HEADS, HEAD_DIM),
            new_cv.reshape(BATCH, DEPTH, SEQ, C_HEADS, HEAD_DIM))
```

```python
import functools

import numpy as np
import jax
import jax.numpy as jnp
from jax import lax
from jax.experimental import pallas as pl
from jax.experimental.pallas import tpu as pltpu

F32 = jnp.float32
BF16 = jnp.bfloat16

D_MODEL = 1024
BATCH = 16
SEQ = 256
DEPTH = 2
DEC_BATCH = 2
DEC_SEQ = 2048
PAST_LEN = 512
GRID_W = 64
HEAD_DIM = 64
SCALE = HEAD_DIM ** -0.5
A_HEADS = 8
A_KV_HEADS = 2
A_GROUP = A_HEADS // A_KV_HEADS
A_WINDOW = 128
A_BLOCK = 128
B_GROUPS = 8
B_GROUP_DIM = 64
B_WIDTH = B_GROUPS * B_GROUP_DIM
C_HEADS = 8
C_WIN_ROWS = 8
C_WIN_COLS = 16
A_Q = A_HEADS * HEAD_DIM
A_KV = A_KV_HEADS * HEAD_DIM
C_W = C_HEADS * HEAD_DIM
QKV_WIDTH = A_Q + 2 * A_KV + B_WIDTH + 3 * C_W
N_BRANCH = 3
GATE_WIDTH = N_BRANCH * D_MODEL
IN_WIDTH = QKV_WIDTH + GATE_WIDTH
D_FF = 2816
N_EXPERTS = 8
D_FF_EXPERT = 3584
ROPE_BASE = 10000.0
RMS_EPS = 1e-6
NEG_INF = -1e30

N_PROMPT = BATCH * SEQ
N_SAMPLE = DEC_BATCH * DEC_SEQ
N_TOK = N_PROMPT + N_SAMPLE
N_COND = 1 + DEC_BATCH
LANES = 128
NORM_SLAB = 256
C_QROWS = 4
C_QBLOCK = C_QROWS * GRID_W
C_DR_SLOTS = 2 * C_WIN_ROWS
VMEM_LIMIT = 56 * 1024 * 1024


def _cparams(sem):
    return pltpu.CompilerParams(dimension_semantics=sem, vmem_limit_bytes=VMEM_LIMIT)


def _sigmoid(x):
    return 1.0 / (1.0 + jnp.exp(-x))


def _cond_row(tile, tm):
    return jnp.maximum(tile * tm // DEC_SEQ - 1, 0)


def _mod_kernel(ct_ref, w_ref, b_ref, o_ref, silu_ref):
    @pl.when((pl.program_id(0) == 0) & (pl.program_id(1) == 0))
    def _():
        cb = ct_ref[...]
        silu_ref[...] = cb * _sigmoid(cb)

    tn = w_ref.shape[1]
    for r in range(N_COND):
        s = silu_ref[r]
        for cc in range(tn // LANES):
            sl = slice(cc * LANES, (cc + 1) * LANES)
            o_ref[r, :, sl] = jnp.sum(w_ref[:, sl] * s, axis=0, keepdims=True) + b_ref[:, sl]


def _modulation(cond_t, w_mod, b_mod):
    tn = 1024
    n = 6 * D_MODEL
    return pl.pallas_call(
        _mod_kernel,
        grid=(DEPTH, n // tn),
        in_specs=[
            pl.BlockSpec((N_COND, D_MODEL, LANES), lambda l, j: (0, 0, 0)),
            pl.BlockSpec((None, D_MODEL, tn), lambda l, j: (l, 0, j)),
            pl.BlockSpec((None, 1, tn), lambda l, j: (l, 0, j)),
        ],
        out_specs=pl.BlockSpec((None, N_COND, 1, tn), lambda l, j: (l, 0, 0, j)),
        out_shape=jax.ShapeDtypeStruct((DEPTH, N_COND, 1, n), F32),
        scratch_shapes=[pltpu.VMEM((N_COND, D_MODEL, LANES), F32)],
        compiler_params=_cparams(("arbitrary", "arbitrary")),
        name="modulation",
    )(cond_t, w_mod, b_mod.reshape(DEPTH, 1, n))


def _adaln_math(x, g, sh, sc):
    ms = jnp.mean(x * x, axis=-1, keepdims=True)
    y = x * lax.rsqrt(ms + RMS_EPS) * g
    return y * (1.0 + sc) + sh


def _split_rows(tm, width, joined=False):
    n_p = N_PROMPT // tm
    latent0 = n_p if joined else 0
    return (pl.BlockSpec((tm, width), lambda i: (jnp.minimum(i, n_p - 1), 0)),
            pl.BlockSpec((tm, width), lambda i: (jnp.maximum(i - n_p, 0) + latent0, 0)))


def _split_operands(x):
    return (x, False) if isinstance(x, tuple) else ((x, x), True)


def _read_split(p_ref, s_ref):
    is_ctx = pl.program_id(0) < N_PROMPT // p_ref.shape[0]
    return jnp.where(is_ctx, p_ref[...], s_ref[...])


WEIGHT_ROW_CHUNK = 128


def _load_weight_bf16(src_rows, dst_ref, stage_ref, sem):
    n = dst_ref.shape[0] // WEIGHT_ROW_CHUNK

    def chunk_copy(k):
        return pltpu.make_async_copy(src_rows(k), stage_ref.at[k % 2], sem.at[k % 2])

    chunk_copy(0).start()
    for k in range(n):
        if k + 1 < n:
            chunk_copy(k + 1).start()
        chunk_copy(k).wait()
        dst_ref[k * WEIGHT_ROW_CHUNK:(k + 1) * WEIGHT_ROW_CHUNK, :] = stage_ref[k % 2].astype(BF16)


def _row_chunk(k):
    return pl.ds(k * WEIGHT_ROW_CHUNK, WEIGHT_ROW_CHUNK)


def _head_norm(x, gain, bd):
    sq = x * x
    hi = sq.astype(BF16)
    lo = (sq - hi.astype(F32)).astype(BF16)
    ms = jnp.dot(hi, bd, preferred_element_type=F32) + jnp.dot(lo, bd, preferred_element_type=F32)
    return x * lax.rsqrt(ms + RMS_EPS) * gain


def _rope(x, cos, sin_signed, first_half):
    half = HEAD_DIM // 2
    swapped = jnp.where(first_half, pltpu.roll(x, x.shape[1] - half, 1), pltpu.roll(x, half, 1))
    return x * cos + swapped * sin_signed


def _in_proj_kernel(*refs, l, fused_norm):
    n_lead = 5 if fused_norm else 1
    (cos_ref, sin_ref, ga_ref, gc_ref, w_hbm, qa_ref, ka_ref, va_ref, fb_ref, qc_ref, kc_ref, vc_ref, gt_ref,
     wb_ref, stage_ref, sem) = refs[n_lead:]

    @pl.when(pl.program_id(0) == 0)
    def _():
        _load_weight_bf16(lambda k: w_hbm.at[l, _row_chunk(k)], wb_ref, stage_ref, sem)

    if fused_norm:
        xp_ref, xs_ref, ng_ref, sh_ref, sc_ref = refs[:n_lead]
        h = _adaln_math(_read_split(xp_ref, xs_ref), ng_ref[...], sh_ref[...], sc_ref[...]).astype(BF16)
    else:
        h = refs[0][...]

    def proj(off, width):
        return jnp.dot(h, wb_ref[:, off:off + width], preferred_element_type=F32)

    r = lax.broadcasted_iota(jnp.int32, (NORM_SLAB, NORM_SLAB), 0) // HEAD_DIM
    c = lax.broadcasted_iota(jnp.int32, (NORM_SLAB, NORM_SLAB), 1) // HEAD_DIM
    bd = jnp.where(r == c, 1.0 / HEAD_DIM, 0.0).astype(BF16)
    lane = lax.broadcasted_iota(jnp.int32, (1, NORM_SLAB), 1)
    first_half = (lane % HEAD_DIM) < HEAD_DIM // 2
    cos = jnp.concatenate([cos_ref[...]] * (NORM_SLAB // LANES), axis=1)
    sin = jnp.concatenate([sin_ref[...]] * (NORM_SLAB // LANES), axis=1)
    gqa, gka = ga_ref[0:1, :], ga_ref[1:2, :]
    gqc, gkc = gc_ref[0:1, :], gc_ref[1:2, :]
    slabs = lambda p: [p[:, s * NORM_SLAB:(s + 1) * NORM_SLAB] for s in range(p.shape[1] // NORM_SLAB)]
    cat = lambda parts: jnp.concatenate(parts, axis=1)

    off = 0
    qa_ref[...] = cat([_rope(_head_norm(x, gqa, bd), cos, sin, first_half) for x in slabs(proj(off, A_Q))]
                      ).astype(BF16)
    off += A_Q
    kv = proj(off, 2 * A_KV)
    ka_ref[...] = _rope(_head_norm(kv, gka, bd), cos, sin, first_half)[:, :A_KV]
    va_ref[...] = kv[:, A_KV:]
    off += 2 * A_KV
    fb_ref[...] = proj(off, B_WIDTH).astype(BF16)
    off += B_WIDTH
    qc_ref[...] = cat([_head_norm(x, gqc, bd) for x in slabs(proj(off, C_W))]).astype(BF16)
    off += C_W
    kc_ref[...] = cat([_head_norm(x, gkc, bd) for x in slabs(proj(off, C_W))])
    off += C_W
    vc_ref[...] = proj(off, C_W)
    off += C_W
    for j in range(N_BRANCH):
        cols = slice(j * D_MODEL, (j + 1) * D_MODEL)
        gt_ref[:, cols] = _sigmoid(proj(off + j * D_MODEL, D_MODEL)).astype(BF16)


def _in_proj(h, w_in, cos_t, sin_t, qk_a, qk_c, l, norm=None):
    tm = 512
    row = lambda w: pl.BlockSpec((tm, w), lambda i: (i, 0))
    widths = (A_Q, A_KV, A_KV, B_WIDTH, C_W, C_W, C_W, GATE_WIDTH)
    dtypes = (BF16, F32, F32, BF16, BF16, F32, F32, BF16)
    if norm is None:
        lead_specs, lead = [row(D_MODEL)], [h]
    else:
        norm_g, mod = norm
        x, joined = _split_operands(h)
        mspec = lambda which: pl.BlockSpec((None, None, 1, D_MODEL), lambda i: (l, _cond_row(i, tm), 0, which))
        lead_specs = [*_split_rows(tm, D_MODEL, joined), pl.BlockSpec((1, D_MODEL), lambda i: (0, 0)),
                      mspec(0), mspec(1)]
        lead = [*x, norm_g.reshape(1, D_MODEL), mod, mod]
    return pl.pallas_call(
        functools.partial(_in_proj_kernel, l=l, fused_norm=norm is not None), grid=(N_TOK // tm,),
        in_specs=[*lead_specs, row(LANES), row(LANES),
                  pl.BlockSpec((2, NORM_SLAB), lambda i: (0, 0)), pl.BlockSpec((2, NORM_SLAB), lambda i: (0, 0)),
                  pl.BlockSpec(memory_space=pl.ANY)],
        out_specs=[row(w) for w in widths],
        out_shape=[jax.ShapeDtypeStruct((N_TOK, w), d) for w, d in zip(widths, dtypes)],
        scratch_shapes=[pltpu.VMEM((D_MODEL, IN_WIDTH), BF16),
                        pltpu.VMEM((2, WEIGHT_ROW_CHUNK, IN_WIDTH), F32), pltpu.SemaphoreType.DMA((2,))],
        compiler_params=_cparams(("arbitrary",)), name="in_proj",
    )(*lead, cos_t, sin_t, jnp.tile(qk_a, (1, NORM_SLAB // HEAD_DIM)), jnp.tile(qk_c, (1, NORM_SLAB // HEAD_DIM)), w_in)


def _nt_dot(a, b):
    return lax.dot_general(a, b, (((1,), (1,)), ((), ())), preferred_element_type=F32)


def _head(x, h):
    return x[:, h * HEAD_DIM:(h + 1) * HEAD_DIM]


def _stacked_softmax(parts, sink):
    m = parts[0].max(axis=-1, keepdims=True)
    for s in parts[1:]:
        m = jnp.maximum(m, s.max(axis=-1, keepdims=True))
    if sink is not None:
        m = jnp.maximum(m, sink)
    den = jnp.exp(sink - m) if sink is not None else 0.0
    es = []
    for s in parts:
        e = jnp.exp(s - m)
        den = den + e.sum(axis=-1, keepdims=True)
        es.append(e.astype(BF16))
    return es, 1.0 / den


def _sink_column(sink_ref, l, rows_per_head):
    return jnp.concatenate([jnp.full((rows_per_head, 1), sink_ref[l, h], F32) for h in range(A_HEADS)], axis=0)


def _gqa_queries(qa, g):
    return jnp.concatenate([_head(qa, g * A_GROUP + i) for i in range(A_GROUP)], axis=0)


def _ctx_attn_kernel(sink_ref, qa_ref, ka_ref, va_ref, qc_ref, kc_ref, vc_ref, oa_ref, oc_ref, *, l):
    t = SEQ
    qa = qa_ref[...] * SCALE
    ka = ka_ref[...].astype(BF16)
    va = va_ref[...].astype(BF16)
    s = jnp.concatenate([_nt_dot(_gqa_queries(qa, g), _head(ka, g)) for g in range(A_KV_HEADS)], axis=0)
    (e,), inv = _stacked_softmax([s], _sink_column(sink_ref, l, t))
    outs = []
    for g in range(A_KV_HEADS):
        rows = slice(g * A_GROUP * t, (g + 1) * A_GROUP * t)
        o = jnp.dot(e[rows], _head(va, g), preferred_element_type=F32) * inv[rows]
        outs += [o[i * t:(i + 1) * t] for i in range(A_GROUP)]
    oa_ref[...] = jnp.concatenate(outs, axis=1).astype(BF16)

    qc = qc_ref[...] * SCALE
    kc = kc_ref[...].astype(BF16)
    vc = vc_ref[...].astype(BF16)
    s = jnp.concatenate([_nt_dot(_head(qc, h), _head(kc, h)) for h in range(C_HEADS)], axis=0)
    (e,), inv = _stacked_softmax([s], None)
    outs = [jnp.dot(e[h * t:(h + 1) * t], _head(vc, h), preferred_element_type=F32) * inv[h * t:(h + 1) * t]
            for h in range(C_HEADS)]
    oc_ref[...] = jnp.concatenate(outs, axis=1).astype(BF16)


def _ctx_attn(sink_a, qa, ka, va, qc, kc, vc, l):
    blk = lambda w: pl.BlockSpec((SEQ, w), lambda b: (b, 0))
    return pl.pallas_call(
        functools.partial(_ctx_attn_kernel, l=l), grid=(BATCH,),
        in_specs=[pl.BlockSpec(memory_space=pltpu.SMEM),
                  blk(A_Q), blk(A_KV), blk(A_KV), blk(C_W), blk(C_W), blk(C_W)],
        out_specs=[blk(A_Q), blk(C_W)],
        out_shape=[jax.ShapeDtypeStruct((N_PROMPT, A_Q), BF16), jax.ShapeDtypeStruct((N_PROMPT, C_W), BF16)],
        compiler_params=_cparams(("parallel",)), name="ctx_attn",
    )(sink_a, qa, ka, va, qc, kc, vc)


def _win_attn_kernel(sink_ref, q_ref, kp_ref, kc_ref, kn_ref, vp_ref, vc_ref, vn_ref, ck_ref, cv_ref, o_ref, *, l):
    t = pl.program_id(1)
    rows = A_GROUP * A_BLOCK
    qi = lax.broadcasted_iota(jnp.int32, (rows, 3 * A_BLOCK), 0) % A_BLOCK
    kj = lax.broadcasted_iota(jnp.int32, (rows, 3 * A_BLOCK), 1) - A_BLOCK
    kpos = t * A_BLOCK + kj
    valid = (jnp.abs(kj - qi) <= A_WINDOW) & (kpos >= 0) & (kpos < DEC_SEQ)
    sink = _sink_column(sink_ref, l, A_BLOCK)
    for g in range(A_KV_HEADS):
        sl = slice(g * HEAD_DIM, (g + 1) * HEAD_DIM)
        q = jnp.concatenate([q_ref[:, (g * A_GROUP + i) * HEAD_DIM:(g * A_GROUP + i + 1) * HEAD_DIM]
                             for i in range(A_GROUP)], axis=0) * SCALE
        k_loc = jnp.concatenate([kp_ref[:, sl], kc_ref[:, sl], kn_ref[:, sl]], axis=0).astype(BF16)
        v_loc = jnp.concatenate([vp_ref[:, sl], vc_ref[:, sl], vn_ref[:, sl]], axis=0).astype(BF16)
        s_loc = jnp.where(valid, _nt_dot(q, k_loc), NEG_INF)
        s_ctx = _nt_dot(q, ck_ref[:, sl].astype(BF16))
        (e_loc, e_ctx), inv = _stacked_softmax([s_loc, s_ctx], sink[g * rows:(g + 1) * rows])
        o = (jnp.dot(e_loc, v_loc, preferred_element_type=F32)
             + jnp.dot(e_ctx, cv_ref[:, sl].astype(BF16), preferred_element_type=F32)) * inv
        for i in range(A_GROUP):
            h = g * A_GROUP + i
            o_ref[:, h * HEAD_DIM:(h + 1) * HEAD_DIM] = o[i * A_BLOCK:(i + 1) * A_BLOCK].astype(BF16)


def _win_attn(sink_a, qa, ka, va, cache_k, cache_v, l):
    nb = DEC_SEQ // A_BLOCK
    base = N_PROMPT // A_BLOCK

    def nbr(d):
        return lambda b, t: (base + b * nb + jnp.clip(t + d, 0, nb - 1), 0)

    kv = lambda d: pl.BlockSpec((A_BLOCK, A_KV), nbr(d))
    cache = pl.BlockSpec((None, None, PAST_LEN, A_KV), lambda b, t: (b, l, 0, 0))
    return pl.pallas_call(
        functools.partial(_win_attn_kernel, l=l), grid=(DEC_BATCH, nb),
        in_specs=[pl.BlockSpec(memory_space=pltpu.SMEM),
                  pl.BlockSpec((A_BLOCK, A_Q), nbr(0)),
                  kv(-1), kv(0), kv(1), kv(-1), kv(0), kv(1), cache, cache],
        out_specs=pl.BlockSpec((A_BLOCK, A_Q), lambda b, t: (b * nb + t, 0)),
        out_shape=jax.ShapeDtypeStruct((N_SAMPLE, A_Q), BF16),
        compiler_params=_cparams(("parallel", "parallel")), name="win_attn",
    )(sink_a, qa, ka, ka, ka, va, va, va, cache_k, cache_v)


def _nbr_attn_kernel(q_ref, kp_ref, kc_ref, kn_ref, vp_ref, vc_ref, vn_ref, ck_ref, cv_ref, tab_ref, o_ref,
                     bias_ref):
    j = pl.program_id(0)
    nb = pl.num_programs(0)
    slots = _nbr_row_slots()

    def build(cls):
        for h in range(C_HEADS):
            for qr in range(C_QROWS):
                for kk in range(3 * C_QROWS):
                    bias_ref[h, qr * GRID_W:(qr + 1) * GRID_W, kk * GRID_W:(kk + 1) * GRID_W] = (
                        tab_ref[h, slots[cls][qr][kk]])

    first_of_batch = pl.program_id(1) == 0
    for cls, at in enumerate((0, 1, nb - 1)):
        @pl.when(first_of_batch & (j == at))
        def _():
            build(cls)

    tq = C_QBLOCK
    q = q_ref[...] * SCALE
    k_loc = jnp.concatenate([kp_ref[...], kc_ref[...], kn_ref[...]], axis=0).astype(BF16)
    v_loc = jnp.concatenate([vp_ref[...], vc_ref[...], vn_ref[...]], axis=0).astype(BF16)
    k_ctx = ck_ref[...].astype(BF16)
    v_ctx = cv_ref[...].astype(BF16)
    s_loc = jnp.concatenate([_nt_dot(_head(q, h), _head(k_loc, h)) for h in range(C_HEADS)], axis=0)
    s_loc = s_loc + bias_ref[...].reshape(C_HEADS * tq, 3 * tq)
    s_ctx = jnp.concatenate([_nt_dot(_head(q, h), _head(k_ctx, h)) for h in range(C_HEADS)], axis=0)
    (e_loc, e_ctx), inv = _stacked_softmax([s_loc, s_ctx], None)
    outs = []
    for h in range(C_HEADS):
        r = slice(h * tq, (h + 1) * tq)
        outs.append((jnp.dot(e_loc[r], _head(v_loc, h), preferred_element_type=F32)
                     + jnp.dot(e_ctx[r], _head(v_ctx, h), preferred_element_type=F32)) * inv[r])
    o_ref[...] = jnp.concatenate(outs, axis=1).astype(BF16)


def _nbr_attn(qc, kc, vc, cache_k, cache_v, bias, l):
    nb = DEC_SEQ // C_QBLOCK
    base = N_PROMPT // C_QBLOCK

    def nbr(d):
        return lambda j, b: (base + b * nb + jnp.clip(j + d, 0, nb - 1), 0)

    kv = lambda d: pl.BlockSpec((C_QBLOCK, C_W), nbr(d))
    cache = pl.BlockSpec((None, None, PAST_LEN, C_W), lambda j, b: (b, l, 0, 0))
    return pl.pallas_call(
        _nbr_attn_kernel, grid=(nb, DEC_BATCH),
        in_specs=[kv(0), kv(-1), kv(0), kv(1), kv(-1), kv(0), kv(1), cache, cache,
                  pl.BlockSpec((None, C_HEADS, C_DR_SLOTS, GRID_W, GRID_W), lambda j, b: (l, 0, 0, 0, 0))],
        out_specs=pl.BlockSpec((C_QBLOCK, C_W), lambda j, b: (b * nb + j, 0)),
        out_shape=jax.ShapeDtypeStruct((N_SAMPLE, C_W), BF16),
        scratch_shapes=[pltpu.VMEM((C_HEADS, C_QBLOCK, 3 * C_QBLOCK), F32)],
        compiler_params=_cparams(("arbitrary", "arbitrary")), name="nbr_attn",
    )(qc, kc, kc, kc, vc, vc, vc, cache_k, cache_v, bias)


def _nbr_bias_tables(rpb):
    qcol = np.arange(GRID_W)
    qcs = np.clip(qcol - C_WIN_COLS // 2, 0, GRID_W - C_WIN_COLS)
    kcol = np.arange(GRID_W)
    col_ok = (kcol[None, :] >= qcs[:, None]) & (kcol[None, :] < qcs[:, None] + C_WIN_COLS)
    dc = np.clip(kcol[None, :] - qcol[:, None], -(C_WIN_COLS - 1), C_WIN_COLS - 1) + C_WIN_COLS - 1
    onehot_dc = (dc.reshape(-1)[None, :] == np.arange(2 * C_WIN_COLS - 1)[:, None]).astype(np.float32)
    t = jnp.einsum('lhab,bx->lhax', rpb, jnp.asarray(onehot_dc), precision=lax.Precision.HIGHEST)
    t = jnp.where(jnp.asarray(col_ok.reshape(-1)), t, NEG_INF)
    t = jnp.concatenate([t, jnp.full((DEPTH, C_HEADS, 1, GRID_W * GRID_W), NEG_INF, F32)], axis=2)
    return t.reshape(DEPTH, C_HEADS, C_DR_SLOTS, GRID_W, GRID_W)


def _nbr_row_slots():
    rows = DEC_SEQ // GRID_W
    slots = []
    for j in (0, 3, rows // C_QROWS - 1):
        per_q = []
        for qr in range(C_QROWS):
            r = C_QROWS * j + qr
            rs = min(max(r - C_WIN_ROWS // 2, 0), rows - C_WIN_ROWS)
            per_k = []
            for kk in range(3 * C_QROWS):
                kabs = C_QROWS * (j - 1) + kk
                per_k.append(kabs - r + C_WIN_ROWS - 1 if rs <= kabs < rs + C_WIN_ROWS else C_DR_SLOTS - 1)
            per_q.append(per_k)
        slots.append(per_q)
    return slots


def _fourier_kernel(u_ref, bc_ref, bs_ref, cl_ref, sl_ref, o_ref, zc_ref, zs_ref, clb_ref, slb_ref):
    t = pl.program_id(0)
    b = pl.program_id(1)

    @pl.when(t == 0)
    def _():
        u = u_ref[...]
        zc_ref[b] = jnp.dot(u, bc_ref[...].astype(BF16), preferred_element_type=F32).astype(BF16)
        zs_ref[b] = jnp.dot(u, bs_ref[...].astype(BF16), preferred_element_type=F32).astype(BF16)

    @pl.when(b == 0)
    def _():
        clb_ref[...] = cl_ref[...].astype(BF16)
        slb_ref[...] = sl_ref[...].astype(BF16)

    o = (jnp.dot(clb_ref[...], zc_ref[b], preferred_element_type=F32)
         - jnp.dot(slb_ref[...], zs_ref[b], preferred_element_type=F32))
    o_ref[...] = o.astype(BF16)


def _dft_tables(n):
    k = np.arange(n)
    ang = 2.0 * np.pi * ((k[:, None] * k[None, :]) % n) / n
    return np.cos(ang) / np.sqrt(n), np.sin(ang) / np.sqrt(n)


def _channel_dft_tables():
    c, s = _dft_tables(B_GROUP_DIM)
    eye = np.eye(B_GROUPS)
    return np.kron(eye, c).astype(np.float32), np.kron(eye, s).astype(np.float32)


def _fourier(fb, n_batch, seq, row0, tr):
    cl, sl = (jnp.asarray(a.astype(np.float32)) for a in _dft_tables(seq))
    bc, bs = (jnp.asarray(a) for a in _channel_dft_tables())
    nt = seq // tr
    const = pl.BlockSpec((B_WIDTH, B_WIDTH), lambda t, b: (0, 0))
    u_map = lambda t, b: (row0 // seq + jnp.where(t == 0, b, n_batch - 1), 0)
    return pl.pallas_call(
        _fourier_kernel, grid=(nt, n_batch),
        in_specs=[pl.BlockSpec((seq, B_WIDTH), u_map), const, const,
                  pl.BlockSpec((tr, seq), lambda t, b: (t, 0)), pl.BlockSpec((tr, seq), lambda t, b: (t, 0))],
        out_specs=pl.BlockSpec((tr, B_WIDTH), lambda t, b: (b * nt + t, 0)),
        out_shape=jax.ShapeDtypeStruct((n_batch * seq, B_WIDTH), BF16),
        scratch_shapes=[pltpu.VMEM((n_batch, seq, B_WIDTH), BF16), pltpu.VMEM((n_batch, seq, B_WIDTH), BF16),
                        pltpu.VMEM((tr, seq), BF16), pltpu.VMEM((tr, seq), BF16)],
        compiler_params=_cparams(("arbitrary", "arbitrary")), name=f"fourier_{seq}",
    )(fb, bc, bs, cl, sl)


def _merge_kernel(oap_ref, obp_ref, ocp_ref, oas_ref, obs_ref, ocs_ref, gt_ref, xp_ref, xs_ref,
                  wa_ref, wb_ref, wc_ref, wo_ref, g1_ref, ng_ref, sh_ref, sc_ref, *rest, router):
    if router:
        wr_ref, o_ref, h2_ref, lg_ref, wab_ref, wbb_ref, wcb_ref, wob_ref = rest
    else:
        o_ref, h2_ref, wab_ref, wbb_ref, wcb_ref, wob_ref = rest

    @pl.when(pl.program_id(0) == 0)
    def _():
        wab_ref[...] = wa_ref[...].astype(BF16)
        wbb_ref[...] = wb_ref[...].astype(BF16)
        wcb_ref[...] = wc_ref[...].astype(BF16)
        wob_ref[...] = wo_ref[...].astype(BF16)

    ya = jnp.dot(_read_split(oap_ref, oas_ref), wab_ref[...], preferred_element_type=F32)
    yb = jnp.dot(_read_split(obp_ref, obs_ref), wbb_ref[...], preferred_element_type=F32)
    yc = jnp.dot(_read_split(ocp_ref, ocs_ref), wcb_ref[...], preferred_element_type=F32)
    d = D_MODEL
    m = (gt_ref[:, 0:d].astype(F32) * ya + gt_ref[:, d:2 * d].astype(F32) * yb
         + gt_ref[:, 2 * d:3 * d].astype(F32) * yc)
    y = jnp.dot(m.astype(BF16), wob_ref[...], preferred_element_type=F32)
    x_new = _read_split(xp_ref, xs_ref) + g1_ref[...] * y
    o_ref[...] = x_new
    h2 = _adaln_math(x_new, ng_ref[...], sh_ref[...], sc_ref[...])
    h2_ref[...] = h2.astype(h2_ref.dtype)
    if router:
        w = wr_ref[...]
        w_hi = w.astype(BF16)
        w_lo = (w - w_hi.astype(F32)).astype(BF16)
        h_hi = h2.astype(BF16)
        h_lo = (h2 - h_hi.astype(F32)).astype(BF16)
        hi_terms = jnp.dot(h_hi, jnp.concatenate([w_hi, w_lo], axis=1), preferred_element_type=F32)
        lg_ref[...] = (hi_terms[:, :LANES] + jnp.dot(h_lo, w_hi, preferred_element_type=F32)
                       + hi_terms[:, LANES:])


def _merge(branches_p, branches_s, gates, x, wa, wb, wc, wo, mod, l, norm_g, w_router=None):
    tm = 512
    router = w_router is not None
    x, joined = _split_operands(x)
    row = lambda w: pl.BlockSpec((tm, w), lambda i: (i, 0))
    row_p = lambda w: _split_rows(tm, w)[0]
    row_s = lambda w: _split_rows(tm, w)[1]
    const = lambda r, c: pl.BlockSpec((None, r, c), lambda i: (l, 0, 0), pipeline_mode=pl.Buffered(1))
    mspec = lambda which: pl.BlockSpec((None, None, 1, D_MODEL), lambda i: (l, _cond_row(i, tm), 0, which))
    in_specs = [row_p(A_Q), row_p(B_WIDTH), row_p(C_W), row_s(A_Q), row_s(B_WIDTH), row_s(C_W),
                row(GATE_WIDTH), *_split_rows(tm, D_MODEL, joined),
                const(A_Q, D_MODEL), const(B_WIDTH, D_MODEL), const(C_W, D_MODEL), const(D_MODEL, D_MODEL),
                mspec(2), pl.BlockSpec((1, D_MODEL), lambda i: (0, 0)), mspec(3), mspec(4)]
    operands = [*branches_p, *branches_s, gates, *x, wa, wb, wc, wo, mod, norm_g.reshape(1, D_MODEL), mod, mod]
    out_specs = [row(D_MODEL), row(D_MODEL)]
    out_shape = [jax.ShapeDtypeStruct((N_TOK, D_MODEL), F32),
                 jax.ShapeDtypeStruct((N_TOK, D_MODEL), F32 if router else BF16)]
    if router:
        in_specs.append(pl.BlockSpec((D_MODEL, LANES), lambda i: (0, 0)))
        operands.append(jnp.pad(w_router, ((0, 0), (0, LANES - N_EXPERTS))))
        out_specs.append(row(LANES))
        out_shape.append(jax.ShapeDtypeStruct((N_TOK, LANES), F32))
    return pl.pallas_call(
        functools.partial(_merge_kernel, router=router), grid=(N_TOK // tm,),
        in_specs=in_specs, out_specs=out_specs, out_shape=out_shape,
        scratch_shapes=[pltpu.VMEM((A_Q, D_MODEL), BF16), pltpu.VMEM((B_WIDTH, D_MODEL), BF16),
                        pltpu.VMEM((C_W, D_MODEL), BF16), pltpu.VMEM((D_MODEL, D_MODEL), BF16)],
        compiler_params=_cparams(("arbitrary",)), name="merge_router" if router else "merge",
    )(*operands)


FFN_COL_CHUNK = D_FF // 2


def _ffn_kernel(h_ref, x_ref, g2_ref, ng_ref, sh_ref, sc_ref, wg_hbm, wu_hbm, wd_hbm, o_ref, hn_ref,
                wgb_ref, wub_ref, wdb_ref, stage_up_ref, stage_dn_ref, sem, *, i_dense):
    @pl.when(pl.program_id(0) == 0)
    def _():
        _load_weight_bf16(lambda k: wg_hbm.at[i_dense, _row_chunk(k)], wgb_ref, stage_up_ref, sem)
        _load_weight_bf16(lambda k: wu_hbm.at[i_dense, _row_chunk(k)], wub_ref, stage_up_ref, sem)
        _load_weight_bf16(lambda k: wd_hbm.at[i_dense, _row_chunk(k)], wdb_ref, stage_dn_ref, sem)

    h = h_ref[...]
    acc = None
    for c in range(D_FF // FFN_COL_CHUNK):
        cols = slice(c * FFN_COL_CHUNK, (c + 1) * FFN_COL_CHUNK)
        g = jnp.dot(h, wgb_ref[:, cols], preferred_element_type=F32)
        u = jnp.dot(h, wub_ref[:, cols], preferred_element_type=F32)
        a = (g * _sigmoid(g) * u).astype(BF16)
        d = jnp.dot(a, wdb_ref[cols, :], preferred_element_type=F32)
        acc = d if acc is None else acc + d
    x_new = x_ref[...] + g2_ref[...] * acc
    o_ref[...] = x_new
    hn_ref[...] = _adaln_math(x_new, ng_ref[...], sh_ref[...], sc_ref[...]).astype(BF16)


def _ffn(h, x, wg, wu, wd, mod, l, next_norm_g):
    tm = 512
    row = lambda dt: pl.BlockSpec((tm, D_MODEL), lambda i: (i, 0))
    mspec = lambda layer, which: pl.BlockSpec((None, None, 1, D_MODEL),
                                              lambda i: (layer, _cond_row(i, tm), 0, which))
    hbm = pl.BlockSpec(memory_space=pl.ANY)
    return pl.pallas_call(
        functools.partial(_ffn_kernel, i_dense=l // 2), grid=(N_TOK // tm,),
        in_specs=[row(BF16), row(F32), mspec(l, 5),
                  pl.BlockSpec((1, D_MODEL), lambda i: (0, 0)), mspec(l + 1, 0), mspec(l + 1, 1),
                  hbm, hbm, hbm],
        out_specs=[row(F32), row(BF16)],
        out_shape=[jax.ShapeDtypeStruct((N_TOK, D_MODEL), F32), jax.ShapeDtypeStruct((N_TOK, D_MODEL), BF16)],
        scratch_shapes=[pltpu.VMEM((D_MODEL, D_FF), BF16), pltpu.VMEM((D_MODEL, D_FF), BF16),
                        pltpu.VMEM((D_FF, D_MODEL), BF16),
                        pltpu.VMEM((2, WEIGHT_ROW_CHUNK, D_FF), F32),
                        pltpu.VMEM((2, WEIGHT_ROW_CHUNK, D_MODEL), F32), pltpu.SemaphoreType.DMA((2,))],
        compiler_params=_cparams(("arbitrary",)), name="ffn",
    )(h, x, mod, next_norm_g.reshape(1, D_MODEL), mod, mod, wg, wu, wd)


TOP_K = 2
MOE_TILE = 256
MOE_TILES = TOP_K * N_TOK // MOE_TILE + N_EXPERTS
MOE_ROWS = MOE_TILES * MOE_TILE
MOE_CHUNK = 10
MOE_CHUNKS = MOE_TILES // MOE_CHUNK + N_EXPERTS
MOE_TF = 896
ROUTE_TM = 512
DISPATCH_TM = 256
COMBINE_TM = 256
ROW_COPY_UNROLL = 16


def _route_kernel(lg_ref, o_ref, cnt_ref, base_ref, tri_ref):
    tm = lg_ref.shape[0]

    @pl.when(pl.program_id(0) == 0)
    def _():
        base_ref[...] = jnp.zeros_like(base_ref)
        r = lax.broadcasted_iota(jnp.int32, (tm, tm), 0)
        c = lax.broadcasted_iota(jnp.int32, (tm, tm), 1)
        tri_ref[...] = jnp.where(r > c, 1.0, 0.0).astype(BF16)

    lane = lax.broadcasted_iota(jnp.int32, lg_ref.shape, 1).astype(F32)
    lg = jnp.where(lane < N_EXPERTS, lg_ref[...], -jnp.inf)
    m1 = lg.max(axis=-1, keepdims=True)
    i1 = jnp.where(lg == m1, lane, float(LANES)).min(axis=-1, keepdims=True)
    rest = jnp.where(lane == i1, -jnp.inf, lg)
    m2 = rest.max(axis=-1, keepdims=True)
    i2 = jnp.where(rest == m2, lane, float(LANES)).min(axis=-1, keepdims=True)
    e2 = jnp.exp(m2 - m1)
    w1 = 1.0 / (1.0 + e2)
    w2 = e2 / (1.0 + e2)

    oh1 = jnp.where(lane == i1, 1.0, 0.0)
    oh2 = jnp.where(lane == i2, 1.0, 0.0)
    pre1 = jnp.dot(tri_ref[...], oh1.astype(BF16), preferred_element_type=F32)
    pre2 = jnp.dot(tri_ref[...], oh2.astype(BF16), preferred_element_type=F32)
    c1 = jnp.sum(oh1, axis=0, keepdims=True)
    c2 = jnp.sum(oh2, axis=0, keepdims=True)
    base = base_ref[...]
    rank1 = jnp.sum(oh1 * (base + pre1), axis=-1, keepdims=True)
    rank2 = jnp.sum(oh2 * (base + c1 + pre2), axis=-1, keepdims=True)
    base_ref[...] = base + c1 + c2

    cols = (i1, i2, rank1, rank2, w1, w2)
    out = jnp.zeros(lg_ref.shape, F32)
    for j, col in enumerate(cols):
        out = jnp.where(lane == float(j), col, out)
    o_ref[...] = out
    cnt_ref[...] = jnp.broadcast_to(base + c1 + c2, cnt_ref.shape)


def _dispatch_kernel(pos_ref, last_ref, nt_ref, h_ref, xs_hbm, zero_ref, stage_ref, zsem, sem):
    i = pl.program_id(0)
    tm = h_ref.shape[0]

    @pl.when(i == 0)
    def _():
        zero_ref[...] = jnp.zeros_like(zero_ref)

        def zero_copy(tile):
            row0 = pl.multiple_of(tile * MOE_TILE, MOE_TILE)
            return pltpu.make_async_copy(zero_ref, xs_hbm.at[pl.ds(row0, MOE_TILE)], zsem)

        def for_zeroed_tiles(fn):
            for e in range(N_EXPERTS):
                @pl.when(last_ref[e] >= 0)
                def _():
                    fn(zero_copy(last_ref[e]))

                tail = MOE_TILES - 1 - e

                @pl.when(tail >= nt_ref[0])
                def _():
                    fn(zero_copy(tail))

        for_zeroed_tiles(lambda cp: cp.start())
        for_zeroed_tiles(lambda cp: cp.wait())

    slot = i % 2
    stage_ref[slot] = h_ref[...]

    def row_copy(step, s, t, k):
        dst = xs_hbm.at[pl.ds(pos_ref[TOP_K * (step * tm + t) + k], 1)]
        return pltpu.make_async_copy(stage_ref.at[s, pl.ds(t, 1)], dst, sem.at[s])

    def issue(t, carry):
        for k in range(TOP_K):
            row_copy(i, slot, t, k).start()
        return carry

    def drain_of(step, s):
        def drain(t, carry):
            for k in range(TOP_K):
                row_copy(step, s, t, k).wait()
            return carry
        lax.fori_loop(0, tm, drain, 0, unroll=ROW_COPY_UNROLL)

    lax.fori_loop(0, tm, issue, 0, unroll=ROW_COPY_UNROLL)

    @pl.when(i > 0)
    def _():
        drain_of(i - 1, 1 - slot)

    @pl.when(i == pl.num_programs(0) - 1)
    def _():
        drain_of(i, slot)


def _expert_kernel(ce_ref, ct_ref, cn_ref, nch_ref, nt_ref, xs_hbm, wg_ref, wu_ref, wd_ref, y_hbm,
                   acc_ref, xb_ref, xstage_ref, wgb_ref, wub_ref, wdb_ref, xsem, osem):
    c = pl.program_id(0)
    f = pl.program_id(1)
    last_f = pl.num_programs(1) - 1

    def tile_rows(tile):
        return pl.ds(pl.multiple_of(tile * MOE_TILE, MOE_TILE), MOE_TILE)

    def out_copy(slot, tile):
        return pltpu.make_async_copy(acc_ref.at[slot], y_hbm.at[tile_rows(tile)], osem)

    @pl.when((c == 0) & (f == 0))
    def _():
        acc_ref[0] = jnp.zeros((MOE_TILE, D_MODEL), F32)
        for e in range(N_EXPERTS):
            tail = MOE_TILES - 1 - e

            @pl.when(tail >= nt_ref[0])
            def _():
                cp = out_copy(0, tail)
                cp.start()
                cp.wait()

    @pl.when(c < nch_ref[0])
    def _():
        n = cn_ref[c]
        t0 = ct_ref[c]

        def x_copy(j):
            return pltpu.make_async_copy(xs_hbm.at[tile_rows(t0 + j)], xstage_ref.at[j % 2], xsem.at[j % 2])

        @pl.when(f == 0)
        def _():
            x_copy(0).start()

        wgb_ref[...] = wg_ref[...].astype(BF16)
        wub_ref[...] = wu_ref[...].astype(BF16)
        wdb_ref[...] = wd_ref[...].astype(BF16)

        def tile_step(j, carry):
            @pl.when(f == 0)
            def _():
                @pl.when(j + 1 < n)
                def _():
                    x_copy(j + 1).start()

                x_copy(j).wait()
                xb_ref[j] = xstage_ref[j % 2].astype(BF16)
                acc_ref[j] = jnp.zeros((MOE_TILE, D_MODEL), F32)

            x = xb_ref[j]
            g = jnp.dot(x, wgb_ref[...], preferred_element_type=F32)
            u = jnp.dot(x, wub_ref[...], preferred_element_type=F32)
            a = (g * _sigmoid(g) * u).astype(BF16)
            acc_ref[j] += jnp.dot(a, wdb_ref[...], preferred_element_type=F32)

            @pl.when(f == last_f)
            def _():
                @pl.when(j > 0)
                def _():
                    out_copy(j - 1, t0 + j - 1).wait()

                out_copy(j, t0 + j).start()

            return carry

        lax.fori_loop(0, n, tile_step, 0)

        @pl.when(f == last_f)
        def _():
            out_copy(n - 1, t0 + n - 1).wait()


def _combine_kernel(pos_ref, route_ref, x_ref, g2_ref, y_hbm, op_ref, os_ref, buf_ref, sem):
    i = pl.program_id(0)
    n = pl.num_programs(0)
    tm = x_ref.shape[0]
    slot = i % 2

    def row_copy(step, s, t, k):
        src = y_hbm.at[pl.ds(pos_ref[TOP_K * (step * tm + t) + k], 1)]
        return pltpu.make_async_copy(src, buf_ref.at[s, k, pl.ds(t, 1)], sem.at[s])

    def issue(step, s):
        def body(t, carry):
            for k in range(TOP_K):
                row_copy(step, s, t, k).start()
            return carry
        lax.fori_loop(0, tm, body, 0, unroll=ROW_COPY_UNROLL)

    @pl.when(i == 0)
    def _():
        issue(0, 0)

    @pl.when(i + 1 < n)
    def _():
        issue(i + 1, 1 - slot)

    def wait_body(t, carry):
        for k in range(TOP_K):
            row_copy(i, slot, t, k).wait()
        return carry

    lax.fori_loop(0, tm, wait_body, 0, unroll=ROW_COPY_UNROLL)

    lane = lax.broadcasted_iota(jnp.int32, route_ref.shape, 1)
    rt = route_ref[...]
    w1 = jnp.sum(jnp.where(lane == 2 * TOP_K, rt, 0.0), axis=-1, keepdims=True)
    w2 = jnp.sum(jnp.where(lane == 2 * TOP_K + 1, rt, 0.0), axis=-1, keepdims=True)
    out = x_ref[...] + g2_ref[...] * (w1 * buf_ref[slot, 0] + w2 * buf_ref[slot, 1])

    @pl.when(i < N_PROMPT // tm)
    def _():
        op_ref[...] = out

    @pl.when(i >= N_PROMPT // tm)
    def _():
        os_ref[...] = out


def _moe(h, x, logits, wg, wu, wd, mod, l):
    route, cnt = pl.pallas_call(
        _route_kernel, grid=(N_TOK // ROUTE_TM,),
        in_specs=[pl.BlockSpec((ROUTE_TM, LANES), lambda i: (i, 0))],
        out_specs=[pl.BlockSpec((ROUTE_TM, LANES), lambda i: (i, 0)), pl.BlockSpec((8, LANES), lambda i: (0, 0))],
        out_shape=[jax.ShapeDtypeStruct((N_TOK, LANES), F32), jax.ShapeDtypeStruct((8, LANES), F32)],
        scratch_shapes=[pltpu.VMEM((1, LANES), F32), pltpu.VMEM((ROUTE_TM, ROUTE_TM), BF16)],
        compiler_params=_cparams(("arbitrary",)), name="route",
    )(logits)

    expert = route[:, 0:TOP_K].astype(jnp.int32)
    rank = route[:, TOP_K:2 * TOP_K].astype(jnp.int32)
    n_sub = (cnt[0, :N_EXPERTS].astype(jnp.int32) + MOE_TILE - 1) // MOE_TILE
    end = jnp.cumsum(n_sub)
    start = end - n_sub
    start_of = jnp.sum(jnp.where(expert[:, :, None] == jnp.arange(N_EXPERTS), start, 0), axis=-1)
    pos = (start_of * MOE_TILE + rank).reshape(-1)
    n_tiles = end[N_EXPERTS - 1:]
    last_tile = jnp.where(n_sub > 0, end - 1, -1).astype(jnp.int32)

    xs = pl.pallas_call(
        _dispatch_kernel,
        grid_spec=pltpu.PrefetchScalarGridSpec(
            num_scalar_prefetch=3, grid=(N_TOK // DISPATCH_TM,),
            in_specs=[pl.BlockSpec((DISPATCH_TM, D_MODEL), lambda i, p, lt, nt: (i, 0))],
            out_specs=pl.BlockSpec(memory_space=pl.ANY),
            scratch_shapes=[pltpu.VMEM((MOE_TILE, D_MODEL), F32), pltpu.VMEM((2, DISPATCH_TM, D_MODEL), F32),
                            pltpu.SemaphoreType.DMA(()), pltpu.SemaphoreType.DMA((2,))]),
        out_shape=jax.ShapeDtypeStruct((MOE_ROWS, D_MODEL), F32),
        compiler_params=_cparams(("arbitrary",)), name="moe_dispatch",
    )(pos, last_tile, n_tiles, h)

    n_chunk = (n_sub + MOE_CHUNK - 1) // MOE_CHUNK
    chunk_end = jnp.cumsum(n_chunk)
    cidx = jnp.arange(MOE_CHUNKS)
    c_expert = jnp.minimum(jnp.sum(cidx[:, None] >= chunk_end[None, :], axis=1), N_EXPERTS - 1).astype(jnp.int32)
    c_k = cidx - (chunk_end - n_chunk)[c_expert]
    c_tile0 = (start[c_expert] + c_k * MOE_CHUNK).astype(jnp.int32)
    c_ntiles = jnp.clip(n_sub[c_expert] - c_k * MOE_CHUNK, 0, MOE_CHUNK).astype(jnp.int32)
    n_chunks = chunk_end[N_EXPERTS - 1:]

    nf = D_FF_EXPERT // MOE_TF

    def w_idx(c, f, ce, nch):
        live = c < nch[0]
        return ce[jnp.minimum(c, nch[0] - 1)], jnp.where(live, f, nf - 1)

    def up_map(c, f, ce, ct, cn, nch, nt):
        e, ff = w_idx(c, f, ce, nch)
        return (e, 0, ff)

    def down_map(c, f, ce, ct, cn, nch, nt):
        e, ff = w_idx(c, f, ce, nch)
        return (e, ff, 0)

    y = pl.pallas_call(
        _expert_kernel,
        grid_spec=pltpu.PrefetchScalarGridSpec(
            num_scalar_prefetch=5, grid=(MOE_CHUNKS, nf),
            in_specs=[pl.BlockSpec(memory_space=pl.ANY),
                      pl.BlockSpec((None, D_MODEL, MOE_TF), up_map),
                      pl.BlockSpec((None, D_MODEL, MOE_TF), up_map),
                      pl.BlockSpec((None, MOE_TF, D_MODEL), down_map)],
            out_specs=pl.BlockSpec(memory_space=pl.ANY),
            scratch_shapes=[pltpu.VMEM((MOE_CHUNK, MOE_TILE, D_MODEL), F32),
                            pltpu.VMEM((MOE_CHUNK, MOE_TILE, D_MODEL), BF16),
                            pltpu.VMEM((2, MOE_TILE, D_MODEL), F32),
                            pltpu.VMEM((D_MODEL, MOE_TF), BF16), pltpu.VMEM((D_MODEL, MOE_TF), BF16),
                            pltpu.VMEM((MOE_TF, D_MODEL), BF16),
                            pltpu.SemaphoreType.DMA((2,)), pltpu.SemaphoreType.DMA(())]),
        out_shape=jax.ShapeDtypeStruct((MOE_ROWS, D_MODEL), F32),
        compiler_params=_cparams(("arbitrary", "arbitrary")), name="moe_experts",
    )(c_expert, c_tile0, c_ntiles, n_chunks, n_tiles, xs, wg, wu, wd)

    tm = COMBINE_TM
    return pl.pallas_call(
        _combine_kernel,
        grid_spec=pltpu.PrefetchScalarGridSpec(
            num_scalar_prefetch=1, grid=(N_TOK // tm,),
            in_specs=[pl.BlockSpec((tm, LANES), lambda i, p: (i, 0)),
                      pl.BlockSpec((tm, D_MODEL), lambda i, p: (i, 0)),
                      pl.BlockSpec((None, None, 1, D_MODEL), lambda i, p: (l, _cond_row(i, tm), 0, 5)),
                      pl.BlockSpec(memory_space=pl.ANY)],
            out_specs=[pl.BlockSpec((tm, D_MODEL), lambda i, p: (jnp.minimum(i, N_PROMPT // tm - 1), 0)),
                       pl.BlockSpec((tm, D_MODEL), lambda i, p: (jnp.maximum(i - N_PROMPT // tm, 0), 0))],
            scratch_shapes=[pltpu.VMEM((2, TOP_K, tm, D_MODEL), F32), pltpu.SemaphoreType.DMA((2,))]),
        out_shape=[jax.ShapeDtypeStruct((N_PROMPT, D_MODEL), F32), jax.ShapeDtypeStruct((N_SAMPLE, D_MODEL), F32)],
        compiler_params=_cparams(("arbitrary",)), name="moe_combine",
    )(pos, route, x, mod, y)


def _kv_leaf_kernel(*refs):
    ins, outs = refs[:4 * DEPTH], refs[4 * DEPTH:]
    for l in range(DEPTH):
        for j in range(4):
            outs[j][l] = ins[4 * l + j][...]


def _kv_leaves(per_layer):
    widths = (A_KV, A_KV, C_W, C_W)
    return pl.pallas_call(
        _kv_leaf_kernel, grid=(BATCH,),
        in_specs=[pl.BlockSpec((SEQ, w), lambda b: (b, 0)) for _ in range(DEPTH) for w in widths],
        out_specs=[pl.BlockSpec((None, DEPTH, SEQ, w), lambda b: (b, 0, 0, 0)) for w in widths],
        out_shape=[jax.ShapeDtypeStruct((BATCH, DEPTH, SEQ, w), F32) for w in widths],
        compiler_params=_cparams(("parallel",)), name="kv_leaves",
    )(*[a for layer in per_layer for a in layer])


def _rope_tables():
    t = jnp.arange(DEC_SEQ)
    row = (t // GRID_W).astype(F32)
    col = (t % GRID_W).astype(F32)
    n_freq = HEAD_DIM // 4
    inv = ROPE_BASE ** (-jnp.arange(n_freq, dtype=F32) / n_freq)
    ang = jnp.concatenate([row[:, None] * inv, col[:, None] * inv], axis=-1)
    cos, sin = jnp.cos(ang), jnp.sin(ang)
    cos_h = jnp.concatenate([cos, cos], axis=-1)
    sin_h = jnp.concatenate([-sin, sin], axis=-1)
    cos_l = jnp.tile(jnp.concatenate([cos_h, cos_h], axis=-1), (DEC_BATCH, 1))
    sin_l = jnp.tile(jnp.concatenate([sin_h, sin_h], axis=-1), (DEC_BATCH, 1))
    cos_t = jnp.concatenate([jnp.ones((N_PROMPT, LANES), F32), cos_l], axis=0)
    sin_t = jnp.concatenate([jnp.zeros((N_PROMPT, LANES), F32), sin_l], axis=0)
    return cos_t, sin_t


def kernel(x_prompt, x_sample, cache_a_k, cache_a_v, cache_c_k, cache_c_v, c, c_ctx, w_mod, b_mod, norm1_g, norm2_g, w_in, qk_norm_a, qk_norm_c, sink_a, rpb_c, w_branch_a, w_branch_b, w_branch_c, w_out, w_ff_gate, w_ff_up, w_ff_down, w_router, w_exp_gate, w_exp_up, w_exp_down):
    x = (x_prompt.reshape(N_PROMPT, D_MODEL), x_sample.reshape(N_SAMPLE, D_MODEL))
    cond =jnp.concatenate([c_ctx[None, :], c], axis=0)
    cond_t = jnp.broadcast_to(cond[:, :, None], (N_COND, D_MODEL, LANES))
    mod = _modulation(cond_t, w_mod, b_mod)
    cos_t, sin_t = _rope_tables()
    bias = _nbr_bias_tables(rpb_c)
    ck_a = cache_a_k.reshape(DEC_BATCH, DEPTH, PAST_LEN, A_KV)
    cv_a = cache_a_v.reshape(DEC_BATCH, DEPTH, PAST_LEN, A_KV)
    ck_c = cache_c_k.reshape(DEC_BATCH, DEPTH, PAST_LEN, C_W)
    cv_c = cache_c_v.reshape(DEC_BATCH, DEPTH, PAST_LEN, C_W)

    new_kv = []
    h = x
    for l in range(DEPTH):
        qa, ka, va, fb, qc, kc, vc, gates = _in_proj(h, w_in, cos_t, sin_t, qk_norm_a[l], qk_norm_c[l], l,
                                                     norm=(norm1_g[l], mod) if l == 0 else None)
        oa_p, oc_p = _ctx_attn(sink_a, qa, ka, va, qc, kc, vc, l)
        oa_s = _win_attn(sink_a, qa, ka, va, ck_a, cv_a, l)
        oc_s = _nbr_attn(qc, kc, vc, ck_c, cv_c, bias, l)
        ob_p = _fourier(fb, BATCH, SEQ, 0, SEQ)
        ob_s = _fourier(fb, DEC_BATCH, DEC_SEQ, N_PROMPT, 512)
        branches = ((oa_p, ob_p, oc_p), (oa_s, ob_s, oc_s), gates, x,
                    w_branch_a, w_branch_b, w_branch_c, w_out, mod, l, norm2_g[l])
        i = l // 2
        if l % 2 == 0:
            x, h2 = _merge(*branches)
            x, h = _ffn(h2, x, w_ff_gate, w_ff_up, w_ff_down, mod, l, norm1_g[l + 1])
        else:
            x, h2, logits = _merge(*branches, w_router=w_router[i])
            xp, xs = _moe(h2, x, logits, w_exp_gate[i], w_exp_up[i], w_exp_down[i], mod, l)
        new_kv.append((ka, va, kc, vc))

    new_ak, new_av, new_ck, new_cv = _kv_leaves(new_kv)
    return (xp.reshape(BATCH, SEQ, D_MODEL), xs.reshape(DEC_BATCH, DEC_SEQ, D_MODEL),
            new_ak.reshape(BATCH, DEPTH, SEQ, A_KV_HEADS, HEAD_DIM),
            new_av.reshape(BATCH, DEPTH, SEQ, A_KV_HEADS, HEAD_DIM),
            new_ck.reshape(BATCH, DEPTH, SEQ, C_HEADS, HEAD_DIM),
            new_cv.reshape(BATCH, DEPTH, SEQ, C_HEADS, HEAD_DIM))
```

```python
import functools

import numpy as np
import jax
import jax.numpy as jnp
from jax import lax
from jax.experimental import pallas as pl
from jax.experimental.pallas import tpu as pltpu

F32 = jnp.float32
BF16 = jnp.bfloat16

D_MODEL = 1024
BATCH = 16
SEQ = 256
DEPTH = 2
DEC_BATCH = 2
DEC_SEQ = 2048
PAST_LEN = 512
GRID_W = 64
HEAD_DIM = 64
SCALE = HEAD_DIM ** -0.5
A_HEADS = 8
A_KV_HEADS = 2
A_GROUP = A_HEADS // A_KV_HEADS
A_WINDOW = 128
A_BLOCK = 128
B_GROUPS = 8
B_GROUP_DIM = 64
B_WIDTH = B_GROUPS * B_GROUP_DIM
C_HEADS = 8
C_WIN_ROWS = 8
C_WIN_COLS = 16
A_Q = A_HEADS * HEAD_DIM
A_KV = A_KV_HEADS * HEAD_DIM
C_W = C_HEADS * HEAD_DIM
QKV_WIDTH = A_Q + 2 * A_KV + B_WIDTH + 3 * C_W
N_BRANCH = 3
GATE_WIDTH = N_BRANCH * D_MODEL
IN_WIDTH = QKV_WIDTH + GATE_WIDTH
D_FF = 2816
N_EXPERTS = 8
D_FF_EXPERT = 3584
ROPE_BASE = 10000.0
RMS_EPS = 1e-6
NEG_INF = -1e30

N_PROMPT = BATCH * SEQ
N_SAMPLE = DEC_BATCH * DEC_SEQ
N_TOK = N_PROMPT + N_SAMPLE
N_COND = 1 + DEC_BATCH
LANES = 128
NORM_SLAB = 256
C_QROWS = 4
C_QBLOCK = C_QROWS * GRID_W
C_DR_SLOTS = 2 * C_WIN_ROWS
VMEM_LIMIT = 56 * 1024 * 1024


def _cparams(sem):
    return pltpu.CompilerParams(dimension_semantics=sem, vmem_limit_bytes=VMEM_LIMIT)


def _sigmoid(x):
    return 1.0 / (1.0 + jnp.exp(-x))


def _cond_row(tile, tm):
    return jnp.maximum(tile * tm // DEC_SEQ - 1, 0)


def _mod_kernel(ct_ref, w_ref, b_ref, o_ref, silu_ref):
    @pl.when((pl.program_id(0) == 0) & (pl.program_id(1) == 0))
    def _():
        cb = ct_ref[...]
        silu_ref[...] = cb * _sigmoid(cb)

    tn = w_ref.shape[1]
    for r in range(N_COND):
        s = silu_ref[r]
        for cc in range(tn // LANES):
            sl = slice(cc * LANES, (cc + 1) * LANES)
            o_ref[r, :, sl] = jnp.sum(w_ref[:, sl] * s, axis=0, keepdims=True) + b_ref[:, sl]


def _modulation(cond_t, w_mod, b_mod):
    tn = 1024
    n = 6 * D_MODEL
    return pl.pallas_call(
        _mod_kernel,
        grid=(DEPTH, n // tn),
        in_specs=[
            pl.BlockSpec((N_COND, D_MODEL, LANES), lambda l, j: (0, 0, 0)),
            pl.BlockSpec((None, D_MODEL, tn), lambda l, j: (l, 0, j)),
            pl.BlockSpec((None, 1, tn), lambda l, j: (l, 0, j)),
        ],
        out_specs=pl.BlockSpec((None, N_COND, 1, tn), lambda l, j: (l, 0, 0, j)),
        out_shape=jax.ShapeDtypeStruct((DEPTH, N_COND, 1, n), F32),
        scratch_shapes=[pltpu.VMEM((N_COND, D_MODEL, LANES), F32)],
        compiler_params=_cparams(("arbitrary", "arbitrary")),
        name="modulation",
    )(cond_t, w_mod, b_mod.reshape(DEPTH, 1, n))


def _adaln_math(x, g, sh, sc):
    ms = jnp.mean(x * x, axis=-1, keepdims=True)
    y = x * lax.rsqrt(ms + RMS_EPS) * g
    return y * (1.0 + sc) + sh


def _split_rows(tm, width, joined=False):
    n_p = N_PROMPT // tm
    latent0 = n_p if joined else 0
    return (pl.BlockSpec((tm, width), lambda i: (jnp.minimum(i, n_p - 1), 0)),
            pl.BlockSpec((tm, width), lambda i: (jnp.maximum(i - n_p, 0) + latent0, 0)))


def _split_operands(x):
    return (x, False) if isinstance(x, tuple) else ((x, x), True)


def _read_split(p_ref, s_ref):
    is_ctx = pl.program_id(0) < N_PROMPT // p_ref.shape[0]
    return jnp.where(is_ctx, p_ref[...], s_ref[...])


WEIGHT_ROW_CHUNK = 128


def _load_weight_bf16(src_rows, dst_ref, stage_ref, sem):
    n = dst_ref.shape[0] // WEIGHT_ROW_CHUNK

    def chunk_copy(k):
        return pltpu.make_async_copy(src_rows(k), stage_ref.at[k % 2], sem.at[k % 2])

    chunk_copy(0).start()
    for k in range(n):
        if k + 1 < n:
            chunk_copy(k + 1).start()
        chunk_copy(k).wait()
        dst_ref[k * WEIGHT_ROW_CHUNK:(k + 1) * WEIGHT_ROW_CHUNK, :] = stage_ref[k % 2].astype(BF16)


def _row_chunk(k):
    return pl.ds(k * WEIGHT_ROW_CHUNK, WEIGHT_ROW_CHUNK)


def _head_norm(x, gain, bd):
    sq = x * x
    hi = sq.astype(BF16)
    lo = (sq - hi.astype(F32)).astype(BF16)
    ms = jnp.dot(hi, bd, preferred_element_type=F32) + jnp.dot(lo, bd, preferred_element_type=F32)
    return x * lax.rsqrt(ms + RMS_EPS) * gain


def _rope(x, cos, sin_signed, first_half):
    half = HEAD_DIM // 2
    swapped = jnp.where(first_half, pltpu.roll(x, x.shape[1] - half, 1), pltpu.roll(x, half, 1))
    return x * cos + swapped * sin_signed


def _in_proj_kernel(*refs, l, fused_norm):
    n_lead = 5 if fused_norm else 1
    (cos_ref, sin_ref, ga_ref, gc_ref, w_hbm, qa_ref, ka_ref, va_ref, fb_ref, qc_ref, kc_ref, vc_ref, gt_ref,
     wb_ref, stage_ref, sem) = refs[n_lead:]

    @pl.when(pl.program_id(0) == 0)
    def _():
        _load_weight_bf16(lambda k: w_hbm.at[l, _row_chunk(k)], wb_ref, stage_ref, sem)

    if fused_norm:
        xp_ref, xs_ref, ng_ref, sh_ref, sc_ref = refs[:n_lead]
        h = _adaln_math(_read_split(xp_ref, xs_ref), ng_ref[...], sh_ref[...], sc_ref[...]).astype(BF16)
    else:
        h = refs[0][...]

    def proj(off, width):
        return jnp.dot(h, wb_ref[:, off:off + width], preferred_element_type=F32)

    r = lax.broadcasted_iota(jnp.int32, (NORM_SLAB, NORM_SLAB), 0) // HEAD_DIM
    c = lax.broadcasted_iota(jnp.int32, (NORM_SLAB, NORM_SLAB), 1) // HEAD_DIM
    bd = jnp.where(r == c, 1.0 / HEAD_DIM, 0.0).astype(BF16)
    lane = lax.broadcasted_iota(jnp.int32, (1, NORM_SLAB), 1)
    first_half = (lane % HEAD_DIM) < HEAD_DIM // 2
    cos = jnp.concatenate([cos_ref[...]] * (NORM_SLAB // LANES), axis=1)
    sin = jnp.concatenate([sin_ref[...]] * (NORM_SLAB // LANES), axis=1)
    gqa, gka = ga_ref[0:1, :], ga_ref[1:2, :]
    gqc, gkc = gc_ref[0:1, :], gc_ref[1:2, :]
    slabs = lambda p: [p[:, s * NORM_SLAB:(s + 1) * NORM_SLAB] for s in range(p.shape[1] // NORM_SLAB)]
    cat = lambda parts: jnp.concatenate(parts, axis=1)

    off = 0
    qa_ref[...] = cat([_rope(_head_norm(x, gqa, bd), cos, sin, first_half) for x in slabs(proj(off, A_Q))]
                      ).astype(BF16)
    off += A_Q
    kv = proj(off, 2 * A_KV)
    ka_ref[...] = _rope(_head_norm(kv, gka, bd), cos, sin, first_half)[:, :A_KV]
    va_ref[...] = kv[:, A_KV:]
    off += 2 * A_KV
    fb_ref[...] = proj(off, B_WIDTH).astype(BF16)
    off += B_WIDTH
    qc_ref[...] = cat([_head_norm(x, gqc, bd) for x in slabs(proj(off, C_W))]).astype(BF16)
    off += C_W
    kc_ref[...] = cat([_head_norm(x, gkc, bd) for x in slabs(proj(off, C_W))])
    off += C_W
    vc_ref[...] = proj(off, C_W)
    off += C_W
    for j in range(N_BRANCH):
        cols = slice(j * D_MODEL, (j + 1) * D_MODEL)
        gt_ref[:, cols] = _sigmoid(proj(off + j * D_MODEL, D_MODEL)).astype(BF16)


def _in_proj(h, w_in, cos_t, sin_t, qk_a, qk_c, l, norm=None):
    tm = 512
    row = lambda w: pl.BlockSpec((tm, w), lambda i: (i, 0))
    widths = (A_Q, A_KV, A_KV, B_WIDTH, C_W, C_W, C_W, GATE_WIDTH)
    dtypes = (BF16, F32, F32, BF16, BF16, F32, F32, BF16)
    if norm is None:
        lead_specs, lead = [row(D_MODEL)], [h]
    else:
        norm_g, mod = norm
        x, joined = _split_operands(h)
        mspec = lambda which: pl.BlockSpec((None, None, 1, D_MODEL), lambda i: (l, _cond_row(i, tm), 0, which))
        lead_specs = [*_split_rows(tm, D_MODEL, joined), pl.BlockSpec((1, D_MODEL), lambda i: (0, 0)),
                      mspec(0), mspec(1)]
        lead = [*x, norm_g.reshape(1, D_MODEL), mod, mod]
    return pl.pallas_call(
        functools.partial(_in_proj_kernel, l=l, fused_norm=norm is not None), grid=(N_TOK // tm,),
        in_specs=[*lead_specs, row(LANES), row(LANES),
                  pl.BlockSpec((2, NORM_SLAB), lambda i: (0, 0)), pl.BlockSpec((2, NORM_SLAB), lambda i: (0, 0)),
                  pl.BlockSpec(memory_space=pl.ANY)],
        out_specs=[row(w) for w in widths],
        out_shape=[jax.ShapeDtypeStruct((N_TOK, w), d) for w, d in zip(widths, dtypes)],
        scratch_shapes=[pltpu.VMEM((D_MODEL, IN_WIDTH), BF16),
                        pltpu.VMEM((2, WEIGHT_ROW_CHUNK, IN_WIDTH), F32), pltpu.SemaphoreType.DMA((2,))],
        compiler_params=_cparams(("arbitrary",)), name="in_proj",
    )(*lead, cos_t, sin_t, jnp.tile(qk_a, (1, NORM_SLAB // HEAD_DIM)), jnp.tile(qk_c, (1, NORM_SLAB // HEAD_DIM)), w_in)


def _nt_dot(a, b):
    return lax.dot_general(a, b, (((1,), (1,)), ((), ())), preferred_element_type=F32)


def _head(x, h):
    return x[:, h * HEAD_DIM:(h + 1) * HEAD_DIM]


def _stacked_softmax(parts, sink):
    m = parts[0].max(axis=-1, keepdims=True)
    for s in parts[1:]:
        m = jnp.maximum(m, s.max(axis=-1, keepdims=True))
    if sink is not None:
        m = jnp.maximum(m, sink)
    den = jnp.exp(sink - m) if sink is not None else 0.0
    es = []
    for s in parts:
        e = jnp.exp(s - m)
        den = den + e.sum(axis=-1, keepdims=True)
        es.append(e.astype(BF16))
    return es, 1.0 / den


def _sink_column(sink_ref, l, rows_per_head):
    return jnp.concatenate([jnp.full((rows_per_head, 1), sink_ref[l, h], F32) for h in range(A_HEADS)], axis=0)


def _gqa_queries(qa, g):
    return jnp.concatenate([_head(qa, g * A_GROUP + i) for i in range(A_GROUP)], axis=0)


def _ctx_attn_kernel(sink_ref, qa_ref, ka_ref, va_ref, qc_ref, kc_ref, vc_ref, oa_ref, oc_ref, *, l):
    t = SEQ
    qa = qa_ref[...] * SCALE
    ka = ka_ref[...].astype(BF16)
    va = va_ref[...].astype(BF16)
    s = jnp.concatenate([_nt_dot(_gqa_queries(qa, g), _head(ka, g)) for g in range(A_KV_HEADS)], axis=0)
    (e,), inv = _stacked_softmax([s], _sink_column(sink_ref, l, t))
    outs = []
    for g in range(A_KV_HEADS):
        rows = slice(g * A_GROUP * t, (g + 1) * A_GROUP * t)
        o = jnp.dot(e[rows], _head(va, g), preferred_element_type=F32) * inv[rows]
        outs += [o[i * t:(i + 1) * t] for i in range(A_GROUP)]
    oa_ref[...] = jnp.concatenate(outs, axis=1).astype(BF16)

    qc = qc_ref[...] * SCALE
    kc = kc_ref[...].astype(BF16)
    vc = vc_ref[...].astype(BF16)
    s = jnp.concatenate([_nt_dot(_head(qc, h), _head(kc, h)) for h in range(C_HEADS)], axis=0)
    (e,), inv = _stacked_softmax([s], None)
    outs = [jnp.dot(e[h * t:(h + 1) * t], _head(vc, h), preferred_element_type=F32) * inv[h * t:(h + 1) * t]
            for h in range(C_HEADS)]
    oc_ref[...] = jnp.concatenate(outs, axis=1).astype(BF16)


def _ctx_attn(sink_a, qa, ka, va, qc, kc, vc, l):
    blk = lambda w: pl.BlockSpec((SEQ, w), lambda b: (b, 0))
    return pl.pallas_call(
        functools.partial(_ctx_attn_kernel, l=l), grid=(BATCH,),
        in_specs=[pl.BlockSpec(memory_space=pltpu.SMEM),
                  blk(A_Q), blk(A_KV), blk(A_KV), blk(C_W), blk(C_W), blk(C_W)],
        out_specs=[blk(A_Q), blk(C_W)],
        out_shape=[jax.ShapeDtypeStruct((N_PROMPT, A_Q), BF16), jax.ShapeDtypeStruct((N_PROMPT, C_W), BF16)],
        compiler_params=_cparams(("parallel",)), name="ctx_attn",
    )(sink_a, qa, ka, va, qc, kc, vc)


def _win_attn_kernel(sink_ref, q_ref, kp_ref, kc_ref, kn_ref, vp_ref, vc_ref, vn_ref, ck_ref, cv_ref, mask_ref,
                     o_ref, *, l):
    rows = A_GROUP * A_BLOCK
    mask = mask_ref[...]
    sink = _sink_column(sink_ref, l, A_BLOCK)
    for g in range(A_KV_HEADS):
        sl = slice(g * HEAD_DIM, (g + 1) * HEAD_DIM)
        q = jnp.concatenate([q_ref[:, (g * A_GROUP + i) * HEAD_DIM:(g * A_GROUP + i + 1) * HEAD_DIM]
                             for i in range(A_GROUP)], axis=0) * SCALE
        k_loc = jnp.concatenate([kp_ref[:, sl], kc_ref[:, sl], kn_ref[:, sl]], axis=0).astype(BF16)
        v_loc = jnp.concatenate([vp_ref[:, sl], vc_ref[:, sl], vn_ref[:, sl]], axis=0).astype(BF16)
        s_loc = _nt_dot(q, k_loc) + mask
        s_ctx = _nt_dot(q, ck_ref[:, sl].astype(BF16))
        (e_loc, e_ctx), inv = _stacked_softmax([s_loc, s_ctx], sink[g * rows:(g + 1) * rows])
        o = (jnp.dot(e_loc, v_loc, preferred_element_type=F32)
             + jnp.dot(e_ctx, cv_ref[:, sl].astype(BF16), preferred_element_type=F32)) * inv
        for i in range(A_GROUP):
            h = g * A_GROUP + i
            o_ref[:, h * HEAD_DIM:(h + 1) * HEAD_DIM] = o[i * A_BLOCK:(i + 1) * A_BLOCK].astype(BF16)


def _win_attn(sink_a, qa, ka, va, cache_k, cache_v, l):
    nb = DEC_SEQ // A_BLOCK
    base = N_PROMPT // A_BLOCK

    def nbr(d):
        return lambda b, t: (base + b * nb + jnp.clip(t + d, 0, nb - 1), 0)

    kv = lambda d: pl.BlockSpec((A_BLOCK, A_KV), nbr(d))
    cache = pl.BlockSpec((None, None, PAST_LEN, A_KV), lambda b, t: (b, l, 0, 0))
    qi = np.arange(A_GROUP * A_BLOCK)[:, None] % A_BLOCK
    kj = np.arange(3 * A_BLOCK)[None, :] - A_BLOCK
    band = np.abs(kj - qi) <= A_WINDOW
    masks = np.stack([band & (kj >= 0), band, band & (kj < A_BLOCK)])
    masks = jnp.asarray(np.where(masks, 0.0, NEG_INF).astype(np.float32))
    position = lambda b, t: (jnp.where(t == 0, 0, jnp.where(t == nb - 1, 2, 1)), 0, 0)
    return pl.pallas_call(
        functools.partial(_win_attn_kernel, l=l), grid=(DEC_BATCH, nb),
        in_specs=[pl.BlockSpec(memory_space=pltpu.SMEM),
                  pl.BlockSpec((A_BLOCK, A_Q), nbr(0)),
                  kv(-1), kv(0), kv(1), kv(-1), kv(0), kv(1), cache, cache,
                  pl.BlockSpec((None, A_GROUP * A_BLOCK, 3 * A_BLOCK), position)],
        out_specs=pl.BlockSpec((A_BLOCK, A_Q), lambda b, t: (b * nb + t, 0)),
        out_shape=jax.ShapeDtypeStruct((N_SAMPLE, A_Q), BF16),
        compiler_params=_cparams(("parallel", "parallel")), name="win_attn",
    )(sink_a, qa, ka, ka, ka, va, va, va, cache_k, cache_v, masks)


def _nbr_attn_kernel(q_ref, kp_ref, kc_ref, kn_ref, vp_ref, vc_ref, vn_ref, ck_ref, cv_ref, tab_ref, o_ref,
                     bias_ref):
    j = pl.program_id(0)
    nb = pl.num_programs(0)
    slots = _nbr_row_slots()

    def build(cls):
        for h in range(C_HEADS):
            for qr in range(C_QROWS):
                for kk in range(3 * C_QROWS):
                    bias_ref[h, qr * GRID_W:(qr + 1) * GRID_W, kk * GRID_W:(kk + 1) * GRID_W] = (
                        tab_ref[h, slots[cls][qr][kk]])

    first_of_batch = pl.program_id(1) == 0
    for cls, at in enumerate((0, 1, nb - 1)):
        @pl.when(first_of_batch & (j == at))
        def _():
            build(cls)

    tq = C_QBLOCK
    q = q_ref[...] * SCALE
    k_loc = jnp.concatenate([kp_ref[...], kc_ref[...], kn_ref[...]], axis=0).astype(BF16)
    v_loc = jnp.concatenate([vp_ref[...], vc_ref[...], vn_ref[...]], axis=0).astype(BF16)
    k_ctx = ck_ref[...].astype(BF16)
    v_ctx = cv_ref[...].astype(BF16)
    s_loc = jnp.concatenate([_nt_dot(_head(q, h), _head(k_loc, h)) for h in range(C_HEADS)], axis=0)
    s_loc = s_loc + bias_ref[...].reshape(C_HEADS * tq, 3 * tq)
    s_ctx = jnp.concatenate([_nt_dot(_head(q, h), _head(k_ctx, h)) for h in range(C_HEADS)], axis=0)
    (e_loc, e_ctx), inv = _stacked_softmax([s_loc, s_ctx], None)
    outs = []
    for h in range(C_HEADS):
        r = slice(h * tq, (h + 1) * tq)
        outs.append((jnp.dot(e_loc[r], _head(v_loc, h), preferred_element_type=F32)
                     + jnp.dot(e_ctx[r], _head(v_ctx, h), preferred_element_type=F32)) * inv[r])
    o_ref[...] = jnp.concatenate(outs, axis=1).astype(BF16)


def _nbr_attn(qc, kc, vc, cache_k, cache_v, bias, l):
    nb = DEC_SEQ // C_QBLOCK
    base = N_PROMPT // C_QBLOCK

    def nbr(d):
        return lambda j, b: (base + b * nb + jnp.clip(j + d, 0, nb - 1), 0)

    kv = lambda d: pl.BlockSpec((C_QBLOCK, C_W), nbr(d))
    cache = pl.BlockSpec((None, None, PAST_LEN, C_W), lambda j, b: (b, l, 0, 0))
    return pl.pallas_call(
        _nbr_attn_kernel, grid=(nb, DEC_BATCH),
        in_specs=[kv(0), kv(-1), kv(0), kv(1), kv(-1), kv(0), kv(1), cache, cache,
                  pl.BlockSpec((None, C_HEADS, C_DR_SLOTS, GRID_W, GRID_W), lambda j, b: (l, 0, 0, 0, 0))],
        out_specs=pl.BlockSpec((C_QBLOCK, C_W), lambda j, b: (b * nb + j, 0)),
        out_shape=jax.ShapeDtypeStruct((N_SAMPLE, C_W), BF16),
        scratch_shapes=[pltpu.VMEM((C_HEADS, C_QBLOCK, 3 * C_QBLOCK), F32)],
        compiler_params=_cparams(("arbitrary", "arbitrary")), name="nbr_attn",
    )(qc, kc, kc, kc, vc, vc, vc, cache_k, cache_v, bias)


def _nbr_bias_tables(rpb):
    qcol = np.arange(GRID_W)
    qcs = np.clip(qcol - C_WIN_COLS // 2, 0, GRID_W - C_WIN_COLS)
    kcol = np.arange(GRID_W)
    col_ok = (kcol[None, :] >= qcs[:, None]) & (kcol[None, :] < qcs[:, None] + C_WIN_COLS)
    dc = np.clip(kcol[None, :] - qcol[:, None], -(C_WIN_COLS - 1), C_WIN_COLS - 1) + C_WIN_COLS - 1
    onehot_dc = (dc.reshape(-1)[None, :] == np.arange(2 * C_WIN_COLS - 1)[:, None]).astype(np.float32)
    t = jnp.einsum('lhab,bx->lhax', rpb, jnp.asarray(onehot_dc), precision=lax.Precision.HIGHEST)
    t = jnp.where(jnp.asarray(col_ok.reshape(-1)), t, NEG_INF)
    t = jnp.concatenate([t, jnp.full((DEPTH, C_HEADS, 1, GRID_W * GRID_W), NEG_INF, F32)], axis=2)
    return t.reshape(DEPTH, C_HEADS, C_DR_SLOTS, GRID_W, GRID_W)


def _nbr_row_slots():
    rows = DEC_SEQ // GRID_W
    slots = []
    for j in (0, 3, rows // C_QROWS - 1):
        per_q = []
        for qr in range(C_QROWS):
            r = C_QROWS * j + qr
            rs = min(max(r - C_WIN_ROWS // 2, 0), rows - C_WIN_ROWS)
            per_k = []
            for kk in range(3 * C_QROWS):
                kabs = C_QROWS * (j - 1) + kk
                per_k.append(kabs - r + C_WIN_ROWS - 1 if rs <= kabs < rs + C_WIN_ROWS else C_DR_SLOTS - 1)
            per_q.append(per_k)
        slots.append(per_q)
    return slots


def _fourier_kernel(u_ref, bc_ref, bs_ref, cl_ref, sl_ref, o_ref, zc_ref, zs_ref):
    @pl.when(pl.program_id(1) == 0)
    def _():
        u = u_ref[...]
        zc_ref[...] = jnp.dot(u, bc_ref[...].astype(BF16), preferred_element_type=F32).astype(BF16)
        zs_ref[...] = jnp.dot(u, bs_ref[...].astype(BF16), preferred_element_type=F32).astype(BF16)

    o = (jnp.dot(cl_ref[...].astype(BF16), zc_ref[...], preferred_element_type=F32)
         - jnp.dot(sl_ref[...].astype(BF16), zs_ref[...], preferred_element_type=F32))
    o_ref[...] = o.astype(BF16)


def _dft_tables(n):
    k = np.arange(n)
    ang = 2.0 * np.pi * ((k[:, None] * k[None, :]) % n) / n
    return np.cos(ang) / np.sqrt(n), np.sin(ang) / np.sqrt(n)


def _channel_dft_tables():
    c, s = _dft_tables(B_GROUP_DIM)
    eye = np.eye(B_GROUPS)
    return np.kron(eye, c).astype(np.float32), np.kron(eye, s).astype(np.float32)


def _fourier(fb, n_batch, seq, row0, tr):
    cl, sl = (jnp.asarray(a.astype(np.float32)) for a in _dft_tables(seq))
    bc, bs = (jnp.asarray(a) for a in _channel_dft_tables())
    nt = seq // tr
    const = pl.BlockSpec((B_WIDTH, B_WIDTH), lambda b, t: (0, 0))
    return pl.pallas_call(
        _fourier_kernel, grid=(n_batch, nt),
        in_specs=[pl.BlockSpec((seq, B_WIDTH), lambda b, t: (row0 // seq + b, 0)), const, const,
                  pl.BlockSpec((tr, seq), lambda b, t: (t, 0)), pl.BlockSpec((tr, seq), lambda b, t: (t, 0))],
        out_specs=pl.BlockSpec((tr, B_WIDTH), lambda b, t: (b * nt + t, 0)),
        out_shape=jax.ShapeDtypeStruct((n_batch * seq, B_WIDTH), BF16),
        scratch_shapes=[pltpu.VMEM((seq, B_WIDTH), BF16), pltpu.VMEM((seq, B_WIDTH), BF16)],
        compiler_params=_cparams(("parallel", "arbitrary")), name=f"fourier_{seq}",
    )(fb, bc, bs, cl, sl)


def _merge_kernel(oap_ref, obp_ref, ocp_ref, oas_ref, obs_ref, ocs_ref, gt_ref, xp_ref, xs_ref,
                  wa_ref, wb_ref, wc_ref, wo_ref, g1_ref, ng_ref, sh_ref, sc_ref, *rest, router):
    if router:
        wr_ref, o_ref, h2_ref, lg_ref, wab_ref, wbb_ref, wcb_ref, wob_ref = rest
    else:
        o_ref, h2_ref, wab_ref, wbb_ref, wcb_ref, wob_ref = rest

    @pl.when(pl.program_id(0) == 0)
    def _():
        wab_ref[...] = wa_ref[...].astype(BF16)
        wbb_ref[...] = wb_ref[...].astype(BF16)
        wcb_ref[...] = wc_ref[...].astype(BF16)
        wob_ref[...] = wo_ref[...].astype(BF16)

    ya = jnp.dot(_read_split(oap_ref, oas_ref), wab_ref[...], preferred_element_type=F32)
    yb = jnp.dot(_read_split(obp_ref, obs_ref), wbb_ref[...], preferred_element_type=F32)
    yc = jnp.dot(_read_split(ocp_ref, ocs_ref), wcb_ref[...], preferred_element_type=F32)
    d = D_MODEL
    m = (gt_ref[:, 0:d].astype(F32) * ya + gt_ref[:, d:2 * d].astype(F32) * yb
         + gt_ref[:, 2 * d:3 * d].astype(F32) * yc)
    y = jnp.dot(m.astype(BF16), wob_ref[...], preferred_element_type=F32)
    x_new = _read_split(xp_ref, xs_ref) + g1_ref[...] * y
    o_ref[...] = x_new
    h2 = _adaln_math(x_new, ng_ref[...], sh_ref[...], sc_ref[...])
    h2_ref[...] = h2.astype(h2_ref.dtype)
    if router:
        w = wr_ref[...]
        w_hi = w.astype(BF16)
        w_lo = (w - w_hi.astype(F32)).astype(BF16)
        h_hi = h2.astype(BF16)
        h_lo = (h2 - h_hi.astype(F32)).astype(BF16)
        hi_terms = jnp.dot(h_hi, jnp.concatenate([w_hi, w_lo], axis=1), preferred_element_type=F32)
        lg_ref[...] = (hi_terms[:, :LANES] + jnp.dot(h_lo, w_hi, preferred_element_type=F32)
                       + hi_terms[:, LANES:])


def _merge(branches_p, branches_s, gates, x, wa, wb, wc, wo, mod, l, norm_g, w_router=None):
    tm = 512
    router = w_router is not None
    x, joined = _split_operands(x)
    row = lambda w: pl.BlockSpec((tm, w), lambda i: (i, 0))
    row_p = lambda w: _split_rows(tm, w)[0]
    row_s = lambda w: _split_rows(tm, w)[1]
    const = lambda r, c: pl.BlockSpec((None, r, c), lambda i: (l, 0, 0), pipeline_mode=pl.Buffered(1))
    mspec = lambda which: pl.BlockSpec((None, None, 1, D_MODEL), lambda i: (l, _cond_row(i, tm), 0, which))
    in_specs = [row_p(A_Q), row_p(B_WIDTH), row_p(C_W), row_s(A_Q), row_s(B_WIDTH), row_s(C_W),
                row(GATE_WIDTH), *_split_rows(tm, D_MODEL, joined),
                const(A_Q, D_MODEL), const(B_WIDTH, D_MODEL), const(C_W, D_MODEL), const(D_MODEL, D_MODEL),
                mspec(2), pl.BlockSpec((1, D_MODEL), lambda i: (0, 0)), mspec(3), mspec(4)]
    operands = [*branches_p, *branches_s, gates, *x, wa, wb, wc, wo, mod, norm_g.reshape(1, D_MODEL), mod, mod]
    out_specs = [row(D_MODEL), row(D_MODEL)]
    out_shape = [jax.ShapeDtypeStruct((N_TOK, D_MODEL), F32),
                 jax.ShapeDtypeStruct((N_TOK, D_MODEL), F32 if router else BF16)]
    if router:
        in_specs.append(pl.BlockSpec((D_MODEL, LANES), lambda i: (0, 0)))
        operands.append(jnp.pad(w_router, ((0, 0), (0, LANES - N_EXPERTS))))
        out_specs.append(row(LANES))
        out_shape.append(jax.ShapeDtypeStruct((N_TOK, LANES), F32))
    return pl.pallas_call(
        functools.partial(_merge_kernel, router=router), grid=(N_TOK // tm,),
        in_specs=in_specs, out_specs=out_specs, out_shape=out_shape,
        scratch_shapes=[pltpu.VMEM((A_Q, D_MODEL), BF16), pltpu.VMEM((B_WIDTH, D_MODEL), BF16),
                        pltpu.VMEM((C_W, D_MODEL), BF16), pltpu.VMEM((D_MODEL, D_MODEL), BF16)],
        compiler_params=_cparams(("arbitrary",)), name="merge_router" if router else "merge",
    )(*operands)


FFN_COL_CHUNK = D_FF // 2


def _ffn_kernel(h_ref, x_ref, g2_ref, ng_ref, sh_ref, sc_ref, wg_hbm, wu_hbm, wd_hbm, o_ref, hn_ref,
                wgb_ref, wub_ref, wdb_ref, stage_up_ref, stage_dn_ref, sem, *, i_dense):
    @pl.when(pl.program_id(0) == 0)
    def _():
        _load_weight_bf16(lambda k: wg_hbm.at[i_dense, _row_chunk(k)], wgb_ref, stage_up_ref, sem)
        _load_weight_bf16(lambda k: wu_hbm.at[i_dense, _row_chunk(k)], wub_ref, stage_up_ref, sem)
        _load_weight_bf16(lambda k: wd_hbm.at[i_dense, _row_chunk(k)], wdb_ref, stage_dn_ref, sem)

    h = h_ref[...]
    acc = None
    for c in range(D_FF // FFN_COL_CHUNK):
        cols = slice(c * FFN_COL_CHUNK, (c + 1) * FFN_COL_CHUNK)
        g = jnp.dot(h, wgb_ref[:, cols], preferred_element_type=F32)
        u = jnp.dot(h, wub_ref[:, cols], preferred_element_type=F32)
        a = (g * _sigmoid(g) * u).astype(BF16)
        d = jnp.dot(a, wdb_ref[cols, :], preferred_element_type=F32)
        acc = d if acc is None else acc + d
    x_new = x_ref[...] + g2_ref[...] * acc
    o_ref[...] = x_new
    hn_ref[...] = _adaln_math(x_new, ng_ref[...], sh_ref[...], sc_ref[...]).astype(BF16)


def _ffn(h, x, wg, wu, wd, mod, l, next_norm_g):
    tm = 512
    row = lambda dt: pl.BlockSpec((tm, D_MODEL), lambda i: (i, 0))
    mspec = lambda layer, which: pl.BlockSpec((None, None, 1, D_MODEL),
                                              lambda i: (layer, _cond_row(i, tm), 0, which))
    hbm = pl.BlockSpec(memory_space=pl.ANY)
    return pl.pallas_call(
        functools.partial(_ffn_kernel, i_dense=l // 2), grid=(N_TOK // tm,),
        in_specs=[row(BF16), row(F32), mspec(l, 5),
                  pl.BlockSpec((1, D_MODEL), lambda i: (0, 0)), mspec(l + 1, 0), mspec(l + 1, 1),
                  hbm, hbm, hbm],
        out_specs=[row(F32), row(BF16)],
        out_shape=[jax.ShapeDtypeStruct((N_TOK, D_MODEL), F32), jax.ShapeDtypeStruct((N_TOK, D_MODEL), BF16)],
        scratch_shapes=[pltpu.VMEM((D_MODEL, D_FF), BF16), pltpu.VMEM((D_MODEL, D_FF), BF16),
                        pltpu.VMEM((D_FF, D_MODEL), BF16),
                        pltpu.VMEM((2, WEIGHT_ROW_CHUNK, D_FF), F32),
                        pltpu.VMEM((2, WEIGHT_ROW_CHUNK, D_MODEL), F32), pltpu.SemaphoreType.DMA((2,))],
        compiler_params=_cparams(("arbitrary",)), name="ffn",
    )(h, x, mod, next_norm_g.reshape(1, D_MODEL), mod, mod, wg, wu, wd)


TOP_K = 2
MOE_TILE = 256
MOE_TILES = TOP_K * N_TOK // MOE_TILE + N_EXPERTS
MOE_ROWS = MOE_TILES * MOE_TILE
MOE_CHUNK = 10
MOE_CHUNKS = MOE_TILES // MOE_CHUNK + N_EXPERTS
MOE_TF = 896
ROUTE_TM = 512
DISPATCH_TM = 512
COMBINE_TM = 512
ROW_COPY_UNROLL = 16


def _route_kernel(lg_ref, o_ref, cnt_ref, base_ref, tri_ref):
    tm = lg_ref.shape[0]

    @pl.when(pl.program_id(0) == 0)
    def _():
        base_ref[...] = jnp.zeros_like(base_ref)
        r = lax.broadcasted_iota(jnp.int32, (tm, tm), 0)
        c = lax.broadcasted_iota(jnp.int32, (tm, tm), 1)
        tri_ref[...] = jnp.where(r > c, 1.0, 0.0).astype(BF16)

    lane = lax.broadcasted_iota(jnp.int32, lg_ref.shape, 1).astype(F32)
    lg = jnp.where(lane < N_EXPERTS, lg_ref[...], -jnp.inf)
    m1 = lg.max(axis=-1, keepdims=True)
    i1 = jnp.where(lg == m1, lane, float(LANES)).min(axis=-1, keepdims=True)
    rest = jnp.where(lane == i1, -jnp.inf, lg)
    m2 = rest.max(axis=-1, keepdims=True)
    i2 = jnp.where(rest == m2, lane, float(LANES)).min(axis=-1, keepdims=True)
    e2 = jnp.exp(m2 - m1)
    w1 = 1.0 / (1.0 + e2)
    w2 = e2 / (1.0 + e2)

    oh1 = jnp.where(lane == i1, 1.0, 0.0)
    oh2 = jnp.where(lane == i2, 1.0, 0.0)
    pre1 = jnp.dot(tri_ref[...], oh1.astype(BF16), preferred_element_type=F32)
    pre2 = jnp.dot(tri_ref[...], oh2.astype(BF16), preferred_element_type=F32)
    c1 = jnp.sum(oh1, axis=0, keepdims=True)
    c2 = jnp.sum(oh2, axis=0, keepdims=True)
    base = base_ref[...]
    rank1 = jnp.sum(oh1 * (base + pre1), axis=-1, keepdims=True)
    rank2 = jnp.sum(oh2 * (base + c1 + pre2), axis=-1, keepdims=True)
    base_ref[...] = base + c1 + c2

    cols = (i1, i2, rank1, rank2, w1, w2)
    out = jnp.zeros(lg_ref.shape, F32)
    for j, col in enumerate(cols):
        out = jnp.where(lane == float(j), col, out)
    o_ref[...] = out
    cnt_ref[...] = jnp.broadcast_to(base + c1 + c2, cnt_ref.shape)


def _dispatch_kernel(pos_ref, last_ref, nt_ref, h_ref, xs_hbm, zero_ref, stage_ref, zsem, sem):
    i = pl.program_id(0)
    tm = h_ref.shape[0]

    @pl.when(i == 0)
    def _():
        zero_ref[...] = jnp.zeros_like(zero_ref)

        def zero_copy(tile):
            row0 = pl.multiple_of(tile * MOE_TILE, MOE_TILE)
            return pltpu.make_async_copy(zero_ref, xs_hbm.at[pl.ds(row0, MOE_TILE)], zsem)

        def for_zeroed_tiles(fn):
            for e in range(N_EXPERTS):
                @pl.when(last_ref[e] >= 0)
                def _():
                    fn(zero_copy(last_ref[e]))

                tail = MOE_TILES - 1 - e

                @pl.when(tail >= nt_ref[0])
                def _():
                    fn(zero_copy(tail))

        for_zeroed_tiles(lambda cp: cp.start())
        for_zeroed_tiles(lambda cp: cp.wait())

    slot = i % 2
    stage_ref[slot] = h_ref[...]

    def row_copy(step, s, t, k):
        dst = xs_hbm.at[pl.ds(pos_ref[TOP_K * (step * tm + t) + k], 1)]
        return pltpu.make_async_copy(stage_ref.at[s, pl.ds(t, 1)], dst, sem.at[s])

    def issue(t, carry):
        for k in range(TOP_K):
            row_copy(i, slot, t, k).start()
        return carry

    def drain_of(step, s):
        def drain(t, carry):
            for k in range(TOP_K):
                row_copy(step, s, t, k).wait()
            return carry
        lax.fori_loop(0, tm, drain, 0, unroll=ROW_COPY_UNROLL)

    lax.fori_loop(0, tm, issue, 0, unroll=ROW_COPY_UNROLL)

    @pl.when(i > 0)
    def _():
        drain_of(i - 1, 1 - slot)

    @pl.when(i == pl.num_programs(0) - 1)
    def _():
        drain_of(i, slot)


def _expert_kernel(ce_ref, ct_ref, cn_ref, nch_ref, nt_ref, xs_hbm, wg_ref, wu_ref, wd_ref, y_hbm,
                   acc_ref, xb_ref, xstage_ref, wgb_ref, wub_ref, wdb_ref, xsem, osem):
    c = pl.program_id(0)
    f = pl.program_id(1)
    last_f = pl.num_programs(1) - 1

    def tile_rows(tile):
        return pl.ds(pl.multiple_of(tile * MOE_TILE, MOE_TILE), MOE_TILE)

    def out_copy(slot, tile):
        return pltpu.make_async_copy(acc_ref.at[slot], y_hbm.at[tile_rows(tile)], osem)

    @pl.when((c == 0) & (f == 0))
    def _():
        acc_ref[0] = jnp.zeros((MOE_TILE, D_MODEL), F32)
        for e in range(N_EXPERTS):
            tail = MOE_TILES - 1 - e

            @pl.when(tail >= nt_ref[0])
            def _():
                cp = out_copy(0, tail)
                cp.start()
                cp.wait()

    @pl.when(c < nch_ref[0])
    def _():
        n = cn_ref[c]
        t0 = ct_ref[c]

        def x_copy(j):
            return pltpu.make_async_copy(xs_hbm.at[tile_rows(t0 + j)], xstage_ref.at[j % 2], xsem.at[j % 2])

        @pl.when(f == 0)
        def _():
            x_copy(0).start()

        wgb_ref[...] = wg_ref[...].astype(BF16)
        wub_ref[...] = wu_ref[...].astype(BF16)
        wdb_ref[...] = wd_ref[...].astype(BF16)

        def tile_step(j, carry):
            @pl.when(f == 0)
            def _():
                @pl.when(j + 1 < n)
                def _():
                    x_copy(j + 1).start()

                x_copy(j).wait()
                xb_ref[j] = xstage_ref[j % 2].astype(BF16)
                acc_ref[j] = jnp.zeros((MOE_TILE, D_MODEL), F32)

            x = xb_ref[j]
            g = jnp.dot(x, wgb_ref[...], preferred_element_type=F32)
            u = jnp.dot(x, wub_ref[...], preferred_element_type=F32)
            a = (g * _sigmoid(g) * u).astype(BF16)
            acc_ref[j] += jnp.dot(a, wdb_ref[...], preferred_element_type=F32)

            @pl.when(f == last_f)
            def _():
                @pl.when(j > 0)
                def _():
                    out_copy(j - 1, t0 + j - 1).wait()

                out_copy(j, t0 + j).start()

            return carry

        lax.fori_loop(0, n, tile_step, 0)

        @pl.when(f == last_f)
        def _():
            out_copy(n - 1, t0 + n - 1).wait()


def _combine_kernel(pos_ref, route_ref, x_ref, g2_ref, y_hbm, op_ref, os_ref, buf_ref, sem):
    i = pl.program_id(0)
    n = pl.num_programs(0)
    tm = x_ref.shape[0]
    slot = i % 2

    def row_copy(step, s, t, k):
        src = y_hbm.at[pl.ds(pos_ref[TOP_K * (step * tm + t) + k], 1)]
        return pltpu.make_async_copy(src, buf_ref.at[s, k, pl.ds(t, 1)], sem.at[s])

    def issue(step, s):
        def body(t, carry):
            for k in range(TOP_K):
                row_copy(step, s, t, k).start()
            return carry
        lax.fori_loop(0, tm, body, 0, unroll=ROW_COPY_UNROLL)

    @pl.when(i == 0)
    def _():
        issue(0, 0)

    @pl.when(i + 1 < n)
    def _():
        issue(i + 1, 1 - slot)

    def wait_body(t, carry):
        for k in range(TOP_K):
            row_copy(i, slot, t, k).wait()
        return carry

    lax.fori_loop(0, tm, wait_body, 0, unroll=ROW_COPY_UNROLL)

    lane = lax.broadcasted_iota(jnp.int32, route_ref.shape, 1)
    rt = route_ref[...]
    w1 = jnp.sum(jnp.where(lane == 2 * TOP_K, rt, 0.0), axis=-1, keepdims=True)
    w2 = jnp.sum(jnp.where(lane == 2 * TOP_K + 1, rt, 0.0), axis=-1, keepdims=True)
    out = x_ref[...] + g2_ref[...] * (w1 * buf_ref[slot, 0] + w2 * buf_ref[slot, 1])

    @pl.when(i < N_PROMPT // tm)
    def _():
        op_ref[...] = out

    @pl.when(i >= N_PROMPT // tm)
    def _():
        os_ref[...] = out


def _moe(h, x, logits, wg, wu, wd, mod, l):
    route, cnt = pl.pallas_call(
        _route_kernel, grid=(N_TOK // ROUTE_TM,),
        in_specs=[pl.BlockSpec((ROUTE_TM, LANES), lambda i: (i, 0))],
        out_specs=[pl.BlockSpec((ROUTE_TM, LANES), lambda i: (i, 0)), pl.BlockSpec((8, LANES), lambda i: (0, 0))],
        out_shape=[jax.ShapeDtypeStruct((N_TOK, LANES), F32), jax.ShapeDtypeStruct((8, LANES), F32)],
        scratch_shapes=[pltpu.VMEM((1, LANES), F32), pltpu.VMEM((ROUTE_TM, ROUTE_TM), BF16)],
        compiler_params=_cparams(("arbitrary",)), name="route",
    )(logits)

    expert = route[:, 0:TOP_K].astype(jnp.int32)
    rank = route[:, TOP_K:2 * TOP_K].astype(jnp.int32)
    n_sub = (cnt[0, :N_EXPERTS].astype(jnp.int32) + MOE_TILE - 1) // MOE_TILE
    end = jnp.cumsum(n_sub)
    start = end - n_sub
    start_of = jnp.sum(jnp.where(expert[:, :, None] == jnp.arange(N_EXPERTS), start, 0), axis=-1)
    pos = (start_of * MOE_TILE + rank).reshape(-1)
    n_tiles = end[N_EXPERTS - 1:]
    last_tile = jnp.where(n_sub > 0, end - 1, -1).astype(jnp.int32)

    xs = pl.pallas_call(
        _dispatch_kernel,
        grid_spec=pltpu.PrefetchScalarGridSpec(
            num_scalar_prefetch=3, grid=(N_TOK // DISPATCH_TM,),
            in_specs=[pl.BlockSpec((DISPATCH_TM, D_MODEL), lambda i, p, lt, nt: (i, 0))],
            out_specs=pl.BlockSpec(memory_space=pl.ANY),
            scratch_shapes=[pltpu.VMEM((MOE_TILE, D_MODEL), F32), pltpu.VMEM((2, DISPATCH_TM, D_MODEL), F32),
                            pltpu.SemaphoreType.DMA(()), pltpu.SemaphoreType.DMA((2,))]),
        out_shape=jax.ShapeDtypeStruct((MOE_ROWS, D_MODEL), F32),
        compiler_params=_cparams(("arbitrary",)), name="moe_dispatch",
    )(pos, last_tile, n_tiles, h)

    n_chunk = (n_sub + MOE_CHUNK - 1) // MOE_CHUNK
    chunk_end = jnp.cumsum(n_chunk)
    cidx = jnp.arange(MOE_CHUNKS)
    c_expert = jnp.minimum(jnp.sum(cidx[:, None] >= chunk_end[None, :], axis=1), N_EXPERTS - 1).astype(jnp.int32)
    c_k = cidx - (chunk_end - n_chunk)[c_expert]
    c_tile0 = (start[c_expert] + c_k * MOE_CHUNK).astype(jnp.int32)
    c_ntiles = jnp.clip(n_sub[c_expert] - c_k * MOE_CHUNK, 0, MOE_CHUNK).astype(jnp.int32)
    n_chunks = chunk_end[N_EXPERTS - 1:]

    nf = D_FF_EXPERT // MOE_TF

    def w_idx(c, f, ce, nch):
        live = c < nch[0]
        return ce[jnp.minimum(c, nch[0] - 1)], jnp.where(live, f, nf - 1)

    def up_map(c, f, ce, ct, cn, nch, nt):
        e, ff = w_idx(c, f, ce, nch)
        return (e, 0, ff)

    def down_map(c, f, ce, ct, cn, nch, nt):
        e, ff = w_idx(c, f, ce, nch)
        return (e, ff, 0)

    y = pl.pallas_call(
        _expert_kernel,
        grid_spec=pltpu.PrefetchScalarGridSpec(
            num_scalar_prefetch=5, grid=(MOE_CHUNKS, nf),
            in_specs=[pl.BlockSpec(memory_space=pl.ANY),
                      pl.BlockSpec((None, D_MODEL, MOE_TF), up_map),
                      pl.BlockSpec((None, D_MODEL, MOE_TF), up_map),
                      pl.BlockSpec((None, MOE_TF, D_MODEL), down_map)],
            out_specs=pl.BlockSpec(memory_space=pl.ANY),
            scratch_shapes=[pltpu.VMEM((MOE_CHUNK, MOE_TILE, D_MODEL), F32),
                            pltpu.VMEM((MOE_CHUNK, MOE_TILE, D_MODEL), BF16),
                            pltpu.VMEM((2, MOE_TILE, D_MODEL), F32),
                            pltpu.VMEM((D_MODEL, MOE_TF), BF16), pltpu.VMEM((D_MODEL, MOE_TF), BF16),
                            pltpu.VMEM((MOE_TF, D_MODEL), BF16),
                            pltpu.SemaphoreType.DMA((2,)), pltpu.SemaphoreType.DMA(())]),
        out_shape=jax.ShapeDtypeStruct((MOE_ROWS, D_MODEL), F32),
        compiler_params=_cparams(("arbitrary", "arbitrary")), name="moe_experts",
    )(c_expert, c_tile0, c_ntiles, n_chunks, n_tiles, xs, wg, wu, wd)

    tm = COMBINE_TM
    return pl.pallas_call(
        _combine_kernel,
        grid_spec=pltpu.PrefetchScalarGridSpec(
            num_scalar_prefetch=1, grid=(N_TOK // tm,),
            in_specs=[pl.BlockSpec((tm, LANES), lambda i, p: (i, 0)),
                      pl.BlockSpec((tm, D_MODEL), lambda i, p: (i, 0)),
                      pl.BlockSpec((None, None, 1, D_MODEL), lambda i, p: (l, _cond_row(i, tm), 0, 5)),
                      pl.BlockSpec(memory_space=pl.ANY)],
            out_specs=[pl.BlockSpec((tm, D_MODEL), lambda i, p: (jnp.minimum(i, N_PROMPT // tm - 1), 0)),
                       pl.BlockSpec((tm, D_MODEL), lambda i, p: (jnp.maximum(i - N_PROMPT // tm, 0), 0))],
            scratch_shapes=[pltpu.VMEM((2, TOP_K, tm, D_MODEL), F32), pltpu.SemaphoreType.DMA((2,))]),
        out_shape=[jax.ShapeDtypeStruct((N_PROMPT, D_MODEL), F32), jax.ShapeDtypeStruct((N_SAMPLE, D_MODEL), F32)],
        compiler_params=_cparams(("arbitrary",)), name="moe_combine",
    )(pos, route, x, mod, y)


def _kv_leaf_kernel(*refs):
    ins, outs = refs[:4 * DEPTH], refs[4 * DEPTH:]
    for l in range(DEPTH):
        for j in range(4):
            outs[j][l] = ins[4 * l + j][...]


def _kv_leaves(per_layer):
    widths = (A_KV, A_KV, C_W, C_W)
    return pl.pallas_call(
        _kv_leaf_kernel, grid=(BATCH,),
        in_specs=[pl.BlockSpec((SEQ, w), lambda b: (b, 0)) for _ in range(DEPTH) for w in widths],
        out_specs=[pl.BlockSpec((None, DEPTH, SEQ, w), lambda b: (b, 0, 0, 0)) for w in widths],
        out_shape=[jax.ShapeDtypeStruct((BATCH, DEPTH, SEQ, w), F32) for w in widths],
        compiler_params=_cparams(("parallel",)), name="kv_leaves",
    )(*[a for layer in per_layer for a in layer])


def _rope_tables():
    t = jnp.arange(DEC_SEQ)
    row = (t // GRID_W).astype(F32)
    col = (t % GRID_W).astype(F32)
    n_freq = HEAD_DIM // 4
    inv = ROPE_BASE ** (-jnp.arange(n_freq, dtype=F32) / n_freq)
    ang = jnp.concatenate([row[:, None] * inv, col[:, None] * inv], axis=-1)
    cos, sin = jnp.cos(ang), jnp.sin(ang)
    cos_h = jnp.concatenate([cos, cos], axis=-1)
    sin_h = jnp.concatenate([-sin, sin], axis=-1)
    cos_l = jnp.tile(jnp.concatenate([cos_h, cos_h], axis=-1), (DEC_BATCH, 1))
    sin_l = jnp.tile(jnp.concatenate([sin_h, sin_h], axis=-1), (DEC_BATCH, 1))
    cos_t = jnp.concatenate([jnp.ones((N_PROMPT, LANES), F32), cos_l], axis=0)
    sin_t = jnp.concatenate([jnp.zeros((N_PROMPT, LANES), F32), sin_l], axis=0)
    return cos_t, sin_t


def kernel(x_prompt, x_sample, cache_a_k, cache_a_v, cache_c_k, cache_c_v, c, c_ctx, w_mod, b_mod, norm1_g, norm2_g, w_in, qk_norm_a, qk_norm_c, sink_a, rpb_c, w_branch_a, w_branch_b, w_branch_c, w_out, w_ff_gate, w_ff_up, w_ff_down, w_router, w_exp_gate, w_exp_up, w_exp_down):
    x = (x_prompt.reshape(N_PROMPT, D_MODEL), x_sample.reshape(N_SAMPLE, D_MODEL))
    cond =jnp.concatenate([c_ctx[None, :], c], axis=0)
    cond_t = jnp.broadcast_to(cond[:, :, None], (N_COND, D_MODEL, LANES))
    mod = _modulation(cond_t, w_mod, b_mod)
    cos_t, sin_t = _rope_tables()
    bias = _nbr_bias_tables(rpb_c)
    ck_a = cache_a_k.reshape(DEC_BATCH, DEPTH, PAST_LEN, A_KV)
    cv_a = cache_a_v.reshape(DEC_BATCH, DEPTH, PAST_LEN, A_KV)
    ck_c = cache_c_k.reshape(DEC_BATCH, DEPTH, PAST_LEN, C_W)
    cv_c = cache_c_v.reshape(DEC_BATCH, DEPTH, PAST_LEN, C_W)

    new_kv = []
    h = x
    for l in range(DEPTH):
        qa, ka, va, fb, qc, kc, vc, gates = _in_proj(h, w_in, cos_t, sin_t, qk_norm_a[l], qk_norm_c[l], l,
                                                     norm=(norm1_g[l], mod) if l == 0 else None)
        oa_p, oc_p = _ctx_attn(sink_a, qa, ka, va, qc, kc, vc, l)
        oa_s = _win_attn(sink_a, qa, ka, va, ck_a, cv_a, l)
        oc_s = _nbr_attn(qc, kc, vc, ck_c, cv_c, bias, l)
        ob_p = _fourier(fb, BATCH, SEQ, 0, SEQ)
        ob_s = _fourier(fb, DEC_BATCH, DEC_SEQ, N_PROMPT, 512)
        branches = ((oa_p, ob_p, oc_p), (oa_s, ob_s, oc_s), gates, x,
                    w_branch_a, w_branch_b, w_branch_c, w_out, mod, l, norm2_g[l])
        i = l // 2
        if l % 2 == 0:
            x, h2 = _merge(*branches)
            x, h = _ffn(h2, x, w_ff_gate, w_ff_up, w_ff_down, mod, l, norm1_g[l + 1])
        else:
            x, h2, logits = _merge(*branches, w_router=w_router[i])
            xp, xs = _moe(h2, x, logits, w_exp_gate[i], w_exp_up[i], w_exp_down[i], mod, l)
        new_kv.append((ka, va, kc, vc))

    new_ak, new_av, new_ck, new_cv = _kv_leaves(new_kv)
    return (xp.reshape(BATCH, SEQ, D_MODEL), xs.reshape(DEC_BATCH, DEC_SEQ, D_MODEL),
            new_ak.reshape(BATCH, DEPTH, SEQ, A_KV_HEADS, HEAD_DIM),
            new_av.reshape(BATCH, DEPTH, SEQ, A_KV_HEADS, HEAD_DIM),
            new_ck.reshape(BATCH, DEPTH, SEQ, C_HEADS, HEAD_DIM),
            new_cv.reshape(BATCH, DEPTH, SEQ, C_HEADS, HEAD_DIM))
```

```python
import functools

import numpy as np
import jax
import jax.numpy as jnp
from jax import lax
from jax.experimental import pallas as pl
from jax.experimental.pallas import tpu as pltpu

F32 = jnp.float32
BF16 = jnp.bfloat16

D_MODEL = 1024
BATCH = 16
SEQ = 256
DEPTH = 2
DEC_BATCH = 2
DEC_SEQ = 2048
PAST_LEN = 512
GRID_W = 64
HEAD_DIM = 64
SCALE = HEAD_DIM ** -0.5
A_HEADS = 8
A_KV_HEADS = 2
A_GROUP = A_HEADS // A_KV_HEADS
A_WINDOW = 128
A_BLOCK = 128
B_GROUPS = 8
B_GROUP_DIM = 64
B_WIDTH = B_GROUPS * B_GROUP_DIM
C_HEADS = 8
C_WIN_ROWS = 8
C_WIN_COLS = 16
A_Q = A_HEADS * HEAD_DIM
A_KV = A_KV_HEADS * HEAD_DIM
C_W = C_HEADS * HEAD_DIM
QKV_WIDTH = A_Q + 2 * A_KV + B_WIDTH + 3 * C_W
N_BRANCH = 3
GATE_WIDTH = N_BRANCH * D_MODEL
IN_WIDTH = QKV_WIDTH + GATE_WIDTH
D_FF = 2816
N_EXPERTS = 8
D_FF_EXPERT = 3584
ROPE_BASE = 10000.0
RMS_EPS = 1e-6
NEG_INF = -1e30

N_PROMPT = BATCH * SEQ
N_SAMPLE = DEC_BATCH * DEC_SEQ
N_TOK = N_PROMPT + N_SAMPLE
N_COND = 1 + DEC_BATCH
LANES = 128
NORM_SLAB = 256
C_QROWS = 4
C_QBLOCK = C_QROWS * GRID_W
C_DR_SLOTS = 2 * C_WIN_ROWS
VMEM_LIMIT = 56 * 1024 * 1024


def _cparams(sem):
    return pltpu.CompilerParams(dimension_semantics=sem, vmem_limit_bytes=VMEM_LIMIT)


def _sigmoid(x):
    return 1.0 / (1.0 + jnp.exp(-x))


def _cond_row(tile, tm):
    return jnp.maximum(tile * tm // DEC_SEQ - 1, 0)


def _mod_kernel(ct_ref, w_ref, b_ref, o_ref, silu_ref):
    @pl.when((pl.program_id(0) == 0) & (pl.program_id(1) == 0))
    def _():
        cb = ct_ref[...]
        silu_ref[...] = cb * _sigmoid(cb)

    tn = w_ref.shape[1]
    for r in range(N_COND):
        s = silu_ref[r]
        for cc in range(tn // LANES):
            sl = slice(cc * LANES, (cc + 1) * LANES)
            o_ref[r, :, sl] = jnp.sum(w_ref[:, sl] * s, axis=0, keepdims=True) + b_ref[:, sl]


def _modulation(cond_t, w_mod, b_mod):
    tn = 1024
    n = 6 * D_MODEL
    return pl.pallas_call(
        _mod_kernel,
        grid=(DEPTH, n // tn),
        in_specs=[
            pl.BlockSpec((N_COND, D_MODEL, LANES), lambda l, j: (0, 0, 0)),
            pl.BlockSpec((None, D_MODEL, tn), lambda l, j: (l, 0, j)),
            pl.BlockSpec((None, 1, tn), lambda l, j: (l, 0, j)),
        ],
        out_specs=pl.BlockSpec((None, N_COND, 1, tn), lambda l, j: (l, 0, 0, j)),
        out_shape=jax.ShapeDtypeStruct((DEPTH, N_COND, 1, n), F32),
        scratch_shapes=[pltpu.VMEM((N_COND, D_MODEL, LANES), F32)],
        compiler_params=_cparams(("arbitrary", "arbitrary")),
        name="modulation",
    )(cond_t, w_mod, b_mod.reshape(DEPTH, 1, n))


def _adaln_math(x, g, sh, sc):
    ms = jnp.mean(x * x, axis=-1, keepdims=True)
    y = x * lax.rsqrt(ms + RMS_EPS) * g
    return y * (1.0 + sc) + sh


def _split_rows(tm, width, joined=False):
    n_p = N_PROMPT // tm
    latent0 = n_p if joined else 0
    return (pl.BlockSpec((tm, width), lambda i: (jnp.minimum(i, n_p - 1), 0)),
            pl.BlockSpec((tm, width), lambda i: (jnp.maximum(i - n_p, 0) + latent0, 0)))


def _split_operands(x):
    return (x, False) if isinstance(x, tuple) else ((x, x), True)


def _read_split(p_ref, s_ref):
    is_ctx = pl.program_id(0) < N_PROMPT // p_ref.shape[0]
    return jnp.where(is_ctx, p_ref[...], s_ref[...])


WEIGHT_ROW_CHUNK = 128


def _load_weight_bf16(src_rows, dst_ref, stage_ref, sem):
    n = dst_ref.shape[0] // WEIGHT_ROW_CHUNK

    def chunk_copy(k):
        return pltpu.make_async_copy(src_rows(k), stage_ref.at[k % 2], sem.at[k % 2])

    chunk_copy(0).start()
    for k in range(n):
        if k + 1 < n:
            chunk_copy(k + 1).start()
        chunk_copy(k).wait()
        dst_ref[k * WEIGHT_ROW_CHUNK:(k + 1) * WEIGHT_ROW_CHUNK, :] = stage_ref[k % 2].astype(BF16)


def _row_chunk(k):
    return pl.ds(k * WEIGHT_ROW_CHUNK, WEIGHT_ROW_CHUNK)


def _head_norm(x, gain, bd):
    sq = x * x
    hi = sq.astype(BF16)
    lo = (sq - hi.astype(F32)).astype(BF16)
    ms = jnp.dot(hi, bd, preferred_element_type=F32) + jnp.dot(lo, bd, preferred_element_type=F32)
    return x * lax.rsqrt(ms + RMS_EPS) * gain


def _rope(x, cos, sin_signed, first_half):
    half = HEAD_DIM // 2
    swapped = jnp.where(first_half, pltpu.roll(x, x.shape[1] - half, 1), pltpu.roll(x, half, 1))
    return x * cos + swapped * sin_signed


def _in_proj_kernel(*refs, l, fused_norm):
    n_lead = 5 if fused_norm else 1
    (cos_ref, sin_ref, ga_ref, gc_ref, w_hbm, qa_ref, ka_ref, va_ref, fb_ref, qc_ref, kc_ref, vc_ref, gt_ref,
     wb_ref, stage_ref, sem) = refs[n_lead:]

    @pl.when(pl.program_id(0) == 0)
    def _():
        _load_weight_bf16(lambda k: w_hbm.at[l, _row_chunk(k)], wb_ref, stage_ref, sem)

    if fused_norm:
        xp_ref, xs_ref, ng_ref, sh_ref, sc_ref = refs[:n_lead]
        h = _adaln_math(_read_split(xp_ref, xs_ref), ng_ref[...], sh_ref[...], sc_ref[...]).astype(BF16)
    else:
        h = refs[0][...]

    def proj(off, width):
        return jnp.dot(h, wb_ref[:, off:off + width], preferred_element_type=F32)

    r = lax.broadcasted_iota(jnp.int32, (NORM_SLAB, NORM_SLAB), 0) // HEAD_DIM
    c = lax.broadcasted_iota(jnp.int32, (NORM_SLAB, NORM_SLAB), 1) // HEAD_DIM
    bd = jnp.where(r == c, 1.0 / HEAD_DIM, 0.0).astype(BF16)
    lane = lax.broadcasted_iota(jnp.int32, (1, NORM_SLAB), 1)
    first_half = (lane % HEAD_DIM) < HEAD_DIM // 2
    cos = jnp.concatenate([cos_ref[...]] * (NORM_SLAB // LANES), axis=1)
    sin = jnp.concatenate([sin_ref[...]] * (NORM_SLAB // LANES), axis=1)
    gqa, gka = ga_ref[0:1, :], ga_ref[1:2, :]
    gqc, gkc = gc_ref[0:1, :], gc_ref[1:2, :]
    slabs = lambda p: [p[:, s * NORM_SLAB:(s + 1) * NORM_SLAB] for s in range(p.shape[1] // NORM_SLAB)]
    cat = lambda parts: jnp.concatenate(parts, axis=1)

    off = 0
    qa_ref[...] = cat([_rope(_head_norm(x, gqa, bd), cos, sin, first_half) for x in slabs(proj(off, A_Q))]
                      ).astype(BF16)
    off += A_Q
    kv = proj(off, 2 * A_KV)
    ka_ref[...] = _rope(_head_norm(kv, gka, bd), cos, sin, first_half)[:, :A_KV]
    va_ref[...] = kv[:, A_KV:]
    off += 2 * A_KV
    fb_ref[...] = proj(off, B_WIDTH).astype(BF16)
    off += B_WIDTH
    qc_ref[...] = cat([_head_norm(x, gqc, bd) for x in slabs(proj(off, C_W))]).astype(BF16)
    off += C_W
    kc_ref[...] = cat([_head_norm(x, gkc, bd) for x in slabs(proj(off, C_W))])
    off += C_W
    vc_ref[...] = proj(off, C_W)
    off += C_W
    for j in range(N_BRANCH):
        cols = slice(j * D_MODEL, (j + 1) * D_MODEL)
        gt_ref[:, cols] = _sigmoid(proj(off + j * D_MODEL, D_MODEL)).astype(BF16)


def _in_proj(h, w_in, cos_t, sin_t, qk_a, qk_c, l, norm=None):
    tm = 512
    row = lambda w: pl.BlockSpec((tm, w), lambda i: (i, 0))
    widths = (A_Q, A_KV, A_KV, B_WIDTH, C_W, C_W, C_W, GATE_WIDTH)
    dtypes = (BF16, F32, F32, BF16, BF16, F32, F32, BF16)
    if norm is None:
        lead_specs, lead = [row(D_MODEL)], [h]
    else:
        norm_g, mod = norm
        x, joined = _split_operands(h)
        mspec = lambda which: pl.BlockSpec((None, None, 1, D_MODEL), lambda i: (l, _cond_row(i, tm), 0, which))
        lead_specs = [*_split_rows(tm, D_MODEL, joined), pl.BlockSpec((1, D_MODEL), lambda i: (0, 0)),
                      mspec(0), mspec(1)]
        lead = [*x, norm_g.reshape(1, D_MODEL), mod, mod]
    return pl.pallas_call(
        functools.partial(_in_proj_kernel, l=l, fused_norm=norm is not None), grid=(N_TOK // tm,),
        in_specs=[*lead_specs, row(LANES), row(LANES),
                  pl.BlockSpec((2, NORM_SLAB), lambda i: (0, 0)), pl.BlockSpec((2, NORM_SLAB), lambda i: (0, 0)),
                  pl.BlockSpec(memory_space=pl.ANY)],
        out_specs=[row(w) for w in widths],
        out_shape=[jax.ShapeDtypeStruct((N_TOK, w), d) for w, d in zip(widths, dtypes)],
        scratch_shapes=[pltpu.VMEM((D_MODEL, IN_WIDTH), BF16),
                        pltpu.VMEM((2, WEIGHT_ROW_CHUNK, IN_WIDTH), F32), pltpu.SemaphoreType.DMA((2,))],
        compiler_params=_cparams(("arbitrary",)), name="in_proj",
    )(*lead, cos_t, sin_t, jnp.tile(qk_a, (1, NORM_SLAB // HEAD_DIM)), jnp.tile(qk_c, (1, NORM_SLAB // HEAD_DIM)), w_in)


def _nt_dot(a, b):
    return lax.dot_general(a, b, (((1,), (1,)), ((), ())), preferred_element_type=F32)


def _head(x, h):
    return x[:, h * HEAD_DIM:(h + 1) * HEAD_DIM]


def _stacked_softmax(parts, sink):
    m = parts[0].max(axis=-1, keepdims=True)
    for s in parts[1:]:
        m = jnp.maximum(m, s.max(axis=-1, keepdims=True))
    if sink is not None:
        m = jnp.maximum(m, sink)
    den = jnp.exp(sink - m) if sink is not None else 0.0
    es = []
    for s in parts:
        e = jnp.exp(s - m)
        den = den + e.sum(axis=-1, keepdims=True)
        es.append(e.astype(BF16))
    return es, 1.0 / den


def _sink_column(sink_ref, l, rows_per_head):
    return jnp.concatenate([jnp.full((rows_per_head, 1), sink_ref[l, h], F32) for h in range(A_HEADS)], axis=0)


def _gqa_queries(qa, g):
    return jnp.concatenate([_head(qa, g * A_GROUP + i) for i in range(A_GROUP)], axis=0)


def _ctx_attn_kernel(sink_ref, qa_ref, ka_ref, va_ref, qc_ref, kc_ref, vc_ref, oa_ref, oc_ref, *, l):
    t = SEQ
    qa = qa_ref[...] * SCALE
    ka = ka_ref[...].astype(BF16)
    va = va_ref[...].astype(BF16)
    s = jnp.concatenate([_nt_dot(_gqa_queries(qa, g), _head(ka, g)) for g in range(A_KV_HEADS)], axis=0)
    (e,), inv = _stacked_softmax([s], _sink_column(sink_ref, l, t))
    outs = []
    for g in range(A_KV_HEADS):
        rows = slice(g * A_GROUP * t, (g + 1) * A_GROUP * t)
        o = jnp.dot(e[rows], _head(va, g), preferred_element_type=F32) * inv[rows]
        outs += [o[i * t:(i + 1) * t] for i in range(A_GROUP)]
    oa_ref[...] = jnp.concatenate(outs, axis=1).astype(BF16)

    qc = qc_ref[...] * SCALE
    kc = kc_ref[...].astype(BF16)
    vc = vc_ref[...].astype(BF16)
    s = jnp.concatenate([_nt_dot(_head(qc, h), _head(kc, h)) for h in range(C_HEADS)], axis=0)
    (e,), inv = _stacked_softmax([s], None)
    outs = [jnp.dot(e[h * t:(h + 1) * t], _head(vc, h), preferred_element_type=F32) * inv[h * t:(h + 1) * t]
            for h in range(C_HEADS)]
    oc_ref[...] = jnp.concatenate(outs, axis=1).astype(BF16)


def _ctx_attn(sink_a, qa, ka, va, qc, kc, vc, l):
    blk = lambda w: pl.BlockSpec((SEQ, w), lambda b: (b, 0))
    return pl.pallas_call(
        functools.partial(_ctx_attn_kernel, l=l), grid=(BATCH,),
        in_specs=[pl.BlockSpec(memory_space=pltpu.SMEM),
                  blk(A_Q), blk(A_KV), blk(A_KV), blk(C_W), blk(C_W), blk(C_W)],
        out_specs=[blk(A_Q), blk(C_W)],
        out_shape=[jax.ShapeDtypeStruct((N_PROMPT, A_Q), BF16), jax.ShapeDtypeStruct((N_PROMPT, C_W), BF16)],
        compiler_params=_cparams(("parallel",)), name="ctx_attn",
    )(sink_a, qa, ka, va, qc, kc, vc)


def _win_attn_kernel(sink_ref, q_ref, kp_ref, kc_ref, kn_ref, vp_ref, vc_ref, vn_ref, ck_ref, cv_ref, mask_ref,
                     o_ref, *, l):
    rows = A_GROUP * A_BLOCK
    mask = mask_ref[...]
    sink = _sink_column(sink_ref, l, A_BLOCK)
    for g in range(A_KV_HEADS):
        sl = slice(g * HEAD_DIM, (g + 1) * HEAD_DIM)
        q = jnp.concatenate([q_ref[:, (g * A_GROUP + i) * HEAD_DIM:(g * A_GROUP + i + 1) * HEAD_DIM]
                             for i in range(A_GROUP)], axis=0) * SCALE
        k_loc = jnp.concatenate([kp_ref[:, sl], kc_ref[:, sl], kn_ref[:, sl]], axis=0).astype(BF16)
        v_loc = jnp.concatenate([vp_ref[:, sl], vc_ref[:, sl], vn_ref[:, sl]], axis=0).astype(BF16)
        s_loc = _nt_dot(q, k_loc) + mask
        s_ctx = _nt_dot(q, ck_ref[:, sl].astype(BF16))
        (e_loc, e_ctx), inv = _stacked_softmax([s_loc, s_ctx], sink[g * rows:(g + 1) * rows])
        o = (jnp.dot(e_loc, v_loc, preferred_element_type=F32)
             + jnp.dot(e_ctx, cv_ref[:, sl].astype(BF16), preferred_element_type=F32)) * inv
        for i in range(A_GROUP):
            h = g * A_GROUP + i
            o_ref[:, h * HEAD_DIM:(h + 1) * HEAD_DIM] = o[i * A_BLOCK:(i + 1) * A_BLOCK].astype(BF16)


def _win_attn(sink_a, qa, ka, va, cache_k, cache_v, l):
    nb = DEC_SEQ // A_BLOCK
    base = N_PROMPT // A_BLOCK

    def nbr(d):
        return lambda b, t: (base + b * nb + jnp.clip(t + d, 0, nb - 1), 0)

    kv = lambda d: pl.BlockSpec((A_BLOCK, A_KV), nbr(d))
    cache = pl.BlockSpec((None, None, PAST_LEN, A_KV), lambda b, t: (b, l, 0, 0))
    qi = np.arange(A_GROUP * A_BLOCK)[:, None] % A_BLOCK
    kj = np.arange(3 * A_BLOCK)[None, :] - A_BLOCK
    band = np.abs(kj - qi) <= A_WINDOW
    masks = np.stack([band & (kj >= 0), band, band & (kj < A_BLOCK)])
    masks = jnp.asarray(np.where(masks, 0.0, NEG_INF).astype(np.float32))
    position = lambda b, t: (jnp.where(t == 0, 0, jnp.where(t == nb - 1, 2, 1)), 0, 0)
    return pl.pallas_call(
        functools.partial(_win_attn_kernel, l=l), grid=(DEC_BATCH, nb),
        in_specs=[pl.BlockSpec(memory_space=pltpu.SMEM),
                  pl.BlockSpec((A_BLOCK, A_Q), nbr(0)),
                  kv(-1), kv(0), kv(1), kv(-1), kv(0), kv(1), cache, cache,
                  pl.BlockSpec((None, A_GROUP * A_BLOCK, 3 * A_BLOCK), position)],
        out_specs=pl.BlockSpec((A_BLOCK, A_Q), lambda b, t: (b * nb + t, 0)),
        out_shape=jax.ShapeDtypeStruct((N_SAMPLE, A_Q), BF16),
        compiler_params=_cparams(("parallel", "parallel")), name="win_attn",
    )(sink_a, qa, ka, ka, ka, va, va, va, cache_k, cache_v, masks)


def _nbr_attn_kernel(q_ref, kp_ref, kc_ref, kn_ref, vp_ref, vc_ref, vn_ref, ck_ref, cv_ref, tab_ref, o_ref,
                     bias_ref):
    j = pl.program_id(0)
    nb = pl.num_programs(0)
    slots = _nbr_row_slots()

    def build(cls):
        for h in range(C_HEADS):
            for qr in range(C_QROWS):
                for kk in range(3 * C_QROWS):
                    bias_ref[h, qr * GRID_W:(qr + 1) * GRID_W, kk * GRID_W:(kk + 1) * GRID_W] = (
                        tab_ref[h, slots[cls][qr][kk]])

    first_of_batch = pl.program_id(1) == 0
    for cls, at in enumerate((0, 1, nb - 1)):
        @pl.when(first_of_batch & (j == at))
        def _():
            build(cls)

    tq = C_QBLOCK
    q = q_ref[...] * SCALE
    k_loc = jnp.concatenate([kp_ref[...], kc_ref[...], kn_ref[...]], axis=0).astype(BF16)
    v_loc = jnp.concatenate([vp_ref[...], vc_ref[...], vn_ref[...]], axis=0).astype(BF16)
    k_ctx = ck_ref[...].astype(BF16)
    v_ctx = cv_ref[...].astype(BF16)
    s_loc = jnp.concatenate([_nt_dot(_head(q, h), _head(k_loc, h)) for h in range(C_HEADS)], axis=0)
    s_loc = s_loc + bias_ref[...].reshape(C_HEADS * tq, 3 * tq)
    s_ctx = jnp.concatenate([_nt_dot(_head(q, h), _head(k_ctx, h)) for h in range(C_HEADS)], axis=0)
    (e_loc, e_ctx), inv = _stacked_softmax([s_loc, s_ctx], None)
    outs = []
    for h in range(C_HEADS):
        r = slice(h * tq, (h + 1) * tq)
        outs.append((jnp.dot(e_loc[r], _head(v_loc, h), preferred_element_type=F32)
                     + jnp.dot(e_ctx[r], _head(v_ctx, h), preferred_element_type=F32)) * inv[r])
    o_ref[...] = jnp.concatenate(outs, axis=1).astype(BF16)


def _nbr_attn(qc, kc, vc, cache_k, cache_v, bias, l):
    nb = DEC_SEQ // C_QBLOCK
    base = N_PROMPT // C_QBLOCK

    def nbr(d):
        return lambda j, b: (base + b * nb + jnp.clip(j + d, 0, nb - 1), 0)

    kv = lambda d: pl.BlockSpec((C_QBLOCK, C_W), nbr(d))
    cache = pl.BlockSpec((None, None, PAST_LEN, C_W), lambda j, b: (b, l, 0, 0))
    return pl.pallas_call(
        _nbr_attn_kernel, grid=(nb, DEC_BATCH),
        in_specs=[kv(0), kv(-1), kv(0), kv(1), kv(-1), kv(0), kv(1), cache, cache,
                  pl.BlockSpec((None, C_HEADS, C_DR_SLOTS, GRID_W, GRID_W), lambda j, b: (l, 0, 0, 0, 0))],
        out_specs=pl.BlockSpec((C_QBLOCK, C_W), lambda j, b: (b * nb + j, 0)),
        out_shape=jax.ShapeDtypeStruct((N_SAMPLE, C_W), BF16),
        scratch_shapes=[pltpu.VMEM((C_HEADS, C_QBLOCK, 3 * C_QBLOCK), F32)],
        compiler_params=_cparams(("arbitrary", "arbitrary")), name="nbr_attn",
    )(qc, kc, kc, kc, vc, vc, vc, cache_k, cache_v, bias)


def _nbr_bias_tables(rpb):
    qcol = np.arange(GRID_W)
    qcs = np.clip(qcol - C_WIN_COLS // 2, 0, GRID_W - C_WIN_COLS)
    kcol = np.arange(GRID_W)
    col_ok = (kcol[None, :] >= qcs[:, None]) & (kcol[None, :] < qcs[:, None] + C_WIN_COLS)
    dc = np.clip(kcol[None, :] - qcol[:, None], -(C_WIN_COLS - 1), C_WIN_COLS - 1) + C_WIN_COLS - 1
    onehot_dc = (dc.reshape(-1)[None, :] == np.arange(2 * C_WIN_COLS - 1)[:, None]).astype(np.float32)
    t = jnp.einsum('lhab,bx->lhax', rpb, jnp.asarray(onehot_dc), precision=lax.Precision.HIGHEST)
    t = jnp.where(jnp.asarray(col_ok.reshape(-1)), t, NEG_INF)
    t = jnp.concatenate([t, jnp.full((DEPTH, C_HEADS, 1, GRID_W * GRID_W), NEG_INF, F32)], axis=2)
    return t.reshape(DEPTH, C_HEADS, C_DR_SLOTS, GRID_W, GRID_W)


def _nbr_row_slots():
    rows = DEC_SEQ // GRID_W
    slots = []
    for j in (0, 3, rows // C_QROWS - 1):
        per_q = []
        for qr in range(C_QROWS):
            r = C_QROWS * j + qr
            rs = min(max(r - C_WIN_ROWS // 2, 0), rows - C_WIN_ROWS)
            per_k = []
            for kk in range(3 * C_QROWS):
                kabs = C_QROWS * (j - 1) + kk
                per_k.append(kabs - r + C_WIN_ROWS - 1 if rs <= kabs < rs + C_WIN_ROWS else C_DR_SLOTS - 1)
            per_q.append(per_k)
        slots.append(per_q)
    return slots


def _fourier_kernel(u_ref, bc_ref, bs_ref, cl_ref, sl_ref, o_ref, zc_ref, zs_ref):
    @pl.when(pl.program_id(1) == 0)
    def _():
        u = u_ref[...]
        zc_ref[...] = jnp.dot(u, bc_ref[...].astype(BF16), preferred_element_type=F32).astype(BF16)
        zs_ref[...] = jnp.dot(u, bs_ref[...].astype(BF16), preferred_element_type=F32).astype(BF16)

    o = (jnp.dot(cl_ref[...].astype(BF16), zc_ref[...], preferred_element_type=F32)
         - jnp.dot(sl_ref[...].astype(BF16), zs_ref[...], preferred_element_type=F32))
    o_ref[...] = o.astype(BF16)


def _dft_tables(n):
    k = np.arange(n)
    ang = 2.0 * np.pi * ((k[:, None] * k[None, :]) % n) / n
    return np.cos(ang) / np.sqrt(n), np.sin(ang) / np.sqrt(n)


def _channel_dft_tables():
    c, s = _dft_tables(B_GROUP_DIM)
    eye = np.eye(B_GROUPS)
    return np.kron(eye, c).astype(np.float32), np.kron(eye, s).astype(np.float32)


def _fourier(fb, n_batch, seq, row0, tr):
    cl, sl = (jnp.asarray(a.astype(np.float32)) for a in _dft_tables(seq))
    bc, bs = (jnp.asarray(a) for a in _channel_dft_tables())
    nt = seq // tr
    const = pl.BlockSpec((B_WIDTH, B_WIDTH), lambda b, t: (0, 0))
    return pl.pallas_call(
        _fourier_kernel, grid=(n_batch, nt),
        in_specs=[pl.BlockSpec((seq, B_WIDTH), lambda b, t: (row0 // seq + b, 0)), const, const,
                  pl.BlockSpec((tr, seq), lambda b, t: (t, 0)), pl.BlockSpec((tr, seq), lambda b, t: (t, 0))],
        out_specs=pl.BlockSpec((tr, B_WIDTH), lambda b, t: (b * nt + t, 0)),
        out_shape=jax.ShapeDtypeStruct((n_batch * seq, B_WIDTH), BF16),
        scratch_shapes=[pltpu.VMEM((seq, B_WIDTH), BF16), pltpu.VMEM((seq, B_WIDTH), BF16)],
        compiler_params=_cparams(("parallel", "arbitrary")), name=f"fourier_{seq}",
    )(fb, bc, bs, cl, sl)


def _merge_kernel(oap_ref, obp_ref, ocp_ref, oas_ref, obs_ref, ocs_ref, gt_ref, xp_ref, xs_ref,
                  wa_ref, wb_ref, wc_ref, wo_ref, g1_ref, ng_ref, sh_ref, sc_ref, *rest, router):
    if router:
        wr_ref, o_ref, h2_ref, lg_ref, wab_ref, wbb_ref, wcb_ref, wob_ref = rest
    else:
        o_ref, h2_ref, wab_ref, wbb_ref, wcb_ref, wob_ref = rest

    @pl.when(pl.program_id(0) == 0)
    def _():
        wab_ref[...] = wa_ref[...].astype(BF16)
        wbb_ref[...] = wb_ref[...].astype(BF16)
        wcb_ref[...] = wc_ref[...].astype(BF16)
        wob_ref[...] = wo_ref[...].astype(BF16)

    ya = jnp.dot(_read_split(oap_ref, oas_ref), wab_ref[...], preferred_element_type=F32)
    yb = jnp.dot(_read_split(obp_ref, obs_ref), wbb_ref[...], preferred_element_type=F32)
    yc = jnp.dot(_read_split(ocp_ref, ocs_ref), wcb_ref[...], preferred_element_type=F32)
    d = D_MODEL
    m = (gt_ref[:, 0:d].astype(F32) * ya + gt_ref[:, d:2 * d].astype(F32) * yb
         + gt_ref[:, 2 * d:3 * d].astype(F32) * yc)
    y = jnp.dot(m.astype(BF16), wob_ref[...], preferred_element_type=F32)
    x_new = _read_split(xp_ref, xs_ref) + g1_ref[...] * y
    o_ref[...] = x_new
    h2 = _adaln_math(x_new, ng_ref[...], sh_ref[...], sc_ref[...])
    h2_ref[...] = h2.astype(h2_ref.dtype)
    if router:
        w = wr_ref[...]
        w_hi = w.astype(BF16)
        w_lo = (w - w_hi.astype(F32)).astype(BF16)
        h_hi = h2.astype(BF16)
        h_lo = (h2 - h_hi.astype(F32)).astype(BF16)
        hi_terms = jnp.dot(h_hi, jnp.concatenate([w_hi, w_lo], axis=1), preferred_element_type=F32)
        lg_ref[...] = (hi_terms[:, :LANES] + jnp.dot(h_lo, w_hi, preferred_element_type=F32)
                       + hi_terms[:, LANES:])


def _merge(branches_p, branches_s, gates, x, wa, wb, wc, wo, mod, l, norm_g, w_router=None):
    tm = 512
    router = w_router is not None
    x, joined = _split_operands(x)
    row = lambda w: pl.BlockSpec((tm, w), lambda i: (i, 0))
    row_p = lambda w: _split_rows(tm, w)[0]
    row_s = lambda w: _split_rows(tm, w)[1]
    const = lambda r, c: pl.BlockSpec((None, r, c), lambda i: (l, 0, 0), pipeline_mode=pl.Buffered(1))
    mspec = lambda which: pl.BlockSpec((None, None, 1, D_MODEL), lambda i: (l, _cond_row(i, tm), 0, which))
    in_specs = [row_p(A_Q), row_p(B_WIDTH), row_p(C_W), row_s(A_Q), row_s(B_WIDTH), row_s(C_W),
                row(GATE_WIDTH), *_split_rows(tm, D_MODEL, joined),
                const(A_Q, D_MODEL), const(B_WIDTH, D_MODEL), const(C_W, D_MODEL), const(D_MODEL, D_MODEL),
                mspec(2), pl.BlockSpec((1, D_MODEL), lambda i: (0, 0)), mspec(3), mspec(4)]
    operands = [*branches_p, *branches_s, gates, *x, wa, wb, wc, wo, mod, norm_g.reshape(1, D_MODEL), mod, mod]
    out_specs = [row(D_MODEL), row(D_MODEL)]
    out_shape = [jax.ShapeDtypeStruct((N_TOK, D_MODEL), F32),
                 jax.ShapeDtypeStruct((N_TOK, D_MODEL), F32 if router else BF16)]
    if router:
        in_specs.append(pl.BlockSpec((D_MODEL, LANES), lambda i: (0, 0)))
        operands.append(jnp.pad(w_router, ((0, 0), (0, LANES - N_EXPERTS))))
        out_specs.append(row(LANES))
        out_shape.append(jax.ShapeDtypeStruct((N_TOK, LANES), F32))
    return pl.pallas_call(
        functools.partial(_merge_kernel, router=router), grid=(N_TOK // tm,),
        in_specs=in_specs, out_specs=out_specs, out_shape=out_shape,
        scratch_shapes=[pltpu.VMEM((A_Q, D_MODEL), BF16), pltpu.VMEM((B_WIDTH, D_MODEL), BF16),
                        pltpu.VMEM((C_W, D_MODEL), BF16), pltpu.VMEM((D_MODEL, D_MODEL), BF16)],
        compiler_params=_cparams(("arbitrary",)), name="merge_router" if router else "merge",
    )(*operands)


FFN_COL_CHUNK = 256


def _ffn_kernel(h_ref, x_ref, g2_ref, ng_ref, sh_ref, sc_ref, wg_hbm, wu_hbm, wd_hbm, o_ref, hn_ref,
                wgb_ref, wub_ref, wdb_ref, stage_up_ref, stage_dn_ref, sem, *, i_dense):
    @pl.when(pl.program_id(0) == 0)
    def _():
        _load_weight_bf16(lambda k: wg_hbm.at[i_dense, _row_chunk(k)], wgb_ref, stage_up_ref, sem)
        _load_weight_bf16(lambda k: wu_hbm.at[i_dense, _row_chunk(k)], wub_ref, stage_up_ref, sem)
        _load_weight_bf16(lambda k: wd_hbm.at[i_dense, _row_chunk(k)], wdb_ref, stage_dn_ref, sem)

    h = h_ref[...]
    acc = None
    for c in range(D_FF // FFN_COL_CHUNK):
        cols = slice(c * FFN_COL_CHUNK, (c + 1) * FFN_COL_CHUNK)
        g = jnp.dot(h, wgb_ref[:, cols], preferred_element_type=F32)
        u = jnp.dot(h, wub_ref[:, cols], preferred_element_type=F32)
        a = (g * _sigmoid(g) * u).astype(BF16)
        d = jnp.dot(a, wdb_ref[cols, :], preferred_element_type=F32)
        acc = d if acc is None else acc + d
    x_new = x_ref[...] + g2_ref[...] * acc
    o_ref[...] = x_new
    hn_ref[...] = _adaln_math(x_new, ng_ref[...], sh_ref[...], sc_ref[...]).astype(BF16)


def _ffn(h, x, wg, wu, wd, mod, l, next_norm_g):
    tm = 512
    row = lambda dt: pl.BlockSpec((tm, D_MODEL), lambda i: (i, 0))
    mspec = lambda layer, which: pl.BlockSpec((None, None, 1, D_MODEL),
                                              lambda i: (layer, _cond_row(i, tm), 0, which))
    hbm = pl.BlockSpec(memory_space=pl.ANY)
    return pl.pallas_call(
        functools.partial(_ffn_kernel, i_dense=l // 2), grid=(N_TOK // tm,),
        in_specs=[row(BF16), row(F32), mspec(l, 5),
                  pl.BlockSpec((1, D_MODEL), lambda i: (0, 0)), mspec(l + 1, 0), mspec(l + 1, 1),
                  hbm, hbm, hbm],
        out_specs=[row(F32), row(BF16)],
        out_shape=[jax.ShapeDtypeStruct((N_TOK, D_MODEL), F32), jax.ShapeDtypeStruct((N_TOK, D_MODEL), BF16)],
        scratch_shapes=[pltpu.VMEM((D_MODEL, D_FF), BF16), pltpu.VMEM((D_MODEL, D_FF), BF16),
                        pltpu.VMEM((D_FF, D_MODEL), BF16),
                        pltpu.VMEM((2, WEIGHT_ROW_CHUNK, D_FF), F32),
                        pltpu.VMEM((2, WEIGHT_ROW_CHUNK, D_MODEL), F32), pltpu.SemaphoreType.DMA((2,))],
        compiler_params=_cparams(("arbitrary",)), name="ffn",
    )(h, x, mod, next_norm_g.reshape(1, D_MODEL), mod, mod, wg, wu, wd)


TOP_K = 2
MOE_TILE = 256
MOE_TILES = TOP_K * N_TOK // MOE_TILE + N_EXPERTS
MOE_ROWS = MOE_TILES * MOE_TILE
MOE_CHUNK = 10
MOE_CHUNKS = MOE_TILES // MOE_CHUNK + N_EXPERTS
MOE_TF = 896
ROUTE_TM = 512
DISPATCH_TM = 512
COMBINE_TM = 512
ROW_COPY_UNROLL = 16


def _route_kernel(lg_ref, o_ref, cnt_ref, base_ref, tri_ref):
    tm = lg_ref.shape[0]

    @pl.when(pl.program_id(0) == 0)
    def _():
        base_ref[...] = jnp.zeros_like(base_ref)
        r = lax.broadcasted_iota(jnp.int32, (tm, tm), 0)
        c = lax.broadcasted_iota(jnp.int32, (tm, tm), 1)
        tri_ref[...] = jnp.where(r > c, 1.0, 0.0).astype(BF16)

    lane = lax.broadcasted_iota(jnp.int32, lg_ref.shape, 1).astype(F32)
    lg = jnp.where(lane < N_EXPERTS, lg_ref[...], -jnp.inf)
    m1 = lg.max(axis=-1, keepdims=True)
    i1 = jnp.where(lg == m1, lane, float(LANES)).min(axis=-1, keepdims=True)
    rest = jnp.where(lane == i1, -jnp.inf, lg)
    m2 = rest.max(axis=-1, keepdims=True)
    i2 = jnp.where(rest == m2, lane, float(LANES)).min(axis=-1, keepdims=True)
    e2 = jnp.exp(m2 - m1)
    w1 = 1.0 / (1.0 + e2)
    w2 = e2 / (1.0 + e2)

    oh1 = jnp.where(lane == i1, 1.0, 0.0)
    oh2 = jnp.where(lane == i2, 1.0, 0.0)
    pre1 = jnp.dot(tri_ref[...], oh1.astype(BF16), preferred_element_type=F32)
    pre2 = jnp.dot(tri_ref[...], oh2.astype(BF16), preferred_element_type=F32)
    c1 = jnp.sum(oh1, axis=0, keepdims=True)
    c2 = jnp.sum(oh2, axis=0, keepdims=True)
    base = base_ref[...]
    rank1 = jnp.sum(oh1 * (base + pre1), axis=-1, keepdims=True)
    rank2 = jnp.sum(oh2 * (base + c1 + pre2), axis=-1, keepdims=True)
    base_ref[...] = base + c1 + c2

    cols = (i1, i2, rank1, rank2, w1, w2)
    out = jnp.zeros(lg_ref.shape, F32)
    for j, col in enumerate(cols):
        out = jnp.where(lane == float(j), col, out)
    o_ref[...] = out
    cnt_ref[...] = jnp.broadcast_to(base + c1 + c2, cnt_ref.shape)


def _dispatch_kernel(pos_ref, last_ref, nt_ref, h_ref, xs_hbm, zero_ref, stage_ref, zsem, sem):
    i = pl.program_id(0)
    tm = h_ref.shape[0]

    @pl.when(i == 0)
    def _():
        zero_ref[...] = jnp.zeros_like(zero_ref)

        def zero_copy(tile):
            row0 = pl.multiple_of(tile * MOE_TILE, MOE_TILE)
            return pltpu.make_async_copy(zero_ref, xs_hbm.at[pl.ds(row0, MOE_TILE)], zsem)

        def for_zeroed_tiles(fn):
            for e in range(N_EXPERTS):
                @pl.when(last_ref[e] >= 0)
                def _():
                    fn(zero_copy(last_ref[e]))

                tail = MOE_TILES - 1 - e

                @pl.when(tail >= nt_ref[0])
                def _():
                    fn(zero_copy(tail))

        for_zeroed_tiles(lambda cp: cp.start())
        for_zeroed_tiles(lambda cp: cp.wait())

    slot = i % 2
    stage_ref[slot] = h_ref[...]

    def row_copy(step, s, t, k):
        dst = xs_hbm.at[pl.ds(pos_ref[TOP_K * (step * tm + t) + k], 1)]
        return pltpu.make_async_copy(stage_ref.at[s, pl.ds(t, 1)], dst, sem.at[s])

    def issue(t, carry):
        for k in range(TOP_K):
            row_copy(i, slot, t, k).start()
        return carry

    def drain_of(step, s):
        def drain(t, carry):
            for k in range(TOP_K):
                row_copy(step, s, t, k).wait()
            return carry
        lax.fori_loop(0, tm, drain, 0, unroll=ROW_COPY_UNROLL)

    lax.fori_loop(0, tm, issue, 0, unroll=ROW_COPY_UNROLL)

    @pl.when(i > 0)
    def _():
        drain_of(i - 1, 1 - slot)

    @pl.when(i == pl.num_programs(0) - 1)
    def _():
        drain_of(i, slot)


def _expert_kernel(ce_ref, ct_ref, cn_ref, nch_ref, nt_ref, xs_hbm, wg_ref, wu_ref, wd_ref, y_hbm,
                   acc_ref, xb_ref, xstage_ref, wgb_ref, wub_ref, wdb_ref, xsem, osem):
    c = pl.program_id(0)
    f = pl.program_id(1)
    last_f = pl.num_programs(1) - 1

    def tile_rows(tile):
        return pl.ds(pl.multiple_of(tile * MOE_TILE, MOE_TILE), MOE_TILE)

    def out_copy(slot, tile):
        return pltpu.make_async_copy(acc_ref.at[slot], y_hbm.at[tile_rows(tile)], osem)

    @pl.when((c == 0) & (f == 0))
    def _():
        acc_ref[0] = jnp.zeros((MOE_TILE, D_MODEL), F32)
        for e in range(N_EXPERTS):
            tail = MOE_TILES - 1 - e

            @pl.when(tail >= nt_ref[0])
            def _():
                cp = out_copy(0, tail)
                cp.start()
                cp.wait()

    @pl.when(c < nch_ref[0])
    def _():
        n = cn_ref[c]
        t0 = ct_ref[c]

        def x_copy(j):
            return pltpu.make_async_copy(xs_hbm.at[tile_rows(t0 + j)], xstage_ref.at[j % 2], xsem.at[j % 2])

        @pl.when(f == 0)
        def _():
            x_copy(0).start()

        wgb_ref[...] = wg_ref[...].astype(BF16)
        wub_ref[...] = wu_ref[...].astype(BF16)
        wdb_ref[...] = wd_ref[...].astype(BF16)

        def tile_step(j, carry):
            @pl.when(f == 0)
            def _():
                @pl.when(j + 1 < n)
                def _():
                    x_copy(j + 1).start()

                x_copy(j).wait()
                xb_ref[j] = xstage_ref[j % 2].astype(BF16)
                acc_ref[j] = jnp.zeros((MOE_TILE, D_MODEL), F32)

            x = xb_ref[j]
            g = jnp.dot(x, wgb_ref[...], preferred_element_type=F32)
            u = jnp.dot(x, wub_ref[...], preferred_element_type=F32)
            a = (g * _sigmoid(g) * u).astype(BF16)
            acc_ref[j] += jnp.dot(a, wdb_ref[...], preferred_element_type=F32)

            @pl.when(f == last_f)
            def _():
                @pl.when(j > 0)
                def _():
                    out_copy(j - 1, t0 + j - 1).wait()

                out_copy(j, t0 + j).start()

            return carry

        lax.fori_loop(0, n, tile_step, 0)

        @pl.when(f == last_f)
        def _():
            out_copy(n - 1, t0 + n - 1).wait()


def _combine_kernel(pos_ref, route_ref, x_ref, g2_ref, y_hbm, op_ref, os_ref, buf_ref, sem):
    i = pl.program_id(0)
    n = pl.num_programs(0)
    tm = x_ref.shape[0]
    slot = i % 2

    def row_copy(step, s, t, k):
        src = y_hbm.at[pl.ds(pos_ref[TOP_K * (step * tm + t) + k], 1)]
        return pltpu.make_async_copy(src, buf_ref.at[s, k, pl.ds(t, 1)], sem.at[s])

    def issue(step, s):
        def body(t, carry):
            for k in range(TOP_K):
                row_copy(step, s, t, k).start()
            return carry
        lax.fori_loop(0, tm, body, 0, unroll=ROW_COPY_UNROLL)

    @pl.when(i == 0)
    def _():
        issue(0, 0)

    @pl.when(i + 1 < n)
    def _():
        issue(i + 1, 1 - slot)

    def wait_body(t, carry):
        for k in range(TOP_K):
            row_copy(i, slot, t, k).wait()
        return carry

    lax.fori_loop(0, tm, wait_body, 0, unroll=ROW_COPY_UNROLL)

    lane = lax.broadcasted_iota(jnp.int32, route_ref.shape, 1)
    rt = route_ref[...]
    w1 = jnp.sum(jnp.where(lane == 2 * TOP_K, rt, 0.0), axis=-1, keepdims=True)
    w2 = jnp.sum(jnp.where(lane == 2 * TOP_K + 1, rt, 0.0), axis=-1, keepdims=True)
    out = x_ref[...] + g2_ref[...] * (w1 * buf_ref[slot, 0] + w2 * buf_ref[slot, 1])

    @pl.when(i < N_PROMPT // tm)
    def _():
        op_ref[...] = out

    @pl.when(i >= N_PROMPT // tm)
    def _():
        os_ref[...] = out


def _moe(h, x, logits, wg, wu, wd, mod, l):
    route, cnt = pl.pallas_call(
        _route_kernel, grid=(N_TOK // ROUTE_TM,),
        in_specs=[pl.BlockSpec((ROUTE_TM, LANES), lambda i: (i, 0))],
        out_specs=[pl.BlockSpec((ROUTE_TM, LANES), lambda i: (i, 0)), pl.BlockSpec((8, LANES), lambda i: (0, 0))],
        out_shape=[jax.ShapeDtypeStruct((N_TOK, LANES), F32), jax.ShapeDtypeStruct((8, LANES), F32)],
        scratch_shapes=[pltpu.VMEM((1, LANES), F32), pltpu.VMEM((ROUTE_TM, ROUTE_TM), BF16)],
        compiler_params=_cparams(("arbitrary",)), name="route",
    )(logits)

    expert = route[:, 0:TOP_K].astype(jnp.int32)
    rank = route[:, TOP_K:2 * TOP_K].astype(jnp.int32)
    n_sub = (cnt[0, :N_EXPERTS].astype(jnp.int32) + MOE_TILE - 1) // MOE_TILE
    end = jnp.cumsum(n_sub)
    start = end - n_sub
    start_of = jnp.sum(jnp.where(expert[:, :, None] == jnp.arange(N_EXPERTS), start, 0), axis=-1)
    pos = (start_of * MOE_TILE + rank).reshape(-1)
    n_tiles = end[N_EXPERTS - 1:]
    last_tile = jnp.where(n_sub > 0, end - 1, -1).astype(jnp.int32)

    xs = pl.pallas_call(
        _dispatch_kernel,
        grid_spec=pltpu.PrefetchScalarGridSpec(
            num_scalar_prefetch=3, grid=(N_TOK // DISPATCH_TM,),
            in_specs=[pl.BlockSpec((DISPATCH_TM, D_MODEL), lambda i, p, lt, nt: (i, 0))],
            out_specs=pl.BlockSpec(memory_space=pl.ANY),
            scratch_shapes=[pltpu.VMEM((MOE_TILE, D_MODEL), F32), pltpu.VMEM((2, DISPATCH_TM, D_MODEL), F32),
                            pltpu.SemaphoreType.DMA(()), pltpu.SemaphoreType.DMA((2,))]),
        out_shape=jax.ShapeDtypeStruct((MOE_ROWS, D_MODEL), F32),
        compiler_params=_cparams(("arbitrary",)), name="moe_dispatch",
    )(pos, last_tile, n_tiles, h)

    n_chunk = (n_sub + MOE_CHUNK - 1) // MOE_CHUNK
    chunk_end = jnp.cumsum(n_chunk)
    cidx = jnp.arange(MOE_CHUNKS)
    c_expert = jnp.minimum(jnp.sum(cidx[:, None] >= chunk_end[None, :], axis=1), N_EXPERTS - 1).astype(jnp.int32)
    c_k = cidx - (chunk_end - n_chunk)[c_expert]
    c_tile0 = (start[c_expert] + c_k * MOE_CHUNK).astype(jnp.int32)
    c_ntiles = jnp.clip(n_sub[c_expert] - c_k * MOE_CHUNK, 0, MOE_CHUNK).astype(jnp.int32)
    n_chunks = chunk_end[N_EXPERTS - 1:]

    nf = D_FF_EXPERT // MOE_TF

    def w_idx(c, f, ce, nch):
        live = c < nch[0]
        return ce[jnp.minimum(c, nch[0] - 1)], jnp.where(live, f, nf - 1)

    def up_map(c, f, ce, ct, cn, nch, nt):
        e, ff = w_idx(c, f, ce, nch)
        return (e, 0, ff)

    def down_map(c, f, ce, ct, cn, nch, nt):
        e, ff = w_idx(c, f, ce, nch)
        return (e, ff, 0)

    y = pl.pallas_call(
        _expert_kernel,
        grid_spec=pltpu.PrefetchScalarGridSpec(
            num_scalar_prefetch=5, grid=(MOE_CHUNKS, nf),
            in_specs=[pl.BlockSpec(memory_space=pl.ANY),
                      pl.BlockSpec((None, D_MODEL, MOE_TF), up_map),
                      pl.BlockSpec((None, D_MODEL, MOE_TF), up_map),
                      pl.BlockSpec((None, MOE_TF, D_MODEL), down_map)],
            out_specs=pl.BlockSpec(memory_space=pl.ANY),
            scratch_shapes=[pltpu.VMEM((MOE_CHUNK, MOE_TILE, D_MODEL), F32),
                            pltpu.VMEM((MOE_CHUNK, MOE_TILE, D_MODEL), BF16),
                            pltpu.VMEM((2, MOE_TILE, D_MODEL), F32),
                            pltpu.VMEM((D_MODEL, MOE_TF), BF16), pltpu.VMEM((D_MODEL, MOE_TF), BF16),
                            pltpu.VMEM((MOE_TF, D_MODEL), BF16),
                            pltpu.SemaphoreType.DMA((2,)), pltpu.SemaphoreType.DMA(())]),
        out_shape=jax.ShapeDtypeStruct((MOE_ROWS, D_MODEL), F32),
        compiler_params=_cparams(("arbitrary", "arbitrary")), name="moe_experts",
    )(c_expert, c_tile0, c_ntiles, n_chunks, n_tiles, xs, wg, wu, wd)

    tm = COMBINE_TM
    return pl.pallas_call(
        _combine_kernel,
        grid_spec=pltpu.PrefetchScalarGridSpec(
            num_scalar_prefetch=1, grid=(N_TOK // tm,),
            in_specs=[pl.BlockSpec((tm, LANES), lambda i, p: (i, 0)),
                      pl.BlockSpec((tm, D_MODEL), lambda i, p: (i, 0)),
                      pl.BlockSpec((None, None, 1, D_MODEL), lambda i, p: (l, _cond_row(i, tm), 0, 5)),
                      pl.BlockSpec(memory_space=pl.ANY)],
            out_specs=[pl.BlockSpec((tm, D_MODEL), lambda i, p: (jnp.minimum(i, N_PROMPT // tm - 1), 0)),
                       pl.BlockSpec((tm, D_MODEL), lambda i, p: (jnp.maximum(i - N_PROMPT // tm, 0), 0))],
            scratch_shapes=[pltpu.VMEM((2, TOP_K, tm, D_MODEL), F32), pltpu.SemaphoreType.DMA((2,))]),
        out_shape=[jax.ShapeDtypeStruct((N_PROMPT, D_MODEL), F32), jax.ShapeDtypeStruct((N_SAMPLE, D_MODEL), F32)],
        compiler_params=_cparams(("arbitrary",)), name="moe_combine",
    )(pos, route, x, mod, y)


def _kv_leaf_kernel(*refs):
    ins, outs = refs[:4 * DEPTH], refs[4 * DEPTH:]
    for l in range(DEPTH):
        for j in range(4):
            outs[j][l] = ins[4 * l + j][...]


def _kv_leaves(per_layer):
    widths = (A_KV, A_KV, C_W, C_W)
    return pl.pallas_call(
        _kv_leaf_kernel, grid=(BATCH,),
        in_specs=[pl.BlockSpec((SEQ, w), lambda b: (b, 0)) for _ in range(DEPTH) for w in widths],
        out_specs=[pl.BlockSpec((None, DEPTH, SEQ, w), lambda b: (b, 0, 0, 0)) for w in widths],
        out_shape=[jax.ShapeDtypeStruct((BATCH, DEPTH, SEQ, w), F32) for w in widths],
        compiler_params=_cparams(("parallel",)), name="kv_leaves",
    )(*[a for layer in per_layer for a in layer])


def _rope_tables():
    t = jnp.arange(DEC_SEQ)
    row = (t // GRID_W).astype(F32)
    col = (t % GRID_W).astype(F32)
    n_freq = HEAD_DIM // 4
    inv = ROPE_BASE ** (-jnp.arange(n_freq, dtype=F32) / n_freq)
    ang = jnp.concatenate([row[:, None] * inv, col[:, None] * inv], axis=-1)
    cos, sin = jnp.cos(ang), jnp.sin(ang)
    cos_h = jnp.concatenate([cos, cos], axis=-1)
    sin_h = jnp.concatenate([-sin, sin], axis=-1)
    cos_l = jnp.tile(jnp.concatenate([cos_h, cos_h], axis=-1), (DEC_BATCH, 1))
    sin_l = jnp.tile(jnp.concatenate([sin_h, sin_h], axis=-1), (DEC_BATCH, 1))
    cos_t = jnp.concatenate([jnp.ones((N_PROMPT, LANES), F32), cos_l], axis=0)
    sin_t = jnp.concatenate([jnp.zeros((N_PROMPT, LANES), F32), sin_l], axis=0)
    return cos_t, sin_t


def kernel(x_prompt, x_sample, cache_a_k, cache_a_v, cache_c_k, cache_c_v, c, c_ctx, w_mod, b_mod, norm1_g, norm2_g, w_in, qk_norm_a, qk_norm_c, sink_a, rpb_c, w_branch_a, w_branch_b, w_branch_c, w_out, w_ff_gate, w_ff_up, w_ff_down, w_router, w_exp_gate, w_exp_up, w_exp_down):
    x = (x_prompt.reshape(N_PROMPT, D_MODEL), x_sample.reshape(N_SAMPLE, D_MODEL))
    cond =jnp.concatenate([c_ctx[None, :], c], axis=0)
    cond_t = jnp.broadcast_to(cond[:, :, None], (N_COND, D_MODEL, LANES))
    mod = _modulation(cond_t, w_mod, b_mod)
    cos_t, sin_t = _rope_tables()
    bias = _nbr_bias_tables(rpb_c)
    ck_a = cache_a_k.reshape(DEC_BATCH, DEPTH, PAST_LEN, A_KV)
    cv_a = cache_a_v.reshape(DEC_BATCH, DEPTH, PAST_LEN, A_KV)
    ck_c = cache_c_k.reshape(DEC_BATCH, DEPTH, PAST_LEN, C_W)
    cv_c = cache_c_v.reshape(DEC_BATCH, DEPTH, PAST_LEN, C_W)

    new_kv = []
    h = x
    for l in range(DEPTH):
        qa, ka, va, fb, qc, kc, vc, gates = _in_proj(h, w_in, cos_t, sin_t, qk_norm_a[l], qk_norm_c[l], l,
                                                     norm=(norm1_g[l], mod) if l == 0 else None)
        oa_p, oc_p = _ctx_attn(sink_a, qa, ka, va, qc, kc, vc, l)
        oa_s = _win_attn(sink_a, qa, ka, va, ck_a, cv_a, l)
        oc_s = _nbr_attn(qc, kc, vc, ck_c, cv_c, bias, l)
        ob_p = _fourier(fb, BATCH, SEQ, 0, SEQ)
        ob_s = _fourier(fb, DEC_BATCH, DEC_SEQ, N_PROMPT, 512)
        branches = ((oa_p, ob_p, oc_p), (oa_s, ob_s, oc_s), gates, x,
                    w_branch_a, w_branch_b, w_branch_c, w_out, mod, l, norm2_g[l])
        i = l // 2
        if l % 2 == 0:
            x, h2 = _merge(*branches)
            x, h = _ffn(h2, x, w_ff_gate, w_ff_up, w_ff_down, mod, l, norm1_g[l + 1])
        else:
            x, h2, logits = _merge(*branches, w_router=w_router[i])
            xp, xs = _moe(h2, x, logits, w_exp_gate[i], w_exp_up[i], w_exp_down[i], mod, l)
        new_kv.append((ka, va, kc, vc))

    new_ak, new_av, new_ck, new_cv = _kv_leaves(new_kv)
    return (xp.reshape(BATCH, SEQ, D_MODEL), xs.reshape(DEC_BATCH, DEC_SEQ, D_MODEL),
            new_ak.reshape(BATCH, DEPTH, SEQ, A_KV_HEADS, HEAD_DIM),
            new_av.reshape(BATCH, DEPTH, SEQ, A_KV_HEADS, HEAD_DIM),
            new_ck.reshape(BATCH, DEPTH, SEQ, C_HEADS, HEAD_DIM),
            new_cv.reshape(BATCH, DEPTH, SEQ, C_HEADS, HEAD_DIM))
```

```python
import functools

import numpy as np
import jax
import jax.numpy as jnp
from jax import lax
from jax.experimental import pallas as pl
from jax.experimental.pallas import tpu as pltpu

F32 = jnp.float32
BF16 = jnp.bfloat16

D_MODEL = 1024
BATCH = 16
SEQ = 256
DEPTH = 2
DEC_BATCH = 2
DEC_SEQ = 2048
PAST_LEN = 512
GRID_W = 64
HEAD_DIM = 64
SCALE = HEAD_DIM ** -0.5
A_HEADS = 8
A_KV_HEADS = 2
A_GROUP = A_HEADS // A_KV_HEADS
A_WINDOW = 128
A_BLOCK = 128
B_GROUPS = 8
B_GROUP_DIM = 64
B_WIDTH = B_GROUPS * B_GROUP_DIM
C_HEADS = 8
C_WIN_ROWS = 8
C_WIN_COLS = 16
A_Q = A_HEADS * HEAD_DIM
A_KV = A_KV_HEADS * HEAD_DIM
C_W = C_HEADS * HEAD_DIM
QKV_WIDTH = A_Q + 2 * A_KV + B_WIDTH + 3 * C_W
N_BRANCH = 3
GATE_WIDTH = N_BRANCH * D_MODEL
IN_WIDTH = QKV_WIDTH + GATE_WIDTH
D_FF = 2816
N_EXPERTS = 8
D_FF_EXPERT = 3584
ROPE_BASE = 10000.0
RMS_EPS = 1e-6
NEG_INF = -1e30

N_PROMPT = BATCH * SEQ
N_SAMPLE = DEC_BATCH * DEC_SEQ
N_TOK = N_PROMPT + N_SAMPLE
N_COND = 1 + DEC_BATCH
LANES = 128
NORM_SLAB = 256
C_QROWS = 4
C_QBLOCK = C_QROWS * GRID_W
C_DR_SLOTS = 2 * C_WIN_ROWS
VMEM_LIMIT = 56 * 1024 * 1024


def _cparams(sem):
    return pltpu.CompilerParams(dimension_semantics=sem, vmem_limit_bytes=VMEM_LIMIT)


def _sigmoid(x):
    return 1.0 / (1.0 + jnp.exp(-x))


def _cond_row(tile, tm):
    return jnp.maximum(tile * tm // DEC_SEQ - 1, 0)


def _mod_kernel(ct_ref, w_ref, b_ref, o_ref, silu_ref):
    @pl.when((pl.program_id(0) == 0) & (pl.program_id(1) == 0))
    def _():
        cb = ct_ref[...]
        silu_ref[...] = cb * _sigmoid(cb)

    tn = w_ref.shape[1]
    for r in range(N_COND):
        s = silu_ref[r]
        for cc in range(tn // LANES):
            sl = slice(cc * LANES, (cc + 1) * LANES)
            o_ref[r, :, sl] = jnp.sum(w_ref[:, sl] * s, axis=0, keepdims=True) + b_ref[:, sl]


def _modulation(cond_t, w_mod, b_mod):
    tn = 1024
    n = 6 * D_MODEL
    return pl.pallas_call(
        _mod_kernel,
        grid=(DEPTH, n // tn),
        in_specs=[
            pl.BlockSpec((N_COND, D_MODEL, LANES), lambda l, j: (0, 0, 0)),
            pl.BlockSpec((None, D_MODEL, tn), lambda l, j: (l, 0, j)),
            pl.BlockSpec((None, 1, tn), lambda l, j: (l, 0, j)),
        ],
        out_specs=pl.BlockSpec((None, N_COND, 1, tn), lambda l, j: (l, 0, 0, j)),
        out_shape=jax.ShapeDtypeStruct((DEPTH, N_COND, 1, n), F32),
        scratch_shapes=[pltpu.VMEM((N_COND, D_MODEL, LANES), F32)],
        compiler_params=_cparams(("arbitrary", "arbitrary")),
        name="modulation",
    )(cond_t, w_mod, b_mod.reshape(DEPTH, 1, n))


def _adaln_math(x, g, sh, sc):
    ms = jnp.mean(x * x, axis=-1, keepdims=True)
    y = x * lax.rsqrt(ms + RMS_EPS) * g
    return y * (1.0 + sc) + sh


def _split_rows(tm, width, joined=False):
    n_p = N_PROMPT // tm
    latent0 = n_p if joined else 0
    return (pl.BlockSpec((tm, width), lambda i: (jnp.minimum(i, n_p - 1), 0)),
            pl.BlockSpec((tm, width), lambda i: (jnp.maximum(i - n_p, 0) + latent0, 0)))


def _split_operands(x):
    return (x, False) if isinstance(x, tuple) else ((x, x), True)


def _read_split(p_ref, s_ref):
    is_ctx = pl.program_id(0) < N_PROMPT // p_ref.shape[0]
    return jnp.where(is_ctx, p_ref[...], s_ref[...])


WEIGHT_ROW_CHUNK = 128


def _load_weight_bf16(src_rows, dst_ref, stage_ref, sem):
    n = dst_ref.shape[0] // WEIGHT_ROW_CHUNK

    def chunk_copy(k):
        return pltpu.make_async_copy(src_rows(k), stage_ref.at[k % 2], sem.at[k % 2])

    chunk_copy(0).start()
    for k in range(n):
        if k + 1 < n:
            chunk_copy(k + 1).start()
        chunk_copy(k).wait()
        dst_ref[k * WEIGHT_ROW_CHUNK:(k + 1) * WEIGHT_ROW_CHUNK, :] = stage_ref[k % 2].astype(BF16)


def _row_chunk(k):
    return pl.ds(k * WEIGHT_ROW_CHUNK, WEIGHT_ROW_CHUNK)


def _head_norm(x, gain, bd):
    sq = x * x
    hi = sq.astype(BF16)
    lo = (sq - hi.astype(F32)).astype(BF16)
    ms = jnp.dot(hi, bd, preferred_element_type=F32) + jnp.dot(lo, bd, preferred_element_type=F32)
    return x * lax.rsqrt(ms + RMS_EPS) * gain


def _rope(x, cos, sin_signed, first_half):
    half = HEAD_DIM // 2
    swapped = jnp.where(first_half, pltpu.roll(x, x.shape[1] - half, 1), pltpu.roll(x, half, 1))
    return x * cos + swapped * sin_signed


def _in_proj_kernel(*refs, l, fused_norm):
    n_lead = 5 if fused_norm else 1
    (cos_ref, sin_ref, ga_ref, gc_ref, w_hbm, qa_ref, ka_ref, va_ref, fb_ref, qc_ref, kc_ref, vc_ref, gt_ref,
     wb_ref, stage_ref, sem) = refs[n_lead:]

    @pl.when(pl.program_id(0) == 0)
    def _():
        _load_weight_bf16(lambda k: w_hbm.at[l, _row_chunk(k)], wb_ref, stage_ref, sem)

    if fused_norm:
        xp_ref, xs_ref, ng_ref, sh_ref, sc_ref = refs[:n_lead]
        h = _adaln_math(_read_split(xp_ref, xs_ref), ng_ref[...], sh_ref[...], sc_ref[...]).astype(BF16)
    else:
        h = refs[0][...]

    def proj(off, width):
        return jnp.dot(h, wb_ref[:, off:off + width], preferred_element_type=F32)

    r = lax.broadcasted_iota(jnp.int32, (NORM_SLAB, NORM_SLAB), 0) // HEAD_DIM
    c = lax.broadcasted_iota(jnp.int32, (NORM_SLAB, NORM_SLAB), 1) // HEAD_DIM
    bd = jnp.where(r == c, 1.0 / HEAD_DIM, 0.0).astype(BF16)
    lane = lax.broadcasted_iota(jnp.int32, (1, NORM_SLAB), 1)
    first_half = (lane % HEAD_DIM) < HEAD_DIM // 2
    cos = jnp.concatenate([cos_ref[...]] * (NORM_SLAB // LANES), axis=1)
    sin = jnp.concatenate([sin_ref[...]] * (NORM_SLAB // LANES), axis=1)
    gqa, gka = ga_ref[0:1, :], ga_ref[1:2, :]
    gqc, gkc = gc_ref[0:1, :], gc_ref[1:2, :]
    slabs = lambda p: [p[:, s * NORM_SLAB:(s + 1) * NORM_SLAB] for s in range(p.shape[1] // NORM_SLAB)]
    cat = lambda parts: jnp.concatenate(parts, axis=1)

    off = 0
    qa_ref[...] = cat([_rope(_head_norm(x, gqa, bd), cos, sin, first_half) for x in slabs(proj(off, A_Q))]
                      ).astype(BF16)
    off += A_Q
    kv = proj(off, 2 * A_KV)
    ka_ref[...] = _rope(_head_norm(kv, gka, bd), cos, sin, first_half)[:, :A_KV]
    va_ref[...] = kv[:, A_KV:]
    off += 2 * A_KV
    fb_ref[...] = proj(off, B_WIDTH).astype(BF16)
    off += B_WIDTH
    qc_ref[...] = cat([_head_norm(x, gqc, bd) for x in slabs(proj(off, C_W))]).astype(BF16)
    off += C_W
    kc_ref[...] = cat([_head_norm(x, gkc, bd) for x in slabs(proj(off, C_W))])
    off += C_W
    vc_ref[...] = proj(off, C_W)
    off += C_W
    for j in range(N_BRANCH):
        cols = slice(j * D_MODEL, (j + 1) * D_MODEL)
        gt_ref[:, cols] = _sigmoid(proj(off + j * D_MODEL, D_MODEL)).astype(BF16)


def _in_proj(h, w_in, cos_t, sin_t, qk_a, qk_c, l, norm=None):
    tm = 512
    row = lambda w: pl.BlockSpec((tm, w), lambda i: (i, 0))
    widths = (A_Q, A_KV, A_KV, B_WIDTH, C_W, C_W, C_W, GATE_WIDTH)
    dtypes = (BF16, F32, F32, BF16, BF16, F32, F32, BF16)
    if norm is None:
        lead_specs, lead = [row(D_MODEL)], [h]
    else:
        norm_g, mod = norm
        x, joined = _split_operands(h)
        mspec = lambda which: pl.BlockSpec((None, None, 1, D_MODEL), lambda i: (l, _cond_row(i, tm), 0, which))
        lead_specs = [*_split_rows(tm, D_MODEL, joined), pl.BlockSpec((1, D_MODEL), lambda i: (0, 0)),
                      mspec(0), mspec(1)]
        lead = [*x, norm_g.reshape(1, D_MODEL), mod, mod]
    return pl.pallas_call(
        functools.partial(_in_proj_kernel, l=l, fused_norm=norm is not None), grid=(N_TOK // tm,),
        in_specs=[*lead_specs, row(LANES), row(LANES),
                  pl.BlockSpec((2, NORM_SLAB), lambda i: (0, 0)), pl.BlockSpec((2, NORM_SLAB), lambda i: (0, 0)),
                  pl.BlockSpec(memory_space=pl.ANY)],
        out_specs=[row(w) for w in widths],
        out_shape=[jax.ShapeDtypeStruct((N_TOK, w), d) for w, d in zip(widths, dtypes)],
        scratch_shapes=[pltpu.VMEM((D_MODEL, IN_WIDTH), BF16),
                        pltpu.VMEM((2, WEIGHT_ROW_CHUNK, IN_WIDTH), F32), pltpu.SemaphoreType.DMA((2,))],
        compiler_params=_cparams(("arbitrary",)), name="in_proj",
    )(*lead, cos_t, sin_t, jnp.tile(qk_a, (1, NORM_SLAB // HEAD_DIM)), jnp.tile(qk_c, (1, NORM_SLAB // HEAD_DIM)), w_in)


def _nt_dot(a, b):
    return lax.dot_general(a, b, (((1,), (1,)), ((), ())), preferred_element_type=F32)


def _head(x, h):
    return x[:, h * HEAD_DIM:(h + 1) * HEAD_DIM]


def _stacked_softmax(parts, sink):
    m = parts[0].max(axis=-1, keepdims=True)
    for s in parts[1:]:
        m = jnp.maximum(m, s.max(axis=-1, keepdims=True))
    if sink is not None:
        m = jnp.maximum(m, sink)
    den = jnp.exp(sink - m) if sink is not None else 0.0
    es = []
    for s in parts:
        e = jnp.exp(s - m)
        den = den + e.sum(axis=-1, keepdims=True)
        es.append(e.astype(BF16))
    return es, 1.0 / den


def _sink_column(sink_ref, l, rows_per_head):
    return jnp.concatenate([jnp.full((rows_per_head, 1), sink_ref[l, h], F32) for h in range(A_HEADS)], axis=0)


def _gqa_queries(qa, g):
    return jnp.concatenate([_head(qa, g * A_GROUP + i) for i in range(A_GROUP)], axis=0)


def _ctx_attn_kernel(sink_ref, qa_ref, ka_ref, va_ref, qc_ref, kc_ref, vc_ref, oa_ref, oc_ref, *, l):
    t = SEQ
    for bb in range(CTX_BATCHES):
        r = slice(bb * t, (bb + 1) * t)
        qa = qa_ref[r, :] * SCALE
        ka = ka_ref[r, :].astype(BF16)
        va = va_ref[r, :].astype(BF16)
        s = jnp.concatenate([_nt_dot(_gqa_queries(qa, g), _head(ka, g)) for g in range(A_KV_HEADS)], axis=0)
        (e,), inv = _stacked_softmax([s], _sink_column(sink_ref, l, t))
        outs = []
        for g in range(A_KV_HEADS):
            rows = slice(g * A_GROUP * t, (g + 1) * A_GROUP * t)
            o = jnp.dot(e[rows], _head(va, g), preferred_element_type=F32) * inv[rows]
            outs += [o[i * t:(i + 1) * t] for i in range(A_GROUP)]
        oa_ref[r, :] = jnp.concatenate(outs, axis=1).astype(BF16)

        qc = qc_ref[r, :] * SCALE
        kc = kc_ref[r, :].astype(BF16)
        vc = vc_ref[r, :].astype(BF16)
        s = jnp.concatenate([_nt_dot(_head(qc, h), _head(kc, h)) for h in range(C_HEADS)], axis=0)
        (e,), inv = _stacked_softmax([s], None)
        outs = [jnp.dot(e[h * t:(h + 1) * t], _head(vc, h), preferred_element_type=F32) * inv[h * t:(h + 1) * t]
                for h in range(C_HEADS)]
        oc_ref[r, :] = jnp.concatenate(outs, axis=1).astype(BF16)


CTX_BATCHES = 4


def _ctx_attn(sink_a, qa, ka, va, qc, kc, vc, l):
    blk = lambda w: pl.BlockSpec((CTX_BATCHES * SEQ, w), lambda b: (b, 0))
    return pl.pallas_call(
        functools.partial(_ctx_attn_kernel, l=l), grid=(BATCH // CTX_BATCHES,),
        in_specs=[pl.BlockSpec(memory_space=pltpu.SMEM),
                  blk(A_Q), blk(A_KV), blk(A_KV), blk(C_W), blk(C_W), blk(C_W)],
        out_specs=[blk(A_Q), blk(C_W)],
        out_shape=[jax.ShapeDtypeStruct((N_PROMPT, A_Q), BF16), jax.ShapeDtypeStruct((N_PROMPT, C_W), BF16)],
        compiler_params=_cparams(("parallel",)), name="ctx_attn",
    )(sink_a, qa, ka, va, qc, kc, vc)


def _win_attn_kernel(sink_ref, q_ref, kp_ref, kc_ref, kn_ref, vp_ref, vc_ref, vn_ref, ck_ref, cv_ref, mask_ref,
                     o_ref, *, l):
    rows = A_GROUP * A_BLOCK
    mask = mask_ref[...]
    sink = _sink_column(sink_ref, l, A_BLOCK)
    for g in range(A_KV_HEADS):
        sl = slice(g * HEAD_DIM, (g + 1) * HEAD_DIM)
        q = jnp.concatenate([q_ref[:, (g * A_GROUP + i) * HEAD_DIM:(g * A_GROUP + i + 1) * HEAD_DIM]
                             for i in range(A_GROUP)], axis=0) * SCALE
        k_loc = jnp.concatenate([kp_ref[:, sl], kc_ref[:, sl], kn_ref[:, sl]], axis=0).astype(BF16)
        v_loc = jnp.concatenate([vp_ref[:, sl], vc_ref[:, sl], vn_ref[:, sl]], axis=0).astype(BF16)
        s_loc = _nt_dot(q, k_loc) + mask
        s_ctx = _nt_dot(q, ck_ref[:, sl].astype(BF16))
        (e_loc, e_ctx), inv = _stacked_softmax([s_loc, s_ctx], sink[g * rows:(g + 1) * rows])
        o = (jnp.dot(e_loc, v_loc, preferred_element_type=F32)
             + jnp.dot(e_ctx, cv_ref[:, sl].astype(BF16), preferred_element_type=F32)) * inv
        for i in range(A_GROUP):
            h = g * A_GROUP + i
            o_ref[:, h * HEAD_DIM:(h + 1) * HEAD_DIM] = o[i * A_BLOCK:(i + 1) * A_BLOCK].astype(BF16)


def _win_attn(sink_a, qa, ka, va, cache_k, cache_v, l):
    nb = DEC_SEQ // A_BLOCK
    base = N_PROMPT // A_BLOCK

    def nbr(d):
        return lambda b, t: (base + b * nb + jnp.clip(t + d, 0, nb - 1), 0)

    kv = lambda d: pl.BlockSpec((A_BLOCK, A_KV), nbr(d))
    cache = pl.BlockSpec((None, None, PAST_LEN, A_KV), lambda b, t: (b, l, 0, 0))
    qi = np.arange(A_GROUP * A_BLOCK)[:, None] % A_BLOCK
    kj = np.arange(3 * A_BLOCK)[None, :] - A_BLOCK
    band = np.abs(kj - qi) <= A_WINDOW
    masks = np.stack([band & (kj >= 0), band, band & (kj < A_BLOCK)])
    masks = jnp.asarray(np.where(masks, 0.0, NEG_INF).astype(np.float32))
    position = lambda b, t: (jnp.where(t == 0, 0, jnp.where(t == nb - 1, 2, 1)), 0, 0)
    return pl.pallas_call(
        functools.partial(_win_attn_kernel, l=l), grid=(DEC_BATCH, nb),
        in_specs=[pl.BlockSpec(memory_space=pltpu.SMEM),
                  pl.BlockSpec((A_BLOCK, A_Q), nbr(0)),
                  kv(-1), kv(0), kv(1), kv(-1), kv(0), kv(1), cache, cache,
                  pl.BlockSpec((None, A_GROUP * A_BLOCK, 3 * A_BLOCK), position)],
        out_specs=pl.BlockSpec((A_BLOCK, A_Q), lambda b, t: (b * nb + t, 0)),
        out_shape=jax.ShapeDtypeStruct((N_SAMPLE, A_Q), BF16),
        compiler_params=_cparams(("parallel", "parallel")), name="win_attn",
    )(sink_a, qa, ka, ka, ka, va, va, va, cache_k, cache_v, masks)


def _nbr_attn_kernel(q_ref, kp_ref, kc_ref, kn_ref, vp_ref, vc_ref, vn_ref, ck_ref, cv_ref, tab_ref, o_ref,
                     bias_ref):
    j = pl.program_id(0)
    nb = pl.num_programs(0)
    slots = _nbr_row_slots()

    def build(cls):
        for h in range(C_HEADS):
            for qr in range(C_QROWS):
                for kk in range(3 * C_QROWS):
                    bias_ref[h, qr * GRID_W:(qr + 1) * GRID_W, kk * GRID_W:(kk + 1) * GRID_W] = (
                        tab_ref[h, slots[cls][qr][kk]])

    first_of_batch = pl.program_id(1) == 0
    for cls, at in enumerate((0, 1, nb - 1)):
        @pl.when(first_of_batch & (j == at))
        def _():
            build(cls)

    tq = C_QBLOCK
    q = q_ref[...] * SCALE
    k_loc = jnp.concatenate([kp_ref[...], kc_ref[...], kn_ref[...]], axis=0).astype(BF16)
    v_loc = jnp.concatenate([vp_ref[...], vc_ref[...], vn_ref[...]], axis=0).astype(BF16)
    k_ctx = ck_ref[...].astype(BF16)
    v_ctx = cv_ref[...].astype(BF16)
    s_loc = jnp.concatenate([_nt_dot(_head(q, h), _head(k_loc, h)) for h in range(C_HEADS)], axis=0)
    s_loc = s_loc + bias_ref[...].reshape(C_HEADS * tq, 3 * tq)
    s_ctx = jnp.concatenate([_nt_dot(_head(q, h), _head(k_ctx, h)) for h in range(C_HEADS)], axis=0)
    (e_loc, e_ctx), inv = _stacked_softmax([s_loc, s_ctx], None)
    outs = []
    for h in range(C_HEADS):
        r = slice(h * tq, (h + 1) * tq)
        outs.append((jnp.dot(e_loc[r], _head(v_loc, h), preferred_element_type=F32)
                     + jnp.dot(e_ctx[r], _head(v_ctx, h), preferred_element_type=F32)) * inv[r])
    o_ref[...] = jnp.concatenate(outs, axis=1).astype(BF16)


def _nbr_attn(qc, kc, vc, cache_k, cache_v, bias, l):
    nb = DEC_SEQ // C_QBLOCK
    base = N_PROMPT // C_QBLOCK

    def nbr(d):
        return lambda j, b: (base + b * nb + jnp.clip(j + d, 0, nb - 1), 0)

    kv = lambda d: pl.BlockSpec((C_QBLOCK, C_W), nbr(d))
    cache = pl.BlockSpec((None, None, PAST_LEN, C_W), lambda j, b: (b, l, 0, 0))
    return pl.pallas_call(
        _nbr_attn_kernel, grid=(nb, DEC_BATCH),
        in_specs=[kv(0), kv(-1), kv(0), kv(1), kv(-1), kv(0), kv(1), cache, cache,
                  pl.BlockSpec((None, C_HEADS, C_DR_SLOTS, GRID_W, GRID_W), lambda j, b: (l, 0, 0, 0, 0))],
        out_specs=pl.BlockSpec((C_QBLOCK, C_W), lambda j, b: (b * nb + j, 0)),
        out_shape=jax.ShapeDtypeStruct((N_SAMPLE, C_W), BF16),
        scratch_shapes=[pltpu.VMEM((C_HEADS, C_QBLOCK, 3 * C_QBLOCK), F32)],
        compiler_params=_cparams(("arbitrary", "arbitrary")), name="nbr_attn",
    )(qc, kc, kc, kc, vc, vc, vc, cache_k, cache_v, bias)


def _nbr_bias_tables(rpb):
    qcol = np.arange(GRID_W)
    qcs = np.clip(qcol - C_WIN_COLS // 2, 0, GRID_W - C_WIN_COLS)
    kcol = np.arange(GRID_W)
    col_ok = (kcol[None, :] >= qcs[:, None]) & (kcol[None, :] < qcs[:, None] + C_WIN_COLS)
    dc = np.clip(kcol[None, :] - qcol[:, None], -(C_WIN_COLS - 1), C_WIN_COLS - 1) + C_WIN_COLS - 1
    onehot_dc = (dc.reshape(-1)[None, :] == np.arange(2 * C_WIN_COLS - 1)[:, None]).astype(np.float32)
    t = jnp.einsum('lhab,bx->lhax', rpb, jnp.asarray(onehot_dc), precision=lax.Precision.HIGHEST)
    t = jnp.where(jnp.asarray(col_ok.reshape(-1)), t, NEG_INF)
    t = jnp.concatenate([t, jnp.full((DEPTH, C_HEADS, 1, GRID_W * GRID_W), NEG_INF, F32)], axis=2)
    return t.reshape(DEPTH, C_HEADS, C_DR_SLOTS, GRID_W, GRID_W)


def _nbr_row_slots():
    rows = DEC_SEQ // GRID_W
    slots = []
    for j in (0, 3, rows // C_QROWS - 1):
        per_q = []
        for qr in range(C_QROWS):
            r = C_QROWS * j + qr
            rs = min(max(r - C_WIN_ROWS // 2, 0), rows - C_WIN_ROWS)
            per_k = []
            for kk in range(3 * C_QROWS):
                kabs = C_QROWS * (j - 1) + kk
                per_k.append(kabs - r + C_WIN_ROWS - 1 if rs <= kabs < rs + C_WIN_ROWS else C_DR_SLOTS - 1)
            per_q.append(per_k)
        slots.append(per_q)
    return slots


def _fourier_kernel(u_ref, bc_ref, bs_ref, cl_ref, sl_ref, o_ref, zc_ref, zs_ref):
    @pl.when(pl.program_id(1) == 0)
    def _():
        u = u_ref[...]
        zc_ref[...] = jnp.dot(u, bc_ref[...].astype(BF16), preferred_element_type=F32).astype(BF16)
        zs_ref[...] = jnp.dot(u, bs_ref[...].astype(BF16), preferred_element_type=F32).astype(BF16)

    o = (jnp.dot(cl_ref[...].astype(BF16), zc_ref[...], preferred_element_type=F32)
         - jnp.dot(sl_ref[...].astype(BF16), zs_ref[...], preferred_element_type=F32))
    o_ref[...] = o.astype(BF16)


def _dft_tables(n):
    k = np.arange(n)
    ang = 2.0 * np.pi * ((k[:, None] * k[None, :]) % n) / n
    return np.cos(ang) / np.sqrt(n), np.sin(ang) / np.sqrt(n)


def _channel_dft_tables():
    c, s = _dft_tables(B_GROUP_DIM)
    eye = np.eye(B_GROUPS)
    return np.kron(eye, c).astype(np.float32), np.kron(eye, s).astype(np.float32)


def _fourier(fb, n_batch, seq, row0, tr):
    cl, sl = (jnp.asarray(a.astype(np.float32)) for a in _dft_tables(seq))
    bc, bs = (jnp.asarray(a) for a in _channel_dft_tables())
    nt = seq // tr
    const = pl.BlockSpec((B_WIDTH, B_WIDTH), lambda b, t: (0, 0))
    return pl.pallas_call(
        _fourier_kernel, grid=(n_batch, nt),
        in_specs=[pl.BlockSpec((seq, B_WIDTH), lambda b, t: (row0 // seq + b, 0)), const, const,
                  pl.BlockSpec((tr, seq), lambda b, t: (t, 0)), pl.BlockSpec((tr, seq), lambda b, t: (t, 0))],
        out_specs=pl.BlockSpec((tr, B_WIDTH), lambda b, t: (b * nt + t, 0)),
        out_shape=jax.ShapeDtypeStruct((n_batch * seq, B_WIDTH), BF16),
        scratch_shapes=[pltpu.VMEM((seq, B_WIDTH), BF16), pltpu.VMEM((seq, B_WIDTH), BF16)],
        compiler_params=_cparams(("parallel", "arbitrary")), name=f"fourier_{seq}",
    )(fb, bc, bs, cl, sl)


def _merge_kernel(oap_ref, obp_ref, ocp_ref, oas_ref, obs_ref, ocs_ref, gt_ref, xp_ref, xs_ref,
                  wa_ref, wb_ref, wc_ref, wo_ref, g1_ref, ng_ref, sh_ref, sc_ref, *rest, router):
    if router:
        wr_ref, o_ref, h2_ref, lg_ref, wab_ref, wbb_ref, wcb_ref, wob_ref = rest
    else:
        o_ref, h2_ref, wab_ref, wbb_ref, wcb_ref, wob_ref = rest

    @pl.when(pl.program_id(0) == 0)
    def _():
        wab_ref[...] = wa_ref[...].astype(BF16)
        wbb_ref[...] = wb_ref[...].astype(BF16)
        wcb_ref[...] = wc_ref[...].astype(BF16)
        wob_ref[...] = wo_ref[...].astype(BF16)

    ya = jnp.dot(_read_split(oap_ref, oas_ref), wab_ref[...], preferred_element_type=F32)
    yb = jnp.dot(_read_split(obp_ref, obs_ref), wbb_ref[...], preferred_element_type=F32)
    yc = jnp.dot(_read_split(ocp_ref, ocs_ref), wcb_ref[...], preferred_element_type=F32)
    d = D_MODEL
    m = (gt_ref[:, 0:d].astype(F32) * ya + gt_ref[:, d:2 * d].astype(F32) * yb
         + gt_ref[:, 2 * d:3 * d].astype(F32) * yc)
    y = jnp.dot(m.astype(BF16), wob_ref[...], preferred_element_type=F32)
    x_new = _read_split(xp_ref, xs_ref) + g1_ref[...] * y
    o_ref[...] = x_new
    h2 = _adaln_math(x_new, ng_ref[...], sh_ref[...], sc_ref[...])
    h2_ref[...] = h2.astype(h2_ref.dtype)
    if router:
        w = wr_ref[...]
        w_hi = w.astype(BF16)
        w_lo = (w - w_hi.astype(F32)).astype(BF16)
        h_hi = h2.astype(BF16)
        h_lo = (h2 - h_hi.astype(F32)).astype(BF16)
        hi_terms = jnp.dot(h_hi, jnp.concatenate([w_hi, w_lo], axis=1), preferred_element_type=F32)
        lg_ref[...] = (hi_terms[:, :LANES] + jnp.dot(h_lo, w_hi, preferred_element_type=F32)
                       + hi_terms[:, LANES:])


def _merge(branches_p, branches_s, gates, x, wa, wb, wc, wo, mod, l, norm_g, w_router=None):
    tm = 512
    router = w_router is not None
    x, joined = _split_operands(x)
    row = lambda w: pl.BlockSpec((tm, w), lambda i: (i, 0))
    row_p = lambda w: _split_rows(tm, w)[0]
    row_s = lambda w: _split_rows(tm, w)[1]
    const = lambda r, c: pl.BlockSpec((None, r, c), lambda i: (l, 0, 0), pipeline_mode=pl.Buffered(1))
    mspec = lambda which: pl.BlockSpec((None, None, 1, D_MODEL), lambda i: (l, _cond_row(i, tm), 0, which))
    in_specs = [row_p(A_Q), row_p(B_WIDTH), row_p(C_W), row_s(A_Q), row_s(B_WIDTH), row_s(C_W),
                row(GATE_WIDTH), *_split_rows(tm, D_MODEL, joined),
                const(A_Q, D_MODEL), const(B_WIDTH, D_MODEL), const(C_W, D_MODEL), const(D_MODEL, D_MODEL),
                mspec(2), pl.BlockSpec((1, D_MODEL), lambda i: (0, 0)), mspec(3), mspec(4)]
    operands = [*branches_p, *branches_s, gates, *x, wa, wb, wc, wo, mod, norm_g.reshape(1, D_MODEL), mod, mod]
    out_specs = [row(D_MODEL), row(D_MODEL)]
    out_shape = [jax.ShapeDtypeStruct((N_TOK, D_MODEL), F32),
                 jax.ShapeDtypeStruct((N_TOK, D_MODEL), F32 if router else BF16)]
    if router:
        in_specs.append(pl.BlockSpec((D_MODEL, LANES), lambda i: (0, 0)))
        operands.append(jnp.pad(w_router, ((0, 0), (0, LANES - N_EXPERTS))))
        out_specs.append(row(LANES))
        out_shape.append(jax.ShapeDtypeStruct((N_TOK, LANES), F32))
    return pl.pallas_call(
        functools.partial(_merge_kernel, router=router), grid=(N_TOK // tm,),
        in_specs=in_specs, out_specs=out_specs, out_shape=out_shape,
        scratch_shapes=[pltpu.VMEM((A_Q, D_MODEL), BF16), pltpu.VMEM((B_WIDTH, D_MODEL), BF16),
                        pltpu.VMEM((C_W, D_MODEL), BF16), pltpu.VMEM((D_MODEL, D_MODEL), BF16)],
        compiler_params=_cparams(("arbitrary",)), name="merge_router" if router else "merge",
    )(*operands)


FFN_COL_CHUNK = 256


def _ffn_kernel(h_ref, x_ref, g2_ref, ng_ref, sh_ref, sc_ref, wg_hbm, wu_hbm, wd_hbm, o_ref, hn_ref,
                wgb_ref, wub_ref, wdb_ref, stage_up_ref, stage_dn_ref, sem, *, i_dense):
    @pl.when(pl.program_id(0) == 0)
    def _():
        _load_weight_bf16(lambda k: wg_hbm.at[i_dense, _row_chunk(k)], wgb_ref, stage_up_ref, sem)
        _load_weight_bf16(lambda k: wu_hbm.at[i_dense, _row_chunk(k)], wub_ref, stage_up_ref, sem)
        _load_weight_bf16(lambda k: wd_hbm.at[i_dense, _row_chunk(k)], wdb_ref, stage_dn_ref, sem)

    h = h_ref[...]
    acc = None
    for c in range(D_FF // FFN_COL_CHUNK):
        cols = slice(c * FFN_COL_CHUNK, (c + 1) * FFN_COL_CHUNK)
        g = jnp.dot(h, wgb_ref[:, cols], preferred_element_type=F32)
        u = jnp.dot(h, wub_ref[:, cols], preferred_element_type=F32)
        a = (g * _sigmoid(g) * u).astype(BF16)
        d = jnp.dot(a, wdb_ref[cols, :], preferred_element_type=F32)
        acc = d if acc is None else acc + d
    x_new = x_ref[...] + g2_ref[...] * acc
    o_ref[...] = x_new
    hn_ref[...] = _adaln_math(x_new, ng_ref[...], sh_ref[...], sc_ref[...]).astype(BF16)


def _ffn(h, x, wg, wu, wd, mod, l, next_norm_g):
    tm = 512
    row = lambda dt: pl.BlockSpec((tm, D_MODEL), lambda i: (i, 0))
    mspec = lambda layer, which: pl.BlockSpec((None, None, 1, D_MODEL),
                                              lambda i: (layer, _cond_row(i, tm), 0, which))
    hbm = pl.BlockSpec(memory_space=pl.ANY)
    return pl.pallas_call(
        functools.partial(_ffn_kernel, i_dense=l // 2), grid=(N_TOK // tm,),
        in_specs=[row(BF16), row(F32), mspec(l, 5),
                  pl.BlockSpec((1, D_MODEL), lambda i: (0, 0)), mspec(l + 1, 0), mspec(l + 1, 1),
                  hbm, hbm, hbm],
        out_specs=[row(F32), row(BF16)],
        out_shape=[jax.ShapeDtypeStruct((N_TOK, D_MODEL), F32), jax.ShapeDtypeStruct((N_TOK, D_MODEL), BF16)],
        scratch_shapes=[pltpu.VMEM((D_MODEL, D_FF), BF16), pltpu.VMEM((D_MODEL, D_FF), BF16),
                        pltpu.VMEM((D_FF, D_MODEL), BF16),
                        pltpu.VMEM((2, WEIGHT_ROW_CHUNK, D_FF), F32),
                        pltpu.VMEM((2, WEIGHT_ROW_CHUNK, D_MODEL), F32), pltpu.SemaphoreType.DMA((2,))],
        compiler_params=_cparams(("arbitrary",)), name="ffn",
    )(h, x, mod, next_norm_g.reshape(1, D_MODEL), mod, mod, wg, wu, wd)


TOP_K = 2
MOE_TILE = 256
MOE_TILES = TOP_K * N_TOK // MOE_TILE + N_EXPERTS
MOE_ROWS = MOE_TILES * MOE_TILE
MOE_CHUNK = 10
MOE_CHUNKS = MOE_TILES // MOE_CHUNK + N_EXPERTS
MOE_TF = 896
ROUTE_TM = 512
DISPATCH_TM = 512
COMBINE_TM = 512
ROW_COPY_UNROLL = 16


def _route_kernel(lg_ref, o_ref, cnt_ref, base_ref, tri_ref):
    tm = lg_ref.shape[0]

    @pl.when(pl.program_id(0) == 0)
    def _():
        base_ref[...] = jnp.zeros_like(base_ref)
        r = lax.broadcasted_iota(jnp.int32, (tm, tm), 0)
        c = lax.broadcasted_iota(jnp.int32, (tm, tm), 1)
        tri_ref[...] = jnp.where(r > c, 1.0, 0.0).astype(BF16)

    lane = lax.broadcasted_iota(jnp.int32, lg_ref.shape, 1).astype(F32)
    lg = jnp.where(lane < N_EXPERTS, lg_ref[...], -jnp.inf)
    m1 = lg.max(axis=-1, keepdims=True)
    i1 = jnp.where(lg == m1, lane, float(LANES)).min(axis=-1, keepdims=True)
    rest = jnp.where(lane == i1, -jnp.inf, lg)
    m2 = rest.max(axis=-1, keepdims=True)
    i2 = jnp.where(rest == m2, lane, float(LANES)).min(axis=-1, keepdims=True)
    e2 = jnp.exp(m2 - m1)
    w1 = 1.0 / (1.0 + e2)
    w2 = e2 / (1.0 + e2)

    oh1 = jnp.where(lane == i1, 1.0, 0.0)
    oh2 = jnp.where(lane == i2, 1.0, 0.0)
    pre1 = jnp.dot(tri_ref[...], oh1.astype(BF16), preferred_element_type=F32)
    pre2 = jnp.dot(tri_ref[...], oh2.astype(BF16), preferred_element_type=F32)
    c1 = jnp.sum(oh1, axis=0, keepdims=True)
    c2 = jnp.sum(oh2, axis=0, keepdims=True)
    base = base_ref[...]
    rank1 = jnp.sum(oh1 * (base + pre1), axis=-1, keepdims=True)
    rank2 = jnp.sum(oh2 * (base + c1 + pre2), axis=-1, keepdims=True)
    base_ref[...] = base + c1 + c2

    cols = (i1, i2, rank1, rank2, w1, w2)
    out = jnp.zeros(lg_ref.shape, F32)
    for j, col in enumerate(cols):
        out = jnp.where(lane == float(j), col, out)
    o_ref[...] = out
    cnt_ref[...] = jnp.broadcast_to(base + c1 + c2, cnt_ref.shape)


def _dispatch_kernel(pos_ref, last_ref, nt_ref, h_ref, xs_hbm, zero_ref, stage_ref, zsem, sem):
    i = pl.program_id(0)
    tm = h_ref.shape[0]

    @pl.when(i == 0)
    def _():
        zero_ref[...] = jnp.zeros_like(zero_ref)

        def zero_copy(tile):
            row0 = pl.multiple_of(tile * MOE_TILE, MOE_TILE)
            return pltpu.make_async_copy(zero_ref, xs_hbm.at[pl.ds(row0, MOE_TILE)], zsem)

        def for_zeroed_tiles(fn):
            for e in range(N_EXPERTS):
                @pl.when(last_ref[e] >= 0)
                def _():
                    fn(zero_copy(last_ref[e]))

                tail = MOE_TILES - 1 - e

                @pl.when(tail >= nt_ref[0])
                def _():
                    fn(zero_copy(tail))

        for_zeroed_tiles(lambda cp: cp.start())
        for_zeroed_tiles(lambda cp: cp.wait())

    slot = i % 2
    stage_ref[slot] = h_ref[...]

    def row_copy(step, s, t, k):
        dst = xs_hbm.at[pl.ds(pos_ref[TOP_K * (step * tm + t) + k], 1)]
        return pltpu.make_async_copy(stage_ref.at[s, pl.ds(t, 1)], dst, sem.at[s])

    def issue(t, carry):
        for k in range(TOP_K):
            row_copy(i, slot, t, k).start()
        return carry

    def drain_of(step, s):
        def drain(t, carry):
            for k in range(TOP_K):
                row_copy(step, s, t, k).wait()
            return carry
        lax.fori_loop(0, tm, drain, 0, unroll=ROW_COPY_UNROLL)

    lax.fori_loop(0, tm, issue, 0, unroll=ROW_COPY_UNROLL)

    @pl.when(i > 0)
    def _():
        drain_of(i - 1, 1 - slot)

    @pl.when(i == pl.num_programs(0) - 1)
    def _():
        drain_of(i, slot)


def _expert_kernel(ce_ref, ct_ref, cn_ref, nch_ref, nt_ref, xs_hbm, wg_ref, wu_ref, wd_ref, y_hbm,
                   acc_ref, xb_ref, xstage_ref, wgb_ref, wub_ref, wdb_ref, xsem, osem):
    c = pl.program_id(0)
    f = pl.program_id(1)
    last_f = pl.num_programs(1) - 1

    def tile_rows(tile):
        return pl.ds(pl.multiple_of(tile * MOE_TILE, MOE_TILE), MOE_TILE)

    def out_copy(slot, tile):
        return pltpu.make_async_copy(acc_ref.at[slot], y_hbm.at[tile_rows(tile)], osem)

    @pl.when((c == 0) & (f == 0))
    def _():
        acc_ref[0] = jnp.zeros((MOE_TILE, D_MODEL), F32)
        for e in range(N_EXPERTS):
            tail = MOE_TILES - 1 - e

            @pl.when(tail >= nt_ref[0])
            def _():
                cp = out_copy(0, tail)
                cp.start()
                cp.wait()

    @pl.when(c < nch_ref[0])
    def _():
        n = cn_ref[c]
        t0 = ct_ref[c]

        def x_copy(j):
            return pltpu.make_async_copy(xs_hbm.at[tile_rows(t0 + j)], xstage_ref.at[j % 2], xsem.at[j % 2])

        @pl.when(f == 0)
        def _():
            x_copy(0).start()

        wgb_ref[...] = wg_ref[...].astype(BF16)
        wub_ref[...] = wu_ref[...].astype(BF16)
        wdb_ref[...] = wd_ref[...].astype(BF16)

        def tile_step(j, carry):
            @pl.when(f == 0)
            def _():
                @pl.when(j + 1 < n)
                def _():
                    x_copy(j + 1).start()

                x_copy(j).wait()
                xb_ref[j] = xstage_ref[j % 2].astype(BF16)
                acc_ref[j] = jnp.zeros((MOE_TILE, D_MODEL), F32)

            x = xb_ref[j]
            g = jnp.dot(x, wgb_ref[...], preferred_element_type=F32)
            u = jnp.dot(x, wub_ref[...], preferred_element_type=F32)
            a = (g * _sigmoid(g) * u).astype(BF16)
            acc_ref[j] += jnp.dot(a, wdb_ref[...], preferred_element_type=F32)

            @pl.when(f == last_f)
            def _():
                @pl.when(j > 0)
                def _():
                    out_copy(j - 1, t0 + j - 1).wait()

                out_copy(j, t0 + j).start()

            return carry

        lax.fori_loop(0, n, tile_step, 0)

        @pl.when(f == last_f)
        def _():
            out_copy(n - 1, t0 + n - 1).wait()


def _combine_kernel(pos_ref, route_ref, x_ref, g2_ref, y_hbm, op_ref, os_ref, buf_ref, sem):
    i = pl.program_id(0)
    n = pl.num_programs(0)
    tm = x_ref.shape[0]
    slot = i % 2

    def row_copy(step, s, t, k):
        src = y_hbm.at[pl.ds(pos_ref[TOP_K * (step * tm + t) + k], 1)]
        return pltpu.make_async_copy(src, buf_ref.at[s, k, pl.ds(t, 1)], sem.at[s])

    def issue(step, s):
        def body(t, carry):
            for k in range(TOP_K):
                row_copy(step, s, t, k).start()
            return carry
        lax.fori_loop(0, tm, body, 0, unroll=ROW_COPY_UNROLL)

    @pl.when(i == 0)
    def _():
        issue(0, 0)

    @pl.when(i + 1 < n)
    def _():
        issue(i + 1, 1 - slot)

    def wait_body(t, carry):
        for k in range(TOP_K):
            row_copy(i, slot, t, k).wait()
        return carry

    lax.fori_loop(0, tm, wait_body, 0, unroll=ROW_COPY_UNROLL)

    lane = lax.broadcasted_iota(jnp.int32, route_ref.shape, 1)
    rt = route_ref[...]
    w1 = jnp.sum(jnp.where(lane == 2 * TOP_K, rt, 0.0), axis=-1, keepdims=True)
    w2 = jnp.sum(jnp.where(lane == 2 * TOP_K + 1, rt, 0.0), axis=-1, keepdims=True)
    out = x_ref[...] + g2_ref[...] * (w1 * buf_ref[slot, 0] + w2 * buf_ref[slot, 1])

    @pl.when(i < N_PROMPT // tm)
    def _():
        op_ref[...] = out

    @pl.when(i >= N_PROMPT // tm)
    def _():
        os_ref[...] = out


def _moe(h, x, logits, wg, wu, wd, mod, l):
    route, cnt = pl.pallas_call(
        _route_kernel, grid=(N_TOK // ROUTE_TM,),
        in_specs=[pl.BlockSpec((ROUTE_TM, LANES), lambda i: (i, 0))],
        out_specs=[pl.BlockSpec((ROUTE_TM, LANES), lambda i: (i, 0)), pl.BlockSpec((8, LANES), lambda i: (0, 0))],
        out_shape=[jax.ShapeDtypeStruct((N_TOK, LANES), F32), jax.ShapeDtypeStruct((8, LANES), F32)],
        scratch_shapes=[pltpu.VMEM((1, LANES), F32), pltpu.VMEM((ROUTE_TM, ROUTE_TM), BF16)],
        compiler_params=_cparams(("arbitrary",)), name="route",
    )(logits)

    expert = route[:, 0:TOP_K].astype(jnp.int32)
    rank = route[:, TOP_K:2 * TOP_K].astype(jnp.int32)
    n_sub = (cnt[0, :N_EXPERTS].astype(jnp.int32) + MOE_TILE - 1) // MOE_TILE
    end = jnp.cumsum(n_sub)
    start = end - n_sub
    start_of = jnp.sum(jnp.where(expert[:, :, None] == jnp.arange(N_EXPERTS), start, 0), axis=-1)
    pos = (start_of * MOE_TILE + rank).reshape(-1)
    n_tiles = end[N_EXPERTS - 1:]
    last_tile = jnp.where(n_sub > 0, end - 1, -1).astype(jnp.int32)

    xs = pl.pallas_call(
        _dispatch_kernel,
        grid_spec=pltpu.PrefetchScalarGridSpec(
            num_scalar_prefetch=3, grid=(N_TOK // DISPATCH_TM,),
            in_specs=[pl.BlockSpec((DISPATCH_TM, D_MODEL), lambda i, p, lt, nt: (i, 0))],
            out_specs=pl.BlockSpec(memory_space=pl.ANY),
            scratch_shapes=[pltpu.VMEM((MOE_TILE, D_MODEL), F32), pltpu.VMEM((2, DISPATCH_TM, D_MODEL), F32),
                            pltpu.SemaphoreType.DMA(()), pltpu.SemaphoreType.DMA((2,))]),
        out_shape=jax.ShapeDtypeStruct((MOE_ROWS, D_MODEL), F32),
        compiler_params=_cparams(("arbitrary",)), name="moe_dispatch",
    )(pos, last_tile, n_tiles, h)

    n_chunk = (n_sub + MOE_CHUNK - 1) // MOE_CHUNK
    chunk_end = jnp.cumsum(n_chunk)
    cidx = jnp.arange(MOE_CHUNKS)
    c_expert = jnp.minimum(jnp.sum(cidx[:, None] >= chunk_end[None, :], axis=1), N_EXPERTS - 1).astype(jnp.int32)
    c_k = cidx - (chunk_end - n_chunk)[c_expert]
    c_tile0 = (start[c_expert] + c_k * MOE_CHUNK).astype(jnp.int32)
    c_ntiles = jnp.clip(n_sub[c_expert] - c_k * MOE_CHUNK, 0, MOE_CHUNK).astype(jnp.int32)
    n_chunks = chunk_end[N_EXPERTS - 1:]

    nf = D_FF_EXPERT // MOE_TF

    def w_idx(c, f, ce, nch):
        live = c < nch[0]
        return ce[jnp.minimum(c, nch[0] - 1)], jnp.where(live, f, nf - 1)

    def up_map(c, f, ce, ct, cn, nch, nt):
        e, ff = w_idx(c, f, ce, nch)
        return (e, 0, ff)

    def down_map(c, f, ce, ct, cn, nch, nt):
        e, ff = w_idx(c, f, ce, nch)
        return (e, ff, 0)

    y = pl.pallas_call(
        _expert_kernel,
        grid_spec=pltpu.PrefetchScalarGridSpec(
            num_scalar_prefetch=5, grid=(MOE_CHUNKS, nf),
            in_specs=[pl.BlockSpec(memory_space=pl.ANY),
                      pl.BlockSpec((None, D_MODEL, MOE_TF), up_map),
                      pl.BlockSpec((None, D_MODEL, MOE_TF), up_map),
                      pl.BlockSpec((None, MOE_TF, D_MODEL), down_map)],
            out_specs=pl.BlockSpec(memory_space=pl.ANY),
            scratch_shapes=[pltpu.VMEM((MOE_CHUNK, MOE_TILE, D_MODEL), F32),
                            pltpu.VMEM((MOE_CHUNK, MOE_TILE, D_MODEL), BF16),
                            pltpu.VMEM((2, MOE_TILE, D_MODEL), F32),
                            pltpu.VMEM((D_MODEL, MOE_TF), BF16), pltpu.VMEM((D_MODEL, MOE_TF), BF16),
                            pltpu.VMEM((MOE_TF, D_MODEL), BF16),
                            pltpu.SemaphoreType.DMA((2,)), pltpu.SemaphoreType.DMA(())]),
        out_shape=jax.ShapeDtypeStruct((MOE_ROWS, D_MODEL), F32),
        compiler_params=_cparams(("arbitrary", "arbitrary")), name="moe_experts",
    )(c_expert, c_tile0, c_ntiles, n_chunks, n_tiles, xs, wg, wu, wd)

    tm = COMBINE_TM
    return pl.pallas_call(
        _combine_kernel,
        grid_spec=pltpu.PrefetchScalarGridSpec(
            num_scalar_prefetch=1, grid=(N_TOK // tm,),
            in_specs=[pl.BlockSpec((tm, LANES), lambda i, p: (i, 0)),
                      pl.BlockSpec((tm, D_MODEL), lambda i, p: (i, 0)),
                      pl.BlockSpec((None, None, 1, D_MODEL), lambda i, p: (l, _cond_row(i, tm), 0, 5)),
                      pl.BlockSpec(memory_space=pl.ANY)],
            out_specs=[pl.BlockSpec((tm, D_MODEL), lambda i, p: (jnp.minimum(i, N_PROMPT // tm - 1), 0)),
                       pl.BlockSpec((tm, D_MODEL), lambda i, p: (jnp.maximum(i - N_PROMPT // tm, 0), 0))],
            scratch_shapes=[pltpu.VMEM((2, TOP_K, tm, D_MODEL), F32), pltpu.SemaphoreType.DMA((2,))]),
        out_shape=[jax.ShapeDtypeStruct((N_PROMPT, D_MODEL), F32), jax.ShapeDtypeStruct((N_SAMPLE, D_MODEL), F32)],
        compiler_params=_cparams(("arbitrary",)), name="moe_combine",
    )(pos, route, x, mod, y)


def _kv_leaf_kernel(*refs):
    ins, outs = refs[:4 * DEPTH], refs[4 * DEPTH:]
    for l in range(DEPTH):
        for j in range(4):
            outs[j][l] = ins[4 * l + j][...]


def _kv_leaves(per_layer):
    widths = (A_KV, A_KV, C_W, C_W)
    return pl.pallas_call(
        _kv_leaf_kernel, grid=(BATCH,),
        in_specs=[pl.BlockSpec((SEQ, w), lambda b: (b, 0)) for _ in range(DEPTH) for w in widths],
        out_specs=[pl.BlockSpec((None, DEPTH, SEQ, w), lambda b: (b, 0, 0, 0)) for w in widths],
        out_shape=[jax.ShapeDtypeStruct((BATCH, DEPTH, SEQ, w), F32) for w in widths],
        compiler_params=_cparams(("parallel",)), name="kv_leaves",
    )(*[a for layer in per_layer for a in layer])


def _rope_tables():
    t = jnp.arange(DEC_SEQ)
    row = (t // GRID_W).astype(F32)
    col = (t % GRID_W).astype(F32)
    n_freq = HEAD_DIM // 4
    inv = ROPE_BASE ** (-jnp.arange(n_freq, dtype=F32) / n_freq)
    ang = jnp.concatenate([row[:, None] * inv, col[:, None] * inv], axis=-1)
    cos, sin = jnp.cos(ang), jnp.sin(ang)
    cos_h = jnp.concatenate([cos, cos], axis=-1)
    sin_h = jnp.concatenate([-sin, sin], axis=-1)
    cos_l = jnp.tile(jnp.concatenate([cos_h, cos_h], axis=-1), (DEC_BATCH, 1))
    sin_l = jnp.tile(jnp.concatenate([sin_h, sin_h], axis=-1), (DEC_BATCH, 1))
    cos_t = jnp.concatenate([jnp.ones((N_PROMPT, LANES), F32), cos_l], axis=0)
    sin_t = jnp.concatenate([jnp.zeros((N_PROMPT, LANES), F32), sin_l], axis=0)
    return cos_t, sin_t


def kernel(x_prompt, x_sample, cache_a_k, cache_a_v, cache_c_k, cache_c_v, c, c_ctx, w_mod, b_mod, norm1_g, norm2_g, w_in, qk_norm_a, qk_norm_c, sink_a, rpb_c, w_branch_a, w_branch_b, w_branch_c, w_out, w_ff_gate, w_ff_up, w_ff_down, w_router, w_exp_gate, w_exp_up, w_exp_down):
    x = (x_prompt.reshape(N_PROMPT, D_MODEL), x_sample.reshape(N_SAMPLE, D_MODEL))
    cond =jnp.concatenate([c_ctx[None, :], c], axis=0)
    cond_t = jnp.broadcast_to(cond[:, :, None], (N_COND, D_MODEL, LANES))
    mod = _modulation(cond_t, w_mod, b_mod)
    cos_t, sin_t = _rope_tables()
    bias = _nbr_bias_tables(rpb_c)
    ck_a = cache_a_k.reshape(DEC_BATCH, DEPTH, PAST_LEN, A_KV)
    cv_a = cache_a_v.reshape(DEC_BATCH, DEPTH, PAST_LEN, A_KV)
    ck_c = cache_c_k.reshape(DEC_BATCH, DEPTH, PAST_LEN, C_W)
    cv_c = cache_c_v.reshape(DEC_BATCH, DEPTH, PAST_LEN, C_W)

    new_kv = []
    h = x
    for l in range(DEPTH):
        qa, ka, va, fb, qc, kc, vc, gates = _in_proj(h, w_in, cos_t, sin_t, qk_norm_a[l], qk_norm_c[l], l,
                                                     norm=(norm1_g[l], mod) if l == 0 else None)
        oa_p, oc_p = _ctx_attn(sink_a, qa, ka, va, qc, kc, vc, l)
        oa_s = _win_attn(sink_a, qa, ka, va, ck_a, cv_a, l)
        oc_s = _nbr_attn(qc, kc, vc, ck_c, cv_c, bias, l)
        ob_p = _fourier(fb, BATCH, SEQ, 0, SEQ)
        ob_s = _fourier(fb, DEC_BATCH, DEC_SEQ, N_PROMPT, 512)
        branches = ((oa_p, ob_p, oc_p), (oa_s, ob_s, oc_s), gates, x,
                    w_branch_a, w_branch_b, w_branch_c, w_out, mod, l, norm2_g[l])
        i = l // 2
        if l % 2 == 0:
            x, h2 = _merge(*branches)
            x, h = _ffn(h2, x, w_ff_gate, w_ff_up, w_ff_down, mod, l, norm1_g[l + 1])
        else:
            x, h2, logits = _merge(*branches, w_router=w_router[i])
            xp, xs = _moe(h2, x, logits, w_exp_gate[i], w_exp_up[i], w_exp_down[i], mod, l)
        new_kv.append((ka, va, kc, vc))

    new_ak, new_av, new_ck, new_cv = _kv_leaves(new_kv)
    return (xp.reshape(BATCH, SEQ, D_MODEL), xs.reshape(DEC_BATCH, DEC_SEQ, D_MODEL),
            new_ak.reshape(BATCH, DEPTH, SEQ, A_KV_HEADS, HEAD_DIM),
            new_av.reshape(BATCH, DEPTH, SEQ, A_KV_HEADS, HEAD_DIM),
            new_ck.reshape(BATCH, DEPTH, SEQ, C_HEADS, HEAD_DIM),
            new_cv.reshape(BATCH, DEPTH, SEQ, C_HEADS, HEAD_DIM))
```

```python
import functools

import numpy as np
import jax
import jax.numpy as jnp
from jax import lax
from jax.experimental import pallas as pl
from jax.experimental.pallas import tpu as pltpu

F32 = jnp.float32
BF16 = jnp.bfloat16

D_MODEL = 1024
BATCH = 16
SEQ = 256
DEPTH = 2
DEC_BATCH = 2
DEC_SEQ = 2048
PAST_LEN = 512
GRID_W = 64
HEAD_DIM = 64
SCALE = HEAD_DIM ** -0.5
A_HEADS = 8
A_KV_HEADS = 2
A_GROUP = A_HEADS // A_KV_HEADS
A_WINDOW = 128
A_BLOCK = 128
B_GROUPS = 8
B_GROUP_DIM = 64
B_WIDTH = B_GROUPS * B_GROUP_DIM
C_HEADS = 8
C_WIN_ROWS = 8
C_WIN_COLS = 16
A_Q = A_HEADS * HEAD_DIM
A_KV = A_KV_HEADS * HEAD_DIM
C_W = C_HEADS * HEAD_DIM
QKV_WIDTH = A_Q + 2 * A_KV + B_WIDTH + 3 * C_W
N_BRANCH = 3
GATE_WIDTH = N_BRANCH * D_MODEL
IN_WIDTH = QKV_WIDTH + GATE_WIDTH
D_FF = 2816
N_EXPERTS = 8
D_FF_EXPERT = 3584
ROPE_BASE = 10000.0
RMS_EPS = 1e-6
NEG_INF = -1e30

N_PROMPT = BATCH * SEQ
N_SAMPLE = DEC_BATCH * DEC_SEQ
N_TOK = N_PROMPT + N_SAMPLE
N_COND = 1 + DEC_BATCH
LANES = 128
NORM_SLAB = 256
C_QROWS = 4
C_QBLOCK = C_QROWS * GRID_W
C_DR_SLOTS = 2 * C_WIN_ROWS
VMEM_LIMIT = 56 * 1024 * 1024


def _cparams(sem):
    return pltpu.CompilerParams(dimension_semantics=sem, vmem_limit_bytes=VMEM_LIMIT)


def _sigmoid(x):
    return 1.0 / (1.0 + jnp.exp(-x))


def _cond_row(tile, tm):
    return jnp.maximum(tile * tm // DEC_SEQ - 1, 0)


def _mod_kernel(ct_ref, w_ref, b_ref, o_ref, silu_ref):
    @pl.when((pl.program_id(0) == 0) & (pl.program_id(1) == 0))
    def _():
        cb = ct_ref[...]
        silu_ref[...] = cb * _sigmoid(cb)

    tn = w_ref.shape[1]
    for r in range(N_COND):
        s = silu_ref[r]
        for cc in range(tn // LANES):
            sl = slice(cc * LANES, (cc + 1) * LANES)
            o_ref[r, :, sl] = jnp.sum(w_ref[:, sl] * s, axis=0, keepdims=True) + b_ref[:, sl]


def _modulation(cond_t, w_mod, b_mod):
    tn = 1024
    n = 6 * D_MODEL
    return pl.pallas_call(
        _mod_kernel,
        grid=(DEPTH, n // tn),
        in_specs=[
            pl.BlockSpec((N_COND, D_MODEL, LANES), lambda l, j: (0, 0, 0)),
            pl.BlockSpec((None, D_MODEL, tn), lambda l, j: (l, 0, j)),
            pl.BlockSpec((None, 1, tn), lambda l, j: (l, 0, j)),
        ],
        out_specs=pl.BlockSpec((None, N_COND, 1, tn), lambda l, j: (l, 0, 0, j)),
        out_shape=jax.ShapeDtypeStruct((DEPTH, N_COND, 1, n), F32),
        scratch_shapes=[pltpu.VMEM((N_COND, D_MODEL, LANES), F32)],
        compiler_params=_cparams(("arbitrary", "arbitrary")),
        name="modulation",
    )(cond_t, w_mod, b_mod.reshape(DEPTH, 1, n))


def _adaln_math(x, g, sh, sc):
    ms = jnp.mean(x * x, axis=-1, keepdims=True)
    y = x * lax.rsqrt(ms + RMS_EPS) * g
    return y * (1.0 + sc) + sh


def _split_rows(tm, width, joined=False):
    n_p = N_PROMPT // tm
    latent0 = n_p if joined else 0
    return (pl.BlockSpec((tm, width), lambda i: (jnp.minimum(i, n_p - 1), 0)),
            pl.BlockSpec((tm, width), lambda i: (jnp.maximum(i - n_p, 0) + latent0, 0)))


def _split_operands(x):
    return (x, False) if isinstance(x, tuple) else ((x, x), True)


def _read_split(p_ref, s_ref):
    is_ctx = pl.program_id(0) < N_PROMPT // p_ref.shape[0]
    return jnp.where(is_ctx, p_ref[...], s_ref[...])


WEIGHT_ROW_CHUNK = 128


def _load_weight_bf16(src_rows, dst_ref, stage_ref, sem):
    n = dst_ref.shape[0] // WEIGHT_ROW_CHUNK

    def chunk_copy(k):
        return pltpu.make_async_copy(src_rows(k), stage_ref.at[k % 2], sem.at[k % 2])

    chunk_copy(0).start()
    for k in range(n):
        if k + 1 < n:
            chunk_copy(k + 1).start()
        chunk_copy(k).wait()
        dst_ref[k * WEIGHT_ROW_CHUNK:(k + 1) * WEIGHT_ROW_CHUNK, :] = stage_ref[k % 2].astype(BF16)


def _row_chunk(k):
    return pl.ds(k * WEIGHT_ROW_CHUNK, WEIGHT_ROW_CHUNK)


def _head_norm(x, gain, bd):
    sq = x * x
    hi = sq.astype(BF16)
    lo = (sq - hi.astype(F32)).astype(BF16)
    ms = jnp.dot(hi, bd, preferred_element_type=F32) + jnp.dot(lo, bd, preferred_element_type=F32)
    return x * lax.rsqrt(ms + RMS_EPS) * gain


def _rope(x, cos, sin_signed, first_half):
    half = HEAD_DIM // 2
    swapped = jnp.where(first_half, pltpu.roll(x, x.shape[1] - half, 1), pltpu.roll(x, half, 1))
    return x * cos + swapped * sin_signed


def _in_proj_kernel(*refs, l, fused_norm):
    n_lead = 5 if fused_norm else 1
    (cos_ref, sin_ref, ga_ref, gc_ref, w_hbm, qa_ref, ka_ref, va_ref, fb_ref, qc_ref, kc_ref, vc_ref, gt_ref,
     wb_ref, stage_ref, sem) = refs[n_lead:]

    @pl.when(pl.program_id(0) == 0)
    def _():
        _load_weight_bf16(lambda k: w_hbm.at[l, _row_chunk(k)], wb_ref, stage_ref, sem)

    if fused_norm:
        xp_ref, xs_ref, ng_ref, sh_ref, sc_ref = refs[:n_lead]
        h = _adaln_math(_read_split(xp_ref, xs_ref), ng_ref[...], sh_ref[...], sc_ref[...]).astype(BF16)
    else:
        h = refs[0][...]

    def proj(off, width):
        return jnp.dot(h, wb_ref[:, off:off + width], preferred_element_type=F32)

    r = lax.broadcasted_iota(jnp.int32, (NORM_SLAB, NORM_SLAB), 0) // HEAD_DIM
    c = lax.broadcasted_iota(jnp.int32, (NORM_SLAB, NORM_SLAB), 1) // HEAD_DIM
    bd = jnp.where(r == c, 1.0 / HEAD_DIM, 0.0).astype(BF16)
    lane = lax.broadcasted_iota(jnp.int32, (1, NORM_SLAB), 1)
    first_half = (lane % HEAD_DIM) < HEAD_DIM // 2
    cos = jnp.concatenate([cos_ref[...]] * (NORM_SLAB // LANES), axis=1)
    sin = jnp.concatenate([sin_ref[...]] * (NORM_SLAB // LANES), axis=1)
    gqa, gka = ga_ref[0:1, :], ga_ref[1:2, :]
    gqc, gkc = gc_ref[0:1, :], gc_ref[1:2, :]
    slabs = lambda p: [p[:, s * NORM_SLAB:(s + 1) * NORM_SLAB] for s in range(p.shape[1] // NORM_SLAB)]
    cat = lambda parts: jnp.concatenate(parts, axis=1)

    off = 0
    qa_ref[...] = cat([_rope(_head_norm(x, gqa, bd), cos, sin, first_half) for x in slabs(proj(off, A_Q))]
                      ).astype(BF16)
    off += A_Q
    kv = proj(off, 2 * A_KV)
    ka_ref[...] = _rope(_head_norm(kv, gka, bd), cos, sin, first_half)[:, :A_KV]
    va_ref[...] = kv[:, A_KV:]
    off += 2 * A_KV
    fb_ref[...] = proj(off, B_WIDTH).astype(BF16)
    off += B_WIDTH
    qc_ref[...] = cat([_head_norm(x, gqc, bd) for x in slabs(proj(off, C_W))]).astype(BF16)
    off += C_W
    kc_ref[...] = cat([_head_norm(x, gkc, bd) for x in slabs(proj(off, C_W))])
    off += C_W
    vc_ref[...] = proj(off, C_W)
    off += C_W
    for j in range(N_BRANCH):
        cols = slice(j * D_MODEL, (j + 1) * D_MODEL)
        gt_ref[:, cols] = _sigmoid(proj(off + j * D_MODEL, D_MODEL)).astype(BF16)


def _in_proj(h, w_in, cos_t, sin_t, qk_a, qk_c, l, norm=None):
    tm = 512
    row = lambda w: pl.BlockSpec((tm, w), lambda i: (i, 0))
    widths = (A_Q, A_KV, A_KV, B_WIDTH, C_W, C_W, C_W, GATE_WIDTH)
    dtypes = (BF16, F32, F32, BF16, BF16, F32, F32, BF16)
    if norm is None:
        lead_specs, lead = [row(D_MODEL)], [h]
    else:
        norm_g, mod = norm
        x, joined = _split_operands(h)
        mspec = lambda which: pl.BlockSpec((None, None, 1, D_MODEL), lambda i: (l, _cond_row(i, tm), 0, which))
        lead_specs = [*_split_rows(tm, D_MODEL, joined), pl.BlockSpec((1, D_MODEL), lambda i: (0, 0)),
                      mspec(0), mspec(1)]
        lead = [*x, norm_g.reshape(1, D_MODEL), mod, mod]
    return pl.pallas_call(
        functools.partial(_in_proj_kernel, l=l, fused_norm=norm is not None), grid=(N_TOK // tm,),
        in_specs=[*lead_specs, row(LANES), row(LANES),
                  pl.BlockSpec((2, NORM_SLAB), lambda i: (0, 0)), pl.BlockSpec((2, NORM_SLAB), lambda i: (0, 0)),
                  pl.BlockSpec(memory_space=pl.ANY)],
        out_specs=[row(w) for w in widths],
        out_shape=[jax.ShapeDtypeStruct((N_TOK, w), d) for w, d in zip(widths, dtypes)],
        scratch_shapes=[pltpu.VMEM((D_MODEL, IN_WIDTH), BF16),
                        pltpu.VMEM((2, WEIGHT_ROW_CHUNK, IN_WIDTH), F32), pltpu.SemaphoreType.DMA((2,))],
        compiler_params=_cparams(("arbitrary",)), name="in_proj",
    )(*lead, cos_t, sin_t, jnp.tile(qk_a, (1, NORM_SLAB // HEAD_DIM)), jnp.tile(qk_c, (1, NORM_SLAB // HEAD_DIM)), w_in)


def _nt_dot(a, b):
    return lax.dot_general(a, b, (((1,), (1,)), ((), ())), preferred_element_type=F32)


def _head(x, h):
    return x[:, h * HEAD_DIM:(h + 1) * HEAD_DIM]


def _stacked_softmax(parts, sink):
    m = parts[0].max(axis=-1, keepdims=True)
    for s in parts[1:]:
        m = jnp.maximum(m, s.max(axis=-1, keepdims=True))
    if sink is not None:
        m = jnp.maximum(m, sink)
    den = jnp.exp(sink - m) if sink is not None else 0.0
    es = []
    for s in parts:
        e = jnp.exp(s - m)
        den = den + e.sum(axis=-1, keepdims=True)
        es.append(e.astype(BF16))
    return es, 1.0 / den


def _sink_column(sink_ref, l, rows_per_head):
    return jnp.concatenate([jnp.full((rows_per_head, 1), sink_ref[l, h], F32) for h in range(A_HEADS)], axis=0)


def _gqa_queries(qa, g):
    return jnp.concatenate([_head(qa, g * A_GROUP + i) for i in range(A_GROUP)], axis=0)


def _ctx_attn_kernel(sink_ref, qa_ref, ka_ref, va_ref, qc_ref, kc_ref, vc_ref, oa_ref, oc_ref, *, l):
    t = SEQ
    for bb in range(CTX_BATCHES):
        r = slice(bb * t, (bb + 1) * t)
        qa = qa_ref[r, :] * SCALE
        ka = ka_ref[r, :].astype(BF16)
        va = va_ref[r, :].astype(BF16)
        s = jnp.concatenate([_nt_dot(_gqa_queries(qa, g), _head(ka, g)) for g in range(A_KV_HEADS)], axis=0)
        (e,), inv = _stacked_softmax([s], _sink_column(sink_ref, l, t))
        outs = []
        for g in range(A_KV_HEADS):
            rows = slice(g * A_GROUP * t, (g + 1) * A_GROUP * t)
            o = jnp.dot(e[rows], _head(va, g), preferred_element_type=F32) * inv[rows]
            outs += [o[i * t:(i + 1) * t] for i in range(A_GROUP)]
        oa_ref[r, :] = jnp.concatenate(outs, axis=1).astype(BF16)

        qc = qc_ref[r, :] * SCALE
        kc = kc_ref[r, :].astype(BF16)
        vc = vc_ref[r, :].astype(BF16)
        s = jnp.concatenate([_nt_dot(_head(qc, h), _head(kc, h)) for h in range(C_HEADS)], axis=0)
        (e,), inv = _stacked_softmax([s], None)
        outs = [jnp.dot(e[h * t:(h + 1) * t], _head(vc, h), preferred_element_type=F32) * inv[h * t:(h + 1) * t]
                for h in range(C_HEADS)]
        oc_ref[r, :] = jnp.concatenate(outs, axis=1).astype(BF16)


CTX_BATCHES = 4


def _ctx_attn(sink_a, qa, ka, va, qc, kc, vc, l):
    blk = lambda w: pl.BlockSpec((CTX_BATCHES * SEQ, w), lambda b: (b, 0))
    return pl.pallas_call(
        functools.partial(_ctx_attn_kernel, l=l), grid=(BATCH // CTX_BATCHES,),
        in_specs=[pl.BlockSpec(memory_space=pltpu.SMEM),
                  blk(A_Q), blk(A_KV), blk(A_KV), blk(C_W), blk(C_W), blk(C_W)],
        out_specs=[blk(A_Q), blk(C_W)],
        out_shape=[jax.ShapeDtypeStruct((N_PROMPT, A_Q), BF16), jax.ShapeDtypeStruct((N_PROMPT, C_W), BF16)],
        compiler_params=_cparams(("parallel",)), name="ctx_attn",
    )(sink_a, qa, ka, va, qc, kc, vc)


def _win_attn_kernel(sink_ref, q_ref, kp_ref, kc_ref, kn_ref, vp_ref, vc_ref, vn_ref, ck_ref, cv_ref, mask_ref,
                     o_ref, *, l):
    rows = A_GROUP * A_BLOCK
    mask = mask_ref[...]
    sink = _sink_column(sink_ref, l, A_BLOCK)
    for g in range(A_KV_HEADS):
        sl = slice(g * HEAD_DIM, (g + 1) * HEAD_DIM)
        q = jnp.concatenate([q_ref[:, (g * A_GROUP + i) * HEAD_DIM:(g * A_GROUP + i + 1) * HEAD_DIM]
                             for i in range(A_GROUP)], axis=0) * SCALE
        k_loc = jnp.concatenate([kp_ref[:, sl], kc_ref[:, sl], kn_ref[:, sl]], axis=0).astype(BF16)
        v_loc = jnp.concatenate([vp_ref[:, sl], vc_ref[:, sl], vn_ref[:, sl]], axis=0).astype(BF16)
        s_loc = _nt_dot(q, k_loc) + mask
        s_ctx = _nt_dot(q, ck_ref[:, sl].astype(BF16))
        (e_loc, e_ctx), inv = _stacked_softmax([s_loc, s_ctx], sink[g * rows:(g + 1) * rows])
        o = (jnp.dot(e_loc, v_loc, preferred_element_type=F32)
             + jnp.dot(e_ctx, cv_ref[:, sl].astype(BF16), preferred_element_type=F32)) * inv
        for i in range(A_GROUP):
            h = g * A_GROUP + i
            o_ref[:, h * HEAD_DIM:(h + 1) * HEAD_DIM] = o[i * A_BLOCK:(i + 1) * A_BLOCK].astype(BF16)


def _win_attn(sink_a, qa, ka, va, cache_k, cache_v, l):
    nb = DEC_SEQ // A_BLOCK
    base = N_PROMPT // A_BLOCK

    def nbr(d):
        return lambda b, t: (base + b * nb + jnp.clip(t + d, 0, nb - 1), 0)

    kv = lambda d: pl.BlockSpec((A_BLOCK, A_KV), nbr(d))
    cache = pl.BlockSpec((None, None, PAST_LEN, A_KV), lambda b, t: (b, l, 0, 0))
    qi = np.arange(A_GROUP * A_BLOCK)[:, None] % A_BLOCK
    kj = np.arange(3 * A_BLOCK)[None, :] - A_BLOCK
    band = np.abs(kj - qi) <= A_WINDOW
    masks = np.stack([band & (kj >= 0), band, band & (kj < A_BLOCK)])
    masks = jnp.asarray(np.where(masks, 0.0, NEG_INF).astype(np.float32))
    position = lambda b, t: (jnp.where(t == 0, 0, jnp.where(t == nb - 1, 2, 1)), 0, 0)
    return pl.pallas_call(
        functools.partial(_win_attn_kernel, l=l), grid=(DEC_BATCH, nb),
        in_specs=[pl.BlockSpec(memory_space=pltpu.SMEM),
                  pl.BlockSpec((A_BLOCK, A_Q), nbr(0)),
                  kv(-1), kv(0), kv(1), kv(-1), kv(0), kv(1), cache, cache,
                  pl.BlockSpec((None, A_GROUP * A_BLOCK, 3 * A_BLOCK), position)],
        out_specs=pl.BlockSpec((A_BLOCK, A_Q), lambda b, t: (b * nb + t, 0)),
        out_shape=jax.ShapeDtypeStruct((N_SAMPLE, A_Q), BF16),
        compiler_params=_cparams(("parallel", "parallel")), name="win_attn",
    )(sink_a, qa, ka, ka, ka, va, va, va, cache_k, cache_v, masks)


def _nbr_attn_kernel(*refs):
    tab_ref, o_ref, bias_ref = refs[9 * DEC_BATCH:]
    j = pl.program_id(0)
    nb = pl.num_programs(0)
    slots = _nbr_row_slots()

    def build(cls):
        for h in range(C_HEADS):
            for qr in range(C_QROWS):
                for kk in range(3 * C_QROWS):
                    bias_ref[h, qr * GRID_W:(qr + 1) * GRID_W, kk * GRID_W:(kk + 1) * GRID_W] = (
                        tab_ref[h, slots[cls][qr][kk]])

    for cls, at in enumerate((0, 1, nb - 1)):
        @pl.when(j == at)
        def _():
            build(cls)

    tq = C_QBLOCK
    for b in range(DEC_BATCH):
        q_ref, kp_ref, kc_ref, kn_ref, vp_ref, vc_ref, vn_ref, ck_ref, cv_ref = refs[9 * b:9 * (b + 1)]
        q = q_ref[...] * SCALE
        k_loc = jnp.concatenate([kp_ref[...], kc_ref[...], kn_ref[...]], axis=0).astype(BF16)
        v_loc = jnp.concatenate([vp_ref[...], vc_ref[...], vn_ref[...]], axis=0).astype(BF16)
        k_ctx = ck_ref[...].astype(BF16)
        v_ctx = cv_ref[...].astype(BF16)
        s_loc = jnp.concatenate([_nt_dot(_head(q, h), _head(k_loc, h)) for h in range(C_HEADS)], axis=0)
        s_loc = s_loc + bias_ref[...].reshape(C_HEADS * tq, 3 * tq)
        s_ctx = jnp.concatenate([_nt_dot(_head(q, h), _head(k_ctx, h)) for h in range(C_HEADS)], axis=0)
        (e_loc, e_ctx), inv = _stacked_softmax([s_loc, s_ctx], None)
        outs = []
        for h in range(C_HEADS):
            r = slice(h * tq, (h + 1) * tq)
            outs.append((jnp.dot(e_loc[r], _head(v_loc, h), preferred_element_type=F32)
                         + jnp.dot(e_ctx[r], _head(v_ctx, h), preferred_element_type=F32)) * inv[r])
        o_ref[b] = jnp.concatenate(outs, axis=1).astype(BF16)


def _nbr_attn(qc, kc, vc, cache_k, cache_v, bias, l):
    nb = DEC_SEQ // C_QBLOCK
    base = N_PROMPT // C_QBLOCK

    def kv(b, d):
        return pl.BlockSpec((C_QBLOCK, C_W), lambda j: (base + b * nb + jnp.clip(j + d, 0, nb - 1), 0))

    def cache(b):
        return pl.BlockSpec((None, None, PAST_LEN, C_W), lambda j: (b, l, 0, 0))

    per_batch_specs, per_batch_args = [], []
    for b in range(DEC_BATCH):
        per_batch_specs += [kv(b, 0), kv(b, -1), kv(b, 0), kv(b, 1), kv(b, -1), kv(b, 0), kv(b, 1),
                            cache(b), cache(b)]
        per_batch_args += [qc, kc, kc, kc, vc, vc, vc, cache_k, cache_v]
    out = pl.pallas_call(
        _nbr_attn_kernel, grid=(nb,),
        in_specs=per_batch_specs + [pl.BlockSpec((None, C_HEADS, C_DR_SLOTS, GRID_W, GRID_W),
                                                 lambda j: (l, 0, 0, 0, 0))],
        out_specs=pl.BlockSpec((DEC_BATCH, C_QBLOCK, C_W), lambda j: (0, j, 0)),
        out_shape=jax.ShapeDtypeStruct((DEC_BATCH, DEC_SEQ, C_W), BF16),
        scratch_shapes=[pltpu.VMEM((C_HEADS, C_QBLOCK, 3 * C_QBLOCK), F32)],
        compiler_params=_cparams(("arbitrary",)), name="nbr_attn",
    )(*per_batch_args, bias)
    return out.reshape(N_SAMPLE, C_W)


def _nbr_bias_tables(rpb):
    qcol = np.arange(GRID_W)
    qcs = np.clip(qcol - C_WIN_COLS // 2, 0, GRID_W - C_WIN_COLS)
    kcol = np.arange(GRID_W)
    col_ok = (kcol[None, :] >= qcs[:, None]) & (kcol[None, :] < qcs[:, None] + C_WIN_COLS)
    dc = np.clip(kcol[None, :] - qcol[:, None], -(C_WIN_COLS - 1), C_WIN_COLS - 1) + C_WIN_COLS - 1
    onehot_dc = (dc.reshape(-1)[None, :] == np.arange(2 * C_WIN_COLS - 1)[:, None]).astype(np.float32)
    t = jnp.einsum('lhab,bx->lhax', rpb, jnp.asarray(onehot_dc), precision=lax.Precision.HIGHEST)
    t = jnp.where(jnp.asarray(col_ok.reshape(-1)), t, NEG_INF)
    t = jnp.concatenate([t, jnp.full((DEPTH, C_HEADS, 1, GRID_W * GRID_W), NEG_INF, F32)], axis=2)
    return t.reshape(DEPTH, C_HEADS, C_DR_SLOTS, GRID_W, GRID_W)


def _nbr_row_slots():
    rows = DEC_SEQ // GRID_W
    slots = []
    for j in (0, 3, rows // C_QROWS - 1):
        per_q = []
        for qr in range(C_QROWS):
            r = C_QROWS * j + qr
            rs = min(max(r - C_WIN_ROWS // 2, 0), rows - C_WIN_ROWS)
            per_k = []
            for kk in range(3 * C_QROWS):
                kabs = C_QROWS * (j - 1) + kk
                per_k.append(kabs - r + C_WIN_ROWS - 1 if rs <= kabs < rs + C_WIN_ROWS else C_DR_SLOTS - 1)
            per_q.append(per_k)
        slots.append(per_q)
    return slots


def _fourier_kernel(u_ref, bc_ref, bs_ref, cl_ref, sl_ref, o_ref, zc_ref, zs_ref):
    @pl.when(pl.program_id(1) == 0)
    def _():
        u = u_ref[...]
        zc_ref[...] = jnp.dot(u, bc_ref[...].astype(BF16), preferred_element_type=F32).astype(BF16)
        zs_ref[...] = jnp.dot(u, bs_ref[...].astype(BF16), preferred_element_type=F32).astype(BF16)

    o = (jnp.dot(cl_ref[...].astype(BF16), zc_ref[...], preferred_element_type=F32)
         - jnp.dot(sl_ref[...].astype(BF16), zs_ref[...], preferred_element_type=F32))
    o_ref[...] = o.astype(BF16)


def _dft_tables(n):
    k = np.arange(n)
    ang = 2.0 * np.pi * ((k[:, None] * k[None, :]) % n) / n
    return np.cos(ang) / np.sqrt(n), np.sin(ang) / np.sqrt(n)


def _channel_dft_tables():
    c, s = _dft_tables(B_GROUP_DIM)
    eye = np.eye(B_GROUPS)
    return np.kron(eye, c).astype(np.float32), np.kron(eye, s).astype(np.float32)


def _fourier(fb, n_batch, seq, row0, tr):
    cl, sl = (jnp.asarray(a.astype(np.float32)) for a in _dft_tables(seq))
    bc, bs = (jnp.asarray(a) for a in _channel_dft_tables())
    nt = seq // tr
    const = pl.BlockSpec((B_WIDTH, B_WIDTH), lambda b, t: (0, 0))
    return pl.pallas_call(
        _fourier_kernel, grid=(n_batch, nt),
        in_specs=[pl.BlockSpec((seq, B_WIDTH), lambda b, t: (row0 // seq + b, 0)), const, const,
                  pl.BlockSpec((tr, seq), lambda b, t: (t, 0)), pl.BlockSpec((tr, seq), lambda b, t: (t, 0))],
        out_specs=pl.BlockSpec((tr, B_WIDTH), lambda b, t: (b * nt + t, 0)),
        out_shape=jax.ShapeDtypeStruct((n_batch * seq, B_WIDTH), BF16),
        scratch_shapes=[pltpu.VMEM((seq, B_WIDTH), BF16), pltpu.VMEM((seq, B_WIDTH), BF16)],
        compiler_params=_cparams(("parallel", "arbitrary")), name=f"fourier_{seq}",
    )(fb, bc, bs, cl, sl)


def _merge_kernel(oap_ref, obp_ref, ocp_ref, oas_ref, obs_ref, ocs_ref, gt_ref, xp_ref, xs_ref,
                  wa_ref, wb_ref, wc_ref, wo_ref, g1_ref, ng_ref, sh_ref, sc_ref, *rest, router):
    if router:
        wr_ref, o_ref, h2_ref, lg_ref, wab_ref, wbb_ref, wcb_ref, wob_ref = rest
    else:
        o_ref, h2_ref, wab_ref, wbb_ref, wcb_ref, wob_ref = rest

    @pl.when(pl.program_id(0) == 0)
    def _():
        wab_ref[...] = wa_ref[...].astype(BF16)
        wbb_ref[...] = wb_ref[...].astype(BF16)
        wcb_ref[...] = wc_ref[...].astype(BF16)
        wob_ref[...] = wo_ref[...].astype(BF16)

    ya = jnp.dot(_read_split(oap_ref, oas_ref), wab_ref[...], preferred_element_type=F32)
    yb = jnp.dot(_read_split(obp_ref, obs_ref), wbb_ref[...], preferred_element_type=F32)
    yc = jnp.dot(_read_split(ocp_ref, ocs_ref), wcb_ref[...], preferred_element_type=F32)
    d = D_MODEL
    m = (gt_ref[:, 0:d].astype(F32) * ya + gt_ref[:, d:2 * d].astype(F32) * yb
         + gt_ref[:, 2 * d:3 * d].astype(F32) * yc)
    y = jnp.dot(m.astype(BF16), wob_ref[...], preferred_element_type=F32)
    x_new = _read_split(xp_ref, xs_ref) + g1_ref[...] * y
    o_ref[...] = x_new
    h2 = _adaln_math(x_new, ng_ref[...], sh_ref[...], sc_ref[...])
    h2_ref[...] = h2.astype(h2_ref.dtype)
    if router:
        w = wr_ref[...]
        w_hi = w.astype(BF16)
        w_lo = (w - w_hi.astype(F32)).astype(BF16)
        h_hi = h2.astype(BF16)
        h_lo = (h2 - h_hi.astype(F32)).astype(BF16)
        hi_terms = jnp.dot(h_hi, jnp.concatenate([w_hi, w_lo], axis=1), preferred_element_type=F32)
        lg_ref[...] = (hi_terms[:, :LANES] + jnp.dot(h_lo, w_hi, preferred_element_type=F32)
                       + hi_terms[:, LANES:])


def _merge(branches_p, branches_s, gates, x, wa, wb, wc, wo, mod, l, norm_g, w_router=None):
    tm = 512
    router = w_router is not None
    x, joined = _split_operands(x)
    row = lambda w: pl.BlockSpec((tm, w), lambda i: (i, 0))
    row_p = lambda w: _split_rows(tm, w)[0]
    row_s = lambda w: _split_rows(tm, w)[1]
    const = lambda r, c: pl.BlockSpec((None, r, c), lambda i: (l, 0, 0), pipeline_mode=pl.Buffered(1))
    mspec = lambda which: pl.BlockSpec((None, None, 1, D_MODEL), lambda i: (l, _cond_row(i, tm), 0, which))
    in_specs = [row_p(A_Q), row_p(B_WIDTH), row_p(C_W), row_s(A_Q), row_s(B_WIDTH), row_s(C_W),
                row(GATE_WIDTH), *_split_rows(tm, D_MODEL, joined),
                const(A_Q, D_MODEL), const(B_WIDTH, D_MODEL), const(C_W, D_MODEL), const(D_MODEL, D_MODEL),
                mspec(2), pl.BlockSpec((1, D_MODEL), lambda i: (0, 0)), mspec(3), mspec(4)]
    operands = [*branches_p, *branches_s, gates, *x, wa, wb, wc, wo, mod, norm_g.reshape(1, D_MODEL), mod, mod]
    out_specs = [row(D_MODEL), row(D_MODEL)]
    out_shape = [jax.ShapeDtypeStruct((N_TOK, D_MODEL), F32),
                 jax.ShapeDtypeStruct((N_TOK, D_MODEL), F32 if router else BF16)]
    if router:
        in_specs.append(pl.BlockSpec((D_MODEL, LANES), lambda i: (0, 0)))
        operands.append(jnp.pad(w_router, ((0, 0), (0, LANES - N_EXPERTS))))
        out_specs.append(row(LANES))
        out_shape.append(jax.ShapeDtypeStruct((N_TOK, LANES), F32))
    return pl.pallas_call(
        functools.partial(_merge_kernel, router=router), grid=(N_TOK // tm,),
        in_specs=in_specs, out_specs=out_specs, out_shape=out_shape,
        scratch_shapes=[pltpu.VMEM((A_Q, D_MODEL), BF16), pltpu.VMEM((B_WIDTH, D_MODEL), BF16),
                        pltpu.VMEM((C_W, D_MODEL), BF16), pltpu.VMEM((D_MODEL, D_MODEL), BF16)],
        compiler_params=_cparams(("arbitrary",)), name="merge_router" if router else "merge",
    )(*operands)


FFN_COL_CHUNK = 256


def _ffn_kernel(h_ref, x_ref, g2_ref, ng_ref, sh_ref, sc_ref, wg_hbm, wu_hbm, wd_hbm, o_ref, hn_ref,
                wgb_ref, wub_ref, wdb_ref, stage_up_ref, stage_dn_ref, sem, *, i_dense):
    @pl.when(pl.program_id(0) == 0)
    def _():
        _load_weight_bf16(lambda k: wg_hbm.at[i_dense, _row_chunk(k)], wgb_ref, stage_up_ref, sem)
        _load_weight_bf16(lambda k: wu_hbm.at[i_dense, _row_chunk(k)], wub_ref, stage_up_ref, sem)
        _load_weight_bf16(lambda k: wd_hbm.at[i_dense, _row_chunk(k)], wdb_ref, stage_dn_ref, sem)

    h = h_ref[...]
    acc = None
    for c in range(D_FF // FFN_COL_CHUNK):
        cols = slice(c * FFN_COL_CHUNK, (c + 1) * FFN_COL_CHUNK)
        g = jnp.dot(h, wgb_ref[:, cols], preferred_element_type=F32)
        u = jnp.dot(h, wub_ref[:, cols], preferred_element_type=F32)
        a = (g * _sigmoid(g) * u).astype(BF16)
        d = jnp.dot(a, wdb_ref[cols, :], preferred_element_type=F32)
        acc = d if acc is None else acc + d
    x_new = x_ref[...] + g2_ref[...] * acc
    o_ref[...] = x_new
    hn_ref[...] = _adaln_math(x_new, ng_ref[...], sh_ref[...], sc_ref[...]).astype(BF16)


def _ffn(h, x, wg, wu, wd, mod, l, next_norm_g):
    tm = 512
    row = lambda dt: pl.BlockSpec((tm, D_MODEL), lambda i: (i, 0))
    mspec = lambda layer, which: pl.BlockSpec((None, None, 1, D_MODEL),
                                              lambda i: (layer, _cond_row(i, tm), 0, which))
    hbm = pl.BlockSpec(memory_space=pl.ANY)
    return pl.pallas_call(
        functools.partial(_ffn_kernel, i_dense=l // 2), grid=(N_TOK // tm,),
        in_specs=[row(BF16), row(F32), mspec(l, 5),
                  pl.BlockSpec((1, D_MODEL), lambda i: (0, 0)), mspec(l + 1, 0), mspec(l + 1, 1),
                  hbm, hbm, hbm],
        out_specs=[row(F32), row(BF16)],
        out_shape=[jax.ShapeDtypeStruct((N_TOK, D_MODEL), F32), jax.ShapeDtypeStruct((N_TOK, D_MODEL), BF16)],
        scratch_shapes=[pltpu.VMEM((D_MODEL, D_FF), BF16), pltpu.VMEM((D_MODEL, D_FF), BF16),
                        pltpu.VMEM((D_FF, D_MODEL), BF16),
                        pltpu.VMEM((2, WEIGHT_ROW_CHUNK, D_FF), F32),
                        pltpu.VMEM((2, WEIGHT_ROW_CHUNK, D_MODEL), F32), pltpu.SemaphoreType.DMA((2,))],
        compiler_params=_cparams(("arbitrary",)), name="ffn",
    )(h, x, mod, next_norm_g.reshape(1, D_MODEL), mod, mod, wg, wu, wd)


TOP_K = 2
MOE_TILE = 256
MOE_TILES = TOP_K * N_TOK // MOE_TILE + N_EXPERTS
MOE_ROWS = MOE_TILES * MOE_TILE
MOE_CHUNK = 10
MOE_CHUNKS = MOE_TILES // MOE_CHUNK + N_EXPERTS
MOE_TF = 896
ROUTE_TM = 512
DISPATCH_TM = 512
COMBINE_TM = 512
ROW_COPY_UNROLL = 16


def _route_kernel(lg_ref, o_ref, cnt_ref, base_ref, tri_ref):
    tm = lg_ref.shape[0]

    @pl.when(pl.program_id(0) == 0)
    def _():
        base_ref[...] = jnp.zeros_like(base_ref)
        r = lax.broadcasted_iota(jnp.int32, (tm, tm), 0)
        c = lax.broadcasted_iota(jnp.int32, (tm, tm), 1)
        tri_ref[...] = jnp.where(r > c, 1.0, 0.0).astype(BF16)

    lane = lax.broadcasted_iota(jnp.int32, lg_ref.shape, 1).astype(F32)
    lg = jnp.where(lane < N_EXPERTS, lg_ref[...], -jnp.inf)
    m1 = lg.max(axis=-1, keepdims=True)
    i1 = jnp.where(lg == m1, lane, float(LANES)).min(axis=-1, keepdims=True)
    rest = jnp.where(lane == i1, -jnp.inf, lg)
    m2 = rest.max(axis=-1, keepdims=True)
    i2 = jnp.where(rest == m2, lane, float(LANES)).min(axis=-1, keepdims=True)
    e2 = jnp.exp(m2 - m1)
    w1 = 1.0 / (1.0 + e2)
    w2 = e2 / (1.0 + e2)

    oh1 = jnp.where(lane == i1, 1.0, 0.0)
    oh2 = jnp.where(lane == i2, 1.0, 0.0)
    pre1 = jnp.dot(tri_ref[...], oh1.astype(BF16), preferred_element_type=F32)
    pre2 = jnp.dot(tri_ref[...], oh2.astype(BF16), preferred_element_type=F32)
    c1 = jnp.sum(oh1, axis=0, keepdims=True)
    c2 = jnp.sum(oh2, axis=0, keepdims=True)
    base = base_ref[...]
    rank1 = jnp.sum(oh1 * (base + pre1), axis=-1, keepdims=True)
    rank2 = jnp.sum(oh2 * (base + c1 + pre2), axis=-1, keepdims=True)
    base_ref[...] = base + c1 + c2

    cols = (i1, i2, rank1, rank2, w1, w2)
    out = jnp.zeros(lg_ref.shape, F32)
    for j, col in enumerate(cols):
        out = jnp.where(lane == float(j), col, out)
    o_ref[...] = out
    cnt_ref[...] = jnp.broadcast_to(base + c1 + c2, cnt_ref.shape)


def _dispatch_kernel(pos_ref, last_ref, nt_ref, h_ref, xs_hbm, zero_ref, stage_ref, zsem, sem):
    i = pl.program_id(0)
    tm = h_ref.shape[0]

    @pl.when(i == 0)
    def _():
        zero_ref[...] = jnp.zeros_like(zero_ref)

        def zero_copy(tile):
            row0 = pl.multiple_of(tile * MOE_TILE, MOE_TILE)
            return pltpu.make_async_copy(zero_ref, xs_hbm.at[pl.ds(row0, MOE_TILE)], zsem)

        def for_zeroed_tiles(fn):
            for e in range(N_EXPERTS):
                @pl.when(last_ref[e] >= 0)
                def _():
                    fn(zero_copy(last_ref[e]))

                tail = MOE_TILES - 1 - e

                @pl.when(tail >= nt_ref[0])
                def _():
                    fn(zero_copy(tail))

        for_zeroed_tiles(lambda cp: cp.start())
        for_zeroed_tiles(lambda cp: cp.wait())

    slot = i % 2
    stage_ref[slot] = h_ref[...]

    def row_copy(step, s, t, k):
        dst = xs_hbm.at[pl.ds(pos_ref[TOP_K * (step * tm + t) + k], 1)]
        return pltpu.make_async_copy(stage_ref.at[s, pl.ds(t, 1)], dst, sem.at[s])

    def issue(t, carry):
        for k in range(TOP_K):
            row_copy(i, slot, t, k).start()
        return carry

    def drain_of(step, s):
        def drain(t, carry):
            for k in range(TOP_K):
                row_copy(step, s, t, k).wait()
            return carry
        lax.fori_loop(0, tm, drain, 0, unroll=ROW_COPY_UNROLL)

    lax.fori_loop(0, tm, issue, 0, unroll=ROW_COPY_UNROLL)

    @pl.when(i > 0)
    def _():
        drain_of(i - 1, 1 - slot)

    @pl.when(i == pl.num_programs(0) - 1)
    def _():
        drain_of(i, slot)


def _expert_kernel(ce_ref, ct_ref, cn_ref, nch_ref, nt_ref, xs_hbm, wg_ref, wu_ref, wd_ref, y_hbm,
                   acc_ref, xb_ref, xstage_ref, wgb_ref, wub_ref, wdb_ref, xsem, osem):
    c = pl.program_id(0)
    f = pl.program_id(1)
    last_f = pl.num_programs(1) - 1

    def tile_rows(tile):
        return pl.ds(pl.multiple_of(tile * MOE_TILE, MOE_TILE), MOE_TILE)

    def out_copy(slot, tile):
        return pltpu.make_async_copy(acc_ref.at[slot], y_hbm.at[tile_rows(tile)], osem)

    @pl.when((c == 0) & (f == 0))
    def _():
        acc_ref[0] = jnp.zeros((MOE_TILE, D_MODEL), F32)
        for e in range(N_EXPERTS):
            tail = MOE_TILES - 1 - e

            @pl.when(tail >= nt_ref[0])
            def _():
                cp = out_copy(0, tail)
                cp.start()
                cp.wait()

    @pl.when(c < nch_ref[0])
    def _():
        n = cn_ref[c]
        t0 = ct_ref[c]

        def x_copy(j):
            return pltpu.make_async_copy(xs_hbm.at[tile_rows(t0 + j)], xstage_ref.at[j % 2], xsem.at[j % 2])

        @pl.when(f == 0)
        def _():
            x_copy(0).start()

        wgb_ref[...] = wg_ref[...].astype(BF16)
        wub_ref[...] = wu_ref[...].astype(BF16)
        wdb_ref[...] = wd_ref[...].astype(BF16)

        def tile_step(j, carry):
            @pl.when(f == 0)
            def _():
                @pl.when(j + 1 < n)
                def _():
                    x_copy(j + 1).start()

                x_copy(j).wait()
                xb_ref[j] = xstage_ref[j % 2].astype(BF16)
                acc_ref[j] = jnp.zeros((MOE_TILE, D_MODEL), F32)

            x = xb_ref[j]
            g = jnp.dot(x, wgb_ref[...], preferred_element_type=F32)
            u = jnp.dot(x, wub_ref[...], preferred_element_type=F32)
            a = (g * _sigmoid(g) * u).astype(BF16)
            acc_ref[j] += jnp.dot(a, wdb_ref[...], preferred_element_type=F32)

            @pl.when(f == last_f)
            def _():
                @pl.when(j > 0)
                def _():
                    out_copy(j - 1, t0 + j - 1).wait()

                out_copy(j, t0 + j).start()

            return carry

        lax.fori_loop(0, n, tile_step, 0)

        @pl.when(f == last_f)
        def _():
            out_copy(n - 1, t0 + n - 1).wait()


def _combine_kernel(pos_ref, route_ref, x_ref, g2_ref, y_hbm, op_ref, os_ref, buf_ref, sem):
    i = pl.program_id(0)
    n = pl.num_programs(0)
    tm = x_ref.shape[0]
    slot = i % 2

    def row_copy(step, s, t, k):
        src = y_hbm.at[pl.ds(pos_ref[TOP_K * (step * tm + t) + k], 1)]
        return pltpu.make_async_copy(src, buf_ref.at[s, k, pl.ds(t, 1)], sem.at[s])

    def issue(step, s):
        def body(t, carry):
            for k in range(TOP_K):
                row_copy(step, s, t, k).start()
            return carry
        lax.fori_loop(0, tm, body, 0, unroll=ROW_COPY_UNROLL)

    @pl.when(i == 0)
    def _():
        issue(0, 0)

    @pl.when(i + 1 < n)
    def _():
        issue(i + 1, 1 - slot)

    def wait_body(t, carry):
        for k in range(TOP_K):
            row_copy(i, slot, t, k).wait()
        return carry

    lax.fori_loop(0, tm, wait_body, 0, unroll=ROW_COPY_UNROLL)

    lane = lax.broadcasted_iota(jnp.int32, route_ref.shape, 1)
    rt = route_ref[...]
    w1 = jnp.sum(jnp.where(lane == 2 * TOP_K, rt, 0.0), axis=-1, keepdims=True)
    w2 = jnp.sum(jnp.where(lane == 2 * TOP_K + 1, rt, 0.0), axis=-1, keepdims=True)
    out = x_ref[...] + g2_ref[...] * (w1 * buf_ref[slot, 0] + w2 * buf_ref[slot, 1])

    @pl.when(i < N_PROMPT // tm)
    def _():
        op_ref[...] = out

    @pl.when(i >= N_PROMPT // tm)
    def _():
        os_ref[...] = out


def _moe(h, x, logits, wg, wu, wd, mod, l):
    route, cnt = pl.pallas_call(
        _route_kernel, grid=(N_TOK // ROUTE_TM,),
        in_specs=[pl.BlockSpec((ROUTE_TM, LANES), lambda i: (i, 0))],
        out_specs=[pl.BlockSpec((ROUTE_TM, LANES), lambda i: (i, 0)), pl.BlockSpec((8, LANES), lambda i: (0, 0))],
        out_shape=[jax.ShapeDtypeStruct((N_TOK, LANES), F32), jax.ShapeDtypeStruct((8, LANES), F32)],
        scratch_shapes=[pltpu.VMEM((1, LANES), F32), pltpu.VMEM((ROUTE_TM, ROUTE_TM), BF16)],
        compiler_params=_cparams(("arbitrary",)), name="route",
    )(logits)

    expert = route[:, 0:TOP_K].astype(jnp.int32)
    rank = route[:, TOP_K:2 * TOP_K].astype(jnp.int32)
    n_sub = (cnt[0, :N_EXPERTS].astype(jnp.int32) + MOE_TILE - 1) // MOE_TILE
    end = jnp.cumsum(n_sub)
    start = end - n_sub
    start_of = jnp.sum(jnp.where(expert[:, :, None] == jnp.arange(N_EXPERTS), start, 0), axis=-1)
    pos = (start_of * MOE_TILE + rank).reshape(-1)
    n_tiles = end[N_EXPERTS - 1:]
    last_tile = jnp.where(n_sub > 0, end - 1, -1).astype(jnp.int32)

    xs = pl.pallas_call(
        _dispatch_kernel,
        grid_spec=pltpu.PrefetchScalarGridSpec(
            num_scalar_prefetch=3, grid=(N_TOK // DISPATCH_TM,),
            in_specs=[pl.BlockSpec((DISPATCH_TM, D_MODEL), lambda i, p, lt, nt: (i, 0))],
            out_specs=pl.BlockSpec(memory_space=pl.ANY),
            scratch_shapes=[pltpu.VMEM((MOE_TILE, D_MODEL), F32), pltpu.VMEM((2, DISPATCH_TM, D_MODEL), F32),
                            pltpu.SemaphoreType.DMA(()), pltpu.SemaphoreType.DMA((2,))]),
        out_shape=jax.ShapeDtypeStruct((MOE_ROWS, D_MODEL), F32),
        compiler_params=_cparams(("arbitrary",)), name="moe_dispatch",
    )(pos, last_tile, n_tiles, h)

    n_chunk = (n_sub + MOE_CHUNK - 1) // MOE_CHUNK
    chunk_end = jnp.cumsum(n_chunk)
    cidx = jnp.arange(MOE_CHUNKS)
    c_expert = jnp.minimum(jnp.sum(cidx[:, None] >= chunk_end[None, :], axis=1), N_EXPERTS - 1).astype(jnp.int32)
    c_k = cidx - (chunk_end - n_chunk)[c_expert]
    c_tile0 = (start[c_expert] + c_k * MOE_CHUNK).astype(jnp.int32)
    c_ntiles = jnp.clip(n_sub[c_expert] - c_k * MOE_CHUNK, 0, MOE_CHUNK).astype(jnp.int32)
    n_chunks = chunk_end[N_EXPERTS - 1:]

    nf = D_FF_EXPERT // MOE_TF

    def w_idx(c, f, ce, nch):
        live = c < nch[0]
        return ce[jnp.minimum(c, nch[0] - 1)], jnp.where(live, f, nf - 1)

    def up_map(c, f, ce, ct, cn, nch, nt):
        e, ff = w_idx(c, f, ce, nch)
        return (e, 0, ff)

    def down_map(c, f, ce, ct, cn, nch, nt):
        e, ff = w_idx(c, f, ce, nch)
        return (e, ff, 0)

    y = pl.pallas_call(
        _expert_kernel,
        grid_spec=pltpu.PrefetchScalarGridSpec(
            num_scalar_prefetch=5, grid=(MOE_CHUNKS, nf),
            in_specs=[pl.BlockSpec(memory_space=pl.ANY),
                      pl.BlockSpec((None, D_MODEL, MOE_TF), up_map),
                      pl.BlockSpec((None, D_MODEL, MOE_TF), up_map),
                      pl.BlockSpec((None, MOE_TF, D_MODEL), down_map)],
            out_specs=pl.BlockSpec(memory_space=pl.ANY),
            scratch_shapes=[pltpu.VMEM((MOE_CHUNK, MOE_TILE, D_MODEL), F32),
                            pltpu.VMEM((MOE_CHUNK, MOE_TILE, D_MODEL), BF16),
                            pltpu.VMEM((2, MOE_TILE, D_MODEL), F32),
                            pltpu.VMEM((D_MODEL, MOE_TF), BF16), pltpu.VMEM((D_MODEL, MOE_TF), BF16),
                            pltpu.VMEM((MOE_TF, D_MODEL), BF16),
                            pltpu.SemaphoreType.DMA((2,)), pltpu.SemaphoreType.DMA(())]),
        out_shape=jax.ShapeDtypeStruct((MOE_ROWS, D_MODEL), F32),
        compiler_params=_cparams(("arbitrary", "arbitrary")), name="moe_experts",
    )(c_expert, c_tile0, c_ntiles, n_chunks, n_tiles, xs, wg, wu, wd)

    tm = COMBINE_TM
    return pl.pallas_call(
        _combine_kernel,
        grid_spec=pltpu.PrefetchScalarGridSpec(
            num_scalar_prefetch=1, grid=(N_TOK // tm,),
            in_specs=[pl.BlockSpec((tm, LANES), lambda i, p: (i, 0)),
                      pl.BlockSpec((tm, D_MODEL), lambda i, p: (i, 0)),
                      pl.BlockSpec((None, None, 1, D_MODEL), lambda i, p: (l, _cond_row(i, tm), 0, 5)),
                      pl.BlockSpec(memory_space=pl.ANY)],
            out_specs=[pl.BlockSpec((tm, D_MODEL), lambda i, p: (jnp.minimum(i, N_PROMPT // tm - 1), 0)),
                       pl.BlockSpec((tm, D_MODEL), lambda i, p: (jnp.maximum(i - N_PROMPT // tm, 0), 0))],
            scratch_shapes=[pltpu.VMEM((2, TOP_K, tm, D_MODEL), F32), pltpu.SemaphoreType.DMA((2,))]),
        out_shape=[jax.ShapeDtypeStruct((N_PROMPT, D_MODEL), F32), jax.ShapeDtypeStruct((N_SAMPLE, D_MODEL), F32)],
        compiler_params=_cparams(("arbitrary",)), name="moe_combine",
    )(pos, route, x, mod, y)


def _kv_leaf_kernel(*refs):
    ins, outs = refs[:4 * DEPTH], refs[4 * DEPTH:]
    for l in range(DEPTH):
        for j in range(4):
            outs[j][l] = ins[4 * l + j][...]


def _kv_leaves(per_layer):
    widths = (A_KV, A_KV, C_W, C_W)
    return pl.pallas_call(
        _kv_leaf_kernel, grid=(BATCH,),
        in_specs=[pl.BlockSpec((SEQ, w), lambda b: (b, 0)) for _ in range(DEPTH) for w in widths],
        out_specs=[pl.BlockSpec((None, DEPTH, SEQ, w), lambda b: (b, 0, 0, 0)) for w in widths],
        out_shape=[jax.ShapeDtypeStruct((BATCH, DEPTH, SEQ, w), F32) for w in widths],
        compiler_params=_cparams(("parallel",)), name="kv_leaves",
    )(*[a for layer in per_layer for a in layer])


def _rope_tables():
    t = jnp.arange(DEC_SEQ)
    row = (t // GRID_W).astype(F32)
    col = (t % GRID_W).astype(F32)
    n_freq = HEAD_DIM // 4
    inv = ROPE_BASE ** (-jnp.arange(n_freq, dtype=F32) / n_freq)
    ang = jnp.concatenate([row[:, None] * inv, col[:, None] * inv], axis=-1)
    cos, sin = jnp.cos(ang), jnp.sin(ang)
    cos_h = jnp.concatenate([cos, cos], axis=-1)
    sin_h = jnp.concatenate([-sin, sin], axis=-1)
    cos_l = jnp.tile(jnp.concatenate([cos_h, cos_h], axis=-1), (DEC_BATCH, 1))
    sin_l = jnp.tile(jnp.concatenate([sin_h, sin_h], axis=-1), (DEC_BATCH, 1))
    cos_t = jnp.concatenate([jnp.ones((N_PROMPT, LANES), F32), cos_l], axis=0)
    sin_t = jnp.concatenate([jnp.zeros((N_PROMPT, LANES), F32), sin_l], axis=0)
    return cos_t, sin_t


def kernel(x_prompt, x_sample, cache_a_k, cache_a_v, cache_c_k, cache_c_v, c, c_ctx, w_mod, b_mod, norm1_g, norm2_g, w_in, qk_norm_a, qk_norm_c, sink_a, rpb_c, w_branch_a, w_branch_b, w_branch_c, w_out, w_ff_gate, w_ff_up, w_ff_down, w_router, w_exp_gate, w_exp_up, w_exp_down):
    x = (x_prompt.reshape(N_PROMPT, D_MODEL), x_sample.reshape(N_SAMPLE, D_MODEL))
    cond =jnp.concatenate([c_ctx[None, :], c], axis=0)
    cond_t = jnp.broadcast_to(cond[:, :, None], (N_COND, D_MODEL, LANES))
    mod = _modulation(cond_t, w_mod, b_mod)
    cos_t, sin_t = _rope_tables()
    bias = _nbr_bias_tables(rpb_c)
    ck_a = cache_a_k.reshape(DEC_BATCH, DEPTH, PAST_LEN, A_KV)
    cv_a = cache_a_v.reshape(DEC_BATCH, DEPTH, PAST_LEN, A_KV)
    ck_c = cache_c_k.reshape(DEC_BATCH, DEPTH, PAST_LEN, C_W)
    cv_c = cache_c_v.reshape(DEC_BATCH, DEPTH, PAST_LEN, C_W)

    new_kv = []
    h = x
    for l in range(DEPTH):
        qa, ka, va, fb, qc, kc, vc, gates = _in_proj(h, w_in, cos_t, sin_t, qk_norm_a[l], qk_norm_c[l], l,
                                                     norm=(norm1_g[l], mod) if l == 0 else None)
        oa_p, oc_p = _ctx_attn(sink_a, qa, ka, va, qc, kc, vc, l)
        oa_s = _win_attn(sink_a, qa, ka, va, ck_a, cv_a, l)
        oc_s = _nbr_attn(qc, kc, vc, ck_c, cv_c, bias, l)
        ob_p = _fourier(fb, BATCH, SEQ, 0, SEQ)
        ob_s = _fourier(fb, DEC_BATCH, DEC_SEQ, N_PROMPT, 512)
        branches = ((oa_p, ob_p, oc_p), (oa_s, ob_s, oc_s), gates, x,
                    w_branch_a, w_branch_b, w_branch_c, w_out, mod, l, norm2_g[l])
        i = l // 2
        if l % 2 == 0:
            x, h2 = _merge(*branches)
            x, h = _ffn(h2, x, w_ff_gate, w_ff_up, w_ff_down, mod, l, norm1_g[l + 1])
        else:
            x, h2, logits = _merge(*branches, w_router=w_router[i])
            xp, xs = _moe(h2, x, logits, w_exp_gate[i], w_exp_up[i], w_exp_down[i], mod, l)
        new_kv.append((ka, va, kc, vc))

    new_ak, new_av, new_ck, new_cv = _kv_leaves(new_kv)
    return (xp.reshape(BATCH, SEQ, D_MODEL), xs.reshape(DEC_BATCH, DEC_SEQ, D_MODEL),
            new_ak.reshape(BATCH, DEPTH, SEQ, A_KV_HEADS, HEAD_DIM),
            new_av.reshape(BATCH, DEPTH, SEQ, A_KV_HEADS, HEAD_DIM),
            new_ck.reshape(BATCH, DEPTH, SEQ, C_HEADS, HEAD_DIM),
            new_cv.reshape(BATCH, DEPTH, SEQ, C_HEADS, HEAD_DIM))
```

```python
import functools

import numpy as np
import jax
import jax.numpy as jnp
from jax import lax
from jax.experimental import pallas as pl
from jax.experimental.pallas import tpu as pltpu

F32 = jnp.float32
BF16 = jnp.bfloat16

D_MODEL = 1024
BATCH = 16
SEQ = 256
DEPTH = 2
DEC_BATCH = 2
DEC_SEQ = 2048
PAST_LEN = 512
GRID_W = 64
HEAD_DIM = 64
SCALE = HEAD_DIM ** -0.5
A_HEADS = 8
A_KV_HEADS = 2
A_GROUP = A_HEADS // A_KV_HEADS
A_WINDOW = 128
A_BLOCK = 128
B_GROUPS = 8
B_GROUP_DIM = 64
B_WIDTH = B_GROUPS * B_GROUP_DIM
C_HEADS = 8
C_WIN_ROWS = 8
C_WIN_COLS = 16
A_Q = A_HEADS * HEAD_DIM
A_KV = A_KV_HEADS * HEAD_DIM
C_W = C_HEADS * HEAD_DIM
QKV_WIDTH = A_Q + 2 * A_KV + B_WIDTH + 3 * C_W
N_BRANCH = 3
GATE_WIDTH = N_BRANCH * D_MODEL
IN_WIDTH = QKV_WIDTH + GATE_WIDTH
D_FF = 2816
N_EXPERTS = 8
D_FF_EXPERT = 3584
ROPE_BASE = 10000.0
RMS_EPS = 1e-6
NEG_INF = -1e30

N_PROMPT = BATCH * SEQ
N_SAMPLE = DEC_BATCH * DEC_SEQ
N_TOK = N_PROMPT + N_SAMPLE
N_COND = 1 + DEC_BATCH
LANES = 128
NORM_SLAB = 256
C_QROWS = 4
C_QBLOCK = C_QROWS * GRID_W
C_DR_SLOTS = 2 * C_WIN_ROWS
VMEM_LIMIT = 56 * 1024 * 1024


def _cparams(sem):
    return pltpu.CompilerParams(dimension_semantics=sem, vmem_limit_bytes=VMEM_LIMIT)


def _sigmoid(x):
    return 1.0 / (1.0 + jnp.exp(-x))


def _cond_row(tile, tm):
    return jnp.maximum(tile * tm // DEC_SEQ - 1, 0)


def _mod_kernel(ct_ref, w_ref, b_ref, o_ref, silu_ref):
    @pl.when((pl.program_id(0) == 0) & (pl.program_id(1) == 0))
    def _():
        cb = ct_ref[...]
        silu_ref[...] = cb * _sigmoid(cb)

    tn = w_ref.shape[1]
    for r in range(N_COND):
        s = silu_ref[r]
        for cc in range(tn // LANES):
            sl = slice(cc * LANES, (cc + 1) * LANES)
            o_ref[r, :, sl] = jnp.sum(w_ref[:, sl] * s, axis=0, keepdims=True) + b_ref[:, sl]


def _modulation(cond_t, w_mod, b_mod):
    tn = 1024
    n = 6 * D_MODEL
    return pl.pallas_call(
        _mod_kernel,
        grid=(DEPTH, n // tn),
        in_specs=[
            pl.BlockSpec((N_COND, D_MODEL, LANES), lambda l, j: (0, 0, 0)),
            pl.BlockSpec((None, D_MODEL, tn), lambda l, j: (l, 0, j)),
            pl.BlockSpec((None, 1, tn), lambda l, j: (l, 0, j)),
        ],
        out_specs=pl.BlockSpec((None, N_COND, 1, tn), lambda l, j: (l, 0, 0, j)),
        out_shape=jax.ShapeDtypeStruct((DEPTH, N_COND, 1, n), F32),
        scratch_shapes=[pltpu.VMEM((N_COND, D_MODEL, LANES), F32)],
        compiler_params=_cparams(("arbitrary", "arbitrary")),
        name="modulation",
    )(cond_t, w_mod, b_mod.reshape(DEPTH, 1, n))


def _adaln_math(x, g, sh, sc):
    ms = jnp.mean(x * x, axis=-1, keepdims=True)
    y = x * lax.rsqrt(ms + RMS_EPS) * g
    return y * (1.0 + sc) + sh


def _split_rows(tm, width, joined=False):
    n_p = N_PROMPT // tm
    latent0 = n_p if joined else 0
    return (pl.BlockSpec((tm, width), lambda i: (jnp.minimum(i, n_p - 1), 0)),
            pl.BlockSpec((tm, width), lambda i: (jnp.maximum(i - n_p, 0) + latent0, 0)))


def _split_operands(x):
    return (x, False) if isinstance(x, tuple) else ((x, x), True)


def _read_split(p_ref, s_ref):
    is_ctx = pl.program_id(0) < N_PROMPT // p_ref.shape[0]
    return jnp.where(is_ctx, p_ref[...], s_ref[...])


WEIGHT_ROW_CHUNK = 128


def _load_weight_bf16(src_rows, dst_ref, stage_ref, sem):
    n = dst_ref.shape[0] // WEIGHT_ROW_CHUNK

    def chunk_copy(k):
        return pltpu.make_async_copy(src_rows(k), stage_ref.at[k % 2], sem.at[k % 2])

    chunk_copy(0).start()
    for k in range(n):
        if k + 1 < n:
            chunk_copy(k + 1).start()
        chunk_copy(k).wait()
        dst_ref[k * WEIGHT_ROW_CHUNK:(k + 1) * WEIGHT_ROW_CHUNK, :] = stage_ref[k % 2].astype(BF16)


def _row_chunk(k):
    return pl.ds(k * WEIGHT_ROW_CHUNK, WEIGHT_ROW_CHUNK)


def _head_norm(x, gain, bd):
    sq = x * x
    hi = sq.astype(BF16)
    lo = (sq - hi.astype(F32)).astype(BF16)
    ms = jnp.dot(hi, bd, preferred_element_type=F32) + jnp.dot(lo, bd, preferred_element_type=F32)
    return x * lax.rsqrt(ms + RMS_EPS) * gain


def _rope(x, cos, sin_signed, first_half):
    half = HEAD_DIM // 2
    swapped = jnp.where(first_half, pltpu.roll(x, x.shape[1] - half, 1), pltpu.roll(x, half, 1))
    return x * cos + swapped * sin_signed


def _in_proj_kernel(*refs, l, fused_norm):
    n_lead = 5 if fused_norm else 1
    (cos_ref, sin_ref, ga_ref, gc_ref, w_hbm, qa_ref, ka_ref, va_ref, fb_ref, qc_ref, kc_ref, vc_ref, gt_ref,
     wb_ref, stage_ref, sem) = refs[n_lead:]

    @pl.when(pl.program_id(0) == 0)
    def _():
        _load_weight_bf16(lambda k: w_hbm.at[l, _row_chunk(k)], wb_ref, stage_ref, sem)

    if fused_norm:
        xp_ref, xs_ref, ng_ref, sh_ref, sc_ref = refs[:n_lead]
        h = _adaln_math(_read_split(xp_ref, xs_ref), ng_ref[...], sh_ref[...], sc_ref[...]).astype(BF16)
    else:
        h = refs[0][...]

    def proj(off, width):
        return jnp.dot(h, wb_ref[:, off:off + width], preferred_element_type=F32)

    r = lax.broadcasted_iota(jnp.int32, (NORM_SLAB, NORM_SLAB), 0) // HEAD_DIM
    c = lax.broadcasted_iota(jnp.int32, (NORM_SLAB, NORM_SLAB), 1) // HEAD_DIM
    bd = jnp.where(r == c, 1.0 / HEAD_DIM, 0.0).astype(BF16)
    lane = lax.broadcasted_iota(jnp.int32, (1, NORM_SLAB), 1)
    first_half = (lane % HEAD_DIM) < HEAD_DIM // 2
    cos = jnp.concatenate([cos_ref[...]] * (NORM_SLAB // LANES), axis=1)
    sin = jnp.concatenate([sin_ref[...]] * (NORM_SLAB // LANES), axis=1)
    gqa, gka = ga_ref[0:1, :], ga_ref[1:2, :]
    gqc, gkc = gc_ref[0:1, :], gc_ref[1:2, :]
    slabs = lambda p: [p[:, s * NORM_SLAB:(s + 1) * NORM_SLAB] for s in range(p.shape[1] // NORM_SLAB)]
    cat = lambda parts: jnp.concatenate(parts, axis=1)

    off = 0
    qa_ref[...] = cat([_rope(_head_norm(x, gqa, bd), cos, sin, first_half) for x in slabs(proj(off, A_Q))]
                      ).astype(BF16)
    off += A_Q
    kv = proj(off, 2 * A_KV)
    ka_ref[...] = _rope(_head_norm(kv, gka, bd), cos, sin, first_half)[:, :A_KV]
    va_ref[...] = kv[:, A_KV:]
    off += 2 * A_KV
    fb_ref[...] = proj(off, B_WIDTH).astype(BF16)
    off += B_WIDTH
    qc_ref[...] = cat([_head_norm(x, gqc, bd) for x in slabs(proj(off, C_W))]).astype(BF16)
    off += C_W
    kc_ref[...] = cat([_head_norm(x, gkc, bd) for x in slabs(proj(off, C_W))])
    off += C_W
    vc_ref[...] = proj(off, C_W)
    off += C_W
    for j in range(N_BRANCH):
        cols = slice(j * D_MODEL, (j + 1) * D_MODEL)
        gt_ref[:, cols] = _sigmoid(proj(off + j * D_MODEL, D_MODEL)).astype(BF16)


def _in_proj(h, w_in, cos_t, sin_t, qk_a, qk_c, l, norm=None):
    tm = 512
    row = lambda w: pl.BlockSpec((tm, w), lambda i: (i, 0))
    widths = (A_Q, A_KV, A_KV, B_WIDTH, C_W, C_W, C_W, GATE_WIDTH)
    dtypes = (BF16, F32, F32, BF16, BF16, F32, F32, BF16)
    if norm is None:
        lead_specs, lead = [row(D_MODEL)], [h]
    else:
        norm_g, mod = norm
        x, joined = _split_operands(h)
        mspec = lambda which: pl.BlockSpec((None, None, 1, D_MODEL), lambda i: (l, _cond_row(i, tm), 0, which))
        lead_specs = [*_split_rows(tm, D_MODEL, joined), pl.BlockSpec((1, D_MODEL), lambda i: (0, 0)),
                      mspec(0), mspec(1)]
        lead = [*x, norm_g.reshape(1, D_MODEL), mod, mod]
    return pl.pallas_call(
        functools.partial(_in_proj_kernel, l=l, fused_norm=norm is not None), grid=(N_TOK // tm,),
        in_specs=[*lead_specs, row(LANES), row(LANES),
                  pl.BlockSpec((2, NORM_SLAB), lambda i: (0, 0)), pl.BlockSpec((2, NORM_SLAB), lambda i: (0, 0)),
                  pl.BlockSpec(memory_space=pl.ANY)],
        out_specs=[row(w) for w in widths],
        out_shape=[jax.ShapeDtypeStruct((N_TOK, w), d) for w, d in zip(widths, dtypes)],
        scratch_shapes=[pltpu.VMEM((D_MODEL, IN_WIDTH), BF16),
                        pltpu.VMEM((2, WEIGHT_ROW_CHUNK, IN_WIDTH), F32), pltpu.SemaphoreType.DMA((2,))],
        compiler_params=_cparams(("arbitrary",)), name="in_proj",
    )(*lead, cos_t, sin_t, jnp.tile(qk_a, (1, NORM_SLAB // HEAD_DIM)), jnp.tile(qk_c, (1, NORM_SLAB // HEAD_DIM)), w_in)


def _nt_dot(a, b):
    return lax.dot_general(a, b, (((1,), (1,)), ((), ())), preferred_element_type=F32)


def _head(x, h):
    return x[:, h * HEAD_DIM:(h + 1) * HEAD_DIM]


def _stacked_softmax(parts, sink):
    m = parts[0].max(axis=-1, keepdims=True)
    for s in parts[1:]:
        m = jnp.maximum(m, s.max(axis=-1, keepdims=True))
    if sink is not None:
        m = jnp.maximum(m, sink)
    den = jnp.exp(sink - m) if sink is not None else 0.0
    es = []
    for s in parts:
        e = jnp.exp(s - m)
        den = den + e.sum(axis=-1, keepdims=True)
        es.append(e.astype(BF16))
    return es, 1.0 / den


def _sink_column(sink_ref, l, rows_per_head):
    return jnp.concatenate([jnp.full((rows_per_head, 1), sink_ref[l, h], F32) for h in range(A_HEADS)], axis=0)


def _gqa_queries(qa, g):
    return jnp.concatenate([_head(qa, g * A_GROUP + i) for i in range(A_GROUP)], axis=0)


def _ctx_attn_kernel(sink_ref, qa_ref, ka_ref, va_ref, qc_ref, kc_ref, vc_ref, oa_ref, oc_ref, *, l):
    t = SEQ
    for bb in range(CTX_BATCHES):
        r = slice(bb * t, (bb + 1) * t)
        qa = qa_ref[r, :] * SCALE
        ka = ka_ref[r, :].astype(BF16)
        va = va_ref[r, :].astype(BF16)
        s = jnp.concatenate([_nt_dot(_gqa_queries(qa, g), _head(ka, g)) for g in range(A_KV_HEADS)], axis=0)
        (e,), inv = _stacked_softmax([s], _sink_column(sink_ref, l, t))
        outs = []
        for g in range(A_KV_HEADS):
            rows = slice(g * A_GROUP * t, (g + 1) * A_GROUP * t)
            o = jnp.dot(e[rows], _head(va, g), preferred_element_type=F32) * inv[rows]
            outs += [o[i * t:(i + 1) * t] for i in range(A_GROUP)]
        oa_ref[r, :] = jnp.concatenate(outs, axis=1).astype(BF16)

        qc = qc_ref[r, :] * SCALE
        kc = kc_ref[r, :].astype(BF16)
        vc = vc_ref[r, :].astype(BF16)
        s = jnp.concatenate([_nt_dot(_head(qc, h), _head(kc, h)) for h in range(C_HEADS)], axis=0)
        (e,), inv = _stacked_softmax([s], None)
        outs = [jnp.dot(e[h * t:(h + 1) * t], _head(vc, h), preferred_element_type=F32) * inv[h * t:(h + 1) * t]
                for h in range(C_HEADS)]
        oc_ref[r, :] = jnp.concatenate(outs, axis=1).astype(BF16)


CTX_BATCHES = 4


def _ctx_attn(sink_a, qa, ka, va, qc, kc, vc, l):
    blk = lambda w: pl.BlockSpec((CTX_BATCHES * SEQ, w), lambda b: (b, 0))
    return pl.pallas_call(
        functools.partial(_ctx_attn_kernel, l=l), grid=(BATCH // CTX_BATCHES,),
        in_specs=[pl.BlockSpec(memory_space=pltpu.SMEM),
                  blk(A_Q), blk(A_KV), blk(A_KV), blk(C_W), blk(C_W), blk(C_W)],
        out_specs=[blk(A_Q), blk(C_W)],
        out_shape=[jax.ShapeDtypeStruct((N_PROMPT, A_Q), BF16), jax.ShapeDtypeStruct((N_PROMPT, C_W), BF16)],
        compiler_params=_cparams(("parallel",)), name="ctx_attn",
    )(sink_a, qa, ka, va, qc, kc, vc)


def _win_attn_kernel(sink_ref, q_ref, kp_ref, kc_ref, kn_ref, vp_ref, vc_ref, vn_ref, ck_ref, cv_ref, mask_ref,
                     o_ref, *, l):
    rows = A_GROUP * A_BLOCK
    mask = mask_ref[...]
    sink = _sink_column(sink_ref, l, A_BLOCK)
    for g in range(A_KV_HEADS):
        sl = slice(g * HEAD_DIM, (g + 1) * HEAD_DIM)
        q = jnp.concatenate([q_ref[:, (g * A_GROUP + i) * HEAD_DIM:(g * A_GROUP + i + 1) * HEAD_DIM]
                             for i in range(A_GROUP)], axis=0) * SCALE
        k_loc = jnp.concatenate([kp_ref[:, sl], kc_ref[:, sl], kn_ref[:, sl]], axis=0).astype(BF16)
        v_loc = jnp.concatenate([vp_ref[:, sl], vc_ref[:, sl], vn_ref[:, sl]], axis=0).astype(BF16)
        s_loc = _nt_dot(q, k_loc) + mask
        s_ctx = _nt_dot(q, ck_ref[:, sl].astype(BF16))
        (e_loc, e_ctx), inv = _stacked_softmax([s_loc, s_ctx], sink[g * rows:(g + 1) * rows])
        o = (jnp.dot(e_loc, v_loc, preferred_element_type=F32)
             + jnp.dot(e_ctx, cv_ref[:, sl].astype(BF16), preferred_element_type=F32)) * inv
        for i in range(A_GROUP):
            h = g * A_GROUP + i
            o_ref[:, h * HEAD_DIM:(h + 1) * HEAD_DIM] = o[i * A_BLOCK:(i + 1) * A_BLOCK].astype(BF16)


def _win_attn(sink_a, qa, ka, va, cache_k, cache_v, l):
    nb = DEC_SEQ // A_BLOCK
    base = N_PROMPT // A_BLOCK

    def nbr(d):
        return lambda b, t: (base + b * nb + jnp.clip(t + d, 0, nb - 1), 0)

    kv = lambda d: pl.BlockSpec((A_BLOCK, A_KV), nbr(d))
    cache = pl.BlockSpec((None, None, PAST_LEN, A_KV), lambda b, t: (b, l, 0, 0))
    qi = np.arange(A_GROUP * A_BLOCK)[:, None] % A_BLOCK
    kj = np.arange(3 * A_BLOCK)[None, :] - A_BLOCK
    band = np.abs(kj - qi) <= A_WINDOW
    masks = np.stack([band & (kj >= 0), band, band & (kj < A_BLOCK)])
    masks = jnp.asarray(np.where(masks, 0.0, NEG_INF).astype(np.float32))
    position = lambda b, t: (jnp.where(t == 0, 0, jnp.where(t == nb - 1, 2, 1)), 0, 0)
    return pl.pallas_call(
        functools.partial(_win_attn_kernel, l=l), grid=(DEC_BATCH, nb),
        in_specs=[pl.BlockSpec(memory_space=pltpu.SMEM),
                  pl.BlockSpec((A_BLOCK, A_Q), nbr(0)),
                  kv(-1), kv(0), kv(1), kv(-1), kv(0), kv(1), cache, cache,
                  pl.BlockSpec((None, A_GROUP * A_BLOCK, 3 * A_BLOCK), position)],
        out_specs=pl.BlockSpec((A_BLOCK, A_Q), lambda b, t: (b * nb + t, 0)),
        out_shape=jax.ShapeDtypeStruct((N_SAMPLE, A_Q), BF16),
        compiler_params=_cparams(("parallel", "parallel")), name="win_attn",
    )(sink_a, qa, ka, ka, ka, va, va, va, cache_k, cache_v, masks)


def _nbr_attn_kernel(q_ref, kp_ref, kc_ref, kn_ref, vp_ref, vc_ref, vn_ref, ck_ref, cv_ref, tab_ref, o_ref,
                     bias_ref):
    j = pl.program_id(0)
    nb = pl.num_programs(0)
    slots = _nbr_row_slots()

    def build(cls):
        for h in range(C_HEADS):
            for qr in range(C_QROWS):
                for kk in range(3 * C_QROWS):
                    bias_ref[h, qr * GRID_W:(qr + 1) * GRID_W, kk * GRID_W:(kk + 1) * GRID_W] = (
                        tab_ref[h, slots[cls][qr][kk]])

    first_of_batch = pl.program_id(1) == 0
    for cls, at in enumerate((0, 1, nb - 1)):
        @pl.when(first_of_batch & (j == at))
        def _():
            build(cls)

    tq = C_QBLOCK
    q = q_ref[...] * SCALE
    k_loc = jnp.concatenate([kp_ref[...], kc_ref[...], kn_ref[...]], axis=0).astype(BF16)
    v_loc = jnp.concatenate([vp_ref[...], vc_ref[...], vn_ref[...]], axis=0).astype(BF16)
    k_ctx = ck_ref[...].astype(BF16)
    v_ctx = cv_ref[...].astype(BF16)
    s_loc = jnp.concatenate([_nt_dot(_head(q, h), _head(k_loc, h)) for h in range(C_HEADS)], axis=0)
    s_loc = s_loc + bias_ref[...].reshape(C_HEADS * tq, 3 * tq)
    s_ctx = jnp.concatenate([_nt_dot(_head(q, h), _head(k_ctx, h)) for h in range(C_HEADS)], axis=0)
    (e_loc, e_ctx), inv = _stacked_softmax([s_loc, s_ctx], None)
    outs = []
    for h in range(C_HEADS):
        r = slice(h * tq, (h + 1) * tq)
        outs.append((jnp.dot(e_loc[r], _head(v_loc, h), preferred_element_type=F32)
                     + jnp.dot(e_ctx[r], _head(v_ctx, h), preferred_element_type=F32)) * inv[r])
    o_ref[...] = jnp.concatenate(outs, axis=1).astype(BF16)


def _nbr_attn(qc, kc, vc, cache_k, cache_v, bias, l):
    nb = DEC_SEQ // C_QBLOCK
    base = N_PROMPT // C_QBLOCK

    def nbr(d):
        return lambda j, b: (base + b * nb + jnp.clip(j + d, 0, nb - 1), 0)

    kv = lambda d: pl.BlockSpec((C_QBLOCK, C_W), nbr(d))
    cache = pl.BlockSpec((None, None, PAST_LEN, C_W), lambda j, b: (b, l, 0, 0))
    return pl.pallas_call(
        _nbr_attn_kernel, grid=(nb, DEC_BATCH),
        in_specs=[kv(0), kv(-1), kv(0), kv(1), kv(-1), kv(0), kv(1), cache, cache,
                  pl.BlockSpec((None, C_HEADS, C_DR_SLOTS, GRID_W, GRID_W), lambda j, b: (l, 0, 0, 0, 0))],
        out_specs=pl.BlockSpec((C_QBLOCK, C_W), lambda j, b: (b * nb + j, 0)),
        out_shape=jax.ShapeDtypeStruct((N_SAMPLE, C_W), BF16),
        scratch_shapes=[pltpu.VMEM((C_HEADS, C_QBLOCK, 3 * C_QBLOCK), F32)],
        compiler_params=_cparams(("arbitrary", "arbitrary")), name="nbr_attn",
    )(qc, kc, kc, kc, vc, vc, vc, cache_k, cache_v, bias)


def _nbr_bias_tables(rpb):
    qcol = np.arange(GRID_W)
    qcs = np.clip(qcol - C_WIN_COLS // 2, 0, GRID_W - C_WIN_COLS)
    kcol = np.arange(GRID_W)
    col_ok = (kcol[None, :] >= qcs[:, None]) & (kcol[None, :] < qcs[:, None] + C_WIN_COLS)
    dc = np.clip(kcol[None, :] - qcol[:, None], -(C_WIN_COLS - 1), C_WIN_COLS - 1) + C_WIN_COLS - 1
    onehot_dc = (dc.reshape(-1)[None, :] == np.arange(2 * C_WIN_COLS - 1)[:, None]).astype(np.float32)
    t = jnp.einsum('lhab,bx->lhax', rpb, jnp.asarray(onehot_dc), precision=lax.Precision.HIGHEST)
    t = jnp.where(jnp.asarray(col_ok.reshape(-1)), t, NEG_INF)
    t = jnp.concatenate([t, jnp.full((DEPTH, C_HEADS, 1, GRID_W * GRID_W), NEG_INF, F32)], axis=2)
    return t.reshape(DEPTH, C_HEADS, C_DR_SLOTS, GRID_W, GRID_W)


def _nbr_row_slots():
    rows = DEC_SEQ // GRID_W
    slots = []
    for j in (0, 3, rows // C_QROWS - 1):
        per_q = []
        for qr in range(C_QROWS):
            r = C_QROWS * j + qr
            rs = min(max(r - C_WIN_ROWS // 2, 0), rows - C_WIN_ROWS)
            per_k = []
            for kk in range(3 * C_QROWS):
                kabs = C_QROWS * (j - 1) + kk
                per_k.append(kabs - r + C_WIN_ROWS - 1 if rs <= kabs < rs + C_WIN_ROWS else C_DR_SLOTS - 1)
            per_q.append(per_k)
        slots.append(per_q)
    return slots


def _fourier_kernel(u_ref, bc_ref, bs_ref, cl_ref, sl_ref, o_ref, zc_ref, zs_ref):
    @pl.when(pl.program_id(1) == 0)
    def _():
        u = u_ref[...]
        zc_ref[...] = jnp.dot(u, bc_ref[...].astype(BF16), preferred_element_type=F32).astype(BF16)
        zs_ref[...] = jnp.dot(u, bs_ref[...].astype(BF16), preferred_element_type=F32).astype(BF16)

    o = (jnp.dot(cl_ref[...].astype(BF16), zc_ref[...], preferred_element_type=F32)
         - jnp.dot(sl_ref[...].astype(BF16), zs_ref[...], preferred_element_type=F32))
    o_ref[...] = o.astype(BF16)


def _dft_tables(n):
    k = np.arange(n)
    ang = 2.0 * np.pi * ((k[:, None] * k[None, :]) % n) / n
    return np.cos(ang) / np.sqrt(n), np.sin(ang) / np.sqrt(n)


def _channel_dft_tables():
    c, s = _dft_tables(B_GROUP_DIM)
    eye = np.eye(B_GROUPS)
    return np.kron(eye, c).astype(np.float32), np.kron(eye, s).astype(np.float32)


def _fourier(fb, n_batch, seq, row0, tr):
    cl, sl = (jnp.asarray(a.astype(np.float32)) for a in _dft_tables(seq))
    bc, bs = (jnp.asarray(a) for a in _channel_dft_tables())
    nt = seq // tr
    const = pl.BlockSpec((B_WIDTH, B_WIDTH), lambda b, t: (0, 0))
    return pl.pallas_call(
        _fourier_kernel, grid=(n_batch, nt),
        in_specs=[pl.BlockSpec((seq, B_WIDTH), lambda b, t: (row0 // seq + b, 0)), const, const,
                  pl.BlockSpec((tr, seq), lambda b, t: (t, 0)), pl.BlockSpec((tr, seq), lambda b, t: (t, 0))],
        out_specs=pl.BlockSpec((tr, B_WIDTH), lambda b, t: (b * nt + t, 0)),
        out_shape=jax.ShapeDtypeStruct((n_batch * seq, B_WIDTH), BF16),
        scratch_shapes=[pltpu.VMEM((seq, B_WIDTH), BF16), pltpu.VMEM((seq, B_WIDTH), BF16)],
        compiler_params=_cparams(("parallel", "arbitrary")), name=f"fourier_{seq}",
    )(fb, bc, bs, cl, sl)


def _merge_kernel(oap_ref, obp_ref, ocp_ref, oas_ref, obs_ref, ocs_ref, gt_ref, xp_ref, xs_ref,
                  wa_ref, wb_ref, wc_ref, wo_ref, g1_ref, ng_ref, sh_ref, sc_ref, *rest, router):
    if router:
        wr_ref, o_ref, h2_ref, lg_ref, wab_ref, wbb_ref, wcb_ref, wob_ref = rest
    else:
        o_ref, h2_ref, wab_ref, wbb_ref, wcb_ref, wob_ref = rest

    @pl.when(pl.program_id(0) == 0)
    def _():
        wab_ref[...] = wa_ref[...].astype(BF16)
        wbb_ref[...] = wb_ref[...].astype(BF16)
        wcb_ref[...] = wc_ref[...].astype(BF16)
        wob_ref[...] = wo_ref[...].astype(BF16)

    ya = jnp.dot(_read_split(oap_ref, oas_ref), wab_ref[...], preferred_element_type=F32)
    yb = jnp.dot(_read_split(obp_ref, obs_ref), wbb_ref[...], preferred_element_type=F32)
    yc = jnp.dot(_read_split(ocp_ref, ocs_ref), wcb_ref[...], preferred_element_type=F32)
    d = D_MODEL
    m = (gt_ref[:, 0:d].astype(F32) * ya + gt_ref[:, d:2 * d].astype(F32) * yb
         + gt_ref[:, 2 * d:3 * d].astype(F32) * yc)
    y = jnp.dot(m.astype(BF16), wob_ref[...], preferred_element_type=F32)
    x_new = _read_split(xp_ref, xs_ref) + g1_ref[...] * y
    o_ref[...] = x_new
    h2 = _adaln_math(x_new, ng_ref[...], sh_ref[...], sc_ref[...])
    h2_ref[...] = h2.astype(h2_ref.dtype)
    if router:
        w = wr_ref[...]
        w_hi = w.astype(BF16)
        w_lo = (w - w_hi.astype(F32)).astype(BF16)
        h_hi = h2.astype(BF16)
        h_lo = (h2 - h_hi.astype(F32)).astype(BF16)
        hi_terms = jnp.dot(h_hi, jnp.concatenate([w_hi, w_lo], axis=1), preferred_element_type=F32)
        lg_ref[...] = (hi_terms[:, :LANES] + jnp.dot(h_lo, w_hi, preferred_element_type=F32)
                       + hi_terms[:, LANES:])


def _merge(branches_p, branches_s, gates, x, wa, wb, wc, wo, mod, l, norm_g, w_router=None):
    tm = 512
    router = w_router is not None
    x, joined = _split_operands(x)
    row = lambda w: pl.BlockSpec((tm, w), lambda i: (i, 0))
    row_p = lambda w: _split_rows(tm, w)[0]
    row_s = lambda w: _split_rows(tm, w)[1]
    const = lambda r, c: pl.BlockSpec((None, r, c), lambda i: (l, 0, 0), pipeline_mode=pl.Buffered(1))
    mspec = lambda which: pl.BlockSpec((None, None, 1, D_MODEL), lambda i: (l, _cond_row(i, tm), 0, which))
    in_specs = [row_p(A_Q), row_p(B_WIDTH), row_p(C_W), row_s(A_Q), row_s(B_WIDTH), row_s(C_W),
                row(GATE_WIDTH), *_split_rows(tm, D_MODEL, joined),
                const(A_Q, D_MODEL), const(B_WIDTH, D_MODEL), const(C_W, D_MODEL), const(D_MODEL, D_MODEL),
                mspec(2), pl.BlockSpec((1, D_MODEL), lambda i: (0, 0)), mspec(3), mspec(4)]
    operands = [*branches_p, *branches_s, gates, *x, wa, wb, wc, wo, mod, norm_g.reshape(1, D_MODEL), mod, mod]
    out_specs = [row(D_MODEL), row(D_MODEL)]
    out_shape = [jax.ShapeDtypeStruct((N_TOK, D_MODEL), F32),
                 jax.ShapeDtypeStruct((N_TOK, D_MODEL), F32 if router else BF16)]
    if router:
        in_specs.append(pl.BlockSpec((D_MODEL, LANES), lambda i: (0, 0)))
        operands.append(jnp.pad(w_router, ((0, 0), (0, LANES - N_EXPERTS))))
        out_specs.append(row(LANES))
        out_shape.append(jax.ShapeDtypeStruct((N_TOK, LANES), F32))
    return pl.pallas_call(
        functools.partial(_merge_kernel, router=router), grid=(N_TOK // tm,),
        in_specs=in_specs, out_specs=out_specs, out_shape=out_shape,
        scratch_shapes=[pltpu.VMEM((A_Q, D_MODEL), BF16), pltpu.VMEM((B_WIDTH, D_MODEL), BF16),
                        pltpu.VMEM((C_W, D_MODEL), BF16), pltpu.VMEM((D_MODEL, D_MODEL), BF16)],
        compiler_params=_cparams(("arbitrary",)), name="merge_router" if router else "merge",
    )(*operands)


FFN_COL_CHUNK = 256


def _ffn_kernel(h_ref, x_ref, g2_ref, ng_ref, sh_ref, sc_ref, wg_hbm, wu_hbm, wd_hbm, o_ref, hn_ref,
                wgb_ref, wub_ref, wdb_ref, stage_up_ref, stage_dn_ref, sem, *, i_dense):
    @pl.when(pl.program_id(0) == 0)
    def _():
        _load_weight_bf16(lambda k: wg_hbm.at[i_dense, _row_chunk(k)], wgb_ref, stage_up_ref, sem)
        _load_weight_bf16(lambda k: wu_hbm.at[i_dense, _row_chunk(k)], wub_ref, stage_up_ref, sem)
        _load_weight_bf16(lambda k: wd_hbm.at[i_dense, _row_chunk(k)], wdb_ref, stage_dn_ref, sem)

    h = h_ref[...]
    acc = None
    for c in range(D_FF // FFN_COL_CHUNK):
        cols = slice(c * FFN_COL_CHUNK, (c + 1) * FFN_COL_CHUNK)
        g = jnp.dot(h, wgb_ref[:, cols], preferred_element_type=F32)
        u = jnp.dot(h, wub_ref[:, cols], preferred_element_type=F32)
        a = (g * _sigmoid(g) * u).astype(BF16)
        d = jnp.dot(a, wdb_ref[cols, :], preferred_element_type=F32)
        acc = d if acc is None else acc + d
    x_new = x_ref[...] + g2_ref[...] * acc
    o_ref[...] = x_new
    hn_ref[...] = _adaln_math(x_new, ng_ref[...], sh_ref[...], sc_ref[...]).astype(BF16)


def _ffn(h, x, wg, wu, wd, mod, l, next_norm_g):
    tm = 512
    row = lambda dt: pl.BlockSpec((tm, D_MODEL), lambda i: (i, 0))
    mspec = lambda layer, which: pl.BlockSpec((None, None, 1, D_MODEL),
                                              lambda i: (layer, _cond_row(i, tm), 0, which))
    hbm = pl.BlockSpec(memory_space=pl.ANY)
    return pl.pallas_call(
        functools.partial(_ffn_kernel, i_dense=l // 2), grid=(N_TOK // tm,),
        in_specs=[row(BF16), row(F32), mspec(l, 5),
                  pl.BlockSpec((1, D_MODEL), lambda i: (0, 0)), mspec(l + 1, 0), mspec(l + 1, 1),
                  hbm, hbm, hbm],
        out_specs=[row(F32), row(BF16)],
        out_shape=[jax.ShapeDtypeStruct((N_TOK, D_MODEL), F32), jax.ShapeDtypeStruct((N_TOK, D_MODEL), BF16)],
        scratch_shapes=[pltpu.VMEM((D_MODEL, D_FF), BF16), pltpu.VMEM((D_MODEL, D_FF), BF16),
                        pltpu.VMEM((D_FF, D_MODEL), BF16),
                        pltpu.VMEM((2, WEIGHT_ROW_CHUNK, D_FF), F32),
                        pltpu.VMEM((2, WEIGHT_ROW_CHUNK, D_MODEL), F32), pltpu.SemaphoreType.DMA((2,))],
        compiler_params=_cparams(("arbitrary",)), name="ffn",
    )(h, x, mod, next_norm_g.reshape(1, D_MODEL), mod, mod, wg, wu, wd)


TOP_K = 2
MOE_TILE = 256
MOE_TILES = TOP_K * N_TOK // MOE_TILE + N_EXPERTS
MOE_ROWS = MOE_TILES * MOE_TILE
MOE_CHUNK = 10
MOE_CHUNKS = MOE_TILES // MOE_CHUNK + N_EXPERTS
MOE_TF = 896
ROUTE_TM = 512
DISPATCH_TM = 512
COMBINE_TM = 512
DMA_PRIORITIES = 2
ROW_COPY_UNROLL = 16


def _route_kernel(lg_ref, o_ref, cnt_ref, base_ref, tri_ref):
    tm = lg_ref.shape[0]

    @pl.when(pl.program_id(0) == 0)
    def _():
        base_ref[...] = jnp.zeros_like(base_ref)
        r = lax.broadcasted_iota(jnp.int32, (tm, tm), 0)
        c = lax.broadcasted_iota(jnp.int32, (tm, tm), 1)
        tri_ref[...] = jnp.where(r > c, 1.0, 0.0).astype(BF16)

    lane = lax.broadcasted_iota(jnp.int32, lg_ref.shape, 1).astype(F32)
    lg = jnp.where(lane < N_EXPERTS, lg_ref[...], -jnp.inf)
    m1 = lg.max(axis=-1, keepdims=True)
    i1 = jnp.where(lg == m1, lane, float(LANES)).min(axis=-1, keepdims=True)
    rest = jnp.where(lane == i1, -jnp.inf, lg)
    m2 = rest.max(axis=-1, keepdims=True)
    i2 = jnp.where(rest == m2, lane, float(LANES)).min(axis=-1, keepdims=True)
    e2 = jnp.exp(m2 - m1)
    w1 = 1.0 / (1.0 + e2)
    w2 = e2 / (1.0 + e2)

    oh1 = jnp.where(lane == i1, 1.0, 0.0)
    oh2 = jnp.where(lane == i2, 1.0, 0.0)
    pre1 = jnp.dot(tri_ref[...], oh1.astype(BF16), preferred_element_type=F32)
    pre2 = jnp.dot(tri_ref[...], oh2.astype(BF16), preferred_element_type=F32)
    c1 = jnp.sum(oh1, axis=0, keepdims=True)
    c2 = jnp.sum(oh2, axis=0, keepdims=True)
    base = base_ref[...]
    rank1 = jnp.sum(oh1 * (base + pre1), axis=-1, keepdims=True)
    rank2 = jnp.sum(oh2 * (base + c1 + pre2), axis=-1, keepdims=True)
    base_ref[...] = base + c1 + c2

    cols = (i1, i2, rank1, rank2, w1, w2)
    out = jnp.zeros(lg_ref.shape, F32)
    for j, col in enumerate(cols):
        out = jnp.where(lane == float(j), col, out)
    o_ref[...] = out
    cnt_ref[...] = jnp.broadcast_to(base + c1 + c2, cnt_ref.shape)


def _dispatch_kernel(pos_ref, last_ref, nt_ref, h_ref, xs_hbm, zero_ref, stage_ref, zsem, sem):
    i = pl.program_id(0)
    tm = h_ref.shape[0]

    @pl.when(i == 0)
    def _():
        zero_ref[...] = jnp.zeros_like(zero_ref)

        def zero_copy(tile):
            row0 = pl.multiple_of(tile * MOE_TILE, MOE_TILE)
            return pltpu.make_async_copy(zero_ref, xs_hbm.at[pl.ds(row0, MOE_TILE)], zsem)

        def for_zeroed_tiles(fn):
            for e in range(N_EXPERTS):
                @pl.when(last_ref[e] >= 0)
                def _():
                    fn(zero_copy(last_ref[e]))

                tail = MOE_TILES - 1 - e

                @pl.when(tail >= nt_ref[0])
                def _():
                    fn(zero_copy(tail))

        for_zeroed_tiles(lambda cp: cp.start())
        for_zeroed_tiles(lambda cp: cp.wait())

    slot = i % 2
    stage_ref[slot] = h_ref[...]

    def row_copy(step, s, t, k):
        dst = xs_hbm.at[pl.ds(pos_ref[TOP_K * (step * tm + t) + k], 1)]
        return pltpu.make_async_copy(stage_ref.at[s, pl.ds(t, 1)], dst, sem.at[s])

    def issue(t, carry):
        for k in range(TOP_K):
            row_copy(i, slot, t, k).start(priority=k % DMA_PRIORITIES)
        return carry

    def drain_of(step, s):
        def drain(t, carry):
            for k in range(TOP_K):
                row_copy(step, s, t, k).wait()
            return carry
        lax.fori_loop(0, tm, drain, 0, unroll=ROW_COPY_UNROLL)

    lax.fori_loop(0, tm, issue, 0, unroll=ROW_COPY_UNROLL)

    @pl.when(i > 0)
    def _():
        drain_of(i - 1, 1 - slot)

    @pl.when(i == pl.num_programs(0) - 1)
    def _():
        drain_of(i, slot)


def _expert_kernel(ce_ref, ct_ref, cn_ref, nch_ref, nt_ref, xs_hbm, wg_ref, wu_ref, wd_ref, y_hbm,
                   acc_ref, xb_ref, xstage_ref, wgb_ref, wub_ref, wdb_ref, xsem, osem):
    c = pl.program_id(0)
    f = pl.program_id(1)
    last_f = pl.num_programs(1) - 1

    def tile_rows(tile):
        return pl.ds(pl.multiple_of(tile * MOE_TILE, MOE_TILE), MOE_TILE)

    def out_copy(slot, tile):
        return pltpu.make_async_copy(acc_ref.at[slot], y_hbm.at[tile_rows(tile)], osem)

    @pl.when((c == 0) & (f == 0))
    def _():
        acc_ref[0] = jnp.zeros((MOE_TILE, D_MODEL), F32)
        for e in range(N_EXPERTS):
            tail = MOE_TILES - 1 - e

            @pl.when(tail >= nt_ref[0])
            def _():
                cp = out_copy(0, tail)
                cp.start()
                cp.wait()

    @pl.when(c < nch_ref[0])
    def _():
        n = cn_ref[c]
        t0 = ct_ref[c]

        def x_copy(j):
            return pltpu.make_async_copy(xs_hbm.at[tile_rows(t0 + j)], xstage_ref.at[j % 2], xsem.at[j % 2])

        @pl.when(f == 0)
        def _():
            x_copy(0).start()

        wgb_ref[...] = wg_ref[...].astype(BF16)
        wub_ref[...] = wu_ref[...].astype(BF16)
        wdb_ref[...] = wd_ref[...].astype(BF16)

        def tile_step(j, carry):
            @pl.when(f == 0)
            def _():
                @pl.when(j + 1 < n)
                def _():
                    x_copy(j + 1).start()

                x_copy(j).wait()
                xb_ref[j] = xstage_ref[j % 2].astype(BF16)
                acc_ref[j] = jnp.zeros((MOE_TILE, D_MODEL), F32)

            x = xb_ref[j]
            g = jnp.dot(x, wgb_ref[...], preferred_element_type=F32)
            u = jnp.dot(x, wub_ref[...], preferred_element_type=F32)
            a = (g * _sigmoid(g) * u).astype(BF16)
            acc_ref[j] += jnp.dot(a, wdb_ref[...], preferred_element_type=F32)

            @pl.when(f == last_f)
            def _():
                @pl.when(j > 0)
                def _():
                    out_copy(j - 1, t0 + j - 1).wait()

                out_copy(j, t0 + j).start()

            return carry

        lax.fori_loop(0, n, tile_step, 0)

        @pl.when(f == last_f)
        def _():
            out_copy(n - 1, t0 + n - 1).wait()


def _combine_kernel(pos_ref, route_ref, x_ref, g2_ref, y_hbm, op_ref, os_ref, buf_ref, sem):
    i = pl.program_id(0)
    n = pl.num_programs(0)
    tm = x_ref.shape[0]
    slot = i % 2

    def row_copy(step, s, t, k):
        src = y_hbm.at[pl.ds(pos_ref[TOP_K * (step * tm + t) + k], 1)]
        return pltpu.make_async_copy(src, buf_ref.at[s, k, pl.ds(t, 1)], sem.at[s])

    def issue(step, s):
        def body(t, carry):
            for k in range(TOP_K):
                row_copy(step, s, t, k).start(priority=k % DMA_PRIORITIES)
            return carry
        lax.fori_loop(0, tm, body, 0, unroll=ROW_COPY_UNROLL)

    @pl.when(i == 0)
    def _():
        issue(0, 0)

    @pl.when(i + 1 < n)
    def _():
        issue(i + 1, 1 - slot)

    def wait_body(t, carry):
        for k in range(TOP_K):
            row_copy(i, slot, t, k).wait()
        return carry

    lax.fori_loop(0, tm, wait_body, 0, unroll=ROW_COPY_UNROLL)

    lane = lax.broadcasted_iota(jnp.int32, route_ref.shape, 1)
    rt = route_ref[...]
    w1 = jnp.sum(jnp.where(lane == 2 * TOP_K, rt, 0.0), axis=-1, keepdims=True)
    w2 = jnp.sum(jnp.where(lane == 2 * TOP_K + 1, rt, 0.0), axis=-1, keepdims=True)
    out = x_ref[...] + g2_ref[...] * (w1 * buf_ref[slot, 0] + w2 * buf_ref[slot, 1])

    @pl.when(i < N_PROMPT // tm)
    def _():
        op_ref[...] = out

    @pl.when(i >= N_PROMPT // tm)
    def _():
        os_ref[...] = out


def _moe(h, x, logits, wg, wu, wd, mod, l):
    route, cnt = pl.pallas_call(
        _route_kernel, grid=(N_TOK // ROUTE_TM,),
        in_specs=[pl.BlockSpec((ROUTE_TM, LANES), lambda i: (i, 0))],
        out_specs=[pl.BlockSpec((ROUTE_TM, LANES), lambda i: (i, 0)), pl.BlockSpec((8, LANES), lambda i: (0, 0))],
        out_shape=[jax.ShapeDtypeStruct((N_TOK, LANES), F32), jax.ShapeDtypeStruct((8, LANES), F32)],
        scratch_shapes=[pltpu.VMEM((1, LANES), F32), pltpu.VMEM((ROUTE_TM, ROUTE_TM), BF16)],
        compiler_params=_cparams(("arbitrary",)), name="route",
    )(logits)

    expert = route[:, 0:TOP_K].astype(jnp.int32)
    rank = route[:, TOP_K:2 * TOP_K].astype(jnp.int32)
    n_sub = (cnt[0, :N_EXPERTS].astype(jnp.int32) + MOE_TILE - 1) // MOE_TILE
    end = jnp.cumsum(n_sub)
    start = end - n_sub
    start_of = jnp.sum(jnp.where(expert[:, :, None] == jnp.arange(N_EXPERTS), start, 0), axis=-1)
    pos = (start_of * MOE_TILE + rank).reshape(-1)
    n_tiles = end[N_EXPERTS - 1:]
    last_tile = jnp.where(n_sub > 0, end - 1, -1).astype(jnp.int32)

    xs = pl.pallas_call(
        _dispatch_kernel,
        grid_spec=pltpu.PrefetchScalarGridSpec(
            num_scalar_prefetch=3, grid=(N_TOK // DISPATCH_TM,),
            in_specs=[pl.BlockSpec((DISPATCH_TM, D_MODEL), lambda i, p, lt, nt: (i, 0))],
            out_specs=pl.BlockSpec(memory_space=pl.ANY),
            scratch_shapes=[pltpu.VMEM((MOE_TILE, D_MODEL), F32), pltpu.VMEM((2, DISPATCH_TM, D_MODEL), F32),
                            pltpu.SemaphoreType.DMA(()), pltpu.SemaphoreType.DMA((2,))]),
        out_shape=jax.ShapeDtypeStruct((MOE_ROWS, D_MODEL), F32),
        compiler_params=_cparams(("arbitrary",)), name="moe_dispatch",
    )(pos, last_tile, n_tiles, h)

    n_chunk = (n_sub + MOE_CHUNK - 1) // MOE_CHUNK
    chunk_end = jnp.cumsum(n_chunk)
    cidx = jnp.arange(MOE_CHUNKS)
    c_expert = jnp.minimum(jnp.sum(cidx[:, None] >= chunk_end[None, :], axis=1), N_EXPERTS - 1).astype(jnp.int32)
    c_k = cidx - (chunk_end - n_chunk)[c_expert]
    c_tile0 = (start[c_expert] + c_k * MOE_CHUNK).astype(jnp.int32)
    c_ntiles = jnp.clip(n_sub[c_expert] - c_k * MOE_CHUNK, 0, MOE_CHUNK).astype(jnp.int32)
    n_chunks = chunk_end[N_EXPERTS - 1:]

    nf = D_FF_EXPERT // MOE_TF

    def w_idx(c, f, ce, nch):
        live = c < nch[0]
        return ce[jnp.minimum(c, nch[0] - 1)], jnp.where(live, f, nf - 1)

    def up_map(c, f, ce, ct, cn, nch, nt):
        e, ff = w_idx(c, f, ce, nch)
        return (e, 0, ff)

    def down_map(c, f, ce, ct, cn, nch, nt):
        e, ff = w_idx(c, f, ce, nch)
        return (e, ff, 0)

    y = pl.pallas_call(
        _expert_kernel,
        grid_spec=pltpu.PrefetchScalarGridSpec(
            num_scalar_prefetch=5, grid=(MOE_CHUNKS, nf),
            in_specs=[pl.BlockSpec(memory_space=pl.ANY),
                      pl.BlockSpec((None, D_MODEL, MOE_TF), up_map),
                      pl.BlockSpec((None, D_MODEL, MOE_TF), up_map),
                      pl.BlockSpec((None, MOE_TF, D_MODEL), down_map)],
            out_specs=pl.BlockSpec(memory_space=pl.ANY),
            scratch_shapes=[pltpu.VMEM((MOE_CHUNK, MOE_TILE, D_MODEL), F32),
                            pltpu.VMEM((MOE_CHUNK, MOE_TILE, D_MODEL), BF16),
                            pltpu.VMEM((2, MOE_TILE, D_MODEL), F32),
                            pltpu.VMEM((D_MODEL, MOE_TF), BF16), pltpu.VMEM((D_MODEL, MOE_TF), BF16),
                            pltpu.VMEM((MOE_TF, D_MODEL), BF16),
                            pltpu.SemaphoreType.DMA((2,)), pltpu.SemaphoreType.DMA(())]),
        out_shape=jax.ShapeDtypeStruct((MOE_ROWS, D_MODEL), F32),
        compiler_params=_cparams(("arbitrary", "arbitrary")), name="moe_experts",
    )(c_expert, c_tile0, c_ntiles, n_chunks, n_tiles, xs, wg, wu, wd)

    tm = COMBINE_TM
    return pl.pallas_call(
        _combine_kernel,
        grid_spec=pltpu.PrefetchScalarGridSpec(
            num_scalar_prefetch=1, grid=(N_TOK // tm,),
            in_specs=[pl.BlockSpec((tm, LANES), lambda i, p: (i, 0)),
                      pl.BlockSpec((tm, D_MODEL), lambda i, p: (i, 0)),
                      pl.BlockSpec((None, None, 1, D_MODEL), lambda i, p: (l, _cond_row(i, tm), 0, 5)),
                      pl.BlockSpec(memory_space=pl.ANY)],
            out_specs=[pl.BlockSpec((tm, D_MODEL), lambda i, p: (jnp.minimum(i, N_PROMPT // tm - 1), 0)),
                       pl.BlockSpec((tm, D_MODEL), lambda i, p: (jnp.maximum(i - N_PROMPT // tm, 0), 0))],
            scratch_shapes=[pltpu.VMEM((2, TOP_K, tm, D_MODEL), F32), pltpu.SemaphoreType.DMA((2,))]),
        out_shape=[jax.ShapeDtypeStruct((N_PROMPT, D_MODEL), F32), jax.ShapeDtypeStruct((N_SAMPLE, D_MODEL), F32)],
        compiler_params=_cparams(("arbitrary",)), name="moe_combine",
    )(pos, route, x, mod, y)


def _kv_leaf_kernel(*refs):
    ins, outs = refs[:4 * DEPTH], refs[4 * DEPTH:]
    for l in range(DEPTH):
        for j in range(4):
            outs[j][l] = ins[4 * l + j][...]


def _kv_leaves(per_layer):
    widths = (A_KV, A_KV, C_W, C_W)
    return pl.pallas_call(
        _kv_leaf_kernel, grid=(BATCH,),
        in_specs=[pl.BlockSpec((SEQ, w), lambda b: (b, 0)) for _ in range(DEPTH) for w in widths],
        out_specs=[pl.BlockSpec((None, DEPTH, SEQ, w), lambda b: (b, 0, 0, 0)) for w in widths],
        out_shape=[jax.ShapeDtypeStruct((BATCH, DEPTH, SEQ, w), F32) for w in widths],
        compiler_params=_cparams(("parallel",)), name="kv_leaves",
    )(*[a for layer in per_layer for a in layer])


def _rope_tables():
    t = jnp.arange(DEC_SEQ)
    row = (t // GRID_W).astype(F32)
    col = (t % GRID_W).astype(F32)
    n_freq = HEAD_DIM // 4
    inv = ROPE_BASE ** (-jnp.arange(n_freq, dtype=F32) / n_freq)
    ang = jnp.concatenate([row[:, None] * inv, col[:, None] * inv], axis=-1)
    cos, sin = jnp.cos(ang), jnp.sin(ang)
    cos_h = jnp.concatenate([cos, cos], axis=-1)
    sin_h = jnp.concatenate([-sin, sin], axis=-1)
    cos_l = jnp.tile(jnp.concatenate([cos_h, cos_h], axis=-1), (DEC_BATCH, 1))
    sin_l = jnp.tile(jnp.concatenate([sin_h, sin_h], axis=-1), (DEC_BATCH, 1))
    cos_t = jnp.concatenate([jnp.ones((N_PROMPT, LANES), F32), cos_l], axis=0)
    sin_t = jnp.concatenate([jnp.zeros((N_PROMPT, LANES), F32), sin_l], axis=0)
    return cos_t, sin_t


def kernel(x_prompt, x_sample, cache_a_k, cache_a_v, cache_c_k, cache_c_v, c, c_ctx, w_mod, b_mod, norm1_g, norm2_g, w_in, qk_norm_a, qk_norm_c, sink_a, rpb_c, w_branch_a, w_branch_b, w_branch_c, w_out, w_ff_gate, w_ff_up, w_ff_down, w_router, w_exp_gate, w_exp_up, w_exp_down):
    x = (x_prompt.reshape(N_PROMPT, D_MODEL), x_sample.reshape(N_SAMPLE, D_MODEL))
    cond =jnp.concatenate([c_ctx[None, :], c], axis=0)
    cond_t = jnp.broadcast_to(cond[:, :, None], (N_COND, D_MODEL, LANES))
    mod = _modulation(cond_t, w_mod, b_mod)
    cos_t, sin_t = _rope_tables()
    bias = _nbr_bias_tables(rpb_c)
    ck_a = cache_a_k.reshape(DEC_BATCH, DEPTH, PAST_LEN, A_KV)
    cv_a = cache_a_v.reshape(DEC_BATCH, DEPTH, PAST_LEN, A_KV)
    ck_c = cache_c_k.reshape(DEC_BATCH, DEPTH, PAST_LEN, C_W)
    cv_c = cache_c_v.reshape(DEC_BATCH, DEPTH, PAST_LEN, C_W)

    new_kv = []
    h = x
    for l in range(DEPTH):
        qa, ka, va, fb, qc, kc, vc, gates = _in_proj(h, w_in, cos_t, sin_t, qk_norm_a[l], qk_norm_c[l], l,
                                                     norm=(norm1_g[l], mod) if l == 0 else None)
        oa_p, oc_p = _ctx_attn(sink_a, qa, ka, va, qc, kc, vc, l)
        oa_s = _win_attn(sink_a, qa, ka, va, ck_a, cv_a, l)
        oc_s = _nbr_attn(qc, kc, vc, ck_c, cv_c, bias, l)
        ob_p = _fourier(fb, BATCH, SEQ, 0, SEQ)
        ob_s = _fourier(fb, DEC_BATCH, DEC_SEQ, N_PROMPT, 512)
        branches = ((oa_p, ob_p, oc_p), (oa_s, ob_s, oc_s), gates, x,
                    w_branch_a, w_branch_b, w_branch_c, w_out, mod, l, norm2_g[l])
        i = l // 2
        if l % 2 == 0:
            x, h2 = _merge(*branches)
            x, h = _ffn(h2, x, w_ff_gate, w_ff_up, w_ff_down, mod, l, norm1_g[l + 1])
        else:
            x, h2, logits = _merge(*branches, w_router=w_router[i])
            xp, xs = _moe(h2, x, logits, w_exp_gate[i], w_exp_up[i], w_exp_down[i], mod, l)
        new_kv.append((ka, va, kc, vc))

    new_ak, new_av, new_ck, new_cv = _kv_leaves(new_kv)
    return (xp.reshape(BATCH, SEQ, D_MODEL), xs.reshape(DEC_BATCH, DEC_SEQ, D_MODEL),
            new_ak.reshape(BATCH, DEPTH, SEQ, A_KV_HEADS, HEAD_DIM),
            new_av.reshape(BATCH, DEPTH, SEQ, A_KV_HEADS, HEAD_DIM),
            new_ck.reshape(BATCH, DEPTH, SEQ, C_HEADS, HEAD_DIM),
            new_cv.reshape(BATCH, DEPTH, SEQ, C_HEADS, HEAD_DIM))
```
